```python
import math
import jax, jax.numpy as jnp
from jax import lax
import numpy as np

D_MODEL = 2048
BATCH = 1
SEQ = 8192
DEPTH = 4

GRID_W = 64
CTX_LEN = 256
HEAD_DIM = 128
BLOCK = 128
ROPE_THETA = 10000.0
EPS = 1e-6
NEG_INF = -1e30

A_HEADS = 8
A_KV_HEADS = 2
A_GROUP = A_HEADS // A_KV_HEADS
WINDOW = 128
A_WIDTH = A_HEADS * HEAD_DIM
A_KV = A_KV_HEADS * HEAD_DIM

B_WIDTH = 1024
HYENA_ORDER = 2
HYENA_BANDS = 16
HYENA_EMB = 1 + 2 * HYENA_BANDS
HYENA_HID = 64
SHORT_CONV = 3
DECAY_TARGET = 1e-2
DECAY_MAX = abs(math.log(DECAY_TARGET)) / 0.3
DECAY_MIN = abs(math.log(DECAY_TARGET)) / 1.5

C_HEADS = 8
C_KV_HEADS = 2
C_GROUP = C_HEADS // C_KV_HEADS
C_WIDTH = C_HEADS * HEAD_DIM
C_KV = C_KV_HEADS * HEAD_DIM

M_HEADS = 8
Q_LORA = 512
KV_LORA = 256
NOPE_DIM = 128
ROPE_DIM = 64
V_DIM = 128
QK_DIM = NOPE_DIM + ROPE_DIM
M_WIDTH = M_HEADS * V_DIM

BRANCH = A_WIDTH + B_WIDTH
EVEN_SPLIT = (A_WIDTH, A_KV, A_KV, 3 * B_WIDTH, BRANCH)
ODD_SPLIT = (C_WIDTH, C_KV, C_KV, Q_LORA, KV_LORA, ROPE_DIM, BRANCH)
EVEN_IN = sum(EVEN_SPLIT)
ODD_IN = sum(ODD_SPLIT)
N_EVEN = (DEPTH + 1) // 2
N_ODD = DEPTH // 2

kernel_name = 'hybrid_prefix_diffusion_backbone'

F32 = jnp.float32


def _rmsnorm(x, g):
    xf = x.astype(F32)
    y = xf * lax.rsqrt(jnp.mean(xf * xf, axis=-1, keepdims=True) + EPS)
    return (y * g.astype(F32)).astype(x.dtype)


def _split(p, sizes):
    idx = np.cumsum(sizes)[:-1].tolist()
    return jnp.split(p, idx, axis=-1)


def _heads(t, n):
    return t.reshape(t.shape[:-1] + (n, t.shape[-1] // n))


def _axial_rope_tables(n_tokens, n_rot):
    rows = n_tokens // GRID_W
    row = jnp.broadcast_to(jnp.arange(rows)[:, None], (rows, GRID_W)).reshape(-1).astype(F32)
    col = jnp.broadcast_to(jnp.arange(GRID_W)[None, :], (rows, GRID_W)).reshape(-1).astype(F32)
    n_freq = n_rot // 4
    inv = ROPE_THETA ** (-jnp.arange(n_freq, dtype=F32) / n_freq)
    ang = jnp.concatenate([row[:, None] * inv, col[:, None] * inv], axis=-1)
    return jnp.cos(ang), jnp.sin(ang)


def _rope(x, cos, sin):
    half = x.shape[-1] // 2
    x1, x2 = x[..., :half], x[..., half:]
    cs = cos[None, :, None, :].astype(x.dtype)
    sn = sin[None, :, None, :].astype(x.dtype)
    return jnp.concatenate([x1 * cs - x2 * sn, x2 * cs + x1 * sn], axis=-1)


def _attend(q, k, v, sink=None):
    s = jnp.einsum('bqhgd,bkhd->bhgqk', q, k, preferred_element_type=F32) * (q.shape[-1] ** -0.5)
    if sink is not None:
        sk = jnp.broadcast_to(sink.astype(F32)[None, :, :, None, None], s.shape[:-1] + (1,))
        p = jax.nn.softmax(jnp.concatenate([s, sk], axis=-1), axis=-1)[..., :-1]
    else:
        p = jax.nn.softmax(s, axis=-1)
    return jnp.einsum('bhgqk,bkhd->bqhgd', p.astype(v.dtype), v)


def _block_sweep(q, k, v):
    B, S, Hk, G, dk = q.shape
    nb = S // BLOCK
    qb = jnp.moveaxis(q.reshape(B, nb, BLOCK, Hk, G, dk), 1, 0)
    o = lax.map(lambda qi: _attend(qi, k, v), qb)
    return jnp.moveaxis(o, 0, 1).reshape(B, S, -1)


def _window_attention(q, k, v, kc, vc, sink):
    B, S, Hk, G, d = q.shape
    nb = S // BLOCK
    qb = q.reshape(B, nb, BLOCK, Hk, G, d)

    def windows(t):
        tp = jnp.pad(t, ((0, 0), (BLOCK, BLOCK), (0, 0), (0, 0))).reshape(B, nb + 2, BLOCK, Hk, t.shape[-1])
        return jnp.concatenate([tp[:, :-2], tp[:, 1:-1], tp[:, 2:]], axis=2)

    kw, vw = windows(k), windows(v)
    scale = d ** -0.5
    s_w = jnp.einsum('bnqhgd,bnkhd->bnhgqk', qb, kw, preferred_element_type=F32) * scale
    s_c = jnp.einsum('bnqhgd,bkhd->bnhgqk', qb, kc, preferred_element_type=F32) * scale
    qpos = jnp.arange(nb)[:, None] * BLOCK + jnp.arange(BLOCK)[None, :]
    kpos = (jnp.arange(nb)[:, None] - 1) * BLOCK + jnp.arange(3 * BLOCK)[None, :]
    valid = ((jnp.abs(qpos[:, :, None] - kpos[:, None, :]) <= WINDOW)
             & ((kpos >= 0) & (kpos < S))[:, None, :])
    s_w = jnp.where(valid[None, :, None, None], s_w, NEG_INF)
    sk = jnp.broadcast_to(sink.astype(F32)[None, None, :, :, None, None], s_w.shape[:-1] + (1,))
    p = jax.nn.softmax(jnp.concatenate([s_w, s_c, sk], axis=-1), axis=-1)
    nw, nc = 3 * BLOCK, kc.shape[1]
    o = (jnp.einsum('bnhgqk,bnkhd->bnqhgd', p[..., :nw].astype(v.dtype), vw)
         + jnp.einsum('bnhgqk,bkhd->bnqhgd', p[..., nw:nw + nc].astype(vc.dtype), vc))
    return o.reshape(B, S, Hk * G * d)


def _conv3(u, w, b):
    up = jnp.pad(u, ((0, 0), (1, 1), (0, 0)))
    return up[:, :-2] * w[0] + up[:, 1:-1] * w[1] + up[:, 2:] * w[2] + b


def _hyena_filters(L, w1, b1, f1, w2, b2, f2, w3):
    pos = jnp.arange(L, dtype=F32)
    t = pos / max(L - 1, 1)
    bands = jnp.linspace(1e-4, HYENA_BANDS - 1, HYENA_BANDS, dtype=F32)
    ang = (2.0 * math.pi / L) * pos[:, None] * bands[None, :]
    z = jnp.concatenate([t[:, None], jnp.cos(ang), -jnp.sin(ang)], axis=-1)
    h = jnp.sin(f1.astype(F32) * (z @ w1.astype(F32) + b1.astype(F32)))
    h = jnp.sin(f2.astype(F32) * (h @ w2.astype(F32) + b2.astype(F32)))
    h = (h @ w3.astype(F32)).reshape(L, HYENA_ORDER, 2, B_WIDTH)
    deltas = jnp.linspace(DECAY_MAX, DECAY_MIN, B_WIDTH, dtype=F32)
    return h * jnp.exp(-t[:, None] * deltas[None, :])[:, None, None, :]


def _long_conv(u, h):
    L = u.shape[1]
    k = jnp.concatenate([h[:, 0], jnp.zeros_like(h[:1, 0]), h[:0:-1, 1]], axis=0)
    k = k / jnp.sum(jnp.abs(k), axis=0, keepdims=True)
    uf = jnp.fft.rfft(u.astype(F32), n=2 * L, axis=1)
    kf = jnp.fft.rfft(k, n=2 * L, axis=0)
    y = jnp.fft.irfft(uf * kf[None], n=2 * L, axis=1)[:, :L]
    return y.astype(u.dtype)


def _hyena(p, conv_w, conv_b, filt, bias):
    u = _conv3(p, conv_w, conv_b)
    x1, x2, z = jnp.split(u, 3, axis=-1)
    z = x1 * (_long_conv(z, filt[:, 0]) + bias[0] * z)
    return x2 * (_long_conv(z, filt[:, 1]) + bias[1] * z)


def _even_mixer(u_lat, u_ctx, w_in, qn_g, kn_g, sink, conv_w, conv_b, fparams, hy_bias, rope_h, need_ctx):
    B, S, _ = u_lat.shape
    C = u_ctx.shape[1]
    q, k, v, hy, gate = _split(u_lat @ w_in, EVEN_SPLIT)
    qc, kc, vc, hyc, gatec = _split(u_ctx @ w_in, EVEN_SPLIT)
    sink_g = sink.reshape(A_KV_HEADS, A_GROUP)
    kc = _rmsnorm(_heads(kc, A_KV_HEADS), kn_g)
    vc = _heads(vc, A_KV_HEADS)
    q = _rope(_rmsnorm(_heads(q, A_HEADS), qn_g), *rope_h).reshape(B, S, A_KV_HEADS, A_GROUP, HEAD_DIM)
    k = _rope(_rmsnorm(_heads(k, A_KV_HEADS), kn_g), *rope_h)
    a = _window_attention(q, k, _heads(v, A_KV_HEADS), kc, vc, sink_g)
    b = _hyena(hy, conv_w, conv_b, _hyena_filters(S, *fparams), hy_bias)
    o_lat = jnp.concatenate([a, b], axis=-1) * jax.nn.silu(gate)
    if not need_ctx:
        return o_lat, None
    qc = _rmsnorm(_heads(qc, A_HEADS), qn_g).reshape(B, C, A_KV_HEADS, A_GROUP, HEAD_DIM)
    ac = _attend(qc, kc, vc, sink_g).reshape(B, C, A_WIDTH)
    bc = _hyena(hyc, conv_w, conv_b, _hyena_filters(C, *fparams), hy_bias)
    o_ctx = jnp.concatenate([ac, bc], axis=-1) * jax.nn.silu(gatec)
    return o_lat, o_ctx


def _mla_q(mq, cq_g, wuq, mq_g, rope):
    B, L, _ = mq.shape
    q = (_rmsnorm(mq, cq_g) @ wuq).reshape(B, L, M_HEADS, QK_DIM)
    q_n = _rmsnorm(q[..., :NOPE_DIM], mq_g[:NOPE_DIM])
    q_r = _rmsnorm(q[..., NOPE_DIM:], mq_g[NOPE_DIM:])
    if rope is not None:
        q_r = _rope(q_r, *rope)
    return jnp.concatenate([q_n, q_r], axis=-1)[:, :, :, None, :]


def _mla_kv(mkv, mkr, ckv_g, wukv, mk_g, rope):
    B, L, _ = mkv.shape
    kv = (_rmsnorm(mkv, ckv_g) @ wukv).reshape(B, L, M_HEADS, NOPE_DIM + V_DIM)
    k_n = _rmsnorm(kv[..., :NOPE_DIM], mk_g[:NOPE_DIM])
    k_r = _rmsnorm(mkr, mk_g[NOPE_DIM:])[:, :, None, :]
    if rope is not None:
        k_r = _rope(k_r, *rope)
    k = jnp.concatenate([k_n, jnp.broadcast_to(k_r, (B, L, M_HEADS, ROPE_DIM))], axis=-1)
    return k, kv[..., NOPE_DIM:]


def _odd_mixer(u_lat, u_ctx, w_in, qn_g, kn_g, cq_g, ckv_g, wuq, wukv, mq_g, mk_g, rope_h, rope_r, need_ctx):
    B, S, _ = u_lat.shape
    C = u_ctx.shape[1]
    q, k, v, mq, mkv, mkr, gate = _split(u_lat @ w_in, ODD_SPLIT)
    qc, kc, vc, mqc, mkvc, mkrc, gatec = _split(u_ctx @ w_in, ODD_SPLIT)
    kc = _rmsnorm(_heads(kc, C_KV_HEADS), kn_g)
    vc = _heads(vc, C_KV_HEADS)
    q = _rope(_rmsnorm(_heads(q, C_HEADS), qn_g), *rope_h).reshape(B, S, C_KV_HEADS, C_GROUP, HEAD_DIM)
    k = _rope(_rmsnorm(_heads(k, C_KV_HEADS), kn_g), *rope_h)
    o_c = _block_sweep(q, jnp.concatenate([kc, k], axis=1), jnp.concatenate([vc, _heads(v, C_KV_HEADS)], axis=1))
    km_c, vm_c = _mla_kv(mkvc, mkrc, ckv_g, wukv, mk_g, None)
    km, vm = _mla_kv(mkv, mkr, ckv_g, wukv, mk_g, rope_r)
    qm = _mla_q(mq, cq_g, wuq, mq_g, rope_r)
    o_d = _block_sweep(qm, jnp.concatenate([km_c, km], axis=1), jnp.concatenate([vm_c, vm], axis=1))
    o_lat = jnp.concatenate([o_c, o_d], axis=-1) * jax.nn.silu(gate)
    if not need_ctx:
        return o_lat, None
    qc = _rmsnorm(_heads(qc, C_HEADS), qn_g).reshape(B, C, C_KV_HEADS, C_GROUP, HEAD_DIM)
    oc_c = _attend(qc, kc, vc).reshape(B, C, C_WIDTH)
    od_c = _attend(_mla_q(mqc, cq_g, wuq, mq_g, None), km_c, vm_c).reshape(B, C, M_WIDTH)
    o_ctx = jnp.concatenate([oc_c, od_c], axis=-1) * jax.nn.silu(gatec)
    return o_lat, o_ctx


def setup_inputs(seed: int = 0) -> dict:
    key = jax.random.key(seed)
    ks = iter(jax.random.split(key, 40))

    def nrm(shape, scale):
        return jax.random.normal(next(ks), shape, F32) * scale

    D = D_MODEL
    return {
        'x': nrm((BATCH, SEQ, D), 1.0),
        'c': nrm((BATCH, D), 1.0),
        'ctx': nrm((BATCH, CTX_LEN, D), 1.0),
        'c_ctx': nrm((D,), 1.0),
        'ada_w': nrm((DEPTH, D, 3 * D), 0.5 * D ** -0.5),
        'ada_b': nrm((DEPTH, 3 * D), 0.01),
        'norm_g': 1.0 + nrm((DEPTH, D), 0.02),
        'w_out': nrm((DEPTH, BRANCH, D), BRANCH ** -0.5),
        'ev_w_in': nrm((N_EVEN, D, EVEN_IN), D ** -0.5),
        'ev_qn_g': 1.0 + nrm((N_EVEN, HEAD_DIM), 0.02),
        'ev_kn_g': 1.0 + nrm((N_EVEN, HEAD_DIM), 0.02),
        'ev_sink': nrm((N_EVEN, A_HEADS), 0.5),
        'ev_conv_w': nrm((N_EVEN, SHORT_CONV, 3 * B_WIDTH), SHORT_CONV ** -0.5),
        'ev_conv_b': nrm((N_EVEN, 3 * B_WIDTH), 0.01),
        'hy_w1': nrm((N_EVEN, HYENA_EMB, HYENA_HID), HYENA_EMB ** -0.5),
        'hy_b1': nrm((N_EVEN, HYENA_HID), 0.1),
        'hy_f1': 1.0 + nrm((N_EVEN, HYENA_HID), 0.1),
        'hy_w2': nrm((N_EVEN, HYENA_HID, HYENA_HID), HYENA_HID ** -0.5),
        'hy_b2': nrm((N_EVEN, HYENA_HID), 0.1),
        'hy_f2': 1.0 + nrm((N_EVEN, HYENA_HID), 0.1),
        'hy_w3': nrm((N_EVEN, HYENA_HID, HYENA_ORDER * 2 * B_WIDTH), HYENA_HID ** -0.5),
        'hy_bias': nrm((N_EVEN, HYENA_ORDER, B_WIDTH), 1.0),
        'od_w_in': nrm((N_ODD, D, ODD_IN), D ** -0.5),
        'od_qn_g': 1.0 + nrm((N_ODD, HEAD_DIM), 0.02),
        'od_kn_g': 1.0 + nrm((N_ODD, HEAD_DIM), 0.02),
        'od_cq_g': 1.0 + nrm((N_ODD, Q_LORA), 0.02),
        'od_ckv_g': 1.0 + nrm((N_ODD, KV_LORA), 0.02),
        'od_wuq': nrm((N_ODD, Q_LORA, M_HEADS * QK_DIM), Q_LORA ** -0.5),
        'od_wukv': nrm((N_ODD, KV_LORA, M_HEADS * (NOPE_DIM + V_DIM)), KV_LORA ** -0.5),
        'od_mq_g': 1.0 + nrm((N_ODD, QK_DIM), 0.02),
        'od_mk_g': 1.0 + nrm((N_ODD, QK_DIM), 0.02),
    }


def reference(x, c, ctx, c_ctx, ada_w, ada_b, norm_g, w_out,
              ev_w_in, ev_qn_g, ev_kn_g, ev_sink, ev_conv_w, ev_conv_b,
              hy_w1, hy_b1, hy_f1, hy_w2, hy_b2, hy_f2, hy_w3, hy_bias,
              od_w_in, od_qn_g, od_kn_g, od_cq_g, od_ckv_g, od_wuq, od_wukv, od_mq_g, od_mk_g):
    S = x.shape[1]
    rope_h = _axial_rope_tables(S, HEAD_DIM)
    rope_r = _axial_rope_tables(S, ROPE_DIM)
    h_lat, h_ctx = x, ctx
    for i in range(DEPTH):
        need_ctx = i < DEPTH - 1
        sh_l, sc_l, g_l = jnp.split((jax.nn.silu(c) @ ada_w[i] + ada_b[i])[:, None, :], 3, axis=-1)
        sh_c, sc_c, g_c = jnp.split(jax.nn.silu(c_ctx) @ ada_w[i] + ada_b[i], 3, axis=-1)
        u_lat = _rmsnorm(h_lat, norm_g[i]) * (1.0 + sc_l) + sh_l
        u_ctx = _rmsnorm(h_ctx, norm_g[i]) * (1.0 + sc_c) + sh_c
        if i % 2 == 0:
            e = i // 2
            o_lat, o_ctx = _even_mixer(
                u_lat, u_ctx, ev_w_in[e], ev_qn_g[e], ev_kn_g[e], ev_sink[e], ev_conv_w[e], ev_conv_b[e],
                (hy_w1[e], hy_b1[e], hy_f1[e], hy_w2[e], hy_b2[e], hy_f2[e], hy_w3[e]), hy_bias[e],
                rope_h, need_ctx)
        else:
            o = i // 2
            o_lat, o_ctx = _odd_mixer(
                u_lat, u_ctx, od_w_in[o], od_qn_g[o], od_kn_g[o], od_cq_g[o], od_ckv_g[o],
                od_wuq[o], od_wukv[o], od_mq_g[o], od_mk_g[o], rope_h, rope_r, need_ctx)
        h_lat = h_lat + g_l * (o_lat @ w_out[i])
        if need_ctx:
            h_ctx = h_ctx + g_c * (o_ctx @ w_out[i])
    return h_lat
```

```python
import functools
import math

import ml_dtypes
import numpy as np
import jax
import jax.numpy as jnp
from jax import lax
from jax.experimental import pallas as pl
from jax.experimental.pallas import tpu as pltpu

F32 = jnp.float32
BF16 = jnp.bfloat16

GRID_W = 64
HEAD_DIM = 128
ROPE_THETA = 10000.0
EPS = 1e-6
NEG_INF = -1e30
A_HEADS, A_KV_HEADS, WINDOW = 8, 2, 128
B_WIDTH = 1024
HYENA_BANDS = 16
HYENA_EMB = 1 + 2 * HYENA_BANDS
DECAY_TARGET = 1e-2
DECAY_MAX = abs(math.log(DECAY_TARGET)) / 0.3
DECAY_MIN = abs(math.log(DECAY_TARGET)) / 1.5
C_HEADS, C_KV_HEADS = 8, 2
M_HEADS, Q_LORA, KV_LORA, NOPE_DIM, ROPE_DIM, V_DIM = 8, 512, 256, 128, 64, 128
QK_DIM = NOPE_DIM + ROPE_DIM
ATT_WIDTH = A_HEADS * HEAD_DIM
KV_WIDTH = A_KV_HEADS * HEAD_DIM
BRANCH = ATT_WIDTH + B_WIDTH

LANE = 128
ROW_TILE = 256
VMEM_LIMIT = 48 * 1024 * 1024

OFF_GATE, OFF_Q, OFF_K, OFF_V, OFF_REST = 0, BRANCH, BRANCH + ATT_WIDTH, BRANCH + ATT_WIDTH + KV_WIDTH, BRANCH + ATT_WIDTH + 2 * KV_WIDTH
OFF_MQ, OFF_MKV, OFF_MKR = OFF_REST, OFF_REST + Q_LORA, OFF_REST + Q_LORA + KV_LORA


def _params(*sem):
    return pltpu.CompilerParams(dimension_semantics=sem, vmem_limit_bytes=VMEM_LIMIT)


def _split_bf16(x):
    hi = x.astype(BF16)
    return hi, (x - hi.astype(F32)).astype(BF16)


def _dot(a, b):
    return jnp.dot(a, b, preferred_element_type=F32)


def _dot3(a_hi, a_lo, b_hi, b_lo):
    return _dot(a_hi, b_hi) + (_dot(a_hi, b_lo) + _dot(a_lo, b_hi))


def _ada_kernel(c_ref, w_ref, b_ref, o_ref):
    c = c_ref[...]
    a = (c * jax.nn.sigmoid(c)).astype(BF16)
    o_ref[0] = _dot(a, w_ref[0].astype(BF16)) + b_ref[0]


def ada_mod(cc, ada_w, ada_b):
    depth, d, n = ada_w.shape
    tn = _pick_tile(n, 1024)
    return pl.pallas_call(
        _ada_kernel, grid=(depth, n // tn),
        in_specs=[pl.BlockSpec((8, d), lambda l, j: (0, 0)),
                  pl.BlockSpec((1, d, tn), lambda l, j: (l, 0, j)),
                  pl.BlockSpec((1, 1, tn), lambda l, j: (l, 0, j))],
        out_specs=pl.BlockSpec((1, 8, tn), lambda l, j: (l, 0, j)),
        out_shape=jax.ShapeDtypeStruct((depth, 8, n), F32),
        compiler_params=_params("parallel", "parallel"), name="ada_mod",
    )(cc, ada_w, ada_b.reshape(depth, 1, n))


def _norm_kernel(x_ref, g_ref, mod_ref, o_ref, *, ctx_tiles):
    x = x_ref[...]
    d = x.shape[1]
    y = x * lax.rsqrt(jnp.mean(x * x, axis=-1, keepdims=True) + EPS) * g_ref[...]
    is_ctx = pl.program_id(0) < ctx_tiles
    sh = jnp.where(is_ctx, mod_ref[1:2, 0:d], mod_ref[0:1, 0:d])
    sc = jnp.where(is_ctx, mod_ref[1:2, d:2 * d], mod_ref[0:1, d:2 * d])
    o_ref[...] = (y * (1.0 + sc) + sh).astype(o_ref.dtype)


def norm_mod(h, g, mod, n_ctx):
    r, d = h.shape
    tm = ROW_TILE
    return pl.pallas_call(
        functools.partial(_norm_kernel, ctx_tiles=n_ctx // tm), grid=(r // tm,),
        in_specs=[pl.BlockSpec((tm, d), lambda i: (i, 0)),
                  pl.BlockSpec((1, d), lambda i: (0, 0)),
                  pl.BlockSpec(mod.shape, lambda i: (0, 0))],
        out_specs=pl.BlockSpec((tm, d), lambda i: (i, 0)),
        out_shape=jax.ShapeDtypeStruct((r, d), BF16),
        compiler_params=_params("parallel"), name="norm_mod",
    )(h, g.reshape(1, d), mod)


def _mm_kernel(a_ref, b_ref, o_ref):
    o_ref[...] = _dot(a_ref[...], b_ref[...]).astype(o_ref.dtype)


def _pick_tile(n, cap, unit=LANE):
    best = unit
    for t in range(unit, cap + 1, unit):
        if n % t == 0:
            best = t
    return best


def matmul(a, b, out_dtype):
    m, k = a.shape
    n = b.shape[1]
    tm = _pick_tile(m, 1536, 8)
    tn = _pick_tile(n, 1024)
    return pl.pallas_call(
        _mm_kernel, grid=(m // tm, n // tn),
        in_specs=[pl.BlockSpec((tm, k), lambda i, j: (i, 0)),
                  pl.BlockSpec((k, tn), lambda i, j: (0, j))],
        out_specs=pl.BlockSpec((tm, tn), lambda i, j: (i, j)),
        out_shape=jax.ShapeDtypeStruct((m, n), out_dtype),
        compiler_params=_params("parallel", "parallel"), name="in_proj",
    )(a, b)


def _normmm_kernel(x_ref, g_ref, w_ref, o_ref):
    x = x_ref[...].astype(F32)
    y = x * lax.rsqrt(jnp.mean(x * x, axis=-1, keepdims=True) + EPS) * g_ref[...]
    o_ref[...] = _dot(y.astype(BF16), w_ref[...]).astype(o_ref.dtype)


def norm_matmul(x, col_off, g, w):
    r = x.shape[0]
    k, n = w.shape
    tm = _pick_tile(r, 768, 8)
    cb = col_off // k
    return pl.pallas_call(
        _normmm_kernel, grid=(r // tm,),
        in_specs=[pl.BlockSpec((tm, k), lambda i: (i, cb)),
                  pl.BlockSpec((1, k), lambda i: (0, 0)),
                  pl.BlockSpec((k, n), lambda i: (0, 0))],
        out_specs=pl.BlockSpec((tm, n), lambda i: (i, 0)),
        out_shape=jax.ShapeDtypeStruct((r, n), F32),
        compiler_params=_params("parallel"), name="norm_matmul",
    )(x, g.reshape(1, k), w)


def _prep_kernel(*refs, mode, n_real, scale):
    x_ref, g_ref = refs[0], refs[1]
    o_ref = refs[-1]
    x = x_ref[...].astype(F32)
    ms = jnp.sum(x * x, axis=-1, keepdims=True) * (1.0 / n_real)
    xn = x * lax.rsqrt(ms + EPS) * g_ref[...]
    if mode == "none":
        y = xn
    elif mode == "half64":
        y = xn * refs[2][...] + pltpu.roll(xn, 64, 1) * refs[3][...]
    else:
        y = xn * refs[2][...] + pltpu.roll(xn, 96, 1) * refs[3][...] + pltpu.roll(xn, 32, 1) * refs[4][...]
    o_ref[...] = (y * scale).astype(o_ref.dtype)


def head_prep(x, col_off, n_heads, g, tables, mode, n_real, scale):
    r = x.shape[0]
    tm = _pick_tile(r, 1536, 8)
    cb = col_off // LANE
    tab_spec = pl.BlockSpec((tm, LANE), lambda i, h: (i, 0))
    return pl.pallas_call(
        functools.partial(_prep_kernel, mode=mode, n_real=n_real, scale=scale),
        grid=(r // tm, n_heads),
        in_specs=[pl.BlockSpec((tm, LANE), lambda i, h: (i, cb + h)),
                  pl.BlockSpec((1, LANE), lambda i, h: (0, 0))] + [tab_spec] * len(tables),
        out_specs=pl.BlockSpec((tm, LANE), lambda i, h: (i, h)),
        out_shape=jax.ShapeDtypeStruct((r, n_heads * LANE), BF16),
        compiler_params=_params("parallel", "parallel"), name="head_prep_" + mode,
    )(x, g.reshape(1, LANE), *tables)


def _flash_kernel(*refs, groups, dk, dv, tq, tk, n_kv, window, ctx_blocks, n_lat, has_qk2, has_sink):
    q_ref, k_ref, v_ref = refs[0], refs[1], refs[2]
    pos = 3
    if has_qk2:
        q2_ref, k2_ref = refs[3], refs[4]
        pos = 5
    if has_sink:
        sink_ref = refs[pos]
        pos += 1
    o_ref, m_ref, l_ref, acc_ref = refs[pos:pos + 4]
    i, j = pl.program_id(1), pl.program_id(2)

    @pl.when(j == 0)
    def _():
        m_ref[...] = jnp.full(m_ref.shape, NEG_INF, F32)
        l_ref[...] = jnp.zeros(l_ref.shape, F32)
        acc_ref[...] = jnp.zeros(acc_ref.shape, F32)

    k = k_ref[...]
    v = v_ref[...].astype(BF16)
    if window:
        qpos = i * tq + lax.broadcasted_iota(jnp.int32, (tq, 1), 0)
        kpos = (i + j - ctx_blocks - 1) * tk + lax.broadcasted_iota(jnp.int32, (1, tk), 1)
        valid = (jnp.abs(qpos - kpos) <= WINDOW) & (kpos >= 0) & (kpos < n_lat)
        valid = valid | (j < ctx_blocks)
    contract_last = (((1,), (1,)), ((), ()))
    for g in range(groups):
        s = lax.dot_general(q_ref[:, g * dk:(g + 1) * dk], k, contract_last, preferred_element_type=F32)
        if has_qk2:
            s = s + lax.dot_general(q2_ref[:, g * LANE:(g + 1) * LANE], k2_ref[...], contract_last,
                                    preferred_element_type=F32)
        if window:
            s = jnp.where(valid, s, NEG_INF)
        m_prev = m_ref[g]
        m_new = jnp.maximum(m_prev, jnp.max(s, axis=-1, keepdims=True))
        alpha = jnp.exp(m_prev - m_new)
        p = jnp.exp(s - m_new)
        l_ref[g] = alpha * l_ref[g] + jnp.sum(p, axis=-1, keepdims=True)
        acc_ref[g] = alpha * acc_ref[g] + _dot(p.astype(BF16), v)
        m_ref[g] = m_new

    @pl.when(j == n_kv - 1)
    def _():
        for g in range(groups):
            m, l, acc = m_ref[g], l_ref[g], acc_ref[g]
            if has_sink:
                sk = sink_ref[0, g:g + 1, 0:1]
                m_fin = jnp.maximum(m, sk)
                a = jnp.exp(m - m_fin)
                l = l * a + jnp.exp(sk - m_fin)
                acc = acc * a
            o_ref[:, g * dv:(g + 1) * dv] = (acc / l).astype(o_ref.dtype)


def flash(q, k, v, *, kv_heads, groups, n_q, q_row0, n_kv_rows, tq, tk, v_col_off=0, dv=HEAD_DIM,
          q2=None, k2=None, sink=None, window=False, ctx_rows=0):
    dk = q.shape[1] // (kv_heads * groups)
    qb0 = q_row0 // tq
    vb0 = v_col_off // dv
    if window:
        ctx_blocks = ctx_rows // tk
        n_lat_blocks = (n_kv_rows - ctx_rows) // tk
        n_kv = ctx_blocks + 3

        def kv_row(i, j):
            return jnp.where(j < ctx_blocks, j, ctx_blocks + jnp.clip(i + j - ctx_blocks - 1, 0, n_lat_blocks - 1))
    else:
        ctx_blocks = 0
        n_kv = n_kv_rows // tk

        def kv_row(i, j):
            return j
    in_specs = [pl.BlockSpec((tq, groups * dk), lambda h, i, j: (qb0 + i, h)),
                pl.BlockSpec((tk, dk), lambda h, i, j: (kv_row(i, j), h)),
                pl.BlockSpec((tk, dv), lambda h, i, j: (kv_row(i, j), vb0 + h))]
    args = [q, k, v]
    if q2 is not None:
        in_specs += [pl.BlockSpec((tq, groups * LANE), lambda h, i, j: (qb0 + i, h)),
                     pl.BlockSpec((tk, LANE), lambda h, i, j: (kv_row(i, j), 0))]
        args += [q2, k2]
    if sink is not None:
        in_specs.append(pl.BlockSpec((1, groups, LANE), lambda h, i, j: (h, 0, 0)))
        args.append(jnp.broadcast_to(sink.astype(F32).reshape(kv_heads, groups, 1), (kv_heads, groups, LANE)))
    kern = functools.partial(
        _flash_kernel, groups=groups, dk=dk, dv=dv, tq=tq, tk=tk, n_kv=n_kv, window=window, ctx_blocks=ctx_blocks,
        n_lat=n_kv_rows - ctx_rows, has_qk2=q2 is not None, has_sink=sink is not None)
    return pl.pallas_call(
        kern, grid=(kv_heads, n_q // tq, n_kv), in_specs=in_specs,
        out_specs=pl.BlockSpec((tq, groups * dv), lambda h, i, j: (i, h)),
        out_shape=jax.ShapeDtypeStruct((n_q, kv_heads * groups * dv), BF16),
        scratch_shapes=[pltpu.VMEM((groups, tq, 1), F32), pltpu.VMEM((groups, tq, 1), F32),
                        pltpu.VMEM((groups, tq, dv), F32)],
        compiler_params=_params("parallel", "parallel", "arbitrary"),
        name="flash_window" if window else "flash_dense",
    )(*args)


def _outproj_kernel(o1c_ref, o1l_ref, o2c_ref, o2l_ref, gate_ref, h_ref, mod_ref, w_ref, out_ref, *, ctx_tiles):
    is_ctx = pl.program_id(0) < ctx_tiles
    w1 = o1l_ref.shape[1]
    d = h_ref.shape[1]
    gt = gate_ref[...].astype(F32)
    sg = gt * jax.nn.sigmoid(gt)
    o1 = jnp.where(is_ctx, o1c_ref[...], o1l_ref[...]).astype(F32)
    o2 = jnp.where(is_ctx, o2c_ref[...], o2l_ref[...]).astype(F32)
    t1 = (o1 * sg[:, :w1]).astype(BF16)
    t2 = (o2 * sg[:, w1:]).astype(BF16)
    y = _dot(t1, w_ref[0:w1, :]) + _dot(t2, w_ref[w1:, :])
    gm = jnp.where(is_ctx, mod_ref[1:2, 2 * d:3 * d], mod_ref[0:1, 2 * d:3 * d])
    out_ref[...] = h_ref[...] + gm * y


def out_proj(o1c, o1l, o2c, o2l, proj, h, mod, w):
    r, d = h.shape
    tm = ROW_TILE
    n_ctx = o1c.shape[0]
    ct = n_ctx // tm
    w1, w2 = o1l.shape[1], o2l.shape[1]
    lat = lambda i: (jnp.maximum(i - ct, 0), 0)
    return pl.pallas_call(
        functools.partial(_outproj_kernel, ctx_tiles=ct), grid=(r // tm,),
        in_specs=[pl.BlockSpec((tm, w1), lambda i: (0, 0)), pl.BlockSpec((tm, w1), lat),
                  pl.BlockSpec((tm, w2), lambda i: (0, 0)), pl.BlockSpec((tm, w2), lat),
                  pl.BlockSpec((tm, w1 + w2), lambda i: (i, OFF_GATE // (w1 + w2))),
                  pl.BlockSpec((tm, d), lambda i: (i, 0)),
                  pl.BlockSpec(mod.shape, lambda i: (0, 0)),
                  pl.BlockSpec(w.shape, lambda i: (0, 0))],
        out_specs=pl.BlockSpec((tm, d), lambda i: (i, 0)),
        out_shape=jax.ShapeDtypeStruct((r, d), F32),
        compiler_params=_params("parallel"), name="out_proj",
    )(o1c, o1l, o2c, o2l, proj, h, mod, w)


def _conv3_kernel(x_ref, w_ref, b_ref, oc_ref, ol_ref, *, n_ctx):
    x = x_ref[...].astype(F32)
    r = x.shape[0]
    row = lax.broadcasted_iota(jnp.int32, (r, 1), 0)
    prev = jnp.where((row == 0) | (row == n_ctx), 0.0, pltpu.roll(x, 1, 0))
    nxt = jnp.where((row == n_ctx - 1) | (row == r - 1), 0.0, pltpu.roll(x, r - 1, 0))
    y = prev * w_ref[0:1, :] + x * w_ref[1:2, :] + nxt * w_ref[2:3, :] + b_ref[...]
    oc_ref[0] = y[:n_ctx]
    ol_ref[0] = y[n_ctx:]


def conv3(proj, col_off, w, b, n_ctx):
    r = proj.shape[0]
    cw = w.shape[1] // 3
    nb = cw // LANE
    cb = col_off // LANE
    return pl.pallas_call(
        functools.partial(_conv3_kernel, n_ctx=n_ctx), grid=(3, nb),
        in_specs=[pl.BlockSpec((r, LANE), lambda p, c: (0, cb + p * nb + c)),
                  pl.BlockSpec((3, LANE), lambda p, c: (0, p * nb + c)),
                  pl.BlockSpec((1, LANE), lambda p, c: (0, p * nb + c))],
        out_specs=[pl.BlockSpec((1, n_ctx, LANE), lambda p, c: (p, 0, c)),
                   pl.BlockSpec((1, r - n_ctx, LANE), lambda p, c: (p, 0, c))],
        out_shape=[jax.ShapeDtypeStruct((3, n_ctx, cw), F32), jax.ShapeDtypeStruct((3, r - n_ctx, cw), F32)],
        compiler_params=_params("parallel", "parallel"), name="conv3",
    )(proj, w, b.reshape(1, -1))


def _filt_kernel(z_ref, w1_ref, b1_ref, f1_ref, w2_ref, b2_ref, f2_ref, w3_ref, dl_ref, h_ref, nrm_ref, *, tm, cw):
    i = pl.program_id(0)
    z = z_ref[...]

    def mm(a, w_r):
        return _dot3(*_split_bf16(a), *_split_bf16(w_r[...]))

    a = jnp.sin(f1_ref[...] * (mm(z, w1_ref) + b1_ref[...]))
    a = jnp.sin(f2_ref[...] * (mm(a, w2_ref) + b2_ref[...]))
    dec = jnp.exp(-z[:, 0:1] * dl_ref[...])
    h = mm(a, w3_ref) * jnp.concatenate([dec] * (h_ref.shape[1] // cw), axis=1)
    h_ref[...] = h
    row = i * tm + lax.broadcasted_iota(jnp.int32, (tm, 1), 0)
    col = lax.broadcasted_iota(jnp.int32, (1, h.shape[1]), 1)
    skip = (row == 0) & ((col // cw) % 2 == 1)
    part = jnp.sum(jnp.where(skip, 0.0, jnp.abs(h)), axis=0, keepdims=True)

    @pl.when(i == 0)
    def _():
        nrm_ref[...] = jnp.zeros(nrm_ref.shape, F32)

    nrm_ref[...] += jnp.broadcast_to(part, nrm_ref.shape)


def hyena_filter(z, w1p, b1, f1, w2, b2, f2, w3, deltas):
    l = z.shape[0]
    n = w3.shape[1]
    cw = deltas.shape[0]
    hid = w2.shape[0]
    tm = min(l, 256)
    full = lambda a: pl.BlockSpec(a.shape, lambda i: (0, 0))
    ins = [w1p, b1.reshape(1, hid), f1.reshape(1, hid), w2, b2.reshape(1, hid), f2.reshape(1, hid), w3,
           deltas.reshape(1, cw)]
    return pl.pallas_call(
        functools.partial(_filt_kernel, tm=tm, cw=cw), grid=(l // tm,),
        in_specs=[pl.BlockSpec((tm, LANE), lambda i: (i, 0))] + [full(a) for a in ins],
        out_specs=[pl.BlockSpec((tm, n), lambda i: (i, 0)), pl.BlockSpec((8, n), lambda i: (0, 0))],
        out_shape=[jax.ShapeDtypeStruct((l, n), F32), jax.ShapeDtypeStruct((8, n), F32)],
        compiler_params=_params("arbitrary"), name="hyena_filter",
    )(z, *ins)


@functools.lru_cache(maxsize=None)
def _dft_consts(a):
    n = a * a
    hh = a // 2
    n1 = np.arange(hh)
    k1 = np.arange(hh)
    n2 = np.arange(a)
    k2 = np.arange(a)
    ang1 = 2 * np.pi * np.outer(k1 + 0.5, n1) / a
    m1 = np.zeros((hh, 2, hh))
    m1[:, 0], m1[:, 1] = np.cos(ang1), -np.sin(ang1)
    m1 = m1.reshape(2 * hh, hh)
    phi = 2 * np.pi * (n2[None, None, :] * (k1[:, None, None] + 0.5) / n + n2[None, None, :] * k2[None, :, None] / a)
    c, s = np.cos(phi), np.sin(phi)
    g = np.zeros((hh, 2, a, 2, a))
    g[:, 0, :, 0, :], g[:, 0, :, 1, :], g[:, 1, :, 0, :], g[:, 1, :, 1, :] = c, s, -s, c
    g = g.reshape(hh, 2 * a, 2 * a)
    gt = np.ascontiguousarray(np.transpose(g, (0, 2, 1)))
    al = 2 * np.pi * np.outer(n1, k1 + 0.5) / a
    mb = np.zeros((hh, hh, 2))
    mb[:, :, 0], mb[:, :, 1] = 2 / n * np.cos(al), -2 / n * np.sin(al)
    mb = mb.reshape(hh, 2 * hh)
    return tuple(_np_split(x) for x in (m1, g, gt, mb))


@functools.lru_cache(maxsize=None)
def _dense_dft_consts(l):
    n = 2 * l
    ang = 2 * np.pi * np.outer(np.arange(l) + 0.5, np.arange(l)) / n
    mf = np.concatenate([np.cos(ang), -np.sin(ang)], axis=0)
    mi = np.concatenate([2 / n * np.cos(ang.T), -2 / n * np.sin(ang.T)], axis=1)
    return _np_split(mf), _np_split(mi)


def _np_split(x):
    hi = x.astype(ml_dtypes.bfloat16)
    lo = (x - hi.astype(np.float64)).astype(ml_dtypes.bfloat16)
    return hi, lo


def _stage1_kernel(x_ref, mh_ref, ml_ref, o_ref):
    o_ref[...] = _dot3(mh_ref[...], ml_ref[...], *_split_bf16(x_ref[0]))


def dft_stage1(x3, p, m1):
    _, hh, w = x3.shape
    tn = _pick_tile(w, 8192)
    mh, ml = m1
    return pl.pallas_call(
        _stage1_kernel, grid=(w // tn,),
        in_specs=[pl.BlockSpec((1, hh, tn), lambda j: (p, 0, j)),
                  pl.BlockSpec(mh.shape, lambda j: (0, 0)), pl.BlockSpec(ml.shape, lambda j: (0, 0))],
        out_specs=pl.BlockSpec((2 * hh, tn), lambda j: (0, j)),
        out_shape=jax.ShapeDtypeStruct((2 * hh, w), F32),
        compiler_params=_params("parallel"), name="dft_stage1",
    )(x3, jnp.asarray(mh), jnp.asarray(ml))


def _spectrum_product(x, h0, h1, tap0, n0, n1):
    half = x.shape[0] // 2
    inv = 1.0 / (n0 + n1)
    kr = (h0[:half] + h1[:half] - tap0) * inv
    ki = (h0[half:] - h1[half:]) * inv
    xr, xi = x[:half], x[half:]
    return jnp.concatenate([xr * kr - xi * ki, xr * ki + xi * kr], axis=0)


def _mid_kernel(*refs, conv):
    y_ref, gh_ref, gl_ref = refs[0], refs[1], refs[2]
    o_ref = refs[-1]
    x = _dot3(gh_ref[0], gl_ref[0], *_split_bf16(y_ref[0]))
    if conv:
        th_ref, tl_ref, h0_ref, h1_ref, tap_ref, n0_ref, n1_ref = refs[3:10]
        z = _spectrum_product(x, h0_ref[0], h1_ref[0], tap_ref[0:1, :], n0_ref[0:1, :], n1_ref[0:1, :])
        x = _dot3(th_ref[0], tl_ref[0], *_split_bf16(z))
    o_ref[0] = x


def dft_mid(y, g, gt=None, filt=None, order=0):
    hh, a2, c = y.shape
    slab = pl.BlockSpec((1, a2, c), lambda k: (k, 0, 0))
    mat = pl.BlockSpec((1, a2, a2), lambda k: (k, 0, 0))
    in_specs = [slab, mat, mat]
    args = [y, jnp.asarray(g[0]), jnp.asarray(g[1])]
    if filt is not None:
        hf, taps, nrm = filt
        s0, s1 = 2 * order, 2 * order + 1
        in_specs += [mat, mat,
                     pl.BlockSpec((1, a2, c), lambda k: (k, 0, s0)), pl.BlockSpec((1, a2, c), lambda k: (k, 0, s1)),
                     pl.BlockSpec((8, c), lambda k: (0, s1)),
                     pl.BlockSpec((8, c), lambda k: (0, s0)), pl.BlockSpec((8, c), lambda k: (0, s1))]
        args += [jnp.asarray(gt[0]), jnp.asarray(gt[1]), hf, hf, taps, nrm, nrm]
    return pl.pallas_call(
        functools.partial(_mid_kernel, conv=filt is not None), grid=(hh,),
        in_specs=in_specs, out_specs=slab, out_shape=jax.ShapeDtypeStruct((hh, a2, c), F32),
        compiler_params=_params("parallel"), name="dft_mid_conv" if filt is not None else "dft_mid_fwd",
    )(*args)


def _last_kernel(b_ref, mh_ref, ml_ref, xa_ref, zb_ref, bias_ref, o_ref):
    conv = _dot3(mh_ref[...], ml_ref[...], *_split_bf16(b_ref[...]))
    z = zb_ref[0]
    o_ref[0] = (xa_ref[0] * (conv + bias_ref[...] * z)).astype(o_ref.dtype)


def dft_last(b, mb, xa3, pa, zb3, pb, bias_row, out_dtype):
    rows, w = b.shape
    hh = rows // 2
    tn = bias_row.shape[1]
    mh, ml = mb
    return pl.pallas_call(
        _last_kernel, grid=(w // tn,),
        in_specs=[pl.BlockSpec((rows, tn), lambda j: (0, j)),
                  pl.BlockSpec(mh.shape, lambda j: (0, 0)), pl.BlockSpec(ml.shape, lambda j: (0, 0)),
                  pl.BlockSpec((1, hh, tn), lambda j: (pa, 0, j)),
                  pl.BlockSpec((1, hh, tn), lambda j: (pb, 0, j)),
                  pl.BlockSpec((1, tn), lambda j: (0, 0))],
        out_specs=pl.BlockSpec((1, hh, tn), lambda j: (0, 0, j)),
        out_shape=jax.ShapeDtypeStruct((1, hh, w), out_dtype),
        compiler_params=_params("parallel"), name="dft_last",
    )(b, jnp.asarray(mh), jnp.asarray(ml), xa3, zb3, bias_row)


def long_conv_gated(u3, hf, taps, nrm, bias, consts, a):
    m1, g, gt, mb = consts
    _, l, c = u3.shape
    hh = a // 2
    u3v = u3.reshape(3, hh, a * c)
    tn = _pick_tile(a * c, 8192)
    filt = (hf, taps, nrm)

    def conv_gate(src3, p_src, xa_p, order, out_dtype):
        y = dft_stage1(src3, p_src, m1).reshape(hh, 2 * a, c)
        bb = dft_mid(y, g, gt, filt, order).reshape(2 * hh, a * c)
        bias_row = jnp.tile(bias[order].reshape(1, c), (1, tn // c))
        return dft_last(bb, mb, u3v, xa_p, src3, p_src, bias_row, out_dtype)

    z2 = conv_gate(u3v, 2, 0, 0, F32)
    return conv_gate(z2, 0, 1, 1, BF16).reshape(l, c)


def filter_spectrum(h, consts, a):
    m1, g, _, _ = consts
    l, n = h.shape
    hh = a // 2
    y = dft_stage1(h.reshape(1, hh, a * n), 0, m1).reshape(hh, 2 * a, n)
    return dft_mid_wide(y, g, n // 4)


def dft_mid_wide(y, g, c):
    hh, a2, n = y.shape
    return pl.pallas_call(
        functools.partial(_mid_kernel, conv=False), grid=(hh, n // c),
        in_specs=[pl.BlockSpec((1, a2, c), lambda k, q: (k, 0, q)),
                  pl.BlockSpec((1, a2, a2), lambda k, q: (k, 0, 0)), pl.BlockSpec((1, a2, a2), lambda k, q: (k, 0, 0))],
        out_specs=pl.BlockSpec((1, a2, c), lambda k, q: (k, 0, q)),
        out_shape=jax.ShapeDtypeStruct((hh, a2, n), F32),
        compiler_params=_params("parallel", "parallel"), name="dft_mid_fwd",
    )(y, jnp.asarray(g[0]), jnp.asarray(g[1]))


def _ctx_spec_kernel(h_ref, mh_ref, ml_ref, o_ref):
    o_ref[...] = _dot3(mh_ref[...], ml_ref[...], *_split_bf16(h_ref[...]))


def ctx_filter_spectrum(h, mf):
    l, n = h.shape
    tn = _pick_tile(n, 1024)
    mh, ml = mf
    return pl.pallas_call(
        _ctx_spec_kernel, grid=(n // tn,),
        in_specs=[pl.BlockSpec((l, tn), lambda j: (0, j)),
                  pl.BlockSpec(mh.shape, lambda j: (0, 0)), pl.BlockSpec(ml.shape, lambda j: (0, 0))],
        out_specs=pl.BlockSpec((2 * l, tn), lambda j: (0, j)),
        out_shape=jax.ShapeDtypeStruct((2 * l, n), F32),
        compiler_params=_params("parallel"), name="ctx_filter_spectrum",
    )(h, jnp.asarray(mh), jnp.asarray(ml))


def _ctx_hyena_kernel(u_ref, fh_ref, fl_ref, ih_ref, il_ref, hf0a, hf1a, hf0b, hf1b, tap0, tap1, n0a, n1a, n0b, n1b,
                      bias_ref, o_ref):
    x1, x2, z = u_ref[0], u_ref[1], u_ref[2]

    def conv(sig, h0_ref, h1_ref, tap_ref, n0_ref, n1_ref):
        x = _dot3(fh_ref[...], fl_ref[...], *_split_bf16(sig))
        zz = _spectrum_product(x, h0_ref[...], h1_ref[...], tap_ref[0:1, :], n0_ref[0:1, :], n1_ref[0:1, :])
        return _dot3(ih_ref[...], il_ref[...], *_split_bf16(zz))

    z2 = x1 * (conv(z, hf0a, hf1a, tap0, n0a, n1a) + bias_ref[0:1, :] * z)
    o_ref[...] = (x2 * (conv(z2, hf0b, hf1b, tap1, n0b, n1b) + bias_ref[1:2, :] * z2)).astype(o_ref.dtype)


def ctx_hyena(u3, hf, taps, nrm, bias, mf, mi):
    _, l, c = u3.shape
    tc = _pick_tile(c, 512)
    nb = c // tc
    const = lambda a: pl.BlockSpec(a.shape, lambda j: (0, 0))
    col = lambda rows, q: pl.BlockSpec((rows, tc), lambda j: (0, q * nb + j))
    mats = [jnp.asarray(m) for m in (*mf, *mi)]
    return pl.pallas_call(
        _ctx_hyena_kernel, grid=(nb,),
        in_specs=[pl.BlockSpec((3, l, tc), lambda j: (0, 0, j))] + [const(m) for m in mats]
        + [col(2 * l, 0), col(2 * l, 1), col(2 * l, 2), col(2 * l, 3), col(8, 1), col(8, 3),
           col(8, 0), col(8, 1), col(8, 2), col(8, 3), pl.BlockSpec((2, tc), lambda j: (0, j))],
        out_specs=pl.BlockSpec((l, tc), lambda j: (0, j)),
        out_shape=jax.ShapeDtypeStruct((l, c), BF16),
        compiler_params=_params("parallel"), name="ctx_hyena",
    )(u3, *mats, hf, hf, hf, hf, taps, taps, nrm, nrm, nrm, nrm, bias)


def _axial_tables(n_tokens, n_rot):
    rows = n_tokens // GRID_W
    row = jnp.broadcast_to(jnp.arange(rows)[:, None], (rows, GRID_W)).reshape(-1).astype(F32)
    col = jnp.broadcast_to(jnp.arange(GRID_W)[None, :], (rows, GRID_W)).reshape(-1).astype(F32)
    n_freq = n_rot // 4
    inv = ROPE_THETA ** (-jnp.arange(n_freq, dtype=F32) / n_freq)
    ang = jnp.concatenate([row[:, None] * inv, col[:, None] * inv], axis=-1)
    return jnp.cos(ang), jnp.sin(ang)


def _rope_tables_full(s, n_ctx):
    cos, sin = _axial_tables(s, HEAD_DIM)
    t0 = jnp.concatenate([jnp.ones((n_ctx, LANE), F32), jnp.concatenate([cos, cos], axis=1)], axis=0)
    t1 = jnp.concatenate([jnp.zeros((n_ctx, LANE), F32), jnp.concatenate([-sin, sin], axis=1)], axis=0)
    return t0, t1


def _rope_tables_half(s, n_ctx):
    cos, sin = _axial_tables(s, ROPE_DIM)
    q = ROPE_DIM // 2
    zq = jnp.zeros((s, q), F32)
    z2 = jnp.zeros((s, LANE - ROPE_DIM), F32)
    t0 = jnp.concatenate([cos, cos, z2], axis=1)
    t1 = jnp.concatenate([-sin, zq, z2], axis=1)
    t2 = jnp.concatenate([zq, sin, z2], axis=1)
    c0 = jnp.concatenate([jnp.ones((n_ctx, ROPE_DIM), F32), jnp.zeros((n_ctx, LANE - ROPE_DIM), F32)], axis=1)
    cz = jnp.zeros((n_ctx, LANE), F32)
    return jnp.concatenate([c0, t0], axis=0), jnp.concatenate([cz, t1], axis=0), jnp.concatenate([cz, t2], axis=0)


def _filter_features(l):
    pos = jnp.arange(l, dtype=F32)
    t = pos / max(l - 1, 1)
    bands = jnp.linspace(1e-4, HYENA_BANDS - 1, HYENA_BANDS, dtype=F32)
    ang = (2.0 * math.pi / l) * pos[:, None] * bands[None, :]
    z = jnp.concatenate([t[:, None], jnp.cos(ang), -jnp.sin(ang)], axis=-1)
    return jnp.pad(z, ((0, 0), (0, LANE - HYENA_EMB)))


def _attention_pair(qp, kp, proj, v_off, s, n_ctx, *, kv_heads, groups, window, sink=None, q2=None, k2=None,
                    v_arr=None):
    v = proj if v_arr is None else v_arr
    common = dict(kv_heads=kv_heads, groups=groups, v_col_off=v_off, q2=q2, k2=k2, sink=sink)
    o_ctx = flash(qp, kp, v, n_q=n_ctx, q_row0=0, n_kv_rows=n_ctx, tq=n_ctx, tk=n_ctx, **common)
    if window:
        o_lat = flash(qp, kp, v, n_q=s, q_row0=n_ctx, n_kv_rows=n_ctx + s, tq=256, tk=256, window=True,
                      ctx_rows=n_ctx, **common)
    else:
        o_lat = flash(qp, kp, v, n_q=s, q_row0=n_ctx, n_kv_rows=n_ctx + s, tq=256, tk=768, **common)
    return o_ctx, o_lat


def _even_layer(u, s, n_ctx, w_in, qn_g, kn_g, sink, conv_w, conv_b, fparams, hy_bias, rope_full, feats):
    wq, wk, wv, whd, wg = jnp.split(w_in, np.cumsum([ATT_WIDTH, KV_WIDTH, KV_WIDTH, 3 * B_WIDTH])[:].tolist(), axis=1)
    proj = matmul(u, jnp.concatenate([wg, wq, wk, wv, whd], axis=1).astype(BF16), F32)
    qp = head_prep(proj, OFF_Q, A_HEADS, qn_g, rope_full, "half64", HEAD_DIM, HEAD_DIM ** -0.5)
    kp = head_prep(proj, OFF_K, A_KV_HEADS, kn_g, rope_full, "half64", HEAD_DIM, 1.0)
    a_ctx, a_lat = _attention_pair(qp, kp, proj, OFF_V, s, n_ctx, kv_heads=A_KV_HEADS, groups=A_HEADS // A_KV_HEADS,
                                   window=True, sink=sink)
    w1, b1, f1, w2, b2, f2, w3 = fparams
    w1p = jnp.pad(w1, ((0, LANE - HYENA_EMB), (0, 0)))
    deltas = jnp.linspace(DECAY_MAX, DECAY_MIN, B_WIDTH, dtype=F32)
    uc3, ul3 = conv3(proj, OFF_REST, conv_w, conv_b, n_ctx)
    a = int(round(math.sqrt(2 * s)))
    consts = _dft_consts(a)
    h_lat, nrm_lat = hyena_filter(feats[0], w1p, b1, f1, w2, b2, f2, w3, deltas)
    hf_lat = filter_spectrum(h_lat, consts, a)
    b_lat = long_conv_gated(ul3, hf_lat, h_lat, nrm_lat, hy_bias, consts, a)
    mf, mi = _dense_dft_consts(n_ctx)
    h_ctx, nrm_ctx = hyena_filter(feats[1], w1p, b1, f1, w2, b2, f2, w3, deltas)
    hf_ctx = ctx_filter_spectrum(h_ctx, mf)
    b_ctx = ctx_hyena(uc3, hf_ctx, h_ctx, nrm_ctx, hy_bias, mf, mi)
    return proj, a_ctx, a_lat, b_ctx, b_lat


def _odd_layer(u, s, n_ctx, w_in, qn_g, kn_g, cq_g, ckv_g, wuq, wukv, mq_g, mk_g, rope_full, rope_half):
    d = w_in.shape[0]
    parts = jnp.split(w_in, np.cumsum([ATT_WIDTH, KV_WIDTH, KV_WIDTH, Q_LORA, KV_LORA, ROPE_DIM]).tolist(), axis=1)
    wq, wk, wv, wmq, wmkv, wmkr, wg = parts
    w_perm = jnp.concatenate([wg, wq, wk, wv, wmq, wmkv, wmkr, jnp.zeros((d, LANE - ROPE_DIM), w_in.dtype)], axis=1)
    proj = matmul(u, w_perm.astype(BF16), F32)
    qp = head_prep(proj, OFF_Q, C_HEADS, qn_g, rope_full, "half64", HEAD_DIM, HEAD_DIM ** -0.5)
    kp = head_prep(proj, OFF_K, C_KV_HEADS, kn_g, rope_full, "half64", HEAD_DIM, 1.0)
    c_ctx, c_lat = _attention_pair(qp, kp, proj, OFF_V, s, n_ctx, kv_heads=C_KV_HEADS, groups=C_HEADS // C_KV_HEADS,
                                   window=False)
    pad_r = LANE - ROPE_DIM
    wuq3 = wuq.reshape(Q_LORA, M_HEADS, QK_DIM)
    wuq_p = jnp.concatenate([wuq3[:, :, :NOPE_DIM].reshape(Q_LORA, -1),
                             jnp.pad(wuq3[:, :, NOPE_DIM:], ((0, 0), (0, 0), (0, pad_r))).reshape(Q_LORA, -1)], axis=1)
    wukv3 = wukv.reshape(KV_LORA, M_HEADS, NOPE_DIM + V_DIM)
    wukv_p = jnp.concatenate([wukv3[:, :, :NOPE_DIM].reshape(KV_LORA, -1),
                              wukv3[:, :, NOPE_DIM:].reshape(KV_LORA, -1)], axis=1)
    q_raw = norm_matmul(proj, OFF_MQ, cq_g, wuq_p.astype(BF16))
    kv_raw = norm_matmul(proj, OFF_MKV, ckv_g, wukv_p.astype(BF16))
    scale = QK_DIM ** -0.5
    gq_r = jnp.pad(mq_g[NOPE_DIM:], (0, pad_r))
    gk_r = jnp.pad(mk_g[NOPE_DIM:], (0, pad_r))
    qn = head_prep(q_raw, 0, M_HEADS, mq_g[:NOPE_DIM], (), "none", NOPE_DIM, scale)
    qr = head_prep(q_raw, M_HEADS * NOPE_DIM, M_HEADS, gq_r, rope_half, "half32", ROPE_DIM, scale)
    kn = head_prep(kv_raw, 0, M_HEADS, mk_g[:NOPE_DIM], (), "none", NOPE_DIM, 1.0)
    kr = head_prep(proj, OFF_MKR, 1, gk_r, rope_half, "half32", ROPE_DIM, 1.0)
    d_ctx, d_lat = _attention_pair(qn, kn, proj, M_HEADS * NOPE_DIM, s, n_ctx, kv_heads=M_HEADS, groups=1,
                                   window=False, q2=qr, k2=kr, v_arr=kv_raw)
    return proj, c_ctx, c_lat, d_ctx, d_lat


def kernel(x, c, ctx, c_ctx, ada_w, ada_b, norm_g, w_out, ev_w_in, ev_qn_g, ev_kn_g, ev_sink, ev_conv_w, ev_conv_b, hy_w1, hy_b1, hy_f1, hy_w2, hy_b2, hy_f2, hy_w3, hy_bias, od_w_in, od_qn_g, od_kn_g, od_cq_g, od_ckv_g, od_wuq, od_wukv, od_mq_g, od_mk_g):
    _, s, d = x.shape
    n_ctx = ctx.shape[1]
    depth = ada_w.shape[0]
    h = jnp.concatenate([ctx[0], x[0]], axis=0)
    cc = jnp.zeros((8, d), F32).at[0].set(c[0]).at[1].set(c_ctx)
    mod = ada_mod(cc, ada_w, ada_b)
    rope_full = _rope_tables_full(s, n_ctx)
    rope_half = _rope_tables_half(s, n_ctx)
    feats = (_filter_features(s), _filter_features(n_ctx))
    for i in range(depth):
        u = norm_mod(h, norm_g[i], mod[i], n_ctx)
        if i % 2 == 0:
            e = i // 2
            proj, o1c, o1l, o2c, o2l = _even_layer(
                u, s, n_ctx, ev_w_in[e], ev_qn_g[e], ev_kn_g[e], ev_sink[e], ev_conv_w[e], ev_conv_b[e],
                (hy_w1[e], hy_b1[e], hy_f1[e], hy_w2[e], hy_b2[e], hy_f2[e], hy_w3[e]), hy_bias[e], rope_full, feats)
        else:
            o = i // 2
            proj, o1c, o1l, o2c, o2l = _odd_layer(
                u, s, n_ctx, od_w_in[o], od_qn_g[o], od_kn_g[o], od_cq_g[o], od_ckv_g[o], od_wuq[o], od_wukv[o],
                od_mq_g[o], od_mk_g[o], rope_full, rope_half)
        h = out_proj(o1c, o1l, o2c, o2l, proj, h, mod[i], w_out[i].astype(BF16))
    return h[n_ctx:][None]
```

```python
import functools
import math

import ml_dtypes
import numpy as np
import jax
import jax.numpy as jnp
from jax import lax
from jax.experimental import pallas as pl
from jax.experimental.pallas import tpu as pltpu

F32 = jnp.float32
BF16 = jnp.bfloat16

GRID_W = 64
HEAD_DIM = 128
ROPE_THETA = 10000.0
EPS = 1e-6
NEG_INF = -1e30
A_HEADS, A_KV_HEADS, WINDOW = 8, 2, 128
B_WIDTH = 1024
HYENA_BANDS = 16
HYENA_EMB = 1 + 2 * HYENA_BANDS
DECAY_TARGET = 1e-2
DECAY_MAX = abs(math.log(DECAY_TARGET)) / 0.3
DECAY_MIN = abs(math.log(DECAY_TARGET)) / 1.5
C_HEADS, C_KV_HEADS = 8, 2
M_HEADS, Q_LORA, KV_LORA, NOPE_DIM, ROPE_DIM, V_DIM = 8, 512, 256, 128, 64, 128
QK_DIM = NOPE_DIM + ROPE_DIM
ATT_WIDTH = A_HEADS * HEAD_DIM
KV_WIDTH = A_KV_HEADS * HEAD_DIM
BRANCH = ATT_WIDTH + B_WIDTH

LOG2E = math.log2(math.e)
LANE = 128
ROW_TILE = 256
VMEM_LIMIT = 48 * 1024 * 1024

OFF_GATE, OFF_Q, OFF_K, OFF_V, OFF_REST = 0, BRANCH, BRANCH + ATT_WIDTH, BRANCH + ATT_WIDTH + KV_WIDTH, BRANCH + ATT_WIDTH + 2 * KV_WIDTH
OFF_MQ, OFF_MKV, OFF_MKR = OFF_REST, OFF_REST + Q_LORA, OFF_REST + Q_LORA + KV_LORA


def _params(*sem):
    return pltpu.CompilerParams(dimension_semantics=sem, vmem_limit_bytes=VMEM_LIMIT)


def _split_bf16(x):
    hi = x.astype(BF16)
    return hi, (x - hi.astype(F32)).astype(BF16)


def _dot(a, b):
    return jnp.dot(a, b, preferred_element_type=F32)


def _dot3(a_hi, a_lo, b_hi, b_lo):
    return _dot(a_hi, b_hi) + (_dot(a_hi, b_lo) + _dot(a_lo, b_hi))


def _ada_kernel(c_ref, w_ref, b_ref, o_ref):
    c = c_ref[...]
    a = (c * jax.nn.sigmoid(c)).astype(BF16)
    o_ref[0] = _dot(a, w_ref[0].astype(BF16)) + b_ref[0]


def ada_mod(cc, ada_w, ada_b):
    depth, d, n = ada_w.shape
    tn = _pick_tile(n, 1024)
    return pl.pallas_call(
        _ada_kernel, grid=(depth, n // tn),
        in_specs=[pl.BlockSpec((8, d), lambda l, j: (0, 0)),
                  pl.BlockSpec((1, d, tn), lambda l, j: (l, 0, j)),
                  pl.BlockSpec((1, 1, tn), lambda l, j: (l, 0, j))],
        out_specs=pl.BlockSpec((1, 8, tn), lambda l, j: (l, 0, j)),
        out_shape=jax.ShapeDtypeStruct((depth, 8, n), F32),
        compiler_params=_params("parallel", "parallel"), name="ada_mod",
    )(cc, ada_w, ada_b.reshape(depth, 1, n))


def _norm_kernel(x_ref, g_ref, mod_ref, o_ref, *, lat_tiles):
    x = x_ref[...]
    d = x.shape[1]
    y = x * lax.rsqrt(jnp.mean(x * x, axis=-1, keepdims=True) + EPS) * g_ref[...]
    is_ctx = pl.program_id(0) >= lat_tiles
    sh = jnp.where(is_ctx, mod_ref[1:2, 0:d], mod_ref[0:1, 0:d])
    sc = jnp.where(is_ctx, mod_ref[1:2, d:2 * d], mod_ref[0:1, d:2 * d])
    o_ref[...] = (y * (1.0 + sc) + sh).astype(o_ref.dtype)


def norm_mod(h, g, mod, n_ctx):
    r, d = h.shape
    tm = ROW_TILE
    return pl.pallas_call(
        functools.partial(_norm_kernel, lat_tiles=(r - n_ctx) // tm), grid=(r // tm,),
        in_specs=[pl.BlockSpec((tm, d), lambda i: (i, 0)),
                  pl.BlockSpec((1, d), lambda i: (0, 0)),
                  pl.BlockSpec(mod.shape, lambda i: (0, 0))],
        out_specs=pl.BlockSpec((tm, d), lambda i: (i, 0)),
        out_shape=jax.ShapeDtypeStruct((r, d), BF16),
        compiler_params=_params("parallel"), name="norm_mod",
    )(h, g.reshape(1, d), mod)


def _mm_kernel(a_ref, b_ref, o_ref):
    o_ref[...] = _dot(a_ref[...], b_ref[...]).astype(o_ref.dtype)


def _pick_tile(n, cap, unit=LANE):
    best = unit
    for t in range(unit, cap + 1, unit):
        if n % t == 0:
            best = t
    return best


def matmul(a, b, out_dtype):
    m, k = a.shape
    n = b.shape[1]
    tm = _pick_tile(m, 1536, 8)
    tn = _pick_tile(n, 1024)
    return pl.pallas_call(
        _mm_kernel, grid=(m // tm, n // tn),
        in_specs=[pl.BlockSpec((tm, k), lambda i, j: (i, 0)),
                  pl.BlockSpec((k, tn), lambda i, j: (0, j))],
        out_specs=pl.BlockSpec((tm, tn), lambda i, j: (i, j)),
        out_shape=jax.ShapeDtypeStruct((m, n), out_dtype),
        compiler_params=_params("parallel", "parallel"), name="in_proj",
    )(a, b)


def _normmm_kernel(x_ref, g_ref, w_ref, o_ref):
    x = x_ref[...].astype(F32)
    y = x * lax.rsqrt(jnp.mean(x * x, axis=-1, keepdims=True) + EPS) * g_ref[...]
    o_ref[...] = _dot(y.astype(BF16), w_ref[...]).astype(o_ref.dtype)


def norm_matmul(x, col_off, g, w):
    r = x.shape[0]
    k, n = w.shape
    tm = _pick_tile(r, 768, 8)
    cb = col_off // k
    return pl.pallas_call(
        _normmm_kernel, grid=(r // tm,),
        in_specs=[pl.BlockSpec((tm, k), lambda i: (i, cb)),
                  pl.BlockSpec((1, k), lambda i: (0, 0)),
                  pl.BlockSpec((k, n), lambda i: (0, 0))],
        out_specs=pl.BlockSpec((tm, n), lambda i: (i, 0)),
        out_shape=jax.ShapeDtypeStruct((r, n), F32),
        compiler_params=_params("parallel"), name="norm_matmul",
    )(x, g.reshape(1, k), w)


def _prep_kernel(*refs, mode, n_real, scale):
    x_ref, g_ref = refs[0], refs[1]
    o_ref = refs[-1]
    x = x_ref[...].astype(F32)
    ms = jnp.sum(x * x, axis=-1, keepdims=True) * (1.0 / n_real)
    xn = x * lax.rsqrt(ms + EPS) * g_ref[...]
    if mode == "none":
        y = xn
    elif mode == "half64":
        y = xn * refs[2][...] + pltpu.roll(xn, 64, 1) * refs[3][...]
    else:
        y = xn * refs[2][...] + pltpu.roll(xn, 96, 1) * refs[3][...] + pltpu.roll(xn, 32, 1) * refs[4][...]
    o_ref[...] = (y * scale).astype(o_ref.dtype).reshape(o_ref.shape)


def head_prep(x, col_off, n_heads, g, tables, mode, n_real, scale, head_major=False):
    r = x.shape[0]
    tm = _pick_tile(r, 1536, 8)
    cb = col_off // LANE
    tab_spec = pl.BlockSpec((tm, LANE), lambda i, h: (i, 0))
    if head_major:
        out_spec = pl.BlockSpec((1, tm, LANE), lambda i, h: (h, i, 0))
        out_shape = jax.ShapeDtypeStruct((n_heads, r, LANE), BF16)
    else:
        out_spec = pl.BlockSpec((tm, LANE), lambda i, h: (i, h))
        out_shape = jax.ShapeDtypeStruct((r, n_heads * LANE), BF16)
    return pl.pallas_call(
        functools.partial(_prep_kernel, mode=mode, n_real=n_real, scale=scale),
        grid=(r // tm, n_heads),
        in_specs=[pl.BlockSpec((tm, LANE), lambda i, h: (i, cb + h)),
                  pl.BlockSpec((1, LANE), lambda i, h: (0, 0))] + [tab_spec] * len(tables),
        out_specs=out_spec, out_shape=out_shape,
        compiler_params=_params("parallel", "parallel"), name="head_prep_" + mode,
    )(x, g.reshape(1, LANE), *tables)


def _vt_kernel(x_ref, o_ref):
    o_ref[...] = x_ref[...].astype(F32).T.astype(o_ref.dtype)


def transpose_heads(x, col_off, n_heads):
    r = x.shape[0]
    tm = _pick_tile(r, 1536, LANE)
    cb = col_off // LANE
    return pl.pallas_call(
        _vt_kernel, grid=(r // tm, n_heads),
        in_specs=[pl.BlockSpec((tm, LANE), lambda i, h: (i, cb + h))],
        out_specs=pl.BlockSpec((LANE, tm), lambda i, h: (h, i)),
        out_shape=jax.ShapeDtypeStruct((n_heads * LANE, r), BF16),
        compiler_params=_params("parallel", "parallel"), name="transpose_heads",
    )(x)


def _flash_kernel(*refs, groups, tq, tk, dv, n_kv, window, ctx_blocks, n_lat, has_qk2, has_sink, chunks):
    q_ref, k_ref, vt_ref = refs[0], refs[1], refs[2]
    pos = 3
    if has_qk2:
        q2_ref, k2_ref = refs[3], refs[4]
        pos = 5
    if has_sink:
        sink_ref = refs[pos]
        pos += 1
    o_ref, m_ref, l_ref, acc_ref = refs[pos:pos + 4]
    i, j = pl.program_id(1), pl.program_id(2)
    n = groups * tq

    @pl.when(j == 0)
    def _():
        m_ref[...] = jnp.full(m_ref.shape, NEG_INF, F32)
        l_ref[...] = jnp.zeros(l_ref.shape, F32)
        acc_ref[...] = jnp.zeros(acc_ref.shape, F32)

    q = q_ref[...].reshape(n, q_ref.shape[2])
    k = k_ref[...]
    if has_qk2:
        q = jnp.concatenate([q, q2_ref[...].reshape(n, LANE)], axis=1)
        k = jnp.concatenate([k, k2_ref[...]], axis=1)
    vt = vt_ref[...]
    cw = n // chunks
    for c in range(chunks):
        cs = slice(c * cw, (c + 1) * cw)
        s = lax.dot_general(k, q[cs], (((1,), (1,)), ((), ())), preferred_element_type=F32)
        if window:
            qpos = i * tq + (c * cw + lax.broadcasted_iota(jnp.int32, (1, cw), 1)) % tq
            kpos = (i + j - ctx_blocks - 1) * tk + lax.broadcasted_iota(jnp.int32, (tk, 1), 0)
            valid = (jnp.abs(qpos - kpos) <= WINDOW) & (kpos >= 0) & (kpos < n_lat)
            s = jnp.where(valid | (j < ctx_blocks), s, NEG_INF)
        m_prev = m_ref[:, cs]
        m_new = jnp.maximum(m_prev, jnp.max(s, axis=0, keepdims=True))
        alpha = jnp.exp2(m_prev - m_new)
        p = jnp.exp2(s - m_new)
        l_ref[:, cs] = alpha * l_ref[:, cs] + jnp.sum(p, axis=0, keepdims=True)
        acc_ref[:, cs] = alpha * acc_ref[:, cs] + _dot(vt, p.astype(BF16))
        m_ref[:, cs] = m_new

    @pl.when(j == n_kv - 1)
    def _():
        m, l, acc = m_ref[...], l_ref[...], acc_ref[...]
        if has_sink:
            sk = sink_ref[0]
            m_fin = jnp.maximum(m, sk)
            a = jnp.exp2(m - m_fin)
            l = l * a + jnp.exp2(sk - m_fin)
            acc = acc * a
        o_t = acc / l
        for g in range(groups):
            o_ref[:, g * dv:(g + 1) * dv] = o_t[:, g * tq:(g + 1) * tq].T.astype(o_ref.dtype)


Q_CHUNK = 1024


def flash(q, k, vt, *, kv_heads, groups, n_q, q_row0, kv_row0, n_kv_rows, tq, tk, dv=HEAD_DIM,
          q2=None, k2=None, sink=None, window=False, ctx_rows=0):
    dk = q.shape[2]
    qb0 = q_row0 // tq
    kb0 = kv_row0 // tk
    if window:
        ctx_blocks = ctx_rows // tk
        n_lat_blocks = n_kv_rows // tk
        n_kv = ctx_blocks + 3

        def kv_row(i, j):
            return jnp.where(j < ctx_blocks, n_lat_blocks + j, jnp.clip(i + j - ctx_blocks - 1, 0, n_lat_blocks - 1))
    else:
        ctx_blocks = 0
        n_kv = n_kv_rows // tk

        def kv_row(i, j):
            return kb0 + j
    n = groups * tq
    in_specs = [pl.BlockSpec((groups, tq, dk), lambda h, i, j: (h, qb0 + i, 0)),
                pl.BlockSpec((tk, dk), lambda h, i, j: (kv_row(i, j), h)),
                pl.BlockSpec((dv, tk), lambda h, i, j: (h, kv_row(i, j)))]
    args = [q, k, vt]
    if q2 is not None:
        in_specs += [pl.BlockSpec((groups, tq, LANE), lambda h, i, j: (h, qb0 + i, 0)),
                     pl.BlockSpec((tk, LANE), lambda h, i, j: (kv_row(i, j), 0))]
        args += [q2, k2]
    if sink is not None:
        in_specs.append(pl.BlockSpec((1, 1, n), lambda h, i, j: (h, 0, 0)))
        sink2 = sink.astype(F32).reshape(kv_heads, groups, 1) * LOG2E
        args.append(jnp.broadcast_to(sink2, (kv_heads, groups, tq)).reshape(kv_heads, 1, n))
    kern = functools.partial(
        _flash_kernel, groups=groups, tq=tq, tk=tk, dv=dv, n_kv=n_kv, window=window, ctx_blocks=ctx_blocks,
        n_lat=n_kv_rows, has_qk2=q2 is not None, has_sink=sink is not None,
        chunks=max(1, n // Q_CHUNK))
    return pl.pallas_call(
        kern, grid=(kv_heads, n_q // tq, n_kv), in_specs=in_specs,
        out_specs=pl.BlockSpec((tq, groups * dv), lambda h, i, j: (i, h)),
        out_shape=jax.ShapeDtypeStruct((n_q, kv_heads * groups * dv), BF16),
        scratch_shapes=[pltpu.VMEM((1, n), F32), pltpu.VMEM((1, n), F32), pltpu.VMEM((dv, n), F32)],
        compiler_params=_params("parallel", "parallel", "arbitrary"),
        name="flash_window" if window else "flash_dense",
    )(*args)


def _outproj_kernel(o1c_ref, o1l_ref, o2c_ref, o2l_ref, gate_ref, h_ref, mod_ref, w_ref, out_ref, *, lat_tiles):
    is_ctx = pl.program_id(0) >= lat_tiles
    w1 = o1l_ref.shape[1]
    d = h_ref.shape[1]
    gt = gate_ref[...].astype(F32)
    sg = gt * jax.nn.sigmoid(gt)
    o1 = jnp.where(is_ctx, o1c_ref[...].astype(F32), o1l_ref[...].astype(F32))
    o2 = jnp.where(is_ctx, o2c_ref[...].astype(F32), o2l_ref[...].astype(F32))
    t1 = (o1 * sg[:, :w1]).astype(BF16)
    t2 = (o2 * sg[:, w1:]).astype(BF16)
    y = _dot(t1, w_ref[0:w1, :]) + _dot(t2, w_ref[w1:, :])
    gm = jnp.where(is_ctx, mod_ref[1:2, 2 * d:3 * d], mod_ref[0:1, 2 * d:3 * d])
    out_ref[...] = h_ref[...] + gm * y


def out_proj(o1c, o1l, o2c, o2l, proj, h, mod, w, latents_only):
    r, d = h.shape
    tm = ROW_TILE
    assert o1c.shape[0] == tm, "the context rows must be exactly one row tile"
    lt = o1l.shape[0] // tm
    w1, w2 = o1l.shape[1], o2l.shape[1]
    lat = lambda i: (jnp.minimum(i, lt - 1), 0)
    if latents_only:
        r = lt * tm
    return pl.pallas_call(
        functools.partial(_outproj_kernel, lat_tiles=lt), grid=(r // tm,),
        in_specs=[pl.BlockSpec((tm, w1), lambda i: (0, 0)), pl.BlockSpec((tm, w1), lat),
                  pl.BlockSpec((tm, w2), lambda i: (0, 0)), pl.BlockSpec((tm, w2), lat),
                  pl.BlockSpec((tm, w1 + w2), lambda i: (i, OFF_GATE // (w1 + w2))),
                  pl.BlockSpec((tm, d), lambda i: (i, 0)),
                  pl.BlockSpec(mod.shape, lambda i: (0, 0)),
                  pl.BlockSpec(w.shape, lambda i: (0, 0))],
        out_specs=pl.BlockSpec((tm, d), lambda i: (i, 0)),
        out_shape=jax.ShapeDtypeStruct((r, d), F32),
        compiler_params=_params("parallel"), name="out_proj",
    )(o1c, o1l, o2c, o2l, proj, h, mod, w)


def _conv3_kernel(x_ref, w_ref, b_ref, oc_ref, ol_ref, *, n_ctx):
    x = x_ref[...].astype(F32)
    r = x.shape[0]
    n_lat = r - n_ctx
    row = lax.broadcasted_iota(jnp.int32, (r, 1), 0)
    prev = jnp.where((row == 0) | (row == n_lat), 0.0, pltpu.roll(x, 1, 0))
    nxt = jnp.where((row == n_lat - 1) | (row == r - 1), 0.0, pltpu.roll(x, r - 1, 0))
    y = prev * w_ref[0:1, :] + x * w_ref[1:2, :] + nxt * w_ref[2:3, :] + b_ref[...]
    ol_ref[0] = y[:n_lat]
    oc_ref[0] = y[n_lat:]


def conv3(proj, col_off, w, b, n_ctx):
    r = proj.shape[0]
    cw = w.shape[1] // 3
    nb = cw // LANE
    cb = col_off // LANE
    return pl.pallas_call(
        functools.partial(_conv3_kernel, n_ctx=n_ctx), grid=(3, nb),
        in_specs=[pl.BlockSpec((r, LANE), lambda p, c: (0, cb + p * nb + c)),
                  pl.BlockSpec((3, LANE), lambda p, c: (0, p * nb + c)),
                  pl.BlockSpec((1, LANE), lambda p, c: (0, p * nb + c))],
        out_specs=[pl.BlockSpec((1, n_ctx, LANE), lambda p, c: (p, 0, c)),
                   pl.BlockSpec((1, r - n_ctx, LANE), lambda p, c: (p, 0, c))],
        out_shape=[jax.ShapeDtypeStruct((3, n_ctx, cw), F32), jax.ShapeDtypeStruct((3, r - n_ctx, cw), F32)],
        compiler_params=_params("parallel", "parallel"), name="conv3",
    )(proj, w, b.reshape(1, -1))


def _filt_kernel(z_ref, w1_ref, b1_ref, f1_ref, w2_ref, b2_ref, f2_ref, w3_ref, dl_ref, h_ref, nrm_ref, *, tm, cw):
    i = pl.program_id(0)
    z = z_ref[...]

    def mm(a, w_r):
        return _dot3(*_split_bf16(a), *_split_bf16(w_r[...]))

    a = jnp.sin(f1_ref[...] * (mm(z, w1_ref) + b1_ref[...]))
    a = jnp.sin(f2_ref[...] * (mm(a, w2_ref) + b2_ref[...]))
    dec = jnp.exp(-z[:, 0:1] * dl_ref[...])
    h = mm(a, w3_ref) * jnp.concatenate([dec] * (h_ref.shape[1] // cw), axis=1)
    h_ref[...] = h
    row = i * tm + lax.broadcasted_iota(jnp.int32, (tm, 1), 0)
    col = lax.broadcasted_iota(jnp.int32, (1, h.shape[1]), 1)
    skip = (row == 0) & ((col // cw) % 2 == 1)
    part = jnp.sum(jnp.where(skip, 0.0, jnp.abs(h)), axis=0, keepdims=True)

    @pl.when(i == 0)
    def _():
        nrm_ref[...] = jnp.zeros(nrm_ref.shape, F32)

    nrm_ref[...] += jnp.broadcast_to(part, nrm_ref.shape)


def hyena_filter(z, w1p, b1, f1, w2, b2, f2, w3, deltas):
    l = z.shape[0]
    n = w3.shape[1]
    cw = deltas.shape[0]
    hid = w2.shape[0]
    tm = min(l, 256)
    full = lambda a: pl.BlockSpec(a.shape, lambda i: (0, 0))
    ins = [w1p, b1.reshape(1, hid), f1.reshape(1, hid), w2, b2.reshape(1, hid), f2.reshape(1, hid), w3,
           deltas.reshape(1, cw)]
    return pl.pallas_call(
        functools.partial(_filt_kernel, tm=tm, cw=cw), grid=(l // tm,),
        in_specs=[pl.BlockSpec((tm, LANE), lambda i: (i, 0))] + [full(a) for a in ins],
        out_specs=[pl.BlockSpec((tm, n), lambda i: (i, 0)), pl.BlockSpec((8, n), lambda i: (0, 0))],
        out_shape=[jax.ShapeDtypeStruct((l, n), F32), jax.ShapeDtypeStruct((8, n), F32)],
        compiler_params=_params("arbitrary"), name="hyena_filter",
    )(z, *ins)


@functools.lru_cache(maxsize=None)
def _dft_consts(a):
    n = a * a
    hh = a // 2
    n1 = np.arange(hh)
    k1 = np.arange(hh)
    n2 = np.arange(a)
    k2 = np.arange(a)
    ang1 = 2 * np.pi * np.outer(k1 + 0.5, n1) / a
    m1 = np.zeros((hh, 2, hh))
    m1[:, 0], m1[:, 1] = np.cos(ang1), -np.sin(ang1)
    m1 = m1.reshape(2 * hh, hh)
    phi = 2 * np.pi * (n2[None, None, :] * (k1[:, None, None] + 0.5) / n + n2[None, None, :] * k2[None, :, None] / a)
    c, s = np.cos(phi), np.sin(phi)
    g = np.zeros((hh, 2, a, 2, a))
    g[:, 0, :, 0, :], g[:, 0, :, 1, :], g[:, 1, :, 0, :], g[:, 1, :, 1, :] = c, s, -s, c
    g = g.reshape(hh, 2 * a, 2 * a)
    gt = np.ascontiguousarray(np.transpose(g, (0, 2, 1)))
    al = 2 * np.pi * np.outer(n1, k1 + 0.5) / a
    mb = np.zeros((hh, hh, 2))
    mb[:, :, 0], mb[:, :, 1] = 2 / n * np.cos(al), -2 / n * np.sin(al)
    mb = mb.reshape(hh, 2 * hh)
    return tuple(_np_split(x) for x in (m1, g, gt, mb))


@functools.lru_cache(maxsize=None)
def _dense_dft_consts(l):
    n = 2 * l
    ang = 2 * np.pi * np.outer(np.arange(l) + 0.5, np.arange(l)) / n
    mf = np.concatenate([np.cos(ang), -np.sin(ang)], axis=0)
    mi = np.concatenate([2 / n * np.cos(ang.T), -2 / n * np.sin(ang.T)], axis=1)
    return _np_split(mf), _np_split(mi)


def _np_split(x):
    hi = x.astype(ml_dtypes.bfloat16)
    lo = (x - hi.astype(np.float64)).astype(ml_dtypes.bfloat16)
    return hi, lo


SUBLANES = 8


def _stage1_kernel(x_ref, mh_ref, ml_ref, o_ref):
    for s in range(SUBLANES):
        o_ref[:, s, :] = _dot3(mh_ref[...], ml_ref[...], *_split_bf16(x_ref[0, :, s, :]))


def dft_stage1(x4, p, m1):
    _, hh, a, c = x4.shape
    ct = _pick_tile(c, 1024)
    mh, ml = m1
    return pl.pallas_call(
        _stage1_kernel, grid=(a // SUBLANES, c // ct),
        in_specs=[pl.BlockSpec((1, hh, SUBLANES, ct), lambda j, q: (p, 0, j, q)),
                  pl.BlockSpec(mh.shape, lambda j, q: (0, 0)), pl.BlockSpec(ml.shape, lambda j, q: (0, 0))],
        out_specs=pl.BlockSpec((2 * hh, SUBLANES, ct), lambda j, q: (0, j, q)),
        out_shape=jax.ShapeDtypeStruct((2 * hh, a, c), F32),
        compiler_params=_params("parallel", "parallel"), name="dft_stage1",
    )(x4, jnp.asarray(mh), jnp.asarray(ml))


def _spectrum_product(x, h0, h1, tap0, n0, n1):
    half = x.shape[0] // 2
    inv = 1.0 / (n0 + n1)
    kr = (h0[:half] + h1[:half] - tap0) * inv
    ki = (h0[half:] - h1[half:]) * inv
    xr, xi = x[:half], x[half:]
    return jnp.concatenate([xr * kr - xi * ki, xr * ki + xi * kr], axis=0)


def _mid_kernel(*refs, conv):
    y_ref, gh_ref, gl_ref = refs[0], refs[1], refs[2]
    o_ref = refs[-1]
    x = _dot3(gh_ref[0], gl_ref[0], *_split_bf16(y_ref[0]))
    if conv:
        th_ref, tl_ref, h0_ref, h1_ref, tap_ref, n0_ref, n1_ref = refs[3:10]
        z = _spectrum_product(x, h0_ref[0], h1_ref[0], tap_ref[0:1, :], n0_ref[0:1, :], n1_ref[0:1, :])
        x = _dot3(th_ref[0], tl_ref[0], *_split_bf16(z))
    o_ref[0] = x


def dft_mid(y, g, gt=None, filt=None, order=0):
    hh, a2, c = y.shape
    slab = pl.BlockSpec((1, a2, c), lambda k: (k, 0, 0))
    mat = pl.BlockSpec((1, a2, a2), lambda k: (k, 0, 0))
    in_specs = [slab, mat, mat]
    args = [y, jnp.asarray(g[0]), jnp.asarray(g[1])]
    if filt is not None:
        hf, taps, nrm = filt
        s0, s1 = 2 * order, 2 * order + 1
        in_specs += [mat, mat,
                     pl.BlockSpec((1, a2, c), lambda k: (k, 0, s0)), pl.BlockSpec((1, a2, c), lambda k: (k, 0, s1)),
                     pl.BlockSpec((8, c), lambda k: (0, s1)),
                     pl.BlockSpec((8, c), lambda k: (0, s0)), pl.BlockSpec((8, c), lambda k: (0, s1))]
        args += [jnp.asarray(gt[0]), jnp.asarray(gt[1]), hf, hf, taps, nrm, nrm]
    return pl.pallas_call(
        functools.partial(_mid_kernel, conv=filt is not None), grid=(hh,),
        in_specs=in_specs, out_specs=slab, out_shape=jax.ShapeDtypeStruct((hh, a2, c), F32),
        compiler_params=_params("parallel"), name="dft_mid_conv" if filt is not None else "dft_mid_fwd",
    )(*args)


def _last_kernel(b_ref, mh_ref, ml_ref, xa_ref, zb_ref, bias_ref, o_ref):
    for s in range(SUBLANES):
        conv = _dot3(mh_ref[...], ml_ref[...], *_split_bf16(b_ref[:, s, :]))
        z = zb_ref[0, :, s, :]
        o_ref[0, :, s, :] = (xa_ref[0, :, s, :] * (conv + bias_ref[...] * z)).astype(o_ref.dtype)


def dft_last(b, mb, xa4, pa, zb4, pb, bias_row, out_dtype):
    rows, a, c = b.shape
    hh = rows // 2
    ct = _pick_tile(c, 1024)
    mh, ml = mb
    sig = lambda pp: pl.BlockSpec((1, hh, SUBLANES, ct), lambda j, q: (pp, 0, j, q))
    return pl.pallas_call(
        _last_kernel, grid=(a // SUBLANES, c // ct),
        in_specs=[pl.BlockSpec((rows, SUBLANES, ct), lambda j, q: (0, j, q)),
                  pl.BlockSpec(mh.shape, lambda j, q: (0, 0)), pl.BlockSpec(ml.shape, lambda j, q: (0, 0)),
                  sig(pa), sig(pb), pl.BlockSpec((1, ct), lambda j, q: (0, q))],
        out_specs=sig(0), out_shape=jax.ShapeDtypeStruct((1, hh, a, c), out_dtype),
        compiler_params=_params("parallel", "parallel"), name="dft_last",
    )(b, jnp.asarray(mh), jnp.asarray(ml), xa4, zb4, bias_row)


def long_conv_gated(u3, hf, taps, nrm, bias, consts, a):
    m1, g, gt, mb = consts
    _, l, c = u3.shape
    hh = a // 2
    u4 = u3.reshape(3, hh, a, c)
    filt = (hf, taps, nrm)

    def conv_gate(src4, p_src, xa_p, order, out_dtype):
        y = dft_stage1(src4, p_src, m1).reshape(hh, 2 * a, c)
        bb = dft_mid(y, g, gt, filt, order).reshape(2 * hh, a, c)
        return dft_last(bb, mb, u4, xa_p, src4, p_src, bias[order].reshape(1, c), out_dtype)

    z2 = conv_gate(u4, 2, 0, 0, F32)
    return conv_gate(z2, 0, 1, 1, F32).reshape(l, c)


def filter_spectrum(h, consts, a):
    m1, g, _, _ = consts
    l, n = h.shape
    hh = a // 2
    y = dft_stage1(h.reshape(1, hh, a, n), 0, m1).reshape(hh, 2 * a, n)
    return dft_mid_wide(y, g, n // 4)


def dft_mid_wide(y, g, c):
    hh, a2, n = y.shape
    return pl.pallas_call(
        functools.partial(_mid_kernel, conv=False), grid=(hh, n // c),
        in_specs=[pl.BlockSpec((1, a2, c), lambda k, q: (k, 0, q)),
                  pl.BlockSpec((1, a2, a2), lambda k, q: (k, 0, 0)), pl.BlockSpec((1, a2, a2), lambda k, q: (k, 0, 0))],
        out_specs=pl.BlockSpec((1, a2, c), lambda k, q: (k, 0, q)),
        out_shape=jax.ShapeDtypeStruct((hh, a2, n), F32),
        compiler_params=_params("parallel", "parallel"), name="dft_mid_fwd",
    )(y, jnp.asarray(g[0]), jnp.asarray(g[1]))


def _ctx_spec_kernel(h_ref, mh_ref, ml_ref, o_ref):
    o_ref[...] = _dot3(mh_ref[...], ml_ref[...], *_split_bf16(h_ref[...]))


def ctx_filter_spectrum(h, mf):
    l, n = h.shape
    tn = _pick_tile(n, 1024)
    mh, ml = mf
    return pl.pallas_call(
        _ctx_spec_kernel, grid=(n // tn,),
        in_specs=[pl.BlockSpec((l, tn), lambda j: (0, j)),
                  pl.BlockSpec(mh.shape, lambda j: (0, 0)), pl.BlockSpec(ml.shape, lambda j: (0, 0))],
        out_specs=pl.BlockSpec((2 * l, tn), lambda j: (0, j)),
        out_shape=jax.ShapeDtypeStruct((2 * l, n), F32),
        compiler_params=_params("parallel"), name="ctx_filter_spectrum",
    )(h, jnp.asarray(mh), jnp.asarray(ml))


def _ctx_hyena_kernel(u_ref, fh_ref, fl_ref, ih_ref, il_ref, hf0a, hf1a, hf0b, hf1b, tap0, tap1, n0a, n1a, n0b, n1b,
                      bias_ref, o_ref):
    x1, x2, z = u_ref[0], u_ref[1], u_ref[2]

    def conv(sig, h0_ref, h1_ref, tap_ref, n0_ref, n1_ref):
        x = _dot3(fh_ref[...], fl_ref[...], *_split_bf16(sig))
        zz = _spectrum_product(x, h0_ref[...], h1_ref[...], tap_ref[0:1, :], n0_ref[0:1, :], n1_ref[0:1, :])
        return _dot3(ih_ref[...], il_ref[...], *_split_bf16(zz))

    z2 = x1 * (conv(z, hf0a, hf1a, tap0, n0a, n1a) + bias_ref[0:1, :] * z)
    o_ref[...] = (x2 * (conv(z2, hf0b, hf1b, tap1, n0b, n1b) + bias_ref[1:2, :] * z2)).astype(o_ref.dtype)


def ctx_hyena(u3, hf, taps, nrm, bias, mf, mi):
    _, l, c = u3.shape
    tc = _pick_tile(c, 512)
    nb = c // tc
    const = lambda a: pl.BlockSpec(a.shape, lambda j: (0, 0))
    col = lambda rows, q: pl.BlockSpec((rows, tc), lambda j: (0, q * nb + j))
    mats = [jnp.asarray(m) for m in (*mf, *mi)]
    return pl.pallas_call(
        _ctx_hyena_kernel, grid=(nb,),
        in_specs=[pl.BlockSpec((3, l, tc), lambda j: (0, 0, j))] + [const(m) for m in mats]
        + [col(2 * l, 0), col(2 * l, 1), col(2 * l, 2), col(2 * l, 3), col(8, 1), col(8, 3),
           col(8, 0), col(8, 1), col(8, 2), col(8, 3), pl.BlockSpec((2, tc), lambda j: (0, j))],
        out_specs=pl.BlockSpec((l, tc), lambda j: (0, j)),
        out_shape=jax.ShapeDtypeStruct((l, c), BF16),
        compiler_params=_params("parallel"), name="ctx_hyena",
    )(u3, *mats, hf, hf, hf, hf, taps, taps, nrm, nrm, nrm, nrm, bias)


def _axial_tables(n_tokens, n_rot):
    rows = n_tokens // GRID_W
    row = jnp.broadcast_to(jnp.arange(rows)[:, None], (rows, GRID_W)).reshape(-1).astype(F32)
    col = jnp.broadcast_to(jnp.arange(GRID_W)[None, :], (rows, GRID_W)).reshape(-1).astype(F32)
    n_freq = n_rot // 4
    inv = ROPE_THETA ** (-jnp.arange(n_freq, dtype=F32) / n_freq)
    ang = jnp.concatenate([row[:, None] * inv, col[:, None] * inv], axis=-1)
    return jnp.cos(ang), jnp.sin(ang)


def _rope_tables_full(s, n_ctx):
    cos, sin = _axial_tables(s, HEAD_DIM)
    t0 = jnp.concatenate([jnp.concatenate([cos, cos], axis=1), jnp.ones((n_ctx, LANE), F32)], axis=0)
    t1 = jnp.concatenate([jnp.concatenate([-sin, sin], axis=1), jnp.zeros((n_ctx, LANE), F32)], axis=0)
    return t0, t1


def _rope_tables_half(s, n_ctx):
    cos, sin = _axial_tables(s, ROPE_DIM)
    q = ROPE_DIM // 2
    zq = jnp.zeros((s, q), F32)
    z2 = jnp.zeros((s, LANE - ROPE_DIM), F32)
    t0 = jnp.concatenate([cos, cos, z2], axis=1)
    t1 = jnp.concatenate([-sin, zq, z2], axis=1)
    t2 = jnp.concatenate([zq, sin, z2], axis=1)
    c0 = jnp.concatenate([jnp.ones((n_ctx, ROPE_DIM), F32), jnp.zeros((n_ctx, LANE - ROPE_DIM), F32)], axis=1)
    cz = jnp.zeros((n_ctx, LANE), F32)
    return jnp.concatenate([t0, c0], axis=0), jnp.concatenate([t1, cz], axis=0), jnp.concatenate([t2, cz], axis=0)


def _filter_features(l):
    pos = jnp.arange(l, dtype=F32)
    t = pos / max(l - 1, 1)
    bands = jnp.linspace(1e-4, HYENA_BANDS - 1, HYENA_BANDS, dtype=F32)
    ang = (2.0 * math.pi / l) * pos[:, None] * bands[None, :]
    z = jnp.concatenate([t[:, None], jnp.cos(ang), -jnp.sin(ang)], axis=-1)
    return jnp.pad(z, ((0, 0), (0, LANE - HYENA_EMB)))


Q_STACK = 1024
KV_TILE_CAP = 1408


def _attention_pair(qp, kp, vt, s, n_ctx, *, kv_heads, groups, window, sink=None, q2=None, k2=None):
    common = dict(kv_heads=kv_heads, groups=groups, q2=q2, k2=k2, sink=sink)
    o_ctx = flash(qp, kp, vt, n_q=n_ctx, q_row0=s, kv_row0=s, n_kv_rows=n_ctx, tq=n_ctx, tk=n_ctx, **common)
    if window:
        o_lat = flash(qp, kp, vt, n_q=s, q_row0=0, kv_row0=0, n_kv_rows=s, tq=ROW_TILE, tk=ROW_TILE, window=True,
                      ctx_rows=n_ctx, **common)
    else:
        o_lat = flash(qp, kp, vt, n_q=s, q_row0=0, kv_row0=0, n_kv_rows=n_ctx + s, tq=Q_STACK // groups,
                      tk=_pick_tile(n_ctx + s, KV_TILE_CAP), **common)
    return o_ctx, o_lat


def _even_layer(u, s, n_ctx, w_in, qn_g, kn_g, sink, conv_w, conv_b, fparams, hy_bias, rope_full, feats):
    wq, wk, wv, whd, wg = jnp.split(w_in, np.cumsum([ATT_WIDTH, KV_WIDTH, KV_WIDTH, 3 * B_WIDTH])[:].tolist(), axis=1)
    proj = matmul(u, jnp.concatenate([wg, wq, wk, wv, whd], axis=1).astype(BF16), F32)
    qp = head_prep(proj, OFF_Q, A_HEADS, qn_g, rope_full, "half64", HEAD_DIM, HEAD_DIM ** -0.5 * LOG2E, True)
    kp = head_prep(proj, OFF_K, A_KV_HEADS, kn_g, rope_full, "half64", HEAD_DIM, 1.0)
    vt = transpose_heads(proj, OFF_V, A_KV_HEADS)
    a_ctx, a_lat = _attention_pair(qp, kp, vt, s, n_ctx, kv_heads=A_KV_HEADS, groups=A_HEADS // A_KV_HEADS,
                                   window=True, sink=sink)
    w1, b1, f1, w2, b2, f2, w3 = fparams
    w1p = jnp.pad(w1, ((0, LANE - HYENA_EMB), (0, 0)))
    deltas = jnp.linspace(DECAY_MAX, DECAY_MIN, B_WIDTH, dtype=F32)
    uc3, ul3 = conv3(proj, OFF_REST, conv_w, conv_b, n_ctx)
    a = int(round(math.sqrt(2 * s)))
    consts = _dft_consts(a)
    h_lat, nrm_lat = hyena_filter(feats[0], w1p, b1, f1, w2, b2, f2, w3, deltas)
    hf_lat = filter_spectrum(h_lat, consts, a)
    b_lat = long_conv_gated(ul3, hf_lat, h_lat, nrm_lat, hy_bias, consts, a)
    mf, mi = _dense_dft_consts(n_ctx)
    h_ctx, nrm_ctx = hyena_filter(feats[1], w1p, b1, f1, w2, b2, f2, w3, deltas)
    hf_ctx = ctx_filter_spectrum(h_ctx, mf)
    b_ctx = ctx_hyena(uc3, hf_ctx, h_ctx, nrm_ctx, hy_bias, mf, mi)
    return proj, a_ctx, a_lat, b_ctx, b_lat


def _odd_layer(u, s, n_ctx, w_in, qn_g, kn_g, cq_g, ckv_g, wuq, wukv, mq_g, mk_g, rope_full, rope_half):
    d = w_in.shape[0]
    parts = jnp.split(w_in, np.cumsum([ATT_WIDTH, KV_WIDTH, KV_WIDTH, Q_LORA, KV_LORA, ROPE_DIM]).tolist(), axis=1)
    wq, wk, wv, wmq, wmkv, wmkr, wg = parts
    w_perm = jnp.concatenate([wg, wq, wk, wv, wmq, wmkv, wmkr, jnp.zeros((d, LANE - ROPE_DIM), w_in.dtype)], axis=1)
    proj = matmul(u, w_perm.astype(BF16), F32)
    qp = head_prep(proj, OFF_Q, C_HEADS, qn_g, rope_full, "half64", HEAD_DIM, HEAD_DIM ** -0.5 * LOG2E, True)
    kp = head_prep(proj, OFF_K, C_KV_HEADS, kn_g, rope_full, "half64", HEAD_DIM, 1.0)
    vt = transpose_heads(proj, OFF_V, C_KV_HEADS)
    c_ctx, c_lat = _attention_pair(qp, kp, vt, s, n_ctx, kv_heads=C_KV_HEADS, groups=C_HEADS // C_KV_HEADS,
                                   window=False)
    pad_r = LANE - ROPE_DIM
    wuq3 = wuq.reshape(Q_LORA, M_HEADS, QK_DIM)
    wuq_p = jnp.concatenate([wuq3[:, :, :NOPE_DIM].reshape(Q_LORA, -1),
                             jnp.pad(wuq3[:, :, NOPE_DIM:], ((0, 0), (0, 0), (0, pad_r))).reshape(Q_LORA, -1)], axis=1)
    wukv3 = wukv.reshape(KV_LORA, M_HEADS, NOPE_DIM + V_DIM)
    wukv_p = jnp.concatenate([wukv3[:, :, :NOPE_DIM].reshape(KV_LORA, -1),
                              wukv3[:, :, NOPE_DIM:].reshape(KV_LORA, -1)], axis=1)
    q_raw = norm_matmul(proj, OFF_MQ, cq_g, wuq_p.astype(BF16))
    kv_raw = norm_matmul(proj, OFF_MKV, ckv_g, wukv_p.astype(BF16))
    scale = QK_DIM ** -0.5 * LOG2E
    gq_r = jnp.pad(mq_g[NOPE_DIM:], (0, pad_r))
    gk_r = jnp.pad(mk_g[NOPE_DIM:], (0, pad_r))
    qn = head_prep(q_raw, 0, M_HEADS, mq_g[:NOPE_DIM], (), "none", NOPE_DIM, scale, True)
    qr = head_prep(q_raw, M_HEADS * NOPE_DIM, M_HEADS, gq_r, rope_half, "half32", ROPE_DIM, scale, True)
    kn = head_prep(kv_raw, 0, M_HEADS, mk_g[:NOPE_DIM], (), "none", NOPE_DIM, 1.0)
    kr = head_prep(proj, OFF_MKR, 1, gk_r, rope_half, "half32", ROPE_DIM, 1.0)
    vmt = transpose_heads(kv_raw, M_HEADS * NOPE_DIM, M_HEADS)
    d_ctx, d_lat = _attention_pair(qn, kn, vmt, s, n_ctx, kv_heads=M_HEADS, groups=1, window=False, q2=qr, k2=kr)
    return proj, c_ctx, c_lat, d_ctx, d_lat


def kernel(x, c, ctx, c_ctx, ada_w, ada_b, norm_g, w_out, ev_w_in, ev_qn_g, ev_kn_g, ev_sink, ev_conv_w, ev_conv_b, hy_w1, hy_b1, hy_f1, hy_w2, hy_b2, hy_f2, hy_w3, hy_bias, od_w_in, od_qn_g, od_kn_g, od_cq_g, od_ckv_g, od_wuq, od_wukv, od_mq_g, od_mk_g):
    _, s, d = x.shape
    n_ctx = ctx.shape[1]
    depth = ada_w.shape[0]
    h = jnp.concatenate([x[0], ctx[0]], axis=0)
    cc = jnp.zeros((8, d), F32).at[0].set(c[0]).at[1].set(c_ctx)
    mod = ada_mod(cc, ada_w, ada_b)
    rope_full = _rope_tables_full(s, n_ctx)
    rope_half = _rope_tables_half(s, n_ctx)
    feats = (_filter_features(s), _filter_features(n_ctx))
    for i in range(depth):
        u = norm_mod(h, norm_g[i], mod[i], n_ctx)
        if i % 2 == 0:
            e = i // 2
            proj, o1c, o1l, o2c, o2l = _even_layer(
                u, s, n_ctx, ev_w_in[e], ev_qn_g[e], ev_kn_g[e], ev_sink[e], ev_conv_w[e], ev_conv_b[e],
                (hy_w1[e], hy_b1[e], hy_f1[e], hy_w2[e], hy_b2[e], hy_f2[e], hy_w3[e]), hy_bias[e], rope_full, feats)
        else:
            o = i // 2
            proj, o1c, o1l, o2c, o2l = _odd_layer(
                u, s, n_ctx, od_w_in[o], od_qn_g[o], od_kn_g[o], od_cq_g[o], od_ckv_g[o], od_wuq[o], od_wukv[o],
                od_mq_g[o], od_mk_g[o], rope_full, rope_half)
        h = out_proj(o1c, o1l, o2c, o2l, proj, h, mod[i], w_out[i].astype(BF16), latents_only=i == depth - 1)
    return h[None]
```

```python
import functools
import math

import numpy as np
import jax
import jax.numpy as jnp
from jax import lax
from jax.experimental import pallas as pl
from jax.experimental.pallas import tpu as pltpu

F32 = jnp.float32
BF16 = jnp.bfloat16

GRID_W = 64
HEAD_DIM = 128
ROPE_THETA = 10000.0
EPS = 1e-6
NEG_INF = -1e30
A_HEADS, A_KV_HEADS, WINDOW = 8, 2, 128
B_WIDTH = 1024
HYENA_BANDS = 16
HYENA_EMB = 1 + 2 * HYENA_BANDS
DECAY_TARGET = 1e-2
DECAY_MAX = abs(math.log(DECAY_TARGET)) / 0.3
DECAY_MIN = abs(math.log(DECAY_TARGET)) / 1.5
C_HEADS, C_KV_HEADS = 8, 2
M_HEADS, Q_LORA, KV_LORA, NOPE_DIM, ROPE_DIM, V_DIM = 8, 512, 256, 128, 64, 128
QK_DIM = NOPE_DIM + ROPE_DIM
ATT_WIDTH = A_HEADS * HEAD_DIM
KV_WIDTH = A_KV_HEADS * HEAD_DIM
BRANCH = ATT_WIDTH + B_WIDTH

LOG2E = math.log2(math.e)
LANE = 128
SUBLANES = 8
ROWS_BF16 = 16
ROW_TILE = 256
VMEM_LIMIT = 48 * 1024 * 1024

OFF_GATE, OFF_Q, OFF_K, OFF_V, OFF_REST = 0, BRANCH, BRANCH + ATT_WIDTH, BRANCH + ATT_WIDTH + KV_WIDTH, BRANCH + ATT_WIDTH + 2 * KV_WIDTH
OFF_MQ, OFF_MKV, OFF_MKR = OFF_REST, OFF_REST + Q_LORA, OFF_REST + Q_LORA + KV_LORA


def _params(*sem):
    return pltpu.CompilerParams(dimension_semantics=sem, vmem_limit_bytes=VMEM_LIMIT)


def _split_bf16(x):
    hi = x.astype(BF16)
    return hi, (x - hi.astype(F32)).astype(BF16)


def _dot(a, b):
    return jnp.dot(a, b, preferred_element_type=F32)


def _dot3(a_hi, a_lo, b_hi, b_lo):
    return _dot(a_hi, b_hi) + (_dot(a_hi, b_lo) + _dot(a_lo, b_hi))


def _ada_kernel(c_ref, w_ref, b_ref, o_ref):
    c = c_ref[...]
    a = (c * jax.nn.sigmoid(c)).astype(BF16)
    o_ref[0] = _dot(a, w_ref[0].astype(BF16)) + b_ref[0]


def ada_mod(cc, ada_w, ada_b):
    depth, d, n = ada_w.shape
    tn = _pick_tile(n, 1024)
    return pl.pallas_call(
        _ada_kernel, grid=(depth, n // tn),
        in_specs=[pl.BlockSpec((8, d), lambda l, j: (0, 0)),
                  pl.BlockSpec((1, d, tn), lambda l, j: (l, 0, j)),
                  pl.BlockSpec((1, 1, tn), lambda l, j: (l, 0, j))],
        out_specs=pl.BlockSpec((1, 8, tn), lambda l, j: (l, 0, j)),
        out_shape=jax.ShapeDtypeStruct((depth, 8, n), F32),
        compiler_params=_params("parallel", "parallel"), name="ada_mod",
    )(cc, ada_w, ada_b.reshape(depth, 1, n))


def _norm_kernel(x_ref, g_ref, mod_ref, o_ref, *, lat_tiles):
    x = x_ref[...]
    d = x.shape[1]
    y = x * lax.rsqrt(jnp.mean(x * x, axis=-1, keepdims=True) + EPS) * g_ref[...]
    is_ctx = pl.program_id(0) >= lat_tiles
    sh = jnp.where(is_ctx, mod_ref[1:2, 0:d], mod_ref[0:1, 0:d])
    sc = jnp.where(is_ctx, mod_ref[1:2, d:2 * d], mod_ref[0:1, d:2 * d])
    o_ref[...] = (y * (1.0 + sc) + sh).astype(o_ref.dtype)


def norm_mod(h, g, mod, n_ctx):
    r, d = h.shape
    tm = ROW_TILE
    return pl.pallas_call(
        functools.partial(_norm_kernel, lat_tiles=(r - n_ctx) // tm), grid=(r // tm,),
        in_specs=[pl.BlockSpec((tm, d), lambda i: (i, 0)),
                  pl.BlockSpec((1, d), lambda i: (0, 0)),
                  pl.BlockSpec(mod.shape, lambda i: (0, 0))],
        out_specs=pl.BlockSpec((tm, d), lambda i: (i, 0)),
        out_shape=jax.ShapeDtypeStruct((r, d), BF16),
        compiler_params=_params("parallel"), name="norm_mod",
    )(h, g.reshape(1, d), mod)


def _mm_kernel(a_ref, b_ref, o_ref):
    o_ref[...] = _dot(a_ref[...], b_ref[...]).astype(o_ref.dtype)


def _pick_tile(n, cap, unit=LANE):
    best = unit
    for t in range(unit, cap + 1, unit):
        if n % t == 0:
            best = t
    return best


def matmul(a, b, out_dtype):
    m, k = a.shape
    n = b.shape[1]
    tm = _pick_tile(m, 1536, 8)
    tn = _pick_tile(n, 1024)
    return pl.pallas_call(
        _mm_kernel, grid=(m // tm, n // tn),
        in_specs=[pl.BlockSpec((tm, k), lambda i, j: (i, 0)),
                  pl.BlockSpec((k, tn), lambda i, j: (0, j))],
        out_specs=pl.BlockSpec((tm, tn), lambda i, j: (i, j)),
        out_shape=jax.ShapeDtypeStruct((m, n), out_dtype),
        compiler_params=_params("parallel", "parallel"), name="in_proj",
    )(a, b)


def _normmm_kernel(x_ref, g_ref, w_ref, o_ref):
    x = x_ref[...].astype(F32)
    y = x * lax.rsqrt(jnp.mean(x * x, axis=-1, keepdims=True) + EPS) * g_ref[...]
    o_ref[...] = _dot(y.astype(BF16), w_ref[...]).astype(o_ref.dtype)


def norm_matmul(x, col_off, g, w):
    r = x.shape[0]
    k, n = w.shape
    tm = _pick_tile(r, 768, 8)
    cb = col_off // k
    return pl.pallas_call(
        _normmm_kernel, grid=(r // tm,),
        in_specs=[pl.BlockSpec((tm, k), lambda i: (i, cb)),
                  pl.BlockSpec((1, k), lambda i: (0, 0)),
                  pl.BlockSpec((k, n), lambda i: (0, 0))],
        out_specs=pl.BlockSpec((tm, n), lambda i: (i, 0)),
        out_shape=jax.ShapeDtypeStruct((r, n), F32),
        compiler_params=_params("parallel"), name="norm_matmul",
    )(x, g.reshape(1, k), w)


def _prep_kernel(*refs, mode, n_real, scale):
    x_ref, g_ref = refs[0], refs[1]
    o_ref = refs[-1]
    x = x_ref[...].astype(F32)
    ms = jnp.sum(x * x, axis=-1, keepdims=True) * (1.0 / n_real)
    xn = x * lax.rsqrt(ms + EPS) * g_ref[...]
    if mode == "none":
        y = xn
    elif mode == "half64":
        y = xn * refs[2][...] + pltpu.roll(xn, 64, 1) * refs[3][...]
    else:
        y = xn * refs[2][...] + pltpu.roll(xn, 96, 1) * refs[3][...] + pltpu.roll(xn, 32, 1) * refs[4][...]
    o_ref[...] = (y * scale).astype(o_ref.dtype).reshape(o_ref.shape)


def head_prep(x, col_off, n_heads, g, tables, mode, n_real, scale, head_major=False):
    r = x.shape[0]
    tm = _pick_tile(r, 1536, 8)
    cb = col_off // LANE
    tab_spec = pl.BlockSpec((tm, LANE), lambda i, h: (i, 0))
    if head_major:
        out_spec = pl.BlockSpec((1, tm, LANE), lambda i, h: (h, i, 0))
        out_shape = jax.ShapeDtypeStruct((n_heads, r, LANE), BF16)
    else:
        out_spec = pl.BlockSpec((tm, LANE), lambda i, h: (i, h))
        out_shape = jax.ShapeDtypeStruct((r, n_heads * LANE), BF16)
    return pl.pallas_call(
        functools.partial(_prep_kernel, mode=mode, n_real=n_real, scale=scale),
        grid=(r // tm, n_heads),
        in_specs=[pl.BlockSpec((tm, LANE), lambda i, h: (i, cb + h)),
                  pl.BlockSpec((1, LANE), lambda i, h: (0, 0))] + [tab_spec] * len(tables),
        out_specs=out_spec, out_shape=out_shape,
        compiler_params=_params("parallel", "parallel"), name="head_prep_" + mode,
    )(x, g.reshape(1, LANE), *tables)


def _vt_kernel(x_ref, o_ref):
    o_ref[...] = x_ref[...].astype(F32).T.astype(o_ref.dtype)


def transpose_heads(x, col_off, n_heads):
    r = x.shape[0]
    tm = _pick_tile(r, 1536, LANE)
    cb = col_off // LANE
    return pl.pallas_call(
        _vt_kernel, grid=(r // tm, n_heads),
        in_specs=[pl.BlockSpec((tm, LANE), lambda i, h: (i, cb + h))],
        out_specs=pl.BlockSpec((LANE, tm), lambda i, h: (h, i)),
        out_shape=jax.ShapeDtypeStruct((n_heads * LANE, r), BF16),
        compiler_params=_params("parallel", "parallel"), name="transpose_heads",
    )(x)


def _flash_kernel(*refs, groups, tq, tk, dv, n_kv, window, ctx_blocks, n_lat, has_qk2, has_sink, chunks):
    q_ref, k_ref, vt_ref = refs[0], refs[1], refs[2]
    pos = 3
    if has_qk2:
        q2_ref, k2_ref = refs[3], refs[4]
        pos = 5
    if has_sink:
        sink_ref = refs[pos]
        pos += 1
    o_ref, m_ref, l_ref, acc_ref = refs[pos:pos + 4]
    i, j = pl.program_id(1), pl.program_id(2)
    n = groups * tq

    @pl.when(j == 0)
    def _():
        m_ref[...] = jnp.full(m_ref.shape, NEG_INF, F32)
        l_ref[...] = jnp.zeros(l_ref.shape, F32)
        acc_ref[...] = jnp.zeros(acc_ref.shape, F32)

    q = q_ref[...].reshape(n, q_ref.shape[2])
    k = k_ref[...]
    if has_qk2:
        q = jnp.concatenate([q, q2_ref[...].reshape(n, LANE)], axis=1)
        k = jnp.concatenate([k, k2_ref[...]], axis=1)
    vt = vt_ref[...]
    cw = n // chunks
    for c in range(chunks):
        cs = slice(c * cw, (c + 1) * cw)
        s = lax.dot_general(k, q[cs], (((1,), (1,)), ((), ())), preferred_element_type=F32)
        if window:
            qpos = i * tq + (c * cw + lax.broadcasted_iota(jnp.int32, (1, cw), 1)) % tq
            kpos = (i + j - ctx_blocks - 1) * tk + lax.broadcasted_iota(jnp.int32, (tk, 1), 0)
            valid = (jnp.abs(qpos - kpos) <= WINDOW) & (kpos >= 0) & (kpos < n_lat)
            s = jnp.where(valid | (j < ctx_blocks), s, NEG_INF)
        m_prev = m_ref[:, cs]
        m_new = jnp.maximum(m_prev, jnp.max(s, axis=0, keepdims=True))
        alpha = jnp.exp2(m_prev - m_new)
        p = jnp.exp2(s - m_new)
        l_ref[:, cs] = alpha * l_ref[:, cs] + jnp.sum(p, axis=0, keepdims=True)
        acc_ref[:, cs] = alpha * acc_ref[:, cs] + _dot(vt, p.astype(BF16))
        m_ref[:, cs] = m_new

    @pl.when(j == n_kv - 1)
    def _():
        m, l, acc = m_ref[...], l_ref[...], acc_ref[...]
        if has_sink:
            sk = sink_ref[0]
            m_fin = jnp.maximum(m, sk)
            a = jnp.exp2(m - m_fin)
            l = l * a + jnp.exp2(sk - m_fin)
            acc = acc * a
        o_t = acc / l
        for g in range(groups):
            o_ref[:, g * dv:(g + 1) * dv] = o_t[:, g * tq:(g + 1) * tq].T.astype(o_ref.dtype)


Q_CHUNK = 1024


def flash(q, k, vt, *, kv_heads, groups, n_q, q_row0, kv_row0, n_kv_rows, tq, tk, dv=HEAD_DIM,
          q2=None, k2=None, sink=None, window=False, ctx_rows=0):
    dk = q.shape[2]
    qb0 = q_row0 // tq
    kb0 = kv_row0 // tk
    if window:
        ctx_blocks = ctx_rows // tk
        n_lat_blocks = n_kv_rows // tk
        n_kv = ctx_blocks + 3

        def kv_row(i, j):
            return jnp.where(j < ctx_blocks, n_lat_blocks + j, jnp.clip(i + j - ctx_blocks - 1, 0, n_lat_blocks - 1))
    else:
        ctx_blocks = 0
        n_kv = n_kv_rows // tk

        def kv_row(i, j):
            return kb0 + j
    n = groups * tq
    in_specs = [pl.BlockSpec((groups, tq, dk), lambda h, i, j: (h, qb0 + i, 0)),
                pl.BlockSpec((tk, dk), lambda h, i, j: (kv_row(i, j), h)),
                pl.BlockSpec((dv, tk), lambda h, i, j: (h, kv_row(i, j)))]
    args = [q, k, vt]
    if q2 is not None:
        in_specs += [pl.BlockSpec((groups, tq, LANE), lambda h, i, j: (h, qb0 + i, 0)),
                     pl.BlockSpec((tk, LANE), lambda h, i, j: (kv_row(i, j), 0))]
        args += [q2, k2]
    if sink is not None:
        in_specs.append(pl.BlockSpec((1, 1, n), lambda h, i, j: (h, 0, 0)))
        sink2 = sink.astype(F32).reshape(kv_heads, groups, 1) * LOG2E
        args.append(jnp.broadcast_to(sink2, (kv_heads, groups, tq)).reshape(kv_heads, 1, n))
    kern = functools.partial(
        _flash_kernel, groups=groups, tq=tq, tk=tk, dv=dv, n_kv=n_kv, window=window, ctx_blocks=ctx_blocks,
        n_lat=n_kv_rows, has_qk2=q2 is not None, has_sink=sink is not None,
        chunks=max(1, n // Q_CHUNK))
    return pl.pallas_call(
        kern, grid=(kv_heads, n_q // tq, n_kv), in_specs=in_specs,
        out_specs=pl.BlockSpec((tq, groups * dv), lambda h, i, j: (i, h)),
        out_shape=jax.ShapeDtypeStruct((n_q, kv_heads * groups * dv), BF16),
        scratch_shapes=[pltpu.VMEM((1, n), F32), pltpu.VMEM((1, n), F32), pltpu.VMEM((dv, n), F32)],
        compiler_params=_params("parallel", "parallel", "arbitrary"),
        name="flash_window" if window else "flash_dense",
    )(*args)


def _outproj_kernel(o1c_ref, o1l_ref, o2c_ref, o2l_ref, gate_ref, h_ref, mod_ref, w_ref, out_ref, *, lat_tiles):
    is_ctx = pl.program_id(0) >= lat_tiles
    w1 = o1l_ref.shape[1]
    d = h_ref.shape[1]
    gt = gate_ref[...].astype(F32)
    sg = gt * jax.nn.sigmoid(gt)
    o1 = jnp.where(is_ctx, o1c_ref[...].astype(F32), o1l_ref[...].astype(F32))
    o2 = jnp.where(is_ctx, o2c_ref[...].astype(F32), o2l_ref[...].astype(F32))
    t1 = (o1 * sg[:, :w1]).astype(BF16)
    t2 = (o2 * sg[:, w1:]).astype(BF16)
    y = _dot(t1, w_ref[0:w1, :]) + _dot(t2, w_ref[w1:, :])
    gm = jnp.where(is_ctx, mod_ref[1:2, 2 * d:3 * d], mod_ref[0:1, 2 * d:3 * d])
    out_ref[...] = h_ref[...] + gm * y


def out_proj(o1c, o1l, o2c, o2l, proj, h, mod, w, latents_only):
    r, d = h.shape
    tm = ROW_TILE
    assert o1c.shape[0] == tm, "the context rows must be exactly one row tile"
    lt = o1l.shape[0] // tm
    w1, w2 = o1l.shape[1], o2l.shape[1]
    lat = lambda i: (jnp.minimum(i, lt - 1), 0)
    if latents_only:
        r = lt * tm
    return pl.pallas_call(
        functools.partial(_outproj_kernel, lat_tiles=lt), grid=(r // tm,),
        in_specs=[pl.BlockSpec((tm, w1), lambda i: (0, 0)), pl.BlockSpec((tm, w1), lat),
                  pl.BlockSpec((tm, w2), lambda i: (0, 0)), pl.BlockSpec((tm, w2), lat),
                  pl.BlockSpec((tm, w1 + w2), lambda i: (i, OFF_GATE // (w1 + w2))),
                  pl.BlockSpec((tm, d), lambda i: (i, 0)),
                  pl.BlockSpec(mod.shape, lambda i: (0, 0)),
                  pl.BlockSpec(w.shape, lambda i: (0, 0))],
        out_specs=pl.BlockSpec((tm, d), lambda i: (i, 0)),
        out_shape=jax.ShapeDtypeStruct((r, d), F32),
        compiler_params=_params("parallel"), name="out_proj",
    )(o1c, o1l, o2c, o2l, proj, h, mod, w)


def _conv3_kernel(x_ref, w_ref, b_ref, oc_ref, ol_ref, *, n_ctx):
    x = x_ref[...].astype(F32)
    r = x.shape[0]
    n_lat = r - n_ctx
    row = lax.broadcasted_iota(jnp.int32, (r, 1), 0)
    prev = jnp.where((row == 0) | (row == n_lat), 0.0, pltpu.roll(x, 1, 0))
    nxt = jnp.where((row == n_lat - 1) | (row == r - 1), 0.0, pltpu.roll(x, r - 1, 0))
    y = prev * w_ref[0:1, :] + x * w_ref[1:2, :] + nxt * w_ref[2:3, :] + b_ref[...]
    ol_ref[0] = y[:n_lat]
    oc_ref[0] = y[n_lat:]


def conv3(proj, col_off, w, b, n_ctx):
    r = proj.shape[0]
    cw = w.shape[1] // 3
    nb = cw // LANE
    cb = col_off // LANE
    return pl.pallas_call(
        functools.partial(_conv3_kernel, n_ctx=n_ctx), grid=(3, nb),
        in_specs=[pl.BlockSpec((r, LANE), lambda p, c: (0, cb + p * nb + c)),
                  pl.BlockSpec((3, LANE), lambda p, c: (0, p * nb + c)),
                  pl.BlockSpec((1, LANE), lambda p, c: (0, p * nb + c))],
        out_specs=[pl.BlockSpec((1, n_ctx, LANE), lambda p, c: (p, 0, c)),
                   pl.BlockSpec((1, r - n_ctx, LANE), lambda p, c: (p, 0, c))],
        out_shape=[jax.ShapeDtypeStruct((3, n_ctx, cw), F32), jax.ShapeDtypeStruct((3, r - n_ctx, cw), F32)],
        compiler_params=_params("parallel", "parallel"), name="conv3",
    )(proj, w, b.reshape(1, -1))


def _filt_kernel(z_ref, w1_ref, b1_ref, f1_ref, w2_ref, b2_ref, f2_ref, w3_ref, dl_ref, h_ref, tap_ref, nrm_ref, *,
                 tm, cw):
    i = pl.program_id(0)
    z = z_ref[...]

    def mm(a, w_r):
        return _dot3(*_split_bf16(a), *_split_bf16(w_r[...]))

    a = jnp.sin(f1_ref[...] * (mm(z, w1_ref) + b1_ref[...]))
    a = jnp.sin(f2_ref[...] * (mm(a, w2_ref) + b2_ref[...]))
    dec = jnp.exp(-z[:, 0:1] * dl_ref[...])
    h = mm(a, w3_ref) * jnp.concatenate([dec] * (h_ref.shape[1] // cw), axis=1)
    h_ref[...] = h.astype(h_ref.dtype)
    row = i * tm + lax.broadcasted_iota(jnp.int32, (tm, 1), 0)
    col = lax.broadcasted_iota(jnp.int32, (1, h.shape[1]), 1)
    skip = (row == 0) & ((col // cw) % 2 == 1)
    part = jnp.sum(jnp.where(skip, 0.0, jnp.abs(h)), axis=0, keepdims=True)

    @pl.when(i == 0)
    def _():
        nrm_ref[...] = jnp.zeros(nrm_ref.shape, F32)
        tap_ref[...] = h[0:SUBLANES]

    nrm_ref[...] += jnp.broadcast_to(part, nrm_ref.shape)


def hyena_filter(z, w1p, b1, f1, w2, b2, f2, w3, deltas):
    l = z.shape[0]
    n = w3.shape[1]
    cw = deltas.shape[0]
    hid = w2.shape[0]
    tm = min(l, 256)
    full = lambda a: pl.BlockSpec(a.shape, lambda i: (0, 0))
    ins = [w1p, b1.reshape(1, hid), f1.reshape(1, hid), w2, b2.reshape(1, hid), f2.reshape(1, hid), w3,
           deltas.reshape(1, cw)]
    small = pl.BlockSpec((SUBLANES, n), lambda i: (0, 0))
    return pl.pallas_call(
        functools.partial(_filt_kernel, tm=tm, cw=cw), grid=(l // tm,),
        in_specs=[pl.BlockSpec((tm, LANE), lambda i: (i, 0))] + [full(a) for a in ins],
        out_specs=[pl.BlockSpec((tm, n), lambda i: (i, 0)), small, small],
        out_shape=[jax.ShapeDtypeStruct((l, n), BF16), jax.ShapeDtypeStruct((SUBLANES, n), F32),
                   jax.ShapeDtypeStruct((SUBLANES, n), F32)],
        compiler_params=_params("arbitrary"), name="hyena_filter",
    )(z, *ins)


@functools.lru_cache(maxsize=None)
def _dft_consts(a):
    n = a * a
    hh = a // 2
    n1 = np.arange(hh)
    k1 = np.arange(hh)
    n2 = np.arange(a)
    k2 = np.arange(a)
    ang1 = 2 * np.pi * np.outer(k1 + 0.5, n1) / a
    m1 = np.zeros((hh, 2, hh))
    m1[:, 0], m1[:, 1] = np.cos(ang1), -np.sin(ang1)
    m1 = m1.reshape(2 * hh, hh)
    phi = 2 * np.pi * (n2[None, None, :] * (k1[:, None, None] + 0.5) / n + n2[None, None, :] * k2[None, :, None] / a)
    c, s = np.cos(phi), np.sin(phi)
    g = np.zeros((hh, 2, a, 2, a))
    g[:, 0, :, 0, :], g[:, 0, :, 1, :], g[:, 1, :, 0, :], g[:, 1, :, 1, :] = c, s, -s, c
    g = g.reshape(hh, 2 * a, 2 * a)
    gt = np.ascontiguousarray(np.transpose(g, (0, 2, 1)))
    al = 2 * np.pi * np.outer(n1, k1 + 0.5) / a
    mb = np.zeros((hh, hh, 2))
    mb[:, :, 0], mb[:, :, 1] = 2 / n * np.cos(al), -2 / n * np.sin(al)
    mb = mb.reshape(hh, 2 * hh)
    eye = np.eye(ROWS_BF16)
    return tuple(x.astype(np.float32) for x in (np.kron(m1, eye), g, gt, np.kron(mb, eye)))


@functools.lru_cache(maxsize=None)
def _dense_dft_consts(l):
    n = 2 * l
    ang = 2 * np.pi * np.outer(np.arange(l) + 0.5, np.arange(l)) / n
    mf = np.concatenate([np.cos(ang), -np.sin(ang)], axis=0)
    mi = np.concatenate([2 / n * np.cos(ang.T), -2 / n * np.sin(ang.T)], axis=1)
    return tuple(x.astype(np.float32) for x in (mf, mi))


def _stage1_kernel(x_ref, a_ref, o_ref):
    _, hh, t, ct = x_ref.shape
    x = x_ref[0].reshape(hh * t, ct).astype(BF16)
    o_ref[...] = _dot(a_ref[...], x).astype(o_ref.dtype).reshape(o_ref.shape)


def dft_stage1(x4, p, a1):
    _, hh, a, c = x4.shape
    ct = _pick_tile(c, 512)
    t = ROWS_BF16
    return pl.pallas_call(
        _stage1_kernel, grid=(a // t, c // ct),
        in_specs=[pl.BlockSpec((1, hh, t, ct), lambda j, q: (p, 0, j, q)),
                  pl.BlockSpec(a1.shape, lambda j, q: (0, 0))],
        out_specs=pl.BlockSpec((2 * hh, t, ct), lambda j, q: (0, j, q)),
        out_shape=jax.ShapeDtypeStruct((2 * hh, a, c), BF16),
        compiler_params=_params("parallel", "parallel"), name="dft_stage1",
    )(x4, a1)


def _spectrum_product(x, h0, h1, tap0, n0, n1):
    half = x.shape[0] // 2
    inv = 1.0 / (n0 + n1)
    kr = (h0[:half] + h1[:half] - tap0) * inv
    ki = (h0[half:] - h1[half:]) * inv
    xr, xi = x[:half], x[half:]
    return jnp.concatenate([xr * kr - xi * ki, xr * ki + xi * kr], axis=0)


K1_PER_STEP = 2


def _mid_filter_kernel(y0_ref, y1_ref, g_ref, tap_ref, n0_ref, n1_ref, o_ref):
    inv = 1.0 / (n0_ref[0:1, :] + n1_ref[0:1, :])
    tap = tap_ref[0:1, :]
    for b in range(y0_ref.shape[0]):
        x0 = _dot(g_ref[b], y0_ref[b])
        x1 = _dot(g_ref[b], y1_ref[b])
        half = x0.shape[0] // 2
        kr = (x0[:half] + x1[:half] - tap) * inv
        ki = (x0[half:] - x1[half:]) * inv
        o_ref[b] = jnp.concatenate([kr, ki], axis=0).astype(o_ref.dtype)


def filter_spectrum(h, taps, nrm, consts, a):
    a1, g, _, _ = consts
    l, n = h.shape
    c = n // 4
    hh = a // 2
    y = dft_stage1(h.reshape(1, hh, a, n), 0, a1).reshape(hh, 2 * a, n)
    kb = K1_PER_STEP
    slab = lambda side: pl.BlockSpec((kb, 2 * a, c), lambda k, o: (k, 0, 2 * o + side))
    row = lambda side: pl.BlockSpec((SUBLANES, c), lambda k, o: (0, 2 * o + side))
    return pl.pallas_call(
        _mid_filter_kernel, grid=(hh // kb, 2),
        in_specs=[slab(0), slab(1), pl.BlockSpec((kb, 2 * a, 2 * a), lambda k, o: (k, 0, 0)), row(1), row(0), row(1)],
        out_specs=pl.BlockSpec((kb, 2 * a, c), lambda k, o: (k, 0, o)),
        out_shape=jax.ShapeDtypeStruct((hh, 2 * a, 2 * c), BF16),
        compiler_params=_params("parallel", "parallel"), name="dft_mid_filter",
    )(y, y, g, taps, nrm, nrm)


def _mid_conv_kernel(y_ref, g_ref, gt_ref, kf_ref, o_ref):
    for b in range(y_ref.shape[0]):
        x = _dot(g_ref[b], y_ref[b])
        kf = kf_ref[b].astype(F32)
        half = x.shape[0] // 2
        xr, xi, kr, ki = x[:half], x[half:], kf[:half], kf[half:]
        z = jnp.concatenate([xr * kr - xi * ki, xr * ki + xi * kr], axis=0).astype(BF16)
        o_ref[b] = _dot(gt_ref[b], z).astype(o_ref.dtype)


def dft_mid_conv(y, g, gt, kf, order):
    hh, a2, c = y.shape
    kb = K1_PER_STEP
    slab = pl.BlockSpec((kb, a2, c), lambda k: (k, 0, 0))
    mat = pl.BlockSpec((kb, a2, a2), lambda k: (k, 0, 0))
    return pl.pallas_call(
        _mid_conv_kernel, grid=(hh // kb,),
        in_specs=[slab, mat, mat, pl.BlockSpec((kb, a2, c), lambda k: (k, 0, order))],
        out_specs=slab, out_shape=jax.ShapeDtypeStruct((hh, a2, c), BF16),
        compiler_params=_params("parallel"), name="dft_mid_conv",
    )(y, g, gt, kf)


def _last_kernel(b_ref, a_ref, xa_ref, zb_ref, bias_ref, o_ref):
    rows, t, ct = b_ref.shape
    conv = _dot(a_ref[...], b_ref[...].reshape(rows * t, ct)).reshape(rows // 2, t, ct)
    o_ref[0] = (xa_ref[0] * (conv + bias_ref[...] * zb_ref[0])).astype(o_ref.dtype)


def dft_last(b, a2m, xa4, pa, zb4, pb, bias_row):
    rows, a, c = b.shape
    hh = rows // 2
    ct = _pick_tile(c, 512)
    t = ROWS_BF16
    sig = lambda pp: pl.BlockSpec((1, hh, t, ct), lambda j, q: (pp, 0, j, q))
    return pl.pallas_call(
        _last_kernel, grid=(a // t, c // ct),
        in_specs=[pl.BlockSpec((rows, t, ct), lambda j, q: (0, j, q)),
                  pl.BlockSpec(a2m.shape, lambda j, q: (0, 0)),
                  sig(pa), sig(pb), pl.BlockSpec((1, ct), lambda j, q: (0, q))],
        out_specs=sig(0), out_shape=jax.ShapeDtypeStruct((1, hh, a, c), F32),
        compiler_params=_params("parallel", "parallel"), name="dft_last",
    )(b, a2m, xa4, zb4, bias_row)


def long_conv_gated(u3, kf, bias, consts, a):
    a1, g, gt, a2m = consts
    _, l, c = u3.shape
    hh = a // 2
    u4 = u3.reshape(3, hh, a, c)

    def conv_gate(src4, p_src, xa_p, order):
        y = dft_stage1(src4, p_src, a1).reshape(hh, 2 * a, c)
        bb = dft_mid_conv(y, g, gt, kf, order).reshape(2 * hh, a, c)
        return dft_last(bb, a2m, u4, xa_p, src4, p_src, bias[order].reshape(1, c))

    z2 = conv_gate(u4, 2, 0, 0)
    return conv_gate(z2, 0, 1, 1).reshape(l, c)


def _ctx_spec_kernel(h_ref, mf_ref, o_ref):
    o_ref[...] = _dot(mf_ref[...], h_ref[...])


def ctx_filter_spectrum(h, mf):
    l, n = h.shape
    tn = _pick_tile(n, 1024)
    return pl.pallas_call(
        _ctx_spec_kernel, grid=(n // tn,),
        in_specs=[pl.BlockSpec((l, tn), lambda j: (0, j)), pl.BlockSpec(mf.shape, lambda j: (0, 0))],
        out_specs=pl.BlockSpec((2 * l, tn), lambda j: (0, j)),
        out_shape=jax.ShapeDtypeStruct((2 * l, n), F32),
        compiler_params=_params("parallel"), name="ctx_filter_spectrum",
    )(h, mf)


def _ctx_hyena_kernel(u_ref, mf_ref, mi_ref, hf0a, hf1a, hf0b, hf1b, tap0, tap1, n0a, n1a, n0b, n1b, bias_ref, o_ref):
    x1, x2, z = u_ref[0], u_ref[1], u_ref[2]

    def conv(sig, h0_ref, h1_ref, tap_ref, n0_ref, n1_ref):
        x = _dot(mf_ref[...], sig.astype(BF16))
        zz = _spectrum_product(x, h0_ref[...], h1_ref[...], tap_ref[0:1, :], n0_ref[0:1, :], n1_ref[0:1, :])
        return _dot(mi_ref[...], zz.astype(BF16))

    z2 = x1 * (conv(z, hf0a, hf1a, tap0, n0a, n1a) + bias_ref[0:1, :] * z)
    o_ref[...] = (x2 * (conv(z2, hf0b, hf1b, tap1, n0b, n1b) + bias_ref[1:2, :] * z2)).astype(o_ref.dtype)


def ctx_hyena(u3, hf, taps, nrm, bias, mf, mi):
    _, l, c = u3.shape
    tc = _pick_tile(c, 512)
    nb = c // tc
    const = lambda a: pl.BlockSpec(a.shape, lambda j: (0, 0))
    col = lambda rows, q: pl.BlockSpec((rows, tc), lambda j: (0, q * nb + j))
    mats = [mf, mi]
    return pl.pallas_call(
        _ctx_hyena_kernel, grid=(nb,),
        in_specs=[pl.BlockSpec((3, l, tc), lambda j: (0, 0, j))] + [const(m) for m in mats]
        + [col(2 * l, 0), col(2 * l, 1), col(2 * l, 2), col(2 * l, 3), col(8, 1), col(8, 3),
           col(8, 0), col(8, 1), col(8, 2), col(8, 3), pl.BlockSpec((2, tc), lambda j: (0, j))],
        out_specs=pl.BlockSpec((l, tc), lambda j: (0, j)),
        out_shape=jax.ShapeDtypeStruct((l, c), BF16),
        compiler_params=_params("parallel"), name="ctx_hyena",
    )(u3, *mats, hf, hf, hf, hf, taps, taps, nrm, nrm, nrm, nrm, bias)


def _axial_tables(n_tokens, n_rot):
    rows = n_tokens // GRID_W
    row = jnp.broadcast_to(jnp.arange(rows)[:, None], (rows, GRID_W)).reshape(-1).astype(F32)
    col = jnp.broadcast_to(jnp.arange(GRID_W)[None, :], (rows, GRID_W)).reshape(-1).astype(F32)
    n_freq = n_rot // 4
    inv = ROPE_THETA ** (-jnp.arange(n_freq, dtype=F32) / n_freq)
    ang = jnp.concatenate([row[:, None] * inv, col[:, None] * inv], axis=-1)
    return jnp.cos(ang), jnp.sin(ang)


def _rope_tables_full(s, n_ctx):
    cos, sin = _axial_tables(s, HEAD_DIM)
    t0 = jnp.concatenate([jnp.concatenate([cos, cos], axis=1), jnp.ones((n_ctx, LANE), F32)], axis=0)
    t1 = jnp.concatenate([jnp.concatenate([-sin, sin], axis=1), jnp.zeros((n_ctx, LANE), F32)], axis=0)
    return t0, t1


def _rope_tables_half(s, n_ctx):
    cos, sin = _axial_tables(s, ROPE_DIM)
    q = ROPE_DIM // 2
    zq = jnp.zeros((s, q), F32)
    z2 = jnp.zeros((s, LANE - ROPE_DIM), F32)
    t0 = jnp.concatenate([cos, cos, z2], axis=1)
    t1 = jnp.concatenate([-sin, zq, z2], axis=1)
    t2 = jnp.concatenate([zq, sin, z2], axis=1)
    c0 = jnp.concatenate([jnp.ones((n_ctx, ROPE_DIM), F32), jnp.zeros((n_ctx, LANE - ROPE_DIM), F32)], axis=1)
    cz = jnp.zeros((n_ctx, LANE), F32)
    return jnp.concatenate([t0, c0], axis=0), jnp.concatenate([t1, cz], axis=0), jnp.concatenate([t2, cz], axis=0)


def _filter_features(l):
    pos = jnp.arange(l, dtype=F32)
    t = pos / max(l - 1, 1)
    bands = jnp.linspace(1e-4, HYENA_BANDS - 1, HYENA_BANDS, dtype=F32)
    ang = (2.0 * math.pi / l) * pos[:, None] * bands[None, :]
    z = jnp.concatenate([t[:, None], jnp.cos(ang), -jnp.sin(ang)], axis=-1)
    return jnp.pad(z, ((0, 0), (0, LANE - HYENA_EMB)))


Q_STACK = 1024
KV_TILE_CAP = 1408


def _attention_pair(qp, kp, vt, s, n_ctx, *, kv_heads, groups, window, sink=None, q2=None, k2=None):
    common = dict(kv_heads=kv_heads, groups=groups, q2=q2, k2=k2, sink=sink)
    o_ctx = flash(qp, kp, vt, n_q=n_ctx, q_row0=s, kv_row0=s, n_kv_rows=n_ctx, tq=n_ctx, tk=n_ctx, **common)
    if window:
        o_lat = flash(qp, kp, vt, n_q=s, q_row0=0, kv_row0=0, n_kv_rows=s, tq=ROW_TILE, tk=ROW_TILE, window=True,
                      ctx_rows=n_ctx, **common)
    else:
        o_lat = flash(qp, kp, vt, n_q=s, q_row0=0, kv_row0=0, n_kv_rows=n_ctx + s, tq=Q_STACK // groups,
                      tk=_pick_tile(n_ctx + s, KV_TILE_CAP), **common)
    return o_ctx, o_lat


def _even_layer(u, s, n_ctx, w_in, qn_g, kn_g, sink, conv_w, conv_b, fparams, hy_bias, rope_full, feats, dft):
    wq, wk, wv, whd, wg = jnp.split(w_in, np.cumsum([ATT_WIDTH, KV_WIDTH, KV_WIDTH, 3 * B_WIDTH])[:].tolist(), axis=1)
    proj = matmul(u, jnp.concatenate([wg, wq, wk, wv, whd], axis=1).astype(BF16), F32)
    qp = head_prep(proj, OFF_Q, A_HEADS, qn_g, rope_full, "half64", HEAD_DIM, HEAD_DIM ** -0.5 * LOG2E, True)
    kp = head_prep(proj, OFF_K, A_KV_HEADS, kn_g, rope_full, "half64", HEAD_DIM, 1.0)
    vt = transpose_heads(proj, OFF_V, A_KV_HEADS)
    a_ctx, a_lat = _attention_pair(qp, kp, vt, s, n_ctx, kv_heads=A_KV_HEADS, groups=A_HEADS // A_KV_HEADS,
                                   window=True, sink=sink)
    w1, b1, f1, w2, b2, f2, w3 = fparams
    w1p = jnp.pad(w1, ((0, LANE - HYENA_EMB), (0, 0)))
    deltas = jnp.linspace(DECAY_MAX, DECAY_MIN, B_WIDTH, dtype=F32)
    uc3, ul3 = conv3(proj, OFF_REST, conv_w, conv_b, n_ctx)
    a = int(round(math.sqrt(2 * s)))
    consts, (mf, mi) = dft
    h_lat, tap_lat, nrm_lat = hyena_filter(feats[0], w1p, b1, f1, w2, b2, f2, w3, deltas)
    kf_lat = filter_spectrum(h_lat, tap_lat, nrm_lat, consts, a)
    b_lat = long_conv_gated(ul3, kf_lat, hy_bias, consts, a)
    h_ctx, tap_ctx, nrm_ctx = hyena_filter(feats[1], w1p, b1, f1, w2, b2, f2, w3, deltas)
    hf_ctx = ctx_filter_spectrum(h_ctx, mf)
    b_ctx = ctx_hyena(uc3, hf_ctx, tap_ctx, nrm_ctx, hy_bias, mf, mi)
    return proj, a_ctx, a_lat, b_ctx, b_lat


def _odd_layer(u, s, n_ctx, w_in, qn_g, kn_g, cq_g, ckv_g, wuq, wukv, mq_g, mk_g, rope_full, rope_half):
    d = w_in.shape[0]
    parts = jnp.split(w_in, np.cumsum([ATT_WIDTH, KV_WIDTH, KV_WIDTH, Q_LORA, KV_LORA, ROPE_DIM]).tolist(), axis=1)
    wq, wk, wv, wmq, wmkv, wmkr, wg = parts
    w_perm = jnp.concatenate([wg, wq, wk, wv, wmq, wmkv, wmkr, jnp.zeros((d, LANE - ROPE_DIM), w_in.dtype)], axis=1)
    proj = matmul(u, w_perm.astype(BF16), F32)
    qp = head_prep(proj, OFF_Q, C_HEADS, qn_g, rope_full, "half64", HEAD_DIM, HEAD_DIM ** -0.5 * LOG2E, True)
    kp = head_prep(proj, OFF_K, C_KV_HEADS, kn_g, rope_full, "half64", HEAD_DIM, 1.0)
    vt = transpose_heads(proj, OFF_V, C_KV_HEADS)
    c_ctx, c_lat = _attention_pair(qp, kp, vt, s, n_ctx, kv_heads=C_KV_HEADS, groups=C_HEADS // C_KV_HEADS,
                                   window=False)
    pad_r = LANE - ROPE_DIM
    wuq3 = wuq.reshape(Q_LORA, M_HEADS, QK_DIM)
    wuq_p = jnp.concatenate([wuq3[:, :, :NOPE_DIM].reshape(Q_LORA, -1),
                             jnp.pad(wuq3[:, :, NOPE_DIM:], ((0, 0), (0, 0), (0, pad_r))).reshape(Q_LORA, -1)], axis=1)
    wukv3 = wukv.reshape(KV_LORA, M_HEADS, NOPE_DIM + V_DIM)
    wukv_p = jnp.concatenate([wukv3[:, :, :NOPE_DIM].reshape(KV_LORA, -1),
                              wukv3[:, :, NOPE_DIM:].reshape(KV_LORA, -1)], axis=1)
    q_raw = norm_matmul(proj, OFF_MQ, cq_g, wuq_p.astype(BF16))
    kv_raw = norm_matmul(proj, OFF_MKV, ckv_g, wukv_p.astype(BF16))
    scale = QK_DIM ** -0.5 * LOG2E
    gq_r = jnp.pad(mq_g[NOPE_DIM:], (0, pad_r))
    gk_r = jnp.pad(mk_g[NOPE_DIM:], (0, pad_r))
    qn = head_prep(q_raw, 0, M_HEADS, mq_g[:NOPE_DIM], (), "none", NOPE_DIM, scale, True)
    qr = head_prep(q_raw, M_HEADS * NOPE_DIM, M_HEADS, gq_r, rope_half, "half32", ROPE_DIM, scale, True)
    kn = head_prep(kv_raw, 0, M_HEADS, mk_g[:NOPE_DIM], (), "none", NOPE_DIM, 1.0)
    kr = head_prep(proj, OFF_MKR, 1, gk_r, rope_half, "half32", ROPE_DIM, 1.0)
    vmt = transpose_heads(kv_raw, M_HEADS * NOPE_DIM, M_HEADS)
    d_ctx, d_lat = _attention_pair(qn, kn, vmt, s, n_ctx, kv_heads=M_HEADS, groups=1, window=False, q2=qr, k2=kr)
    return proj, c_ctx, c_lat, d_ctx, d_lat


def kernel(x, c, ctx, c_ctx, ada_w, ada_b, norm_g, w_out, ev_w_in, ev_qn_g, ev_kn_g, ev_sink, ev_conv_w, ev_conv_b, hy_w1, hy_b1, hy_f1, hy_w2, hy_b2, hy_f2, hy_w3, hy_bias, od_w_in, od_qn_g, od_kn_g, od_cq_g, od_ckv_g, od_wuq, od_wukv, od_mq_g, od_mk_g):
    _, s, d = x.shape
    n_ctx = ctx.shape[1]
    depth = ada_w.shape[0]
    h = jnp.concatenate([x[0], ctx[0]], axis=0)
    cc = jnp.zeros((8, d), F32).at[0].set(c[0]).at[1].set(c_ctx)
    mod = ada_mod(cc, ada_w, ada_b)
    rope_full = _rope_tables_full(s, n_ctx)
    rope_half = _rope_tables_half(s, n_ctx)
    feats = (_filter_features(s), _filter_features(n_ctx))
    to_bf16 = lambda mats: tuple(jnp.asarray(m).astype(BF16) for m in mats)
    dft = (to_bf16(_dft_consts(int(round(math.sqrt(2 * s))))), to_bf16(_dense_dft_consts(n_ctx)))
    for i in range(depth):
        u = norm_mod(h, norm_g[i], mod[i], n_ctx)
        if i % 2 == 0:
            e = i // 2
            proj, o1c, o1l, o2c, o2l = _even_layer(
                u, s, n_ctx, ev_w_in[e], ev_qn_g[e], ev_kn_g[e], ev_sink[e], ev_conv_w[e], ev_conv_b[e],
                (hy_w1[e], hy_b1[e], hy_f1[e], hy_w2[e], hy_b2[e], hy_f2[e], hy_w3[e]), hy_bias[e], rope_full, feats,
                dft)
        else:
            o = i // 2
            proj, o1c, o1l, o2c, o2l = _odd_layer(
                u, s, n_ctx, od_w_in[o], od_qn_g[o], od_kn_g[o], od_cq_g[o], od_ckv_g[o], od_wuq[o], od_wukv[o],
                od_mq_g[o], od_mk_g[o], rope_full, rope_half)
        h = out_proj(o1c, o1l, o2c, o2l, proj, h, mod[i], w_out[i].astype(BF16), latents_only=i == depth - 1)
    return h[None]
```

```python
import functools
import math

import numpy as np
import jax
import jax.numpy as jnp
from jax import lax
from jax.experimental import pallas as pl
from jax.experimental.pallas import tpu as pltpu

F32 = jnp.float32
BF16 = jnp.bfloat16

GRID_W = 64
HEAD_DIM = 128
ROPE_THETA = 10000.0
EPS = 1e-6
NEG_INF = -1e30
A_HEADS, A_KV_HEADS, WINDOW = 8, 2, 128
B_WIDTH = 1024
HYENA_BANDS = 16
HYENA_EMB = 1 + 2 * HYENA_BANDS
DECAY_TARGET = 1e-2
DECAY_MAX = abs(math.log(DECAY_TARGET)) / 0.3
DECAY_MIN = abs(math.log(DECAY_TARGET)) / 1.5
C_HEADS, C_KV_HEADS = 8, 2
M_HEADS, Q_LORA, KV_LORA, NOPE_DIM, ROPE_DIM, V_DIM = 8, 512, 256, 128, 64, 128
QK_DIM = NOPE_DIM + ROPE_DIM
ATT_WIDTH = A_HEADS * HEAD_DIM
KV_WIDTH = A_KV_HEADS * HEAD_DIM
BRANCH = ATT_WIDTH + B_WIDTH

LOG2E = math.log2(math.e)
LANE = 128
SUBLANES = 8
ROWS_BF16 = 16
ROW_TILE = 256
VMEM_LIMIT = 48 * 1024 * 1024

OFF_GATE, OFF_Q, OFF_K, OFF_V, OFF_REST = 0, BRANCH, BRANCH + ATT_WIDTH, BRANCH + ATT_WIDTH + KV_WIDTH, BRANCH + ATT_WIDTH + 2 * KV_WIDTH
OFF_MQ, OFF_MKV, OFF_MKR = OFF_REST, OFF_REST + Q_LORA, OFF_REST + Q_LORA + KV_LORA


def _params(*sem):
    return pltpu.CompilerParams(dimension_semantics=sem, vmem_limit_bytes=VMEM_LIMIT)


def _split_bf16(x):
    hi = x.astype(BF16)
    return hi, (x - hi.astype(F32)).astype(BF16)


def _dot(a, b):
    return jnp.dot(a, b, preferred_element_type=F32)


def _dot3(a_hi, a_lo, b_hi, b_lo):
    return _dot(a_hi, b_hi) + (_dot(a_hi, b_lo) + _dot(a_lo, b_hi))


def _ada_kernel(c_ref, w_ref, b_ref, o_ref):
    c = c_ref[...]
    a = (c * jax.nn.sigmoid(c)).astype(BF16)
    o_ref[0] = _dot(a, w_ref[0].astype(BF16)) + b_ref[0]


def ada_mod(cc, ada_w, ada_b):
    depth, d, n = ada_w.shape
    tn = _pick_tile(n, 1024)
    return pl.pallas_call(
        _ada_kernel, grid=(depth, n // tn),
        in_specs=[pl.BlockSpec((8, d), lambda l, j: (0, 0)),
                  pl.BlockSpec((1, d, tn), lambda l, j: (l, 0, j)),
                  pl.BlockSpec((1, 1, tn), lambda l, j: (l, 0, j))],
        out_specs=pl.BlockSpec((1, 8, tn), lambda l, j: (l, 0, j)),
        out_shape=jax.ShapeDtypeStruct((depth, 8, n), F32),
        compiler_params=_params("parallel", "parallel"), name="ada_mod",
    )(cc, ada_w, ada_b.reshape(depth, 1, n))


def _norm_kernel(x_ref, g_ref, mod_ref, o_ref, *, lat_tiles):
    x = x_ref[...]
    d = x.shape[1]
    y = x * lax.rsqrt(jnp.mean(x * x, axis=-1, keepdims=True) + EPS) * g_ref[...]
    is_ctx = pl.program_id(0) >= lat_tiles
    sh = jnp.where(is_ctx, mod_ref[1:2, 0:d], mod_ref[0:1, 0:d])
    sc = jnp.where(is_ctx, mod_ref[1:2, d:2 * d], mod_ref[0:1, d:2 * d])
    o_ref[...] = (y * (1.0 + sc) + sh).astype(o_ref.dtype)


def norm_mod(h, g, mod, n_ctx):
    r, d = h.shape
    tm = ROW_TILE
    return pl.pallas_call(
        functools.partial(_norm_kernel, lat_tiles=(r - n_ctx) // tm), grid=(r // tm,),
        in_specs=[pl.BlockSpec((tm, d), lambda i: (i, 0)),
                  pl.BlockSpec((1, d), lambda i: (0, 0)),
                  pl.BlockSpec(mod.shape, lambda i: (0, 0))],
        out_specs=pl.BlockSpec((tm, d), lambda i: (i, 0)),
        out_shape=jax.ShapeDtypeStruct((r, d), BF16),
        compiler_params=_params("parallel"), name="norm_mod",
    )(h, g.reshape(1, d), mod)


def _mm_kernel(a_ref, b_ref, o_ref):
    o_ref[...] = _dot(a_ref[...], b_ref[...]).astype(o_ref.dtype)


def _pick_tile(n, cap, unit=LANE):
    best = unit
    for t in range(unit, cap + 1, unit):
        if n % t == 0:
            best = t
    return best


def matmul(a, b, out_dtype):
    m, k = a.shape
    n = b.shape[1]
    tm = _pick_tile(m, 1536, 8)
    tn = _pick_tile(n, 1024)
    return pl.pallas_call(
        _mm_kernel, grid=(m // tm, n // tn),
        in_specs=[pl.BlockSpec((tm, k), lambda i, j: (i, 0)),
                  pl.BlockSpec((k, tn), lambda i, j: (0, j))],
        out_specs=pl.BlockSpec((tm, tn), lambda i, j: (i, j)),
        out_shape=jax.ShapeDtypeStruct((m, n), out_dtype),
        compiler_params=_params("parallel", "parallel"), name="in_proj",
    )(a, b)


def _normmm_kernel(x_ref, g_ref, w_ref, o_ref):
    x = x_ref[...].astype(F32)
    y = x * lax.rsqrt(jnp.mean(x * x, axis=-1, keepdims=True) + EPS) * g_ref[...]
    o_ref[...] = _dot(y.astype(BF16), w_ref[...]).astype(o_ref.dtype)


def norm_matmul(x, col_off, g, w):
    r = x.shape[0]
    k, n = w.shape
    tm = _pick_tile(r, 768, 8)
    cb = col_off // k
    return pl.pallas_call(
        _normmm_kernel, grid=(r // tm,),
        in_specs=[pl.BlockSpec((tm, k), lambda i: (i, cb)),
                  pl.BlockSpec((1, k), lambda i: (0, 0)),
                  pl.BlockSpec((k, n), lambda i: (0, 0))],
        out_specs=pl.BlockSpec((tm, n), lambda i: (i, 0)),
        out_shape=jax.ShapeDtypeStruct((r, n), F32),
        compiler_params=_params("parallel"), name="norm_matmul",
    )(x, g.reshape(1, k), w)


def _prep_kernel(*refs, mode, n_real, scale):
    x_ref, g_ref = refs[0], refs[1]
    o_ref = refs[-1]
    x = x_ref[...].astype(F32)
    ms = jnp.sum(x * x, axis=-1, keepdims=True) * (1.0 / n_real)
    xn = x * lax.rsqrt(ms + EPS) * g_ref[...]
    if mode == "none":
        y = xn
    elif mode == "half64":
        y = xn * refs[2][...] + pltpu.roll(xn, 64, 1) * refs[3][...]
    else:
        y = xn * refs[2][...] + pltpu.roll(xn, 96, 1) * refs[3][...] + pltpu.roll(xn, 32, 1) * refs[4][...]
    o_ref[...] = (y * scale).astype(o_ref.dtype).reshape(o_ref.shape)


def head_prep(x, col_off, n_heads, g, tables, mode, n_real, scale, head_major=False):
    r = x.shape[0]
    tm = _pick_tile(r, 1536, 8)
    cb = col_off // LANE
    tab_spec = pl.BlockSpec((tm, LANE), lambda i, h: (i, 0))
    if head_major:
        out_spec = pl.BlockSpec((1, tm, LANE), lambda i, h: (h, i, 0))
        out_shape = jax.ShapeDtypeStruct((n_heads, r, LANE), BF16)
    else:
        out_spec = pl.BlockSpec((tm, LANE), lambda i, h: (i, h))
        out_shape = jax.ShapeDtypeStruct((r, n_heads * LANE), BF16)
    return pl.pallas_call(
        functools.partial(_prep_kernel, mode=mode, n_real=n_real, scale=scale),
        grid=(r // tm, n_heads),
        in_specs=[pl.BlockSpec((tm, LANE), lambda i, h: (i, cb + h)),
                  pl.BlockSpec((1, LANE), lambda i, h: (0, 0))] + [tab_spec] * len(tables),
        out_specs=out_spec, out_shape=out_shape,
        compiler_params=_params("parallel", "parallel"), name="head_prep_" + mode,
    )(x, g.reshape(1, LANE), *tables)


def _vt_kernel(x_ref, o_ref):
    o_ref[...] = x_ref[...].astype(F32).T.astype(o_ref.dtype)


def transpose_heads(x, col_off, n_heads):
    r = x.shape[0]
    tm = _pick_tile(r, 1536, LANE)
    cb = col_off // LANE
    return pl.pallas_call(
        _vt_kernel, grid=(r // tm, n_heads),
        in_specs=[pl.BlockSpec((tm, LANE), lambda i, h: (i, cb + h))],
        out_specs=pl.BlockSpec((LANE, tm), lambda i, h: (h, i)),
        out_shape=jax.ShapeDtypeStruct((n_heads * LANE, r), BF16),
        compiler_params=_params("parallel", "parallel"), name="transpose_heads",
    )(x)


def _flash_kernel(*refs, groups, tq, tk, dv, n_kv, window, ctx_blocks, n_lat, has_qk2, has_sink, key_chunk):
    q_ref, k_ref, kn_ref, vt_ref = refs[0], refs[1], refs[2], refs[3]
    pos = 4
    if has_qk2:
        q2_ref, k2_ref, k2n_ref = refs[4], refs[5], refs[6]
        pos = 7
    if has_sink:
        sink_ref = refs[pos]
        pos += 1
    o_ref, m_ref, l_ref, acc_ref, s_ref, mt_ref = refs[pos:pos + 6]
    i, j = pl.program_id(1), pl.program_id(2)
    n = groups * tq
    bounds = [(r0, min(r0 + key_chunk, tk)) for r0 in range(0, tk, key_chunk)]

    q = q_ref[...].reshape(n, q_ref.shape[2])
    if has_qk2:
        q = jnp.concatenate([q, q2_ref[...].reshape(n, LANE)], axis=1)
    if window:
        qpos = i * tq + lax.broadcasted_iota(jnp.int32, (1, n), 1) % tq

    def scores(kr, k2r, step, r0, r1):
        kk = kr[r0:r1, :]
        if has_qk2:
            kk = jnp.concatenate([kk, k2r[r0:r1, :]], axis=1)
        s = lax.dot_general(kk, q, (((1,), (1,)), ((), ())), preferred_element_type=F32)
        if window:
            kpos = (i + step - ctx_blocks - 1) * tk + r0 + lax.broadcasted_iota(jnp.int32, (r1 - r0, 1), 0)
            valid = (jnp.abs(qpos - kpos) <= WINDOW) & (kpos >= 0) & (kpos < n_lat)
            s = jnp.where(valid | (step < ctx_blocks), s, NEG_INF)
        return s

    @pl.when(j == 0)
    def _():
        m_ref[...] = jnp.full(m_ref.shape, NEG_INF, F32)
        l_ref[...] = jnp.zeros(l_ref.shape, F32)
        acc_ref[...] = jnp.zeros(acc_ref.shape, F32)
        mt = jnp.full(mt_ref.shape, NEG_INF, F32)
        for r0, r1 in bounds:
            s = scores(k_ref, k2_ref if has_qk2 else None, j, r0, r1)
            s_ref[r0:r1, :] = s
            mt = jnp.maximum(mt, jnp.max(s, axis=0, keepdims=True))
        mt_ref[...] = mt

    def step(prefetch):
        m_prev = m_ref[...]
        m_new = jnp.maximum(m_prev, mt_ref[...])
        alpha = jnp.exp2(m_prev - m_new)
        l_new = alpha * l_ref[...]
        acc_new = alpha * acc_ref[...]
        mt = jnp.full(mt_ref.shape, NEG_INF, F32)
        for r0, r1 in bounds:
            if prefetch:
                s_next = scores(kn_ref, k2n_ref if has_qk2 else None, j + 1, r0, r1)
            p = jnp.exp2(s_ref[r0:r1, :] - m_new)
            l_new = l_new + jnp.sum(p, axis=0, keepdims=True)
            acc_new = acc_new + _dot(vt_ref[:, r0:r1], p.astype(BF16))
            if prefetch:
                s_ref[r0:r1, :] = s_next
                mt = jnp.maximum(mt, jnp.max(s_next, axis=0, keepdims=True))
        l_ref[...] = l_new
        acc_ref[...] = acc_new
        m_ref[...] = m_new
        if prefetch:
            mt_ref[...] = mt

    if n_kv > 1:
        pl.when(j < n_kv - 1)(lambda: step(True))

    @pl.when(j == n_kv - 1)
    def _():
        step(False)
        m, l, acc = m_ref[...], l_ref[...], acc_ref[...]
        if has_sink:
            sk = sink_ref[0]
            m_fin = jnp.maximum(m, sk)
            a = jnp.exp2(m - m_fin)
            l = l * a + jnp.exp2(sk - m_fin)
            acc = acc * a
        o_t = acc / l
        for g in range(groups):
            o_ref[:, g * dv:(g + 1) * dv] = o_t[:, g * tq:(g + 1) * tq].T.astype(o_ref.dtype)


KEY_CHUNK = 256


def flash(q, k, vt, *, kv_heads, groups, n_q, q_row0, kv_row0, n_kv_rows, tq, tk, dv=HEAD_DIM,
          q2=None, k2=None, sink=None, window=False, ctx_rows=0):
    dk = q.shape[2]
    qb0 = q_row0 // tq
    kb0 = kv_row0 // tk
    if window:
        ctx_blocks = ctx_rows // tk
        n_lat_blocks = n_kv_rows // tk
        n_kv = ctx_blocks + 3

        def kv_row(i, j):
            return jnp.where(j < ctx_blocks, n_lat_blocks + j, jnp.clip(i + j - ctx_blocks - 1, 0, n_lat_blocks - 1))
    else:
        ctx_blocks = 0
        n_kv = n_kv_rows // tk

        def kv_row(i, j):
            return kb0 + j
    n = groups * tq
    nxt = lambda j: jnp.minimum(j + 1, n_kv - 1)
    in_specs = [pl.BlockSpec((groups, tq, dk), lambda h, i, j: (h, qb0 + i, 0)),
                pl.BlockSpec((tk, dk), lambda h, i, j: (kv_row(i, j), h)),
                pl.BlockSpec((tk, dk), lambda h, i, j: (kv_row(i, nxt(j)), h)),
                pl.BlockSpec((dv, tk), lambda h, i, j: (h, kv_row(i, j)))]
    args = [q, k, k, vt]
    if q2 is not None:
        in_specs += [pl.BlockSpec((groups, tq, LANE), lambda h, i, j: (h, qb0 + i, 0)),
                     pl.BlockSpec((tk, LANE), lambda h, i, j: (kv_row(i, j), 0)),
                     pl.BlockSpec((tk, LANE), lambda h, i, j: (kv_row(i, nxt(j)), 0))]
        args += [q2, k2, k2]
    if sink is not None:
        in_specs.append(pl.BlockSpec((1, 1, n), lambda h, i, j: (h, 0, 0)))
        sink2 = sink.astype(F32).reshape(kv_heads, groups, 1) * LOG2E
        args.append(jnp.broadcast_to(sink2, (kv_heads, groups, tq)).reshape(kv_heads, 1, n))
    kern = functools.partial(
        _flash_kernel, groups=groups, tq=tq, tk=tk, dv=dv, n_kv=n_kv, window=window, ctx_blocks=ctx_blocks,
        n_lat=n_kv_rows, has_qk2=q2 is not None, has_sink=sink is not None,
        key_chunk=KEY_CHUNK)
    return pl.pallas_call(
        kern, grid=(kv_heads, n_q // tq, n_kv), in_specs=in_specs,
        out_specs=pl.BlockSpec((tq, groups * dv), lambda h, i, j: (i, h)),
        out_shape=jax.ShapeDtypeStruct((n_q, kv_heads * groups * dv), BF16),
        scratch_shapes=[pltpu.VMEM((1, n), F32), pltpu.VMEM((1, n), F32), pltpu.VMEM((dv, n), F32),
                        pltpu.VMEM((tk, n), F32), pltpu.VMEM((1, n), F32)],
        compiler_params=_params("parallel", "parallel", "arbitrary"),
        name="flash_window" if window else "flash_dense",
    )(*args)


def _outproj_kernel(o1c_ref, o1l_ref, o2c_ref, o2l_ref, gate_ref, h_ref, mod_ref, w_ref, out_ref, *, lat_tiles):
    is_ctx = pl.program_id(0) >= lat_tiles
    w1 = o1l_ref.shape[1]
    d = h_ref.shape[1]
    gt = gate_ref[...].astype(F32)
    sg = gt * jax.nn.sigmoid(gt)
    o1 = jnp.where(is_ctx, o1c_ref[...].astype(F32), o1l_ref[...].astype(F32))
    o2 = jnp.where(is_ctx, o2c_ref[...].astype(F32), o2l_ref[...].astype(F32))
    t1 = (o1 * sg[:, :w1]).astype(BF16)
    t2 = (o2 * sg[:, w1:]).astype(BF16)
    y = _dot(t1, w_ref[0:w1, :]) + _dot(t2, w_ref[w1:, :])
    gm = jnp.where(is_ctx, mod_ref[1:2, 2 * d:3 * d], mod_ref[0:1, 2 * d:3 * d])
    out_ref[...] = h_ref[...] + gm * y


def out_proj(o1c, o1l, o2c, o2l, proj, h, mod, w, latents_only):
    r, d = h.shape
    tm = ROW_TILE
    assert o1c.shape[0] == tm, "the context rows must be exactly one row tile"
    lt = o1l.shape[0] // tm
    w1, w2 = o1l.shape[1], o2l.shape[1]
    lat = lambda i: (jnp.minimum(i, lt - 1), 0)
    if latents_only:
        r = lt * tm
    return pl.pallas_call(
        functools.partial(_outproj_kernel, lat_tiles=lt), grid=(r // tm,),
        in_specs=[pl.BlockSpec((tm, w1), lambda i: (0, 0)), pl.BlockSpec((tm, w1), lat),
                  pl.BlockSpec((tm, w2), lambda i: (0, 0)), pl.BlockSpec((tm, w2), lat),
                  pl.BlockSpec((tm, w1 + w2), lambda i: (i, OFF_GATE // (w1 + w2))),
                  pl.BlockSpec((tm, d), lambda i: (i, 0)),
                  pl.BlockSpec(mod.shape, lambda i: (0, 0)),
                  pl.BlockSpec(w.shape, lambda i: (0, 0))],
        out_specs=pl.BlockSpec((tm, d), lambda i: (i, 0)),
        out_shape=jax.ShapeDtypeStruct((r, d), F32),
        compiler_params=_params("parallel"), name="out_proj",
    )(o1c, o1l, o2c, o2l, proj, h, mod, w)


def _conv3_kernel(x_ref, w_ref, b_ref, oc_ref, ol_ref, *, n_ctx):
    x = x_ref[...].astype(F32)
    r = x.shape[0]
    n_lat = r - n_ctx
    row = lax.broadcasted_iota(jnp.int32, (r, 1), 0)
    prev = jnp.where((row == 0) | (row == n_lat), 0.0, pltpu.roll(x, 1, 0))
    nxt = jnp.where((row == n_lat - 1) | (row == r - 1), 0.0, pltpu.roll(x, r - 1, 0))
    y = prev * w_ref[0:1, :] + x * w_ref[1:2, :] + nxt * w_ref[2:3, :] + b_ref[...]
    ol_ref[0] = y[:n_lat]
    oc_ref[0] = y[n_lat:]


def conv3(proj, col_off, w, b, n_ctx):
    r = proj.shape[0]
    cw = w.shape[1] // 3
    nb = cw // LANE
    cb = col_off // LANE
    return pl.pallas_call(
        functools.partial(_conv3_kernel, n_ctx=n_ctx), grid=(3, nb),
        in_specs=[pl.BlockSpec((r, LANE), lambda p, c: (0, cb + p * nb + c)),
                  pl.BlockSpec((3, LANE), lambda p, c: (0, p * nb + c)),
                  pl.BlockSpec((1, LANE), lambda p, c: (0, p * nb + c))],
        out_specs=[pl.BlockSpec((1, n_ctx, LANE), lambda p, c: (p, 0, c)),
                   pl.BlockSpec((1, r - n_ctx, LANE), lambda p, c: (p, 0, c))],
        out_shape=[jax.ShapeDtypeStruct((3, n_ctx, cw), F32), jax.ShapeDtypeStruct((3, r - n_ctx, cw), F32)],
        compiler_params=_params("parallel", "parallel"), name="conv3",
    )(proj, w, b.reshape(1, -1))


def _filt_kernel(z_ref, w1_ref, b1_ref, f1_ref, w2_ref, b2_ref, f2_ref, w3_ref, dl_ref, h_ref, tap_ref, nrm_ref, *,
                 tm, cw):
    i = pl.program_id(0)
    z = z_ref[...]

    def mm(a, w_r):
        return _dot3(*_split_bf16(a), *_split_bf16(w_r[...]))

    a = jnp.sin(f1_ref[...] * (mm(z, w1_ref) + b1_ref[...]))
    a = jnp.sin(f2_ref[...] * (mm(a, w2_ref) + b2_ref[...]))
    dec = jnp.exp(-z[:, 0:1] * dl_ref[...])
    h = mm(a, w3_ref) * jnp.concatenate([dec] * (h_ref.shape[1] // cw), axis=1)
    h_ref[...] = h.astype(h_ref.dtype)
    row = i * tm + lax.broadcasted_iota(jnp.int32, (tm, 1), 0)
    col = lax.broadcasted_iota(jnp.int32, (1, h.shape[1]), 1)
    skip = (row == 0) & ((col // cw) % 2 == 1)
    part = jnp.sum(jnp.where(skip, 0.0, jnp.abs(h)), axis=0, keepdims=True)

    @pl.when(i == 0)
    def _():
        nrm_ref[...] = jnp.zeros(nrm_ref.shape, F32)
        tap_ref[...] = h[0:SUBLANES]

    nrm_ref[...] += jnp.broadcast_to(part, nrm_ref.shape)


def hyena_filter(z, w1p, b1, f1, w2, b2, f2, w3, deltas):
    l = z.shape[0]
    n = w3.shape[1]
    cw = deltas.shape[0]
    hid = w2.shape[0]
    tm = min(l, 256)
    full = lambda a: pl.BlockSpec(a.shape, lambda i: (0, 0))
    ins = [w1p, b1.reshape(1, hid), f1.reshape(1, hid), w2, b2.reshape(1, hid), f2.reshape(1, hid), w3,
           deltas.reshape(1, cw)]
    small = pl.BlockSpec((SUBLANES, n), lambda i: (0, 0))
    return pl.pallas_call(
        functools.partial(_filt_kernel, tm=tm, cw=cw), grid=(l // tm,),
        in_specs=[pl.BlockSpec((tm, LANE), lambda i: (i, 0))] + [full(a) for a in ins],
        out_specs=[pl.BlockSpec((tm, n), lambda i: (i, 0)), small, small],
        out_shape=[jax.ShapeDtypeStruct((l, n), BF16), jax.ShapeDtypeStruct((SUBLANES, n), F32),
                   jax.ShapeDtypeStruct((SUBLANES, n), F32)],
        compiler_params=_params("arbitrary"), name="hyena_filter",
    )(z, *ins)


@functools.lru_cache(maxsize=None)
def _dft_consts(a):
    n = a * a
    hh = a // 2
    n1 = np.arange(hh)
    k1 = np.arange(hh)
    n2 = np.arange(a)
    k2 = np.arange(a)
    ang1 = 2 * np.pi * np.outer(k1 + 0.5, n1) / a
    m1 = np.zeros((hh, 2, hh))
    m1[:, 0], m1[:, 1] = np.cos(ang1), -np.sin(ang1)
    m1 = m1.reshape(2 * hh, hh)
    phi = 2 * np.pi * (n2[None, None, :] * (k1[:, None, None] + 0.5) / n + n2[None, None, :] * k2[None, :, None] / a)
    c, s = np.cos(phi), np.sin(phi)
    g = np.zeros((hh, 2, a, 2, a))
    g[:, 0, :, 0, :], g[:, 0, :, 1, :], g[:, 1, :, 0, :], g[:, 1, :, 1, :] = c, s, -s, c
    g = g.reshape(hh, 2 * a, 2 * a)
    gt = np.ascontiguousarray(np.transpose(g, (0, 2, 1)))
    al = 2 * np.pi * np.outer(n1, k1 + 0.5) / a
    mb = np.zeros((hh, hh, 2))
    mb[:, :, 0], mb[:, :, 1] = 2 / n * np.cos(al), -2 / n * np.sin(al)
    mb = mb.reshape(hh, 2 * hh)
    eye = np.eye(ROWS_BF16)
    return tuple(x.astype(np.float32) for x in (np.kron(m1, eye), g, gt, np.kron(mb, eye)))


@functools.lru_cache(maxsize=None)
def _dense_dft_consts(l):
    n = 2 * l
    ang = 2 * np.pi * np.outer(np.arange(l) + 0.5, np.arange(l)) / n
    mf = np.concatenate([np.cos(ang), -np.sin(ang)], axis=0)
    mi = np.concatenate([2 / n * np.cos(ang.T), -2 / n * np.sin(ang.T)], axis=1)
    return tuple(x.astype(np.float32) for x in (mf, mi))


def _stage1_kernel(x_ref, a_ref, o_ref):
    _, hh, t, ct = x_ref.shape
    x = x_ref[0].reshape(hh * t, ct).astype(BF16)
    o_ref[...] = _dot(a_ref[...], x).astype(o_ref.dtype).reshape(o_ref.shape)


def dft_stage1(x4, p, a1):
    _, hh, a, c = x4.shape
    ct = _pick_tile(c, 512)
    t = ROWS_BF16
    return pl.pallas_call(
        _stage1_kernel, grid=(a // t, c // ct),
        in_specs=[pl.BlockSpec((1, hh, t, ct), lambda j, q: (p, 0, j, q)),
                  pl.BlockSpec(a1.shape, lambda j, q: (0, 0))],
        out_specs=pl.BlockSpec((2 * hh, t, ct), lambda j, q: (0, j, q)),
        out_shape=jax.ShapeDtypeStruct((2 * hh, a, c), BF16),
        compiler_params=_params("parallel", "parallel"), name="dft_stage1",
    )(x4, a1)


def _spectrum_product(x, h0, h1, tap0, n0, n1):
    half = x.shape[0] // 2
    inv = 1.0 / (n0 + n1)
    kr = (h0[:half] + h1[:half] - tap0) * inv
    ki = (h0[half:] - h1[half:]) * inv
    xr, xi = x[:half], x[half:]
    return jnp.concatenate([xr * kr - xi * ki, xr * ki + xi * kr], axis=0)


K1_PER_STEP = 2


def _mid_filter_kernel(y0_ref, y1_ref, g_ref, tap_ref, n0_ref, n1_ref, o_ref):
    inv = 1.0 / (n0_ref[0:1, :] + n1_ref[0:1, :])
    tap = tap_ref[0:1, :]
    for b in range(y0_ref.shape[0]):
        x0 = _dot(g_ref[b], y0_ref[b])
        x1 = _dot(g_ref[b], y1_ref[b])
        half = x0.shape[0] // 2
        kr = (x0[:half] + x1[:half] - tap) * inv
        ki = (x0[half:] - x1[half:]) * inv
        o_ref[b] = jnp.concatenate([kr, ki], axis=0).astype(o_ref.dtype)


def filter_spectrum(h, taps, nrm, consts, a):
    a1, g, _, _ = consts
    l, n = h.shape
    c = n // 4
    hh = a // 2
    y = dft_stage1(h.reshape(1, hh, a, n), 0, a1).reshape(hh, 2 * a, n)
    kb = K1_PER_STEP
    slab = lambda side: pl.BlockSpec((kb, 2 * a, c), lambda k, o: (k, 0, 2 * o + side))
    row = lambda side: pl.BlockSpec((SUBLANES, c), lambda k, o: (0, 2 * o + side))
    return pl.pallas_call(
        _mid_filter_kernel, grid=(hh // kb, 2),
        in_specs=[slab(0), slab(1), pl.BlockSpec((kb, 2 * a, 2 * a), lambda k, o: (k, 0, 0)), row(1), row(0), row(1)],
        out_specs=pl.BlockSpec((kb, 2 * a, c), lambda k, o: (k, 0, o)),
        out_shape=jax.ShapeDtypeStruct((hh, 2 * a, 2 * c), BF16),
        compiler_params=_params("parallel", "parallel"), name="dft_mid_filter",
    )(y, y, g, taps, nrm, nrm)


def _mid_conv_kernel(y_ref, g_ref, gt_ref, kf_ref, o_ref):
    for b in range(y_ref.shape[0]):
        x = _dot(g_ref[b], y_ref[b])
        kf = kf_ref[b].astype(F32)
        half = x.shape[0] // 2
        xr, xi, kr, ki = x[:half], x[half:], kf[:half], kf[half:]
        z = jnp.concatenate([xr * kr - xi * ki, xr * ki + xi * kr], axis=0).astype(BF16)
        o_ref[b] = _dot(gt_ref[b], z).astype(o_ref.dtype)


def dft_mid_conv(y, g, gt, kf, order):
    hh, a2, c = y.shape
    kb = K1_PER_STEP
    slab = pl.BlockSpec((kb, a2, c), lambda k: (k, 0, 0))
    mat = pl.BlockSpec((kb, a2, a2), lambda k: (k, 0, 0))
    return pl.pallas_call(
        _mid_conv_kernel, grid=(hh // kb,),
        in_specs=[slab, mat, mat, pl.BlockSpec((kb, a2, c), lambda k: (k, 0, order))],
        out_specs=slab, out_shape=jax.ShapeDtypeStruct((hh, a2, c), BF16),
        compiler_params=_params("parallel"), name="dft_mid_conv",
    )(y, g, gt, kf)


def _last_kernel(b_ref, a_ref, xa_ref, zb_ref, bias_ref, o_ref):
    rows, t, ct = b_ref.shape
    conv = _dot(a_ref[...], b_ref[...].reshape(rows * t, ct)).reshape(rows // 2, t, ct)
    o_ref[0] = (xa_ref[0] * (conv + bias_ref[...] * zb_ref[0])).astype(o_ref.dtype)


def dft_last(b, a2m, xa4, pa, zb4, pb, bias_row):
    rows, a, c = b.shape
    hh = rows // 2
    ct = _pick_tile(c, 512)
    t = ROWS_BF16
    sig = lambda pp: pl.BlockSpec((1, hh, t, ct), lambda j, q: (pp, 0, j, q))
    return pl.pallas_call(
        _last_kernel, grid=(a // t, c // ct),
        in_specs=[pl.BlockSpec((rows, t, ct), lambda j, q: (0, j, q)),
                  pl.BlockSpec(a2m.shape, lambda j, q: (0, 0)),
                  sig(pa), sig(pb), pl.BlockSpec((1, ct), lambda j, q: (0, q))],
        out_specs=sig(0), out_shape=jax.ShapeDtypeStruct((1, hh, a, c), F32),
        compiler_params=_params("parallel", "parallel"), name="dft_last",
    )(b, a2m, xa4, zb4, bias_row)


def long_conv_gated(u3, kf, bias, consts, a):
    a1, g, gt, a2m = consts
    _, l, c = u3.shape
    hh = a // 2
    u4 = u3.reshape(3, hh, a, c)

    def conv_gate(src4, p_src, xa_p, order):
        y = dft_stage1(src4, p_src, a1).reshape(hh, 2 * a, c)
        bb = dft_mid_conv(y, g, gt, kf, order).reshape(2 * hh, a, c)
        return dft_last(bb, a2m, u4, xa_p, src4, p_src, bias[order].reshape(1, c))

    z2 = conv_gate(u4, 2, 0, 0)
    return conv_gate(z2, 0, 1, 1).reshape(l, c)


def _ctx_spec_kernel(h_ref, mf_ref, o_ref):
    o_ref[...] = _dot(mf_ref[...], h_ref[...])


def ctx_filter_spectrum(h, mf):
    l, n = h.shape
    tn = _pick_tile(n, 1024)
    return pl.pallas_call(
        _ctx_spec_kernel, grid=(n // tn,),
        in_specs=[pl.BlockSpec((l, tn), lambda j: (0, j)), pl.BlockSpec(mf.shape, lambda j: (0, 0))],
        out_specs=pl.BlockSpec((2 * l, tn), lambda j: (0, j)),
        out_shape=jax.ShapeDtypeStruct((2 * l, n), F32),
        compiler_params=_params("parallel"), name="ctx_filter_spectrum",
    )(h, mf)


def _ctx_hyena_kernel(u_ref, mf_ref, mi_ref, hf0a, hf1a, hf0b, hf1b, tap0, tap1, n0a, n1a, n0b, n1b, bias_ref, o_ref):
    x1, x2, z = u_ref[0], u_ref[1], u_ref[2]

    def conv(sig, h0_ref, h1_ref, tap_ref, n0_ref, n1_ref):
        x = _dot(mf_ref[...], sig.astype(BF16))
        zz = _spectrum_product(x, h0_ref[...], h1_ref[...], tap_ref[0:1, :], n0_ref[0:1, :], n1_ref[0:1, :])
        return _dot(mi_ref[...], zz.astype(BF16))

    z2 = x1 * (conv(z, hf0a, hf1a, tap0, n0a, n1a) + bias_ref[0:1, :] * z)
    o_ref[...] = (x2 * (conv(z2, hf0b, hf1b, tap1, n0b, n1b) + bias_ref[1:2, :] * z2)).astype(o_ref.dtype)


def ctx_hyena(u3, hf, taps, nrm, bias, mf, mi):
    _, l, c = u3.shape
    tc = _pick_tile(c, 512)
    nb = c // tc
    const = lambda a: pl.BlockSpec(a.shape, lambda j: (0, 0))
    col = lambda rows, q: pl.BlockSpec((rows, tc), lambda j: (0, q * nb + j))
    mats = [mf, mi]
    return pl.pallas_call(
        _ctx_hyena_kernel, grid=(nb,),
        in_specs=[pl.BlockSpec((3, l, tc), lambda j: (0, 0, j))] + [const(m) for m in mats]
        + [col(2 * l, 0), col(2 * l, 1), col(2 * l, 2), col(2 * l, 3), col(8, 1), col(8, 3),
           col(8, 0), col(8, 1), col(8, 2), col(8, 3), pl.BlockSpec((2, tc), lambda j: (0, j))],
        out_specs=pl.BlockSpec((l, tc), lambda j: (0, j)),
        out_shape=jax.ShapeDtypeStruct((l, c), BF16),
        compiler_params=_params("parallel"), name="ctx_hyena",
    )(u3, *mats, hf, hf, hf, hf, taps, taps, nrm, nrm, nrm, nrm, bias)


def _axial_tables(n_tokens, n_rot):
    rows = n_tokens // GRID_W
    row = jnp.broadcast_to(jnp.arange(rows)[:, None], (rows, GRID_W)).reshape(-1).astype(F32)
    col = jnp.broadcast_to(jnp.arange(GRID_W)[None, :], (rows, GRID_W)).reshape(-1).astype(F32)
    n_freq = n_rot // 4
    inv = ROPE_THETA ** (-jnp.arange(n_freq, dtype=F32) / n_freq)
    ang = jnp.concatenate([row[:, None] * inv, col[:, None] * inv], axis=-1)
    return jnp.cos(ang), jnp.sin(ang)


def _rope_tables_full(s, n_ctx):
    cos, sin = _axial_tables(s, HEAD_DIM)
    t0 = jnp.concatenate([jnp.concatenate([cos, cos], axis=1), jnp.ones((n_ctx, LANE), F32)], axis=0)
    t1 = jnp.concatenate([jnp.concatenate([-sin, sin], axis=1), jnp.zeros((n_ctx, LANE), F32)], axis=0)
    return t0, t1


def _rope_tables_half(s, n_ctx):
    cos, sin = _axial_tables(s, ROPE_DIM)
    q = ROPE_DIM // 2
    zq = jnp.zeros((s, q), F32)
    z2 = jnp.zeros((s, LANE - ROPE_DIM), F32)
    t0 = jnp.concatenate([cos, cos, z2], axis=1)
    t1 = jnp.concatenate([-sin, zq, z2], axis=1)
    t2 = jnp.concatenate([zq, sin, z2], axis=1)
    c0 = jnp.concatenate([jnp.ones((n_ctx, ROPE_DIM), F32), jnp.zeros((n_ctx, LANE - ROPE_DIM), F32)], axis=1)
    cz = jnp.zeros((n_ctx, LANE), F32)
    return jnp.concatenate([t0, c0], axis=0), jnp.concatenate([t1, cz], axis=0), jnp.concatenate([t2, cz], axis=0)


def _filter_features(l):
    pos = jnp.arange(l, dtype=F32)
    t = pos / max(l - 1, 1)
    bands = jnp.linspace(1e-4, HYENA_BANDS - 1, HYENA_BANDS, dtype=F32)
    ang = (2.0 * math.pi / l) * pos[:, None] * bands[None, :]
    z = jnp.concatenate([t[:, None], jnp.cos(ang), -jnp.sin(ang)], axis=-1)
    return jnp.pad(z, ((0, 0), (0, LANE - HYENA_EMB)))


Q_STACK = 1024
KV_TILE_CAP = 1408


def _attention_pair(qp, kp, vt, s, n_ctx, *, kv_heads, groups, window, sink=None, q2=None, k2=None):
    common = dict(kv_heads=kv_heads, groups=groups, q2=q2, k2=k2, sink=sink)
    o_ctx = flash(qp, kp, vt, n_q=n_ctx, q_row0=s, kv_row0=s, n_kv_rows=n_ctx, tq=n_ctx, tk=n_ctx, **common)
    if window:
        o_lat = flash(qp, kp, vt, n_q=s, q_row0=0, kv_row0=0, n_kv_rows=s, tq=ROW_TILE, tk=ROW_TILE, window=True,
                      ctx_rows=n_ctx, **common)
    else:
        o_lat = flash(qp, kp, vt, n_q=s, q_row0=0, kv_row0=0, n_kv_rows=n_ctx + s, tq=Q_STACK // groups,
                      tk=_pick_tile(n_ctx + s, KV_TILE_CAP), **common)
    return o_ctx, o_lat


def _even_layer(u, s, n_ctx, w_in, qn_g, kn_g, sink, conv_w, conv_b, fparams, hy_bias, rope_full, feats, dft):
    wq, wk, wv, whd, wg = jnp.split(w_in, np.cumsum([ATT_WIDTH, KV_WIDTH, KV_WIDTH, 3 * B_WIDTH])[:].tolist(), axis=1)
    proj = matmul(u, jnp.concatenate([wg, wq, wk, wv, whd], axis=1).astype(BF16), F32)
    qp = head_prep(proj, OFF_Q, A_HEADS, qn_g, rope_full, "half64", HEAD_DIM, HEAD_DIM ** -0.5 * LOG2E, True)
    kp = head_prep(proj, OFF_K, A_KV_HEADS, kn_g, rope_full, "half64", HEAD_DIM, 1.0)
    vt = transpose_heads(proj, OFF_V, A_KV_HEADS)
    a_ctx, a_lat = _attention_pair(qp, kp, vt, s, n_ctx, kv_heads=A_KV_HEADS, groups=A_HEADS // A_KV_HEADS,
                                   window=True, sink=sink)
    w1, b1, f1, w2, b2, f2, w3 = fparams
    w1p = jnp.pad(w1, ((0, LANE - HYENA_EMB), (0, 0)))
    deltas = jnp.linspace(DECAY_MAX, DECAY_MIN, B_WIDTH, dtype=F32)
    uc3, ul3 = conv3(proj, OFF_REST, conv_w, conv_b, n_ctx)
    a = int(round(math.sqrt(2 * s)))
    consts, (mf, mi) = dft
    h_lat, tap_lat, nrm_lat = hyena_filter(feats[0], w1p, b1, f1, w2, b2, f2, w3, deltas)
    kf_lat = filter_spectrum(h_lat, tap_lat, nrm_lat, consts, a)
    b_lat = long_conv_gated(ul3, kf_lat, hy_bias, consts, a)
    h_ctx, tap_ctx, nrm_ctx = hyena_filter(feats[1], w1p, b1, f1, w2, b2, f2, w3, deltas)
    hf_ctx = ctx_filter_spectrum(h_ctx, mf)
    b_ctx = ctx_hyena(uc3, hf_ctx, tap_ctx, nrm_ctx, hy_bias, mf, mi)
    return proj, a_ctx, a_lat, b_ctx, b_lat


def _odd_layer(u, s, n_ctx, w_in, qn_g, kn_g, cq_g, ckv_g, wuq, wukv, mq_g, mk_g, rope_full, rope_half):
    d = w_in.shape[0]
    parts = jnp.split(w_in, np.cumsum([ATT_WIDTH, KV_WIDTH, KV_WIDTH, Q_LORA, KV_LORA, ROPE_DIM]).tolist(), axis=1)
    wq, wk, wv, wmq, wmkv, wmkr, wg = parts
    w_perm = jnp.concatenate([wg, wq, wk, wv, wmq, wmkv, wmkr, jnp.zeros((d, LANE - ROPE_DIM), w_in.dtype)], axis=1)
    proj = matmul(u, w_perm.astype(BF16), F32)
    qp = head_prep(proj, OFF_Q, C_HEADS, qn_g, rope_full, "half64", HEAD_DIM, HEAD_DIM ** -0.5 * LOG2E, True)
    kp = head_prep(proj, OFF_K, C_KV_HEADS, kn_g, rope_full, "half64", HEAD_DIM, 1.0)
    vt = transpose_heads(proj, OFF_V, C_KV_HEADS)
    c_ctx, c_lat = _attention_pair(qp, kp, vt, s, n_ctx, kv_heads=C_KV_HEADS, groups=C_HEADS // C_KV_HEADS,
                                   window=False)
    pad_r = LANE - ROPE_DIM
    wuq3 = wuq.reshape(Q_LORA, M_HEADS, QK_DIM)
    wuq_p = jnp.concatenate([wuq3[:, :, :NOPE_DIM].reshape(Q_LORA, -1),
                             jnp.pad(wuq3[:, :, NOPE_DIM:], ((0, 0), (0, 0), (0, pad_r))).reshape(Q_LORA, -1)], axis=1)
    wukv3 = wukv.reshape(KV_LORA, M_HEADS, NOPE_DIM + V_DIM)
    wukv_p = jnp.concatenate([wukv3[:, :, :NOPE_DIM].reshape(KV_LORA, -1),
                              wukv3[:, :, NOPE_DIM:].reshape(KV_LORA, -1)], axis=1)
    q_raw = norm_matmul(proj, OFF_MQ, cq_g, wuq_p.astype(BF16))
    kv_raw = norm_matmul(proj, OFF_MKV, ckv_g, wukv_p.astype(BF16))
    scale = QK_DIM ** -0.5 * LOG2E
    gq_r = jnp.pad(mq_g[NOPE_DIM:], (0, pad_r))
    gk_r = jnp.pad(mk_g[NOPE_DIM:], (0, pad_r))
    qn = head_prep(q_raw, 0, M_HEADS, mq_g[:NOPE_DIM], (), "none", NOPE_DIM, scale, True)
    qr = head_prep(q_raw, M_HEADS * NOPE_DIM, M_HEADS, gq_r, rope_half, "half32", ROPE_DIM, scale, True)
    kn = head_prep(kv_raw, 0, M_HEADS, mk_g[:NOPE_DIM], (), "none", NOPE_DIM, 1.0)
    kr = head_prep(proj, OFF_MKR, 1, gk_r, rope_half, "half32", ROPE_DIM, 1.0)
    vmt = transpose_heads(kv_raw, M_HEADS * NOPE_DIM, M_HEADS)
    d_ctx, d_lat = _attention_pair(qn, kn, vmt, s, n_ctx, kv_heads=M_HEADS, groups=1, window=False, q2=qr, k2=kr)
    return proj, c_ctx, c_lat, d_ctx, d_lat


def kernel(x, c, ctx, c_ctx, ada_w, ada_b, norm_g, w_out, ev_w_in, ev_qn_g, ev_kn_g, ev_sink, ev_conv_w, ev_conv_b, hy_w1, hy_b1, hy_f1, hy_w2, hy_b2, hy_f2, hy_w3, hy_bias, od_w_in, od_qn_g, od_kn_g, od_cq_g, od_ckv_g, od_wuq, od_wukv, od_mq_g, od_mk_g):
    _, s, d = x.shape
    n_ctx = ctx.shape[1]
    depth = ada_w.shape[0]
    h = jnp.concatenate([x[0], ctx[0]], axis=0)
    cc = jnp.zeros((8, d), F32).at[0].set(c[0]).at[1].set(c_ctx)
    mod = ada_mod(cc, ada_w, ada_b)
    rope_full = _rope_tables_full(s, n_ctx)
    rope_half = _rope_tables_half(s, n_ctx)
    feats = (_filter_features(s), _filter_features(n_ctx))
    to_bf16 = lambda mats: tuple(jnp.asarray(m).astype(BF16) for m in mats)
    dft = (to_bf16(_dft_consts(int(round(math.sqrt(2 * s))))), to_bf16(_dense_dft_consts(n_ctx)))
    for i in range(depth):
        u = norm_mod(h, norm_g[i], mod[i], n_ctx)
        if i % 2 == 0:
            e = i // 2
            proj, o1c, o1l, o2c, o2l = _even_layer(
                u, s, n_ctx, ev_w_in[e], ev_qn_g[e], ev_kn_g[e], ev_sink[e], ev_conv_w[e], ev_conv_b[e],
                (hy_w1[e], hy_b1[e], hy_f1[e], hy_w2[e], hy_b2[e], hy_f2[e], hy_w3[e]), hy_bias[e], rope_full, feats,
                dft)
        else:
            o = i // 2
            proj, o1c, o1l, o2c, o2l = _odd_layer(
                u, s, n_ctx, od_w_in[o], od_qn_g[o], od_kn_g[o], od_cq_g[o], od_ckv_g[o], od_wuq[o], od_wukv[o],
                od_mq_g[o], od_mk_g[o], rope_full, rope_half)
        h = out_proj(o1c, o1l, o2c, o2l, proj, h, mod[i], w_out[i].astype(BF16), latents_only=i == depth - 1)
    return h[None]
```

```python
import functools
import math

import numpy as np
import jax
import jax.numpy as jnp
from jax import lax
from jax.experimental import pallas as pl
from jax.experimental.pallas import tpu as pltpu

F32 = jnp.float32
BF16 = jnp.bfloat16

GRID_W = 64
HEAD_DIM = 128
ROPE_THETA = 10000.0
EPS = 1e-6
NEG_INF = -1e30
A_HEADS, A_KV_HEADS, WINDOW = 8, 2, 128
B_WIDTH = 1024
HYENA_BANDS = 16
HYENA_EMB = 1 + 2 * HYENA_BANDS
DECAY_TARGET = 1e-2
DECAY_MAX = abs(math.log(DECAY_TARGET)) / 0.3
DECAY_MIN = abs(math.log(DECAY_TARGET)) / 1.5
C_HEADS, C_KV_HEADS = 8, 2
M_HEADS, Q_LORA, KV_LORA, NOPE_DIM, ROPE_DIM, V_DIM = 8, 512, 256, 128, 64, 128
QK_DIM = NOPE_DIM + ROPE_DIM
ATT_WIDTH = A_HEADS * HEAD_DIM
KV_WIDTH = A_KV_HEADS * HEAD_DIM
BRANCH = ATT_WIDTH + B_WIDTH

LOG2E = math.log2(math.e)
LANE = 128
SUBLANES = 8
ROWS_BF16 = 16
ROW_TILE = 256
VMEM_LIMIT = 48 * 1024 * 1024

OFF_GATE, OFF_Q, OFF_K, OFF_V, OFF_REST = 0, BRANCH, BRANCH + ATT_WIDTH, BRANCH + ATT_WIDTH + KV_WIDTH, BRANCH + ATT_WIDTH + 2 * KV_WIDTH
OFF_MQ, OFF_MKV, OFF_MKR = OFF_REST, OFF_REST + Q_LORA, OFF_REST + Q_LORA + KV_LORA


def _params(*sem):
    return pltpu.CompilerParams(dimension_semantics=sem, vmem_limit_bytes=VMEM_LIMIT)


def _split_bf16(x):
    hi = x.astype(BF16)
    return hi, (x - hi.astype(F32)).astype(BF16)


def _dot(a, b):
    return jnp.dot(a, b, preferred_element_type=F32)


def _dot3(a_hi, a_lo, b_hi, b_lo):
    return _dot(a_hi, b_hi) + (_dot(a_hi, b_lo) + _dot(a_lo, b_hi))


def _ada_kernel(c_ref, w_ref, b_ref, o_ref):
    c = c_ref[...]
    a = (c * jax.nn.sigmoid(c)).astype(BF16)
    o_ref[0] = _dot(a, w_ref[0].astype(BF16)) + b_ref[0]


def ada_mod(cc, ada_w, ada_b):
    depth, d, n = ada_w.shape
    tn = _pick_tile(n, 1024)
    return pl.pallas_call(
        _ada_kernel, grid=(depth, n // tn),
        in_specs=[pl.BlockSpec((8, d), lambda l, j: (0, 0)),
                  pl.BlockSpec((1, d, tn), lambda l, j: (l, 0, j)),
                  pl.BlockSpec((1, 1, tn), lambda l, j: (l, 0, j))],
        out_specs=pl.BlockSpec((1, 8, tn), lambda l, j: (l, 0, j)),
        out_shape=jax.ShapeDtypeStruct((depth, 8, n), F32),
        compiler_params=_params("parallel", "parallel"), name="ada_mod",
    )(cc, ada_w, ada_b.reshape(depth, 1, n))


def _norm_kernel(x_ref, g_ref, mod_ref, o_ref, *, lat_tiles):
    x = x_ref[...]
    d = x.shape[1]
    y = x * lax.rsqrt(jnp.mean(x * x, axis=-1, keepdims=True) + EPS) * g_ref[...]
    is_ctx = pl.program_id(0) >= lat_tiles
    sh = jnp.where(is_ctx, mod_ref[1:2, 0:d], mod_ref[0:1, 0:d])
    sc = jnp.where(is_ctx, mod_ref[1:2, d:2 * d], mod_ref[0:1, d:2 * d])
    o_ref[...] = (y * (1.0 + sc) + sh).astype(o_ref.dtype)


def norm_mod(h, g, mod, n_ctx):
    r, d = h.shape
    tm = ROW_TILE
    return pl.pallas_call(
        functools.partial(_norm_kernel, lat_tiles=(r - n_ctx) // tm), grid=(r // tm,),
        in_specs=[pl.BlockSpec((tm, d), lambda i: (i, 0)),
                  pl.BlockSpec((1, d), lambda i: (0, 0)),
                  pl.BlockSpec(mod.shape, lambda i: (0, 0))],
        out_specs=pl.BlockSpec((tm, d), lambda i: (i, 0)),
        out_shape=jax.ShapeDtypeStruct((r, d), BF16),
        compiler_params=_params("parallel"), name="norm_mod",
    )(h, g.reshape(1, d), mod)


def _mm_kernel(a_ref, b_ref, o_ref):
    o_ref[...] = _dot(a_ref[...], b_ref[...]).astype(o_ref.dtype)


def _pick_tile(n, cap, unit=LANE):
    best = unit
    for t in range(unit, cap + 1, unit):
        if n % t == 0:
            best = t
    return best


def matmul(a, b, out_dtype):
    m, k = a.shape
    n = b.shape[1]
    tm = _pick_tile(m, 1536, 8)
    tn = _pick_tile(n, 1024)
    return pl.pallas_call(
        _mm_kernel, grid=(m // tm, n // tn),
        in_specs=[pl.BlockSpec((tm, k), lambda i, j: (i, 0)),
                  pl.BlockSpec((k, tn), lambda i, j: (0, j))],
        out_specs=pl.BlockSpec((tm, tn), lambda i, j: (i, j)),
        out_shape=jax.ShapeDtypeStruct((m, n), out_dtype),
        compiler_params=_params("parallel", "parallel"), name="in_proj",
    )(a, b)


def _normmm_kernel(x_ref, g_ref, w_ref, o_ref):
    x = x_ref[...].astype(F32)
    y = x * lax.rsqrt(jnp.mean(x * x, axis=-1, keepdims=True) + EPS) * g_ref[...]
    o_ref[...] = _dot(y.astype(BF16), w_ref[...]).astype(o_ref.dtype)


def norm_matmul(x, col_off, g, w):
    r = x.shape[0]
    k, n = w.shape
    tm = _pick_tile(r, 768, 8)
    cb = col_off // k
    return pl.pallas_call(
        _normmm_kernel, grid=(r // tm,),
        in_specs=[pl.BlockSpec((tm, k), lambda i: (i, cb)),
                  pl.BlockSpec((1, k), lambda i: (0, 0)),
                  pl.BlockSpec((k, n), lambda i: (0, 0))],
        out_specs=pl.BlockSpec((tm, n), lambda i: (i, 0)),
        out_shape=jax.ShapeDtypeStruct((r, n), F32),
        compiler_params=_params("parallel"), name="norm_matmul",
    )(x, g.reshape(1, k), w)


def _prep_kernel(*refs, mode, n_real, scale, n_heads, head_major):
    x_ref, g_ref = refs[0], refs[1]
    o_ref = refs[-1]
    g = g_ref[...]
    tabs = [t[...] for t in refs[2:-1]]
    for h in range(n_heads):
        x = x_ref[:, h * LANE:(h + 1) * LANE].astype(F32)
        ms = jnp.sum(x * x, axis=-1, keepdims=True) * (1.0 / n_real)
        xn = x * lax.rsqrt(ms + EPS) * g
        if mode == "none":
            y = xn
        elif mode == "half64":
            y = xn * tabs[0] + pltpu.roll(xn, 64, 1) * tabs[1]
        else:
            y = xn * tabs[0] + pltpu.roll(xn, 96, 1) * tabs[1] + pltpu.roll(xn, 32, 1) * tabs[2]
        y = (y * scale).astype(o_ref.dtype)
        if head_major:
            o_ref[h] = y
        else:
            o_ref[:, h * LANE:(h + 1) * LANE] = y


def head_prep(x, col_off, n_heads, g, tables, mode, n_real, scale, head_major=False):
    r = x.shape[0]
    w = n_heads * LANE
    tm = _pick_tile(r, 768, 8)
    tab_spec = pl.BlockSpec((tm, LANE), lambda i: (i, 0))
    if head_major:
        out_spec = pl.BlockSpec((n_heads, tm, LANE), lambda i: (0, i, 0))
        out_shape = jax.ShapeDtypeStruct((n_heads, r, LANE), BF16)
    else:
        out_spec = pl.BlockSpec((tm, w), lambda i: (i, 0))
        out_shape = jax.ShapeDtypeStruct((r, w), BF16)
    return pl.pallas_call(
        functools.partial(_prep_kernel, mode=mode, n_real=n_real, scale=scale, n_heads=n_heads,
                          head_major=head_major),
        grid=(r // tm,),
        in_specs=[pl.BlockSpec((tm, w), lambda i: (i, col_off // w)),
                  pl.BlockSpec((1, LANE), lambda i: (0, 0))] + [tab_spec] * len(tables),
        out_specs=out_spec, out_shape=out_shape,
        compiler_params=_params("parallel"), name="head_prep_" + mode,
    )(x, g.reshape(1, LANE), *tables)


def _vt_kernel(x_ref, o_ref):
    o_ref[...] = x_ref[...].astype(F32).T.astype(o_ref.dtype)


def transpose_heads(x, col_off, n_heads):
    r = x.shape[0]
    w = n_heads * LANE
    tm = _pick_tile(r, 768, LANE)
    return pl.pallas_call(
        _vt_kernel, grid=(r // tm,),
        in_specs=[pl.BlockSpec((tm, w), lambda i: (i, col_off // w))],
        out_specs=pl.BlockSpec((w, tm), lambda i: (0, i)),
        out_shape=jax.ShapeDtypeStruct((w, r), BF16),
        compiler_params=_params("parallel"), name="transpose_heads",
    )(x)


def _flash_kernel(*refs, groups, tq, tk, dv, n_kv, has_qk2, has_sink, key_chunk):
    q_ref, k_ref, kn_ref, vt_ref = refs[0], refs[1], refs[2], refs[3]
    pos = 4
    if has_qk2:
        q2_ref, k2_ref, k2n_ref = refs[4], refs[5], refs[6]
        pos = 7
    if has_sink:
        sink_ref = refs[pos]
        pos += 1
    o_ref, m_ref, l_ref, acc_ref, s_ref, mt_ref = refs[pos:pos + 6]
    j = pl.program_id(2)
    n = groups * tq
    bounds = [(r0, min(r0 + key_chunk, tk)) for r0 in range(0, tk, key_chunk)]

    q = q_ref[...].reshape(n, q_ref.shape[2])
    if has_qk2:
        q = jnp.concatenate([q, q2_ref[...].reshape(n, LANE)], axis=1)

    def scores(kr, k2r, r0, r1):
        kk = kr[r0:r1, :]
        if has_qk2:
            kk = jnp.concatenate([kk, k2r[r0:r1, :]], axis=1)
        return lax.dot_general(kk, q, (((1,), (1,)), ((), ())), preferred_element_type=F32)

    @pl.when(j == 0)
    def _():
        m_ref[...] = jnp.full(m_ref.shape, NEG_INF, F32)
        l_ref[...] = jnp.zeros(l_ref.shape, F32)
        acc_ref[...] = jnp.zeros(acc_ref.shape, F32)
        mt = jnp.full(mt_ref.shape, NEG_INF, F32)
        for r0, r1 in bounds:
            s = scores(k_ref, k2_ref if has_qk2 else None, r0, r1)
            s_ref[r0:r1, :] = s
            mt = jnp.maximum(mt, jnp.max(s, axis=0, keepdims=True))
        mt_ref[...] = mt

    def step(prefetch):
        m_prev = m_ref[...]
        m_new = jnp.maximum(m_prev, mt_ref[...])
        alpha = jnp.exp2(m_prev - m_new)
        l_new = alpha * l_ref[...]
        acc_new = alpha * acc_ref[...]
        mt = jnp.full(mt_ref.shape, NEG_INF, F32)
        for r0, r1 in bounds:
            if prefetch:
                s_next = scores(kn_ref, k2n_ref if has_qk2 else None, r0, r1)
            p = jnp.exp2(s_ref[r0:r1, :] - m_new)
            l_new = l_new + jnp.sum(p, axis=0, keepdims=True)
            acc_new = acc_new + _dot(vt_ref[:, r0:r1], p.astype(BF16))
            if prefetch:
                s_ref[r0:r1, :] = s_next
                mt = jnp.maximum(mt, jnp.max(s_next, axis=0, keepdims=True))
        l_ref[...] = l_new
        acc_ref[...] = acc_new
        m_ref[...] = m_new
        if prefetch:
            mt_ref[...] = mt

    if n_kv > 1:
        pl.when(j < n_kv - 1)(lambda: step(True))

    @pl.when(j == n_kv - 1)
    def _():
        step(False)
        m, l, acc = m_ref[...], l_ref[...], acc_ref[...]
        if has_sink:
            sk = sink_ref[0]
            m_fin = jnp.maximum(m, sk)
            a = jnp.exp2(m - m_fin)
            l = l * a + jnp.exp2(sk - m_fin)
            acc = acc * a
        o_t = acc / l
        for g in range(groups):
            o_ref[:, g * dv:(g + 1) * dv] = o_t[:, g * tq:(g + 1) * tq].T.astype(o_ref.dtype)


KEY_CHUNK = 256


def flash(q, k, vt, *, kv_heads, groups, n_q, q_row0, kv_row0, n_kv_rows, tq, tk, dv=HEAD_DIM,
          q2=None, k2=None, sink=None):
    dk = q.shape[2]
    qb0 = q_row0 // tq
    kb0 = kv_row0 // tk
    n_kv = n_kv_rows // tk

    def kv_row(i, j):
        return kb0 + j
    n = groups * tq
    nxt = lambda j: jnp.minimum(j + 1, n_kv - 1)
    in_specs = [pl.BlockSpec((groups, tq, dk), lambda h, i, j: (h, qb0 + i, 0)),
                pl.BlockSpec((tk, dk), lambda h, i, j: (kv_row(i, 0), h)),
                pl.BlockSpec((tk, dk), lambda h, i, j: (kv_row(i, nxt(j)), h)),
                pl.BlockSpec((dv, tk), lambda h, i, j: (h, kv_row(i, j)))]
    args = [q, k, k, vt]
    if q2 is not None:
        in_specs += [pl.BlockSpec((groups, tq, LANE), lambda h, i, j: (h, qb0 + i, 0)),
                     pl.BlockSpec((tk, LANE), lambda h, i, j: (kv_row(i, 0), 0)),
                     pl.BlockSpec((tk, LANE), lambda h, i, j: (kv_row(i, nxt(j)), 0))]
        args += [q2, k2, k2]
    if sink is not None:
        in_specs.append(pl.BlockSpec((1, 1, n), lambda h, i, j: (h, 0, 0)))
        sink2 = sink.astype(F32).reshape(kv_heads, groups, 1) * LOG2E
        args.append(jnp.broadcast_to(sink2, (kv_heads, groups, tq)).reshape(kv_heads, 1, n))
    kern = functools.partial(
        _flash_kernel, groups=groups, tq=tq, tk=tk, dv=dv, n_kv=n_kv, has_qk2=q2 is not None,
        has_sink=sink is not None, key_chunk=KEY_CHUNK)
    return pl.pallas_call(
        kern, grid=(kv_heads, n_q // tq, n_kv), in_specs=in_specs,
        out_specs=pl.BlockSpec((tq, groups * dv), lambda h, i, j: (i, h)),
        out_shape=jax.ShapeDtypeStruct((n_q, kv_heads * groups * dv), BF16),
        scratch_shapes=[pltpu.VMEM((1, n), F32), pltpu.VMEM((1, n), F32), pltpu.VMEM((dv, n), F32),
                        pltpu.VMEM((tk, n), F32), pltpu.VMEM((1, n), F32)],
        compiler_params=_params("parallel", "parallel", "arbitrary"),
        name="flash_dense",
    )(*args)


def _window_kernel(*refs, groups, tq, n_lat, has_sink):
    q_ref = refs[0]
    n_half = tq // WINDOW + 2
    k_refs = refs[1:2 + n_half]
    vt_refs = refs[2 + n_half:3 + 2 * n_half]
    pos = 3 + 2 * n_half
    if has_sink:
        sink_ref = refs[pos]
        pos += 1
    o_ref = refs[pos]
    i = pl.program_id(1)
    n = groups * tq
    dv = vt_refs[0].shape[0]
    q = q_ref[...].reshape(n, q_ref.shape[2])
    qpos = i * tq + lax.broadcasted_iota(jnp.int32, (1, n), 1) % tq
    scores = []
    m = sink_ref[0] if has_sink else jnp.full((1, n), NEG_INF, F32)
    for b, k_ref in enumerate(k_refs):
        s = lax.dot_general(k_ref[...], q, (((1,), (1,)), ((), ())), preferred_element_type=F32)
        if b > 0:
            kpos = i * tq + (b - 2) * WINDOW + lax.broadcasted_iota(jnp.int32, (WINDOW, 1), 0)
            valid = (jnp.abs(qpos - kpos) <= WINDOW) & (kpos >= 0) & (kpos < n_lat)
            s = jnp.where(valid, s, NEG_INF)
        scores.append(s)
        m = jnp.maximum(m, jnp.max(s, axis=0, keepdims=True))
    l = jnp.exp2(sink_ref[0] - m) if has_sink else jnp.zeros((1, n), F32)
    acc = jnp.zeros((dv, n), F32)
    for s, vt_ref in zip(scores, vt_refs):
        p = jnp.exp2(s - m)
        l = l + jnp.sum(p, axis=0, keepdims=True)
        acc = acc + _dot(vt_ref[...], p.astype(BF16))
    o_t = acc / l
    for g in range(groups):
        o_ref[:, g * dv:(g + 1) * dv] = o_t[:, g * tq:(g + 1) * tq].T.astype(o_ref.dtype)


def window_attention(q, k, vt, *, kv_heads, groups, n_lat, n_ctx, tq, sink=None, dv=HEAD_DIM):
    dk = q.shape[2]
    per = tq // WINDOW
    n_half_blocks = n_lat // WINDOW
    n = groups * tq
    ctx_blk = n_lat // n_ctx
    half = lambda b: (lambda i: jnp.clip(i * per + b - 1, 0, n_half_blocks - 1))
    halves = [half(b) for b in range(per + 2)]
    in_specs = [pl.BlockSpec((groups, tq, dk), lambda h, i: (h, i, 0)),
                pl.BlockSpec((n_ctx, dk), lambda h, i: (ctx_blk, h))]
    in_specs += [pl.BlockSpec((WINDOW, dk), lambda h, i, f=f: (f(i), h)) for f in halves]
    in_specs += [pl.BlockSpec((dv, n_ctx), lambda h, i: (h, ctx_blk))]
    in_specs += [pl.BlockSpec((dv, WINDOW), lambda h, i, f=f: (h, f(i))) for f in halves]
    args = [q] + [k] * (per + 3) + [vt] * (per + 3)
    if sink is not None:
        in_specs.append(pl.BlockSpec((1, 1, n), lambda h, i: (h, 0, 0)))
        sink2 = sink.astype(F32).reshape(kv_heads, groups, 1) * LOG2E
        args.append(jnp.broadcast_to(sink2, (kv_heads, groups, tq)).reshape(kv_heads, 1, n))
    return pl.pallas_call(
        functools.partial(_window_kernel, groups=groups, tq=tq, n_lat=n_lat, has_sink=sink is not None),
        grid=(kv_heads, n_lat // tq), in_specs=in_specs,
        out_specs=pl.BlockSpec((tq, groups * dv), lambda h, i: (i, h)),
        out_shape=jax.ShapeDtypeStruct((n_lat, kv_heads * groups * dv), BF16),
        compiler_params=_params("parallel", "parallel"), name="window_attention",
    )(*args)


def _outproj_kernel(o1c_ref, o1l_ref, o2c_ref, o2l_ref, gate_ref, h_ref, mod_ref, w_ref, out_ref, *, lat_tiles):
    is_ctx = pl.program_id(0) >= lat_tiles
    w1 = o1l_ref.shape[1]
    d = h_ref.shape[1]
    gt = gate_ref[...].astype(F32)
    sg = gt * jax.nn.sigmoid(gt)
    o1 = jnp.where(is_ctx, o1c_ref[...].astype(F32), o1l_ref[...].astype(F32))
    o2 = jnp.where(is_ctx, o2c_ref[...].astype(F32), o2l_ref[...].astype(F32))
    t1 = (o1 * sg[:, :w1]).astype(BF16)
    t2 = (o2 * sg[:, w1:]).astype(BF16)
    y = _dot(t1, w_ref[0:w1, :]) + _dot(t2, w_ref[w1:, :])
    gm = jnp.where(is_ctx, mod_ref[1:2, 2 * d:3 * d], mod_ref[0:1, 2 * d:3 * d])
    out_ref[...] = h_ref[...] + gm * y


def out_proj(o1c, o1l, o2c, o2l, proj, h, mod, w, latents_only):
    r, d = h.shape
    tm = ROW_TILE
    assert o1c.shape[0] == tm, "the context rows must be exactly one row tile"
    lt = o1l.shape[0] // tm
    w1, w2 = o1l.shape[1], o2l.shape[1]
    lat = lambda i: (jnp.minimum(i, lt - 1), 0)
    if latents_only:
        r = lt * tm
    return pl.pallas_call(
        functools.partial(_outproj_kernel, lat_tiles=lt), grid=(r // tm,),
        in_specs=[pl.BlockSpec((tm, w1), lambda i: (0, 0)), pl.BlockSpec((tm, w1), lat),
                  pl.BlockSpec((tm, w2), lambda i: (0, 0)), pl.BlockSpec((tm, w2), lat),
                  pl.BlockSpec((tm, w1 + w2), lambda i: (i, OFF_GATE // (w1 + w2))),
                  pl.BlockSpec((tm, d), lambda i: (i, 0)),
                  pl.BlockSpec(mod.shape, lambda i: (0, 0)),
                  pl.BlockSpec(w.shape, lambda i: (0, 0))],
        out_specs=pl.BlockSpec((tm, d), lambda i: (i, 0)),
        out_shape=jax.ShapeDtypeStruct((r, d), F32),
        compiler_params=_params("parallel"), name="out_proj",
    )(o1c, o1l, o2c, o2l, proj, h, mod, w)


def _conv3_kernel(x_ref, w_ref, b_ref, oc_ref, ol_ref, *, n_ctx):
    x = x_ref[...].astype(F32)
    r = x.shape[0]
    n_lat = r - n_ctx
    row = lax.broadcasted_iota(jnp.int32, (r, 1), 0)
    prev = jnp.where((row == 0) | (row == n_lat), 0.0, pltpu.roll(x, 1, 0))
    nxt = jnp.where((row == n_lat - 1) | (row == r - 1), 0.0, pltpu.roll(x, r - 1, 0))
    y = prev * w_ref[0:1, :] + x * w_ref[1:2, :] + nxt * w_ref[2:3, :] + b_ref[...]
    ol_ref[0] = y[:n_lat]
    oc_ref[0] = y[n_lat:]


def conv3(proj, col_off, w, b, n_ctx):
    r = proj.shape[0]
    cw = w.shape[1] // 3
    nb = cw // LANE
    cb = col_off // LANE
    return pl.pallas_call(
        functools.partial(_conv3_kernel, n_ctx=n_ctx), grid=(3, nb),
        in_specs=[pl.BlockSpec((r, LANE), lambda p, c: (0, cb + p * nb + c)),
                  pl.BlockSpec((3, LANE), lambda p, c: (0, p * nb + c)),
                  pl.BlockSpec((1, LANE), lambda p, c: (0, p * nb + c))],
        out_specs=[pl.BlockSpec((1, n_ctx, LANE), lambda p, c: (p, 0, c)),
                   pl.BlockSpec((1, r - n_ctx, LANE), lambda p, c: (p, 0, c))],
        out_shape=[jax.ShapeDtypeStruct((3, n_ctx, cw), F32), jax.ShapeDtypeStruct((3, r - n_ctx, cw), F32)],
        compiler_params=_params("parallel", "parallel"), name="conv3",
    )(proj, w, b.reshape(1, -1))


def _filt_kernel(z_ref, w1_ref, b1_ref, f1_ref, w2_ref, b2_ref, f2_ref, w3_ref, dl_ref, h_ref, tap_ref, nrm_ref, *,
                 tm, cw):
    i = pl.program_id(0)
    z = z_ref[...]

    def mm(a, w_r):
        return _dot3(*_split_bf16(a), *_split_bf16(w_r[...]))

    a = jnp.sin(f1_ref[...] * (mm(z, w1_ref) + b1_ref[...]))
    a = jnp.sin(f2_ref[...] * (mm(a, w2_ref) + b2_ref[...]))
    dec = jnp.exp(-z[:, 0:1] * dl_ref[...])
    h = mm(a, w3_ref) * jnp.concatenate([dec] * (h_ref.shape[1] // cw), axis=1)
    h_ref[...] = h.astype(h_ref.dtype)
    row = i * tm + lax.broadcasted_iota(jnp.int32, (tm, 1), 0)
    col = lax.broadcasted_iota(jnp.int32, (1, h.shape[1]), 1)
    skip = (row == 0) & ((col // cw) % 2 == 1)
    part = jnp.sum(jnp.where(skip, 0.0, jnp.abs(h)), axis=0, keepdims=True)

    @pl.when(i == 0)
    def _():
        nrm_ref[...] = jnp.zeros(nrm_ref.shape, F32)
        tap_ref[...] = h[0:SUBLANES]

    nrm_ref[...] += jnp.broadcast_to(part, nrm_ref.shape)


def hyena_filter(z, w1p, b1, f1, w2, b2, f2, w3, deltas):
    l = z.shape[0]
    n = w3.shape[1]
    cw = deltas.shape[0]
    hid = w2.shape[0]
    tm = min(l, 256)
    full = lambda a: pl.BlockSpec(a.shape, lambda i: (0, 0))
    ins = [w1p, b1.reshape(1, hid), f1.reshape(1, hid), w2, b2.reshape(1, hid), f2.reshape(1, hid), w3,
           deltas.reshape(1, cw)]
    small = pl.BlockSpec((SUBLANES, n), lambda i: (0, 0))
    return pl.pallas_call(
        functools.partial(_filt_kernel, tm=tm, cw=cw), grid=(l // tm,),
        in_specs=[pl.BlockSpec((tm, LANE), lambda i: (i, 0))] + [full(a) for a in ins],
        out_specs=[pl.BlockSpec((tm, n), lambda i: (i, 0)), small, small],
        out_shape=[jax.ShapeDtypeStruct((l, n), BF16), jax.ShapeDtypeStruct((SUBLANES, n), F32),
                   jax.ShapeDtypeStruct((SUBLANES, n), F32)],
        compiler_params=_params("arbitrary"), name="hyena_filter",
    )(z, *ins)


@functools.lru_cache(maxsize=None)
def _dft_consts(a):
    n = a * a
    hh = a // 2
    n1 = np.arange(hh)
    k1 = np.arange(hh)
    n2 = np.arange(a)
    k2 = np.arange(a)
    ang1 = 2 * np.pi * np.outer(k1 + 0.5, n1) / a
    m1 = np.zeros((hh, 2, hh))
    m1[:, 0], m1[:, 1] = np.cos(ang1), -np.sin(ang1)
    m1 = m1.reshape(2 * hh, hh)
    phi = 2 * np.pi * (n2[None, None, :] * (k1[:, None, None] + 0.5) / n + n2[None, None, :] * k2[None, :, None] / a)
    c, s = np.cos(phi), np.sin(phi)
    g = np.zeros((hh, 2, a, 2, a))
    g[:, 0, :, 0, :], g[:, 0, :, 1, :], g[:, 1, :, 0, :], g[:, 1, :, 1, :] = c, s, -s, c
    g = g.reshape(hh, 2 * a, 2 * a)
    gt = np.ascontiguousarray(np.transpose(g, (0, 2, 1)))
    al = 2 * np.pi * np.outer(n1, k1 + 0.5) / a
    mb = np.zeros((hh, hh, 2))
    mb[:, :, 0], mb[:, :, 1] = 2 / n * np.cos(al), -2 / n * np.sin(al)
    mb = mb.reshape(hh, 2 * hh)
    eye = np.eye(ROWS_BF16)
    return tuple(x.astype(np.float32) for x in (np.kron(m1, eye), g, gt, np.kron(mb, eye)))


@functools.lru_cache(maxsize=None)
def _dense_dft_consts(l):
    n = 2 * l
    ang = 2 * np.pi * np.outer(np.arange(l) + 0.5, np.arange(l)) / n
    mf = np.concatenate([np.cos(ang), -np.sin(ang)], axis=0)
    mi = np.concatenate([2 / n * np.cos(ang.T), -2 / n * np.sin(ang.T)], axis=1)
    return tuple(x.astype(np.float32) for x in (mf, mi))


def _stage1_kernel(x_ref, a_ref, o_ref):
    _, hh, t, ct = x_ref.shape
    x = x_ref[0].reshape(hh * t, ct).astype(BF16)
    o_ref[...] = _dot(a_ref[...], x).astype(o_ref.dtype).reshape(o_ref.shape)


def dft_stage1(x4, p, a1):
    _, hh, a, c = x4.shape
    ct = _pick_tile(c, 512)
    t = ROWS_BF16
    return pl.pallas_call(
        _stage1_kernel, grid=(a // t, c // ct),
        in_specs=[pl.BlockSpec((1, hh, t, ct), lambda j, q: (p, 0, j, q)),
                  pl.BlockSpec(a1.shape, lambda j, q: (0, 0))],
        out_specs=pl.BlockSpec((2 * hh, t, ct), lambda j, q: (0, j, q)),
        out_shape=jax.ShapeDtypeStruct((2 * hh, a, c), BF16),
        compiler_params=_params("parallel", "parallel"), name="dft_stage1",
    )(x4, a1)


def _spectrum_product(x, h0, h1, tap0, n0, n1):
    half = x.shape[0] // 2
    inv = 1.0 / (n0 + n1)
    kr = (h0[:half] + h1[:half] - tap0) * inv
    ki = (h0[half:] - h1[half:]) * inv
    xr, xi = x[:half], x[half:]
    return jnp.concatenate([xr * kr - xi * ki, xr * ki + xi * kr], axis=0)


K1_PER_STEP = 2


def _mid_filter_kernel(y0_ref, y1_ref, g_ref, tap_ref, n0_ref, n1_ref, o_ref):
    inv = 1.0 / (n0_ref[0:1, :] + n1_ref[0:1, :])
    tap = tap_ref[0:1, :]
    for b in range(y0_ref.shape[0]):
        x0 = _dot(g_ref[b], y0_ref[b])
        x1 = _dot(g_ref[b], y1_ref[b])
        half = x0.shape[0] // 2
        kr = (x0[:half] + x1[:half] - tap) * inv
        ki = (x0[half:] - x1[half:]) * inv
        o_ref[b] = jnp.concatenate([kr, ki], axis=0).astype(o_ref.dtype)


def filter_spectrum(h, taps, nrm, consts, a):
    a1, g, _, _ = consts
    l, n = h.shape
    c = n // 4
    hh = a // 2
    y = dft_stage1(h.reshape(1, hh, a, n), 0, a1).reshape(hh, 2 * a, n)
    kb = K1_PER_STEP
    slab = lambda side: pl.BlockSpec((kb, 2 * a, c), lambda k, o: (k, 0, 2 * o + side))
    row = lambda side: pl.BlockSpec((SUBLANES, c), lambda k, o: (0, 2 * o + side))
    return pl.pallas_call(
        _mid_filter_kernel, grid=(hh // kb, 2),
        in_specs=[slab(0), slab(1), pl.BlockSpec((kb, 2 * a, 2 * a), lambda k, o: (k, 0, 0)), row(1), row(0), row(1)],
        out_specs=pl.BlockSpec((kb, 2 * a, c), lambda k, o: (k, 0, o)),
        out_shape=jax.ShapeDtypeStruct((hh, 2 * a, 2 * c), BF16),
        compiler_params=_params("parallel", "parallel"), name="dft_mid_filter",
    )(y, y, g, taps, nrm, nrm)


def _mid_conv_kernel(y_ref, g_ref, gt_ref, kf_ref, o_ref):
    for b in range(y_ref.shape[0]):
        x = _dot(g_ref[b], y_ref[b])
        kf = kf_ref[b].astype(F32)
        half = x.shape[0] // 2
        xr, xi, kr, ki = x[:half], x[half:], kf[:half], kf[half:]
        z = jnp.concatenate([xr * kr - xi * ki, xr * ki + xi * kr], axis=0).astype(BF16)
        o_ref[b] = _dot(gt_ref[b], z).astype(o_ref.dtype)


def dft_mid_conv(y, g, gt, kf, order):
    hh, a2, c = y.shape
    kb = K1_PER_STEP
    slab = pl.BlockSpec((kb, a2, c), lambda k: (k, 0, 0))
    mat = pl.BlockSpec((kb, a2, a2), lambda k: (k, 0, 0))
    return pl.pallas_call(
        _mid_conv_kernel, grid=(hh // kb,),
        in_specs=[slab, mat, mat, pl.BlockSpec((kb, a2, c), lambda k: (k, 0, order))],
        out_specs=slab, out_shape=jax.ShapeDtypeStruct((hh, a2, c), BF16),
        compiler_params=_params("parallel"), name="dft_mid_conv",
    )(y, g, gt, kf)


def _last_kernel(b_ref, a_ref, xa_ref, zb_ref, bias_ref, o_ref):
    rows, t, ct = b_ref.shape
    conv = _dot(a_ref[...], b_ref[...].reshape(rows * t, ct)).reshape(rows // 2, t, ct)
    o_ref[0] = (xa_ref[0] * (conv + bias_ref[...] * zb_ref[0])).astype(o_ref.dtype)


def dft_last(b, a2m, xa4, pa, zb4, pb, bias_row):
    rows, a, c = b.shape
    hh = rows // 2
    ct = _pick_tile(c, 512)
    t = ROWS_BF16
    sig = lambda pp: pl.BlockSpec((1, hh, t, ct), lambda j, q: (pp, 0, j, q))
    return pl.pallas_call(
        _last_kernel, grid=(a // t, c // ct),
        in_specs=[pl.BlockSpec((rows, t, ct), lambda j, q: (0, j, q)),
                  pl.BlockSpec(a2m.shape, lambda j, q: (0, 0)),
                  sig(pa), sig(pb), pl.BlockSpec((1, ct), lambda j, q: (0, q))],
        out_specs=sig(0), out_shape=jax.ShapeDtypeStruct((1, hh, a, c), F32),
        compiler_params=_params("parallel", "parallel"), name="dft_last",
    )(b, a2m, xa4, zb4, bias_row)


def long_conv_gated(u3, kf, bias, consts, a):
    a1, g, gt, a2m = consts
    _, l, c = u3.shape
    hh = a // 2
    u4 = u3.reshape(3, hh, a, c)

    def conv_gate(src4, p_src, xa_p, order):
        y = dft_stage1(src4, p_src, a1).reshape(hh, 2 * a, c)
        bb = dft_mid_conv(y, g, gt, kf, order).reshape(2 * hh, a, c)
        return dft_last(bb, a2m, u4, xa_p, src4, p_src, bias[order].reshape(1, c))

    z2 = conv_gate(u4, 2, 0, 0)
    return conv_gate(z2, 0, 1, 1).reshape(l, c)


def _ctx_spec_kernel(h_ref, mf_ref, o_ref):
    o_ref[...] = _dot(mf_ref[...], h_ref[...])


def ctx_filter_spectrum(h, mf):
    l, n = h.shape
    tn = _pick_tile(n, 1024)
    return pl.pallas_call(
        _ctx_spec_kernel, grid=(n // tn,),
        in_specs=[pl.BlockSpec((l, tn), lambda j: (0, j)), pl.BlockSpec(mf.shape, lambda j: (0, 0))],
        out_specs=pl.BlockSpec((2 * l, tn), lambda j: (0, j)),
        out_shape=jax.ShapeDtypeStruct((2 * l, n), F32),
        compiler_params=_params("parallel"), name="ctx_filter_spectrum",
    )(h, mf)


def _ctx_hyena_kernel(u_ref, mf_ref, mi_ref, hf0a, hf1a, hf0b, hf1b, tap0, tap1, n0a, n1a, n0b, n1b, bias_ref, o_ref):
    x1, x2, z = u_ref[0], u_ref[1], u_ref[2]

    def conv(sig, h0_ref, h1_ref, tap_ref, n0_ref, n1_ref):
        x = _dot(mf_ref[...], sig.astype(BF16))
        zz = _spectrum_product(x, h0_ref[...], h1_ref[...], tap_ref[0:1, :], n0_ref[0:1, :], n1_ref[0:1, :])
        return _dot(mi_ref[...], zz.astype(BF16))

    z2 = x1 * (conv(z, hf0a, hf1a, tap0, n0a, n1a) + bias_ref[0:1, :] * z)
    o_ref[...] = (x2 * (conv(z2, hf0b, hf1b, tap1, n0b, n1b) + bias_ref[1:2, :] * z2)).astype(o_ref.dtype)


def ctx_hyena(u3, hf, taps, nrm, bias, mf, mi):
    _, l, c = u3.shape
    tc = _pick_tile(c, 512)
    nb = c // tc
    const = lambda a: pl.BlockSpec(a.shape, lambda j: (0, 0))
    col = lambda rows, q: pl.BlockSpec((rows, tc), lambda j: (0, q * nb + j))
    mats = [mf, mi]
    return pl.pallas_call(
        _ctx_hyena_kernel, grid=(nb,),
        in_specs=[pl.BlockSpec((3, l, tc), lambda j: (0, 0, j))] + [const(m) for m in mats]
        + [col(2 * l, 0), col(2 * l, 1), col(2 * l, 2), col(2 * l, 3), col(8, 1), col(8, 3),
           col(8, 0), col(8, 1), col(8, 2), col(8, 3), pl.BlockSpec((2, tc), lambda j: (0, j))],
        out_specs=pl.BlockSpec((l, tc), lambda j: (0, j)),
        out_shape=jax.ShapeDtypeStruct((l, c), BF16),
        compiler_params=_params("parallel"), name="ctx_hyena",
    )(u3, *mats, hf, hf, hf, hf, taps, taps, nrm, nrm, nrm, nrm, bias)


def _axial_tables(n_tokens, n_rot):
    rows = n_tokens // GRID_W
    row = jnp.broadcast_to(jnp.arange(rows)[:, None], (rows, GRID_W)).reshape(-1).astype(F32)
    col = jnp.broadcast_to(jnp.arange(GRID_W)[None, :], (rows, GRID_W)).reshape(-1).astype(F32)
    n_freq = n_rot // 4
    inv = ROPE_THETA ** (-jnp.arange(n_freq, dtype=F32) / n_freq)
    ang = jnp.concatenate([row[:, None] * inv, col[:, None] * inv], axis=-1)
    return jnp.cos(ang), jnp.sin(ang)


def _rope_tables_full(s, n_ctx):
    cos, sin = _axial_tables(s, HEAD_DIM)
    t0 = jnp.concatenate([jnp.concatenate([cos, cos], axis=1), jnp.ones((n_ctx, LANE), F32)], axis=0)
    t1 = jnp.concatenate([jnp.concatenate([-sin, sin], axis=1), jnp.zeros((n_ctx, LANE), F32)], axis=0)
    return t0, t1


def _rope_tables_half(s, n_ctx):
    cos, sin = _axial_tables(s, ROPE_DIM)
    q = ROPE_DIM // 2
    zq = jnp.zeros((s, q), F32)
    z2 = jnp.zeros((s, LANE - ROPE_DIM), F32)
    t0 = jnp.concatenate([cos, cos, z2], axis=1)
    t1 = jnp.concatenate([-sin, zq, z2], axis=1)
    t2 = jnp.concatenate([zq, sin, z2], axis=1)
    c0 = jnp.concatenate([jnp.ones((n_ctx, ROPE_DIM), F32), jnp.zeros((n_ctx, LANE - ROPE_DIM), F32)], axis=1)
    cz = jnp.zeros((n_ctx, LANE), F32)
    return jnp.concatenate([t0, c0], axis=0), jnp.concatenate([t1, cz], axis=0), jnp.concatenate([t2, cz], axis=0)


def _filter_features(l):
    pos = jnp.arange(l, dtype=F32)
    t = pos / max(l - 1, 1)
    bands = jnp.linspace(1e-4, HYENA_BANDS - 1, HYENA_BANDS, dtype=F32)
    ang = (2.0 * math.pi / l) * pos[:, None] * bands[None, :]
    z = jnp.concatenate([t[:, None], jnp.cos(ang), -jnp.sin(ang)], axis=-1)
    return jnp.pad(z, ((0, 0), (0, LANE - HYENA_EMB)))


Q_STACK = 1024
KV_TILE_CAP = 1408


def _attention_pair(qp, kp, vt, s, n_ctx, *, kv_heads, groups, window, sink=None, q2=None, k2=None):
    common = dict(kv_heads=kv_heads, groups=groups, q2=q2, k2=k2, sink=sink)
    o_ctx = flash(qp, kp, vt, n_q=n_ctx, q_row0=s, kv_row0=s, n_kv_rows=n_ctx, tq=n_ctx, tk=n_ctx, **common)
    if window:
        o_lat = window_attention(qp, kp, vt, kv_heads=kv_heads, groups=groups, n_lat=s, n_ctx=n_ctx, tq=ROW_TILE,
                                 sink=sink)
    else:
        o_lat = flash(qp, kp, vt, n_q=s, q_row0=0, kv_row0=0, n_kv_rows=n_ctx + s, tq=Q_STACK // groups,
                      tk=_pick_tile(n_ctx + s, KV_TILE_CAP), **common)
    return o_ctx, o_lat


def _even_layer(u, s, n_ctx, w_in, qn_g, kn_g, sink, conv_w, conv_b, fparams, hy_bias, rope_full, feats, dft):
    wq, wk, wv, whd, wg = jnp.split(w_in, np.cumsum([ATT_WIDTH, KV_WIDTH, KV_WIDTH, 3 * B_WIDTH])[:].tolist(), axis=1)
    proj = matmul(u, jnp.concatenate([wg, wq, wk, wv, whd], axis=1).astype(BF16), F32)
    qp = head_prep(proj, OFF_Q, A_HEADS, qn_g, rope_full, "half64", HEAD_DIM, HEAD_DIM ** -0.5 * LOG2E, True)
    kp = head_prep(proj, OFF_K, A_KV_HEADS, kn_g, rope_full, "half64", HEAD_DIM, 1.0)
    vt = transpose_heads(proj, OFF_V, A_KV_HEADS)
    a_ctx, a_lat = _attention_pair(qp, kp, vt, s, n_ctx, kv_heads=A_KV_HEADS, groups=A_HEADS // A_KV_HEADS,
                                   window=True, sink=sink)
    w1, b1, f1, w2, b2, f2, w3 = fparams
    w1p = jnp.pad(w1, ((0, LANE - HYENA_EMB), (0, 0)))
    deltas = jnp.linspace(DECAY_MAX, DECAY_MIN, B_WIDTH, dtype=F32)
    uc3, ul3 = conv3(proj, OFF_REST, conv_w, conv_b, n_ctx)
    a = int(round(math.sqrt(2 * s)))
    consts, (mf, mi) = dft
    h_lat, tap_lat, nrm_lat = hyena_filter(feats[0], w1p, b1, f1, w2, b2, f2, w3, deltas)
    kf_lat = filter_spectrum(h_lat, tap_lat, nrm_lat, consts, a)
    b_lat = long_conv_gated(ul3, kf_lat, hy_bias, consts, a)
    h_ctx, tap_ctx, nrm_ctx = hyena_filter(feats[1], w1p, b1, f1, w2, b2, f2, w3, deltas)
    hf_ctx = ctx_filter_spectrum(h_ctx, mf)
    b_ctx = ctx_hyena(uc3, hf_ctx, tap_ctx, nrm_ctx, hy_bias, mf, mi)
    return proj, a_ctx, a_lat, b_ctx, b_lat


def _odd_layer(u, s, n_ctx, w_in, qn_g, kn_g, cq_g, ckv_g, wuq, wukv, mq_g, mk_g, rope_full, rope_half):
    d = w_in.shape[0]
    parts = jnp.split(w_in, np.cumsum([ATT_WIDTH, KV_WIDTH, KV_WIDTH, Q_LORA, KV_LORA, ROPE_DIM]).tolist(), axis=1)
    wq, wk, wv, wmq, wmkv, wmkr, wg = parts
    w_perm = jnp.concatenate([wg, wq, wk, wv, wmq, wmkv, wmkr, jnp.zeros((d, LANE - ROPE_DIM), w_in.dtype)], axis=1)
    proj = matmul(u, w_perm.astype(BF16), F32)
    qp = head_prep(proj, OFF_Q, C_HEADS, qn_g, rope_full, "half64", HEAD_DIM, HEAD_DIM ** -0.5 * LOG2E, True)
    kp = head_prep(proj, OFF_K, C_KV_HEADS, kn_g, rope_full, "half64", HEAD_DIM, 1.0)
    vt = transpose_heads(proj, OFF_V, C_KV_HEADS)
    c_ctx, c_lat = _attention_pair(qp, kp, vt, s, n_ctx, kv_heads=C_KV_HEADS, groups=C_HEADS // C_KV_HEADS,
                                   window=False)
    pad_r = LANE - ROPE_DIM
    wuq3 = wuq.reshape(Q_LORA, M_HEADS, QK_DIM)
    wuq_p = jnp.concatenate([wuq3[:, :, :NOPE_DIM].reshape(Q_LORA, -1),
                             jnp.pad(wuq3[:, :, NOPE_DIM:], ((0, 0), (0, 0), (0, pad_r))).reshape(Q_LORA, -1)], axis=1)
    wukv3 = wukv.reshape(KV_LORA, M_HEADS, NOPE_DIM + V_DIM)
    wukv_p = jnp.concatenate([wukv3[:, :, :NOPE_DIM].reshape(KV_LORA, -1),
                              wukv3[:, :, NOPE_DIM:].reshape(KV_LORA, -1)], axis=1)
    q_raw = norm_matmul(proj, OFF_MQ, cq_g, wuq_p.astype(BF16))
    kv_raw = norm_matmul(proj, OFF_MKV, ckv_g, wukv_p.astype(BF16))
    scale = QK_DIM ** -0.5 * LOG2E
    gq_r = jnp.pad(mq_g[NOPE_DIM:], (0, pad_r))
    gk_r = jnp.pad(mk_g[NOPE_DIM:], (0, pad_r))
    qn = head_prep(q_raw, 0, M_HEADS, mq_g[:NOPE_DIM], (), "none", NOPE_DIM, scale, True)
    qr = head_prep(q_raw, M_HEADS * NOPE_DIM, M_HEADS, gq_r, rope_half, "half32", ROPE_DIM, scale, True)
    kn = head_prep(kv_raw, 0, M_HEADS, mk_g[:NOPE_DIM], (), "none", NOPE_DIM, 1.0)
    kr = head_prep(proj, OFF_MKR, 1, gk_r, rope_half, "half32", ROPE_DIM, 1.0)
    vmt = transpose_heads(kv_raw, M_HEADS * NOPE_DIM, M_HEADS)
    d_ctx, d_lat = _attention_pair(qn, kn, vmt, s, n_ctx, kv_heads=M_HEADS, groups=1, window=False, q2=qr, k2=kr)
    return proj, c_ctx, c_lat, d_ctx, d_lat


def kernel(x, c, ctx, c_ctx, ada_w, ada_b, norm_g, w_out, ev_w_in, ev_qn_g, ev_kn_g, ev_sink, ev_conv_w, ev_conv_b, hy_w1, hy_b1, hy_f1, hy_w2, hy_b2, hy_f2, hy_w3, hy_bias, od_w_in, od_qn_g, od_kn_g, od_cq_g, od_ckv_g, od_wuq, od_wukv, od_mq_g, od_mk_g):
    _, s, d = x.shape
    n_ctx = ctx.shape[1]
    depth = ada_w.shape[0]
    h = jnp.concatenate([x[0], ctx[0]], axis=0)
    cc = jnp.zeros((8, d), F32).at[0].set(c[0]).at[1].set(c_ctx)
    mod = ada_mod(cc, ada_w, ada_b)
    rope_full = _rope_tables_full(s, n_ctx)
    rope_half = _rope_tables_half(s, n_ctx)
    feats = (_filter_features(s), _filter_features(n_ctx))
    to_bf16 = lambda mats: tuple(jnp.asarray(m).astype(BF16) for m in mats)
    dft = (to_bf16(_dft_consts(int(round(math.sqrt(2 * s))))), to_bf16(_dense_dft_consts(n_ctx)))
    for i in range(depth):
        u = norm_mod(h, norm_g[i], mod[i], n_ctx)
        if i % 2 == 0:
            e = i // 2
            proj, o1c, o1l, o2c, o2l = _even_layer(
                u, s, n_ctx, ev_w_in[e], ev_qn_g[e], ev_kn_g[e], ev_sink[e], ev_conv_w[e], ev_conv_b[e],
                (hy_w1[e], hy_b1[e], hy_f1[e], hy_w2[e], hy_b2[e], hy_f2[e], hy_w3[e]), hy_bias[e], rope_full, feats,
                dft)
        else:
            o = i // 2
            proj, o1c, o1l, o2c, o2l = _odd_layer(
                u, s, n_ctx, od_w_in[o], od_qn_g[o], od_kn_g[o], od_cq_g[o], od_ckv_g[o], od_wuq[o], od_wukv[o],
                od_mq_g[o], od_mk_g[o], rope_full, rope_half)
        h = out_proj(o1c, o1l, o2c, o2l, proj, h, mod[i], w_out[i].astype(BF16), latents_only=i == depth - 1)
    return h[None]
```

```python
import functools
import math

import numpy as np
import jax
import jax.numpy as jnp
from jax import lax
from jax.experimental import pallas as pl
from jax.experimental.pallas import tpu as pltpu

F32 = jnp.float32
BF16 = jnp.bfloat16

GRID_W = 64
HEAD_DIM = 128
ROPE_THETA = 10000.0
EPS = 1e-6
NEG_INF = -1e30
A_HEADS, A_KV_HEADS, WINDOW = 8, 2, 128
B_WIDTH = 1024
HYENA_BANDS = 16
HYENA_EMB = 1 + 2 * HYENA_BANDS
DECAY_TARGET = 1e-2
DECAY_MAX = abs(math.log(DECAY_TARGET)) / 0.3
DECAY_MIN = abs(math.log(DECAY_TARGET)) / 1.5
C_HEADS, C_KV_HEADS = 8, 2
M_HEADS, Q_LORA, KV_LORA, NOPE_DIM, ROPE_DIM, V_DIM = 8, 512, 256, 128, 64, 128
QK_DIM = NOPE_DIM + ROPE_DIM
ATT_WIDTH = A_HEADS * HEAD_DIM
KV_WIDTH = A_KV_HEADS * HEAD_DIM
BRANCH = ATT_WIDTH + B_WIDTH

LOG2E = math.log2(math.e)
LANE = 128
SUBLANES = 8
ROWS_BF16 = 16
ROW_TILE = 256
VMEM_LIMIT = 48 * 1024 * 1024

OFF_GATE, OFF_Q, OFF_K, OFF_V, OFF_REST = 0, BRANCH, BRANCH + ATT_WIDTH, BRANCH + ATT_WIDTH + KV_WIDTH, BRANCH + ATT_WIDTH + 2 * KV_WIDTH
OFF_MQ, OFF_MKV, OFF_MKR = OFF_REST, OFF_REST + Q_LORA, OFF_REST + Q_LORA + KV_LORA


def _params(*sem):
    return pltpu.CompilerParams(dimension_semantics=sem, vmem_limit_bytes=VMEM_LIMIT)


def _split_bf16(x):
    hi = x.astype(BF16)
    return hi, (x - hi.astype(F32)).astype(BF16)


def _dot(a, b):
    return jnp.dot(a, b, preferred_element_type=F32)


def _dot3(a_hi, a_lo, b_hi, b_lo):
    return _dot(a_hi, b_hi) + (_dot(a_hi, b_lo) + _dot(a_lo, b_hi))


def _ada_kernel(c_ref, w_ref, b_ref, o_ref):
    c = c_ref[...]
    a = (c * jax.nn.sigmoid(c)).astype(BF16)
    o_ref[0] = _dot(a, w_ref[0].astype(BF16)) + b_ref[0]


def ada_mod(cc, ada_w, ada_b):
    depth, d, n = ada_w.shape
    tn = _pick_tile(n, 1024)
    return pl.pallas_call(
        _ada_kernel, grid=(depth, n // tn),
        in_specs=[pl.BlockSpec((8, d), lambda l, j: (0, 0)),
                  pl.BlockSpec((1, d, tn), lambda l, j: (l, 0, j)),
                  pl.BlockSpec((1, 1, tn), lambda l, j: (l, 0, j))],
        out_specs=pl.BlockSpec((1, 8, tn), lambda l, j: (l, 0, j)),
        out_shape=jax.ShapeDtypeStruct((depth, 8, n), F32),
        compiler_params=_params("parallel", "parallel"), name="ada_mod",
    )(cc, ada_w, ada_b.reshape(depth, 1, n))


def _modulated_norm(x, g, mod_ref, is_ctx):
    d = x.shape[1]
    y = x * lax.rsqrt(jnp.mean(x * x, axis=-1, keepdims=True) + EPS) * g
    sh = jnp.where(is_ctx, mod_ref[1:2, 0:d], mod_ref[0:1, 0:d])
    sc = jnp.where(is_ctx, mod_ref[1:2, d:2 * d], mod_ref[0:1, d:2 * d])
    return y * (1.0 + sc) + sh


def _norm_kernel(x_ref, g_ref, mod_ref, o_ref, *, lat_tiles):
    is_ctx = pl.program_id(0) >= lat_tiles
    o_ref[...] = _modulated_norm(x_ref[...], g_ref[...], mod_ref, is_ctx).astype(o_ref.dtype)


def norm_mod(h, g, mod, n_ctx):
    r, d = h.shape
    tm = ROW_TILE
    return pl.pallas_call(
        functools.partial(_norm_kernel, lat_tiles=(r - n_ctx) // tm), grid=(r // tm,),
        in_specs=[pl.BlockSpec((tm, d), lambda i: (i, 0)),
                  pl.BlockSpec((1, d), lambda i: (0, 0)),
                  pl.BlockSpec(mod.shape, lambda i: (0, 0))],
        out_specs=pl.BlockSpec((tm, d), lambda i: (i, 0)),
        out_shape=jax.ShapeDtypeStruct((r, d), BF16),
        compiler_params=_params("parallel"), name="norm_mod",
    )(h, g.reshape(1, d), mod)


def _mm_kernel(a_ref, b_ref, o_ref):
    o_ref[...] = _dot(a_ref[...], b_ref[...]).astype(o_ref.dtype)


def _pick_tile(n, cap, unit=LANE):
    best = unit
    for t in range(unit, cap + 1, unit):
        if n % t == 0:
            best = t
    return best


def matmul(a, b, out_dtype):
    m, k = a.shape
    n = b.shape[1]
    tm = _pick_tile(m, 1536, 8)
    tn = _pick_tile(n, 1024)
    return pl.pallas_call(
        _mm_kernel, grid=(m // tm, n // tn),
        in_specs=[pl.BlockSpec((tm, k), lambda i, j: (i, 0)),
                  pl.BlockSpec((k, tn), lambda i, j: (0, j))],
        out_specs=pl.BlockSpec((tm, tn), lambda i, j: (i, j)),
        out_shape=jax.ShapeDtypeStruct((m, n), out_dtype),
        compiler_params=_params("parallel", "parallel"), name="in_proj",
    )(a, b)


def _normmm_kernel(x_ref, g_ref, w_ref, o_ref):
    x = x_ref[...].astype(F32)
    y = x * lax.rsqrt(jnp.mean(x * x, axis=-1, keepdims=True) + EPS) * g_ref[...]
    o_ref[...] = _dot(y.astype(BF16), w_ref[...]).astype(o_ref.dtype)


def norm_matmul(x, col_off, g, w):
    r = x.shape[0]
    k, n = w.shape
    tm = _pick_tile(r, 768, 8)
    cb = col_off // k
    return pl.pallas_call(
        _normmm_kernel, grid=(r // tm,),
        in_specs=[pl.BlockSpec((tm, k), lambda i: (i, cb)),
                  pl.BlockSpec((1, k), lambda i: (0, 0)),
                  pl.BlockSpec((k, n), lambda i: (0, 0))],
        out_specs=pl.BlockSpec((tm, n), lambda i: (i, 0)),
        out_shape=jax.ShapeDtypeStruct((r, n), F32),
        compiler_params=_params("parallel"), name="norm_matmul",
    )(x, g.reshape(1, k), w)


def _prep_kernel(*refs, mode, n_real, scale, n_heads, head_major):
    x_ref, g_ref = refs[0], refs[1]
    o_ref = refs[-1]
    g = g_ref[...]
    tabs = [t[...] for t in refs[2:-1]]
    for h in range(n_heads):
        x = x_ref[:, h * LANE:(h + 1) * LANE].astype(F32)
        ms = jnp.sum(x * x, axis=-1, keepdims=True) * (1.0 / n_real)
        xn = x * lax.rsqrt(ms + EPS) * g
        if mode == "none":
            y = xn
        elif mode == "half64":
            y = xn * tabs[0] + pltpu.roll(xn, 64, 1) * tabs[1]
        else:
            y = xn * tabs[0] + pltpu.roll(xn, 96, 1) * tabs[1] + pltpu.roll(xn, 32, 1) * tabs[2]
        y = (y * scale).astype(o_ref.dtype)
        if head_major:
            o_ref[h] = y
        else:
            o_ref[:, h * LANE:(h + 1) * LANE] = y


def head_prep(x, col_off, n_heads, g, tables, mode, n_real, scale, head_major=False):
    r = x.shape[0]
    w = n_heads * LANE
    tm = _pick_tile(r, 768, 8)
    tab_spec = pl.BlockSpec((tm, LANE), lambda i: (i, 0))
    if head_major:
        out_spec = pl.BlockSpec((n_heads, tm, LANE), lambda i: (0, i, 0))
        out_shape = jax.ShapeDtypeStruct((n_heads, r, LANE), BF16)
    else:
        out_spec = pl.BlockSpec((tm, w), lambda i: (i, 0))
        out_shape = jax.ShapeDtypeStruct((r, w), BF16)
    return pl.pallas_call(
        functools.partial(_prep_kernel, mode=mode, n_real=n_real, scale=scale, n_heads=n_heads,
                          head_major=head_major),
        grid=(r // tm,),
        in_specs=[pl.BlockSpec((tm, w), lambda i: (i, col_off // w)),
                  pl.BlockSpec((1, LANE), lambda i: (0, 0))] + [tab_spec] * len(tables),
        out_specs=out_spec, out_shape=out_shape,
        compiler_params=_params("parallel"), name="head_prep_" + mode,
    )(x, g.reshape(1, LANE), *tables)


def _vt_kernel(x_ref, o_ref):
    o_ref[...] = x_ref[...].astype(F32).T.astype(o_ref.dtype)


def transpose_heads(x, col_off, n_heads):
    r = x.shape[0]
    w = n_heads * LANE
    tm = _pick_tile(r, 768, LANE)
    return pl.pallas_call(
        _vt_kernel, grid=(r // tm,),
        in_specs=[pl.BlockSpec((tm, w), lambda i: (i, col_off // w))],
        out_specs=pl.BlockSpec((w, tm), lambda i: (0, i)),
        out_shape=jax.ShapeDtypeStruct((w, r), BF16),
        compiler_params=_params("parallel"), name="transpose_heads",
    )(x)


def _flash_kernel(*refs, groups, tq, tk, dv, n_kv, has_qk2, has_sink, key_chunk):
    q_ref, k_ref, kn_ref, vt_ref = refs[0], refs[1], refs[2], refs[3]
    pos = 4
    if has_qk2:
        q2_ref, k2_ref, k2n_ref = refs[4], refs[5], refs[6]
        pos = 7
    if has_sink:
        sink_ref = refs[pos]
        pos += 1
    o_ref, m_ref, l_ref, acc_ref, s_ref, mt_ref = refs[pos:pos + 6]
    j = pl.program_id(2)
    n = groups * tq
    bounds = [(r0, min(r0 + key_chunk, tk)) for r0 in range(0, tk, key_chunk)]

    q = q_ref[...].reshape(n, q_ref.shape[2])
    if has_qk2:
        q = jnp.concatenate([q, q2_ref[...].reshape(n, LANE)], axis=1)

    def scores(kr, k2r, r0, r1):
        kk = kr[r0:r1, :]
        if has_qk2:
            kk = jnp.concatenate([kk, k2r[r0:r1, :]], axis=1)
        return lax.dot_general(kk, q, (((1,), (1,)), ((), ())), preferred_element_type=F32)

    @pl.when(j == 0)
    def _():
        m_ref[...] = jnp.full(m_ref.shape, NEG_INF, F32)
        l_ref[...] = jnp.zeros(l_ref.shape, F32)
        acc_ref[...] = jnp.zeros(acc_ref.shape, F32)
        mt = jnp.full(mt_ref.shape, NEG_INF, F32)
        for r0, r1 in bounds:
            s = scores(k_ref, k2_ref if has_qk2 else None, r0, r1)
            s_ref[r0:r1, :] = s
            mt = jnp.maximum(mt, jnp.max(s, axis=0, keepdims=True))
        mt_ref[...] = mt

    def step(prefetch):
        m_prev = m_ref[...]
        m_new = jnp.maximum(m_prev, mt_ref[...])
        alpha = jnp.exp2(m_prev - m_new)
        l_new = alpha * l_ref[...]
        acc_new = alpha * acc_ref[...]
        mt = jnp.full(mt_ref.shape, NEG_INF, F32)
        for r0, r1 in bounds:
            if prefetch:
                s_next = scores(kn_ref, k2n_ref if has_qk2 else None, r0, r1)
            p = jnp.exp2(s_ref[r0:r1, :] - m_new)
            l_new = l_new + jnp.sum(p, axis=0, keepdims=True)
            acc_new = acc_new + _dot(vt_ref[:, r0:r1], p.astype(BF16))
            if prefetch:
                s_ref[r0:r1, :] = s_next
                mt = jnp.maximum(mt, jnp.max(s_next, axis=0, keepdims=True))
        l_ref[...] = l_new
        acc_ref[...] = acc_new
        m_ref[...] = m_new
        if prefetch:
            mt_ref[...] = mt

    if n_kv > 1:
        pl.when(j < n_kv - 1)(lambda: step(True))

    @pl.when(j == n_kv - 1)
    def _():
        step(False)
        m, l, acc = m_ref[...], l_ref[...], acc_ref[...]
        if has_sink:
            sk = sink_ref[0]
            m_fin = jnp.maximum(m, sk)
            a = jnp.exp2(m - m_fin)
            l = l * a + jnp.exp2(sk - m_fin)
            acc = acc * a
        o_t = acc / l
        for g in range(groups):
            o_ref[:, g * dv:(g + 1) * dv] = o_t[:, g * tq:(g + 1) * tq].T.astype(o_ref.dtype)


KEY_CHUNK = 256


def flash(q, k, vt, *, kv_heads, groups, n_q, q_row0, kv_row0, n_kv_rows, tq, tk, dv=HEAD_DIM,
          q2=None, k2=None, sink=None):
    dk = q.shape[2]
    qb0 = q_row0 // tq
    kb0 = kv_row0 // tk
    n_kv = n_kv_rows // tk

    def kv_row(i, j):
        return kb0 + j
    n = groups * tq
    nxt = lambda j: jnp.minimum(j + 1, n_kv - 1)
    in_specs = [pl.BlockSpec((groups, tq, dk), lambda h, i, j: (h, qb0 + i, 0)),
                pl.BlockSpec((tk, dk), lambda h, i, j: (kv_row(i, 0), h)),
                pl.BlockSpec((tk, dk), lambda h, i, j: (kv_row(i, nxt(j)), h)),
                pl.BlockSpec((dv, tk), lambda h, i, j: (h, kv_row(i, j)))]
    args = [q, k, k, vt]
    if q2 is not None:
        in_specs += [pl.BlockSpec((groups, tq, LANE), lambda h, i, j: (h, qb0 + i, 0)),
                     pl.BlockSpec((tk, LANE), lambda h, i, j: (kv_row(i, 0), 0)),
                     pl.BlockSpec((tk, LANE), lambda h, i, j: (kv_row(i, nxt(j)), 0))]
        args += [q2, k2, k2]
    if sink is not None:
        in_specs.append(pl.BlockSpec((1, 1, n), lambda h, i, j: (h, 0, 0)))
        sink2 = sink.astype(F32).reshape(kv_heads, groups, 1) * LOG2E
        args.append(jnp.broadcast_to(sink2, (kv_heads, groups, tq)).reshape(kv_heads, 1, n))
    kern = functools.partial(
        _flash_kernel, groups=groups, tq=tq, tk=tk, dv=dv, n_kv=n_kv, has_qk2=q2 is not None,
        has_sink=sink is not None, key_chunk=KEY_CHUNK)
    return pl.pallas_call(
        kern, grid=(kv_heads, n_q // tq, n_kv), in_specs=in_specs,
        out_specs=pl.BlockSpec((tq, groups * dv), lambda h, i, j: (i, h)),
        out_shape=jax.ShapeDtypeStruct((n_q, kv_heads * groups * dv), BF16),
        scratch_shapes=[pltpu.VMEM((1, n), F32), pltpu.VMEM((1, n), F32), pltpu.VMEM((dv, n), F32),
                        pltpu.VMEM((tk, n), F32), pltpu.VMEM((1, n), F32)],
        compiler_params=_params("parallel", "parallel", "arbitrary"),
        name="flash_dense",
    )(*args)


def _window_kernel(*refs, groups, tq, n_lat, has_sink):
    q_ref = refs[0]
    n_half = tq // WINDOW + 2
    k_refs = refs[1:2 + n_half]
    vt_refs = refs[2 + n_half:3 + 2 * n_half]
    pos = 3 + 2 * n_half
    if has_sink:
        sink_ref = refs[pos]
        pos += 1
    o_ref = refs[pos]
    i = pl.program_id(1)
    n = groups * tq
    dv = vt_refs[0].shape[0]
    q = q_ref[...].reshape(n, q_ref.shape[2])
    qpos = i * tq + lax.broadcasted_iota(jnp.int32, (1, n), 1) % tq
    scores = []
    m = sink_ref[0] if has_sink else jnp.full((1, n), NEG_INF, F32)
    for b, k_ref in enumerate(k_refs):
        s = lax.dot_general(k_ref[...], q, (((1,), (1,)), ((), ())), preferred_element_type=F32)
        if b > 0:
            kpos = i * tq + (b - 2) * WINDOW + lax.broadcasted_iota(jnp.int32, (WINDOW, 1), 0)
            valid = (jnp.abs(qpos - kpos) <= WINDOW) & (kpos >= 0) & (kpos < n_lat)
            s = jnp.where(valid, s, NEG_INF)
        scores.append(s)
        m = jnp.maximum(m, jnp.max(s, axis=0, keepdims=True))
    l = jnp.exp2(sink_ref[0] - m) if has_sink else jnp.zeros((1, n), F32)
    acc = jnp.zeros((dv, n), F32)
    for s, vt_ref in zip(scores, vt_refs):
        p = jnp.exp2(s - m)
        l = l + jnp.sum(p, axis=0, keepdims=True)
        acc = acc + _dot(vt_ref[...], p.astype(BF16))
    o_t = acc / l
    for g in range(groups):
        o_ref[:, g * dv:(g + 1) * dv] = o_t[:, g * tq:(g + 1) * tq].T.astype(o_ref.dtype)


def window_attention(q, k, vt, *, kv_heads, groups, n_lat, n_ctx, tq, sink=None, dv=HEAD_DIM):
    dk = q.shape[2]
    per = tq // WINDOW
    n_half_blocks = n_lat // WINDOW
    n = groups * tq
    ctx_blk = n_lat // n_ctx
    half = lambda b: (lambda i: jnp.clip(i * per + b - 1, 0, n_half_blocks - 1))
    halves = [half(b) for b in range(per + 2)]
    in_specs = [pl.BlockSpec((groups, tq, dk), lambda h, i: (h, i, 0)),
                pl.BlockSpec((n_ctx, dk), lambda h, i: (ctx_blk, h))]
    in_specs += [pl.BlockSpec((WINDOW, dk), lambda h, i, f=f: (f(i), h)) for f in halves]
    in_specs += [pl.BlockSpec((dv, n_ctx), lambda h, i: (h, ctx_blk))]
    in_specs += [pl.BlockSpec((dv, WINDOW), lambda h, i, f=f: (h, f(i))) for f in halves]
    args = [q] + [k] * (per + 3) + [vt] * (per + 3)
    if sink is not None:
        in_specs.append(pl.BlockSpec((1, 1, n), lambda h, i: (h, 0, 0)))
        sink2 = sink.astype(F32).reshape(kv_heads, groups, 1) * LOG2E
        args.append(jnp.broadcast_to(sink2, (kv_heads, groups, tq)).reshape(kv_heads, 1, n))
    return pl.pallas_call(
        functools.partial(_window_kernel, groups=groups, tq=tq, n_lat=n_lat, has_sink=sink is not None),
        grid=(kv_heads, n_lat // tq), in_specs=in_specs,
        out_specs=pl.BlockSpec((tq, groups * dv), lambda h, i: (i, h)),
        out_shape=jax.ShapeDtypeStruct((n_lat, kv_heads * groups * dv), BF16),
        compiler_params=_params("parallel", "parallel"), name="window_attention",
    )(*args)


def _outproj_kernel(*refs, lat_tiles, with_next):
    o1c_ref, o1l_ref, o2c_ref, o2l_ref, gate_ref, h_ref, mod_ref, w_ref = refs[:8]
    is_ctx = pl.program_id(0) >= lat_tiles
    w1 = o1l_ref.shape[1]
    d = h_ref.shape[1]
    gt = gate_ref[...].astype(F32)
    sg = gt * jax.nn.sigmoid(gt)
    o1 = jnp.where(is_ctx, o1c_ref[...].astype(F32), o1l_ref[...].astype(F32))
    o2 = jnp.where(is_ctx, o2c_ref[...].astype(F32), o2l_ref[...].astype(F32))
    t1 = (o1 * sg[:, :w1]).astype(BF16)
    t2 = (o2 * sg[:, w1:]).astype(BF16)
    y = _dot(t1, w_ref[0:w1, :]) + _dot(t2, w_ref[w1:, :])
    gm = jnp.where(is_ctx, mod_ref[1:2, 2 * d:3 * d], mod_ref[0:1, 2 * d:3 * d])
    h_new = h_ref[...] + gm * y
    if with_next:
        g_next_ref, mod_next_ref, out_ref, u_ref = refs[8:]
        u_ref[...] = _modulated_norm(h_new, g_next_ref[...], mod_next_ref, is_ctx).astype(u_ref.dtype)
    else:
        out_ref = refs[8]
    out_ref[...] = h_new


def out_proj(o1c, o1l, o2c, o2l, proj, h, mod, w, nxt):
    r, d = h.shape
    tm = ROW_TILE
    assert o1c.shape[0] == tm, "the context rows must be exactly one row tile"
    lt = o1l.shape[0] // tm
    w1, w2 = o1l.shape[1], o2l.shape[1]
    lat = lambda i: (jnp.minimum(i, lt - 1), 0)
    row = pl.BlockSpec((tm, d), lambda i: (i, 0))
    in_specs = [pl.BlockSpec((tm, w1), lambda i: (0, 0)), pl.BlockSpec((tm, w1), lat),
                pl.BlockSpec((tm, w2), lambda i: (0, 0)), pl.BlockSpec((tm, w2), lat),
                pl.BlockSpec((tm, w1 + w2), lambda i: (i, OFF_GATE // (w1 + w2))),
                row, pl.BlockSpec(mod.shape, lambda i: (0, 0)), pl.BlockSpec(w.shape, lambda i: (0, 0))]
    args = [o1c, o1l, o2c, o2l, proj, h, mod, w]
    if nxt is None:
        r = lt * tm
        out_specs, out_shape = row, jax.ShapeDtypeStruct((r, d), F32)
    else:
        in_specs += [pl.BlockSpec((1, d), lambda i: (0, 0)), pl.BlockSpec(nxt[1].shape, lambda i: (0, 0))]
        args += [nxt[0].reshape(1, d), nxt[1]]
        out_specs = [row, row]
        out_shape = [jax.ShapeDtypeStruct((r, d), F32), jax.ShapeDtypeStruct((r, d), BF16)]
    return pl.pallas_call(
        functools.partial(_outproj_kernel, lat_tiles=lt, with_next=nxt is not None), grid=(r // tm,),
        in_specs=in_specs, out_specs=out_specs, out_shape=out_shape,
        compiler_params=_params("parallel"), name="out_proj",
    )(*args)


def _conv3_kernel(x_ref, w_ref, b_ref, oc_ref, ol_ref, *, n_ctx):
    x = x_ref[...].astype(F32)
    r = x.shape[0]
    n_lat = r - n_ctx
    row = lax.broadcasted_iota(jnp.int32, (r, 1), 0)
    prev = jnp.where((row == 0) | (row == n_lat), 0.0, pltpu.roll(x, 1, 0))
    nxt = jnp.where((row == n_lat - 1) | (row == r - 1), 0.0, pltpu.roll(x, r - 1, 0))
    y = prev * w_ref[0:1, :] + x * w_ref[1:2, :] + nxt * w_ref[2:3, :] + b_ref[...]
    ol_ref[0] = y[:n_lat]
    oc_ref[0] = y[n_lat:]


def conv3(proj, col_off, w, b, n_ctx):
    r = proj.shape[0]
    cw = w.shape[1] // 3
    nb = cw // LANE
    cb = col_off // LANE
    return pl.pallas_call(
        functools.partial(_conv3_kernel, n_ctx=n_ctx), grid=(3, nb),
        in_specs=[pl.BlockSpec((r, LANE), lambda p, c: (0, cb + p * nb + c)),
                  pl.BlockSpec((3, LANE), lambda p, c: (0, p * nb + c)),
                  pl.BlockSpec((1, LANE), lambda p, c: (0, p * nb + c))],
        out_specs=[pl.BlockSpec((1, n_ctx, LANE), lambda p, c: (p, 0, c)),
                   pl.BlockSpec((1, r - n_ctx, LANE), lambda p, c: (p, 0, c))],
        out_shape=[jax.ShapeDtypeStruct((3, n_ctx, cw), F32), jax.ShapeDtypeStruct((3, r - n_ctx, cw), F32)],
        compiler_params=_params("parallel", "parallel"), name="conv3",
    )(proj, w, b.reshape(1, -1))


def _filt_kernel(z_ref, w1_ref, b1_ref, f1_ref, w2_ref, b2_ref, f2_ref, w3_ref, dl_ref, h_ref, tap_ref, nrm_ref, *,
                 tm, cw):
    i = pl.program_id(0)
    z = z_ref[...]

    def mm(a, w_r):
        return _dot3(*_split_bf16(a), *_split_bf16(w_r[...]))

    a = jnp.sin(f1_ref[...] * (mm(z, w1_ref) + b1_ref[...]))
    a = jnp.sin(f2_ref[...] * (mm(a, w2_ref) + b2_ref[...]))
    dec = jnp.exp(-z[:, 0:1] * dl_ref[...])
    h = _dot(a.astype(BF16), w3_ref[...]) * jnp.concatenate([dec] * (h_ref.shape[1] // cw), axis=1)
    h_ref[...] = h.astype(h_ref.dtype)
    row = i * tm + lax.broadcasted_iota(jnp.int32, (tm, 1), 0)
    col = lax.broadcasted_iota(jnp.int32, (1, h.shape[1]), 1)
    skip = (row == 0) & ((col // cw) % 2 == 1)
    part = jnp.sum(jnp.where(skip, 0.0, jnp.abs(h)), axis=0, keepdims=True)

    @pl.when(i == 0)
    def _():
        nrm_ref[...] = jnp.zeros(nrm_ref.shape, F32)
        tap_ref[...] = h[0:SUBLANES]

    nrm_ref[...] += jnp.broadcast_to(part, nrm_ref.shape)


def hyena_filter(z, w1p, b1, f1, w2, b2, f2, w3, deltas):
    l = z.shape[0]
    n = w3.shape[1]
    cw = deltas.shape[0]
    hid = w2.shape[0]
    tm = min(l, 256)
    full = lambda a: pl.BlockSpec(a.shape, lambda i: (0, 0))
    ins = [w1p, b1.reshape(1, hid), f1.reshape(1, hid), w2, b2.reshape(1, hid), f2.reshape(1, hid), w3,
           deltas.reshape(1, cw)]
    small = pl.BlockSpec((SUBLANES, n), lambda i: (0, 0))
    return pl.pallas_call(
        functools.partial(_filt_kernel, tm=tm, cw=cw), grid=(l // tm,),
        in_specs=[pl.BlockSpec((tm, LANE), lambda i: (i, 0))] + [full(a) for a in ins],
        out_specs=[pl.BlockSpec((tm, n), lambda i: (i, 0)), small, small],
        out_shape=[jax.ShapeDtypeStruct((l, n), BF16), jax.ShapeDtypeStruct((SUBLANES, n), F32),
                   jax.ShapeDtypeStruct((SUBLANES, n), F32)],
        compiler_params=_params("arbitrary"), name="hyena_filter",
    )(z, *ins)


@functools.lru_cache(maxsize=None)
def _dft_consts(a):
    n = a * a
    hh = a // 2
    n1 = np.arange(hh)
    k1 = np.arange(hh)
    n2 = np.arange(a)
    k2 = np.arange(a)
    ang1 = 2 * np.pi * np.outer(k1 + 0.5, n1) / a
    m1 = np.zeros((hh, 2, hh))
    m1[:, 0], m1[:, 1] = np.cos(ang1), -np.sin(ang1)
    m1 = m1.reshape(2 * hh, hh)
    phi = 2 * np.pi * (n2[None, None, :] * (k1[:, None, None] + 0.5) / n + n2[None, None, :] * k2[None, :, None] / a)
    c, s = np.cos(phi), np.sin(phi)
    g = np.zeros((hh, 2, a, 2, a))
    g[:, 0, :, 0, :], g[:, 0, :, 1, :], g[:, 1, :, 0, :], g[:, 1, :, 1, :] = c, s, -s, c
    g = g.reshape(hh, 2 * a, 2 * a)
    gt = np.ascontiguousarray(np.transpose(g, (0, 2, 1)))
    al = 2 * np.pi * np.outer(n1, k1 + 0.5) / a
    mb = np.zeros((hh, hh, 2))
    mb[:, :, 0], mb[:, :, 1] = 2 / n * np.cos(al), -2 / n * np.sin(al)
    mb = mb.reshape(hh, 2 * hh)
    eye = np.eye(ROWS_BF16)
    return tuple(x.astype(np.float32) for x in (np.kron(m1, eye), g, gt, np.kron(mb, eye)))


@functools.lru_cache(maxsize=None)
def _dense_dft_consts(l):
    n = 2 * l
    ang = 2 * np.pi * np.outer(np.arange(l) + 0.5, np.arange(l)) / n
    mf = np.concatenate([np.cos(ang), -np.sin(ang)], axis=0)
    mi = np.concatenate([2 / n * np.cos(ang.T), -2 / n * np.sin(ang.T)], axis=1)
    return tuple(x.astype(np.float32) for x in (mf, mi))


def _stage1_kernel(x_ref, a_ref, o_ref):
    _, hh, t, ct = x_ref.shape
    x = x_ref[0].reshape(hh * t, ct).astype(BF16)
    o_ref[...] = _dot(a_ref[...], x).astype(o_ref.dtype).reshape(o_ref.shape)


def dft_stage1(x4, p, a1):
    _, hh, a, c = x4.shape
    ct = _pick_tile(c, 512)
    t = ROWS_BF16
    return pl.pallas_call(
        _stage1_kernel, grid=(a // t, c // ct),
        in_specs=[pl.BlockSpec((1, hh, t, ct), lambda j, q: (p, 0, j, q)),
                  pl.BlockSpec(a1.shape, lambda j, q: (0, 0))],
        out_specs=pl.BlockSpec((2 * hh, t, ct), lambda j, q: (0, j, q)),
        out_shape=jax.ShapeDtypeStruct((2 * hh, a, c), BF16),
        compiler_params=_params("parallel", "parallel"), name="dft_stage1",
    )(x4, a1)


def _spectrum_product(x, h0, h1, tap0, n0, n1):
    half = x.shape[0] // 2
    inv = 1.0 / (n0 + n1)
    kr = (h0[:half] + h1[:half] - tap0) * inv
    ki = (h0[half:] - h1[half:]) * inv
    xr, xi = x[:half], x[half:]
    return jnp.concatenate([xr * kr - xi * ki, xr * ki + xi * kr], axis=0)


K1_PER_STEP = 2


def _mid_filter_kernel(y0_ref, y1_ref, g_ref, tap_ref, n0_ref, n1_ref, o_ref):
    inv = 1.0 / (n0_ref[0:1, :] + n1_ref[0:1, :])
    tap = tap_ref[0:1, :]
    for b in range(y0_ref.shape[0]):
        x0 = _dot(g_ref[b], y0_ref[b])
        x1 = _dot(g_ref[b], y1_ref[b])
        half = x0.shape[0] // 2
        kr = (x0[:half] + x1[:half] - tap) * inv
        ki = (x0[half:] - x1[half:]) * inv
        o_ref[b] = jnp.concatenate([kr, ki], axis=0).astype(o_ref.dtype)


def filter_spectrum(h, taps, nrm, consts, a):
    a1, g, _, _ = consts
    l, n = h.shape
    c = n // 4
    hh = a // 2
    y = dft_stage1(h.reshape(1, hh, a, n), 0, a1).reshape(hh, 2 * a, n)
    kb = K1_PER_STEP
    slab = lambda side: pl.BlockSpec((kb, 2 * a, c), lambda k, o: (k, 0, 2 * o + side))
    row = lambda side: pl.BlockSpec((SUBLANES, c), lambda k, o: (0, 2 * o + side))
    return pl.pallas_call(
        _mid_filter_kernel, grid=(hh // kb, 2),
        in_specs=[slab(0), slab(1), pl.BlockSpec((kb, 2 * a, 2 * a), lambda k, o: (k, 0, 0)), row(1), row(0), row(1)],
        out_specs=pl.BlockSpec((kb, 2 * a, c), lambda k, o: (k, 0, o)),
        out_shape=jax.ShapeDtypeStruct((hh, 2 * a, 2 * c), BF16),
        compiler_params=_params("parallel", "parallel"), name="dft_mid_filter",
    )(y, y, g, taps, nrm, nrm)


def _mid_conv_kernel(y_ref, g_ref, gt_ref, kf_ref, o_ref):
    for b in range(y_ref.shape[0]):
        x = _dot(g_ref[b], y_ref[b])
        kf = kf_ref[b].astype(F32)
        half = x.shape[0] // 2
        xr, xi, kr, ki = x[:half], x[half:], kf[:half], kf[half:]
        z = jnp.concatenate([xr * kr - xi * ki, xr * ki + xi * kr], axis=0).astype(BF16)
        o_ref[b] = _dot(gt_ref[b], z).astype(o_ref.dtype)


def dft_mid_conv(y, g, gt, kf, order):
    hh, a2, c = y.shape
    kb = K1_PER_STEP
    slab = pl.BlockSpec((kb, a2, c), lambda k: (k, 0, 0))
    mat = pl.BlockSpec((kb, a2, a2), lambda k: (k, 0, 0))
    return pl.pallas_call(
        _mid_conv_kernel, grid=(hh // kb,),
        in_specs=[slab, mat, mat, pl.BlockSpec((kb, a2, c), lambda k: (k, 0, order))],
        out_specs=slab, out_shape=jax.ShapeDtypeStruct((hh, a2, c), BF16),
        compiler_params=_params("parallel"), name="dft_mid_conv",
    )(y, g, gt, kf)


def _last_kernel(b_ref, a_ref, xa_ref, zb_ref, bias_ref, o_ref):
    rows, t, ct = b_ref.shape
    conv = _dot(a_ref[...], b_ref[...].reshape(rows * t, ct)).reshape(rows // 2, t, ct)
    o_ref[0] = (xa_ref[0] * (conv + bias_ref[...] * zb_ref[0])).astype(o_ref.dtype)


def dft_last(b, a2m, xa4, pa, zb4, pb, bias_row):
    rows, a, c = b.shape
    hh = rows // 2
    ct = _pick_tile(c, 512)
    t = ROWS_BF16
    sig = lambda pp: pl.BlockSpec((1, hh, t, ct), lambda j, q: (pp, 0, j, q))
    return pl.pallas_call(
        _last_kernel, grid=(a // t, c // ct),
        in_specs=[pl.BlockSpec((rows, t, ct), lambda j, q: (0, j, q)),
                  pl.BlockSpec(a2m.shape, lambda j, q: (0, 0)),
                  sig(pa), sig(pb), pl.BlockSpec((1, ct), lambda j, q: (0, q))],
        out_specs=sig(0), out_shape=jax.ShapeDtypeStruct((1, hh, a, c), F32),
        compiler_params=_params("parallel", "parallel"), name="dft_last",
    )(b, a2m, xa4, zb4, bias_row)


def long_conv_gated(u3, kf, bias, consts, a):
    a1, g, gt, a2m = consts
    _, l, c = u3.shape
    hh = a // 2
    u4 = u3.reshape(3, hh, a, c)

    def conv_gate(src4, p_src, xa_p, order):
        y = dft_stage1(src4, p_src, a1).reshape(hh, 2 * a, c)
        bb = dft_mid_conv(y, g, gt, kf, order).reshape(2 * hh, a, c)
        return dft_last(bb, a2m, u4, xa_p, src4, p_src, bias[order].reshape(1, c))

    z2 = conv_gate(u4, 2, 0, 0)
    return conv_gate(z2, 0, 1, 1).reshape(l, c)


def _ctx_spec_kernel(h_ref, mf_ref, o_ref):
    o_ref[...] = _dot(mf_ref[...], h_ref[...])


def ctx_filter_spectrum(h, mf):
    l, n = h.shape
    tn = _pick_tile(n, 1024)
    return pl.pallas_call(
        _ctx_spec_kernel, grid=(n // tn,),
        in_specs=[pl.BlockSpec((l, tn), lambda j: (0, j)), pl.BlockSpec(mf.shape, lambda j: (0, 0))],
        out_specs=pl.BlockSpec((2 * l, tn), lambda j: (0, j)),
        out_shape=jax.ShapeDtypeStruct((2 * l, n), F32),
        compiler_params=_params("parallel"), name="ctx_filter_spectrum",
    )(h, mf)


def _ctx_hyena_kernel(u_ref, mf_ref, mi_ref, hf0a, hf1a, hf0b, hf1b, tap0, tap1, n0a, n1a, n0b, n1b, bias_ref, o_ref):
    x1, x2, z = u_ref[0], u_ref[1], u_ref[2]

    def conv(sig, h0_ref, h1_ref, tap_ref, n0_ref, n1_ref):
        x = _dot(mf_ref[...], sig.astype(BF16))
        zz = _spectrum_product(x, h0_ref[...], h1_ref[...], tap_ref[0:1, :], n0_ref[0:1, :], n1_ref[0:1, :])
        return _dot(mi_ref[...], zz.astype(BF16))

    z2 = x1 * (conv(z, hf0a, hf1a, tap0, n0a, n1a) + bias_ref[0:1, :] * z)
    o_ref[...] = (x2 * (conv(z2, hf0b, hf1b, tap1, n0b, n1b) + bias_ref[1:2, :] * z2)).astype(o_ref.dtype)


def ctx_hyena(u3, hf, taps, nrm, bias, mf, mi):
    _, l, c = u3.shape
    tc = _pick_tile(c, 512)
    nb = c // tc
    const = lambda a: pl.BlockSpec(a.shape, lambda j: (0, 0))
    col = lambda rows, q: pl.BlockSpec((rows, tc), lambda j: (0, q * nb + j))
    mats = [mf, mi]
    return pl.pallas_call(
        _ctx_hyena_kernel, grid=(nb,),
        in_specs=[pl.BlockSpec((3, l, tc), lambda j: (0, 0, j))] + [const(m) for m in mats]
        + [col(2 * l, 0), col(2 * l, 1), col(2 * l, 2), col(2 * l, 3), col(8, 1), col(8, 3),
           col(8, 0), col(8, 1), col(8, 2), col(8, 3), pl.BlockSpec((2, tc), lambda j: (0, j))],
        out_specs=pl.BlockSpec((l, tc), lambda j: (0, j)),
        out_shape=jax.ShapeDtypeStruct((l, c), BF16),
        compiler_params=_params("parallel"), name="ctx_hyena",
    )(u3, *mats, hf, hf, hf, hf, taps, taps, nrm, nrm, nrm, nrm, bias)


def _axial_tables(n_tokens, n_rot):
    rows = n_tokens // GRID_W
    row = jnp.broadcast_to(jnp.arange(rows)[:, None], (rows, GRID_W)).reshape(-1).astype(F32)
    col = jnp.broadcast_to(jnp.arange(GRID_W)[None, :], (rows, GRID_W)).reshape(-1).astype(F32)
    n_freq = n_rot // 4
    inv = ROPE_THETA ** (-jnp.arange(n_freq, dtype=F32) / n_freq)
    ang = jnp.concatenate([row[:, None] * inv, col[:, None] * inv], axis=-1)
    return jnp.cos(ang), jnp.sin(ang)


def _rope_tables_full(s, n_ctx):
    cos, sin = _axial_tables(s, HEAD_DIM)
    t0 = jnp.concatenate([jnp.concatenate([cos, cos], axis=1), jnp.ones((n_ctx, LANE), F32)], axis=0)
    t1 = jnp.concatenate([jnp.concatenate([-sin, sin], axis=1), jnp.zeros((n_ctx, LANE), F32)], axis=0)
    return t0, t1


def _rope_tables_half(s, n_ctx):
    cos, sin = _axial_tables(s, ROPE_DIM)
    q = ROPE_DIM // 2
    zq = jnp.zeros((s, q), F32)
    z2 = jnp.zeros((s, LANE - ROPE_DIM), F32)
    t0 = jnp.concatenate([cos, cos, z2], axis=1)
    t1 = jnp.concatenate([-sin, zq, z2], axis=1)
    t2 = jnp.concatenate([zq, sin, z2], axis=1)
    c0 = jnp.concatenate([jnp.ones((n_ctx, ROPE_DIM), F32), jnp.zeros((n_ctx, LANE - ROPE_DIM), F32)], axis=1)
    cz = jnp.zeros((n_ctx, LANE), F32)
    return jnp.concatenate([t0, c0], axis=0), jnp.concatenate([t1, cz], axis=0), jnp.concatenate([t2, cz], axis=0)


def _filter_features(l):
    pos = jnp.arange(l, dtype=F32)
    t = pos / max(l - 1, 1)
    bands = jnp.linspace(1e-4, HYENA_BANDS - 1, HYENA_BANDS, dtype=F32)
    ang = (2.0 * math.pi / l) * pos[:, None] * bands[None, :]
    z = jnp.concatenate([t[:, None], jnp.cos(ang), -jnp.sin(ang)], axis=-1)
    return jnp.pad(z, ((0, 0), (0, LANE - HYENA_EMB)))


Q_STACK = 2048
KV_TILE_CAP = 1408


def _attention_pair(qp, kp, vt, s, n_ctx, *, kv_heads, groups, window, sink=None, q2=None, k2=None):
    common = dict(kv_heads=kv_heads, groups=groups, q2=q2, k2=k2, sink=sink)
    o_ctx = flash(qp, kp, vt, n_q=n_ctx, q_row0=s, kv_row0=s, n_kv_rows=n_ctx, tq=n_ctx, tk=n_ctx, **common)
    if window:
        o_lat = window_attention(qp, kp, vt, kv_heads=kv_heads, groups=groups, n_lat=s, n_ctx=n_ctx, tq=ROW_TILE,
                                 sink=sink)
    else:
        o_lat = flash(qp, kp, vt, n_q=s, q_row0=0, kv_row0=0, n_kv_rows=n_ctx + s, tq=Q_STACK // groups,
                      tk=_pick_tile(n_ctx + s, KV_TILE_CAP), **common)
    return o_ctx, o_lat


def _even_layer(u, s, n_ctx, w_in, qn_g, kn_g, sink, conv_w, conv_b, fparams, hy_bias, rope_full, feats, dft):
    wq, wk, wv, whd, wg = jnp.split(w_in, np.cumsum([ATT_WIDTH, KV_WIDTH, KV_WIDTH, 3 * B_WIDTH])[:].tolist(), axis=1)
    proj = matmul(u, jnp.concatenate([wg, wq, wk, wv, whd], axis=1).astype(BF16), F32)
    qp = head_prep(proj, OFF_Q, A_HEADS, qn_g, rope_full, "half64", HEAD_DIM, HEAD_DIM ** -0.5 * LOG2E, True)
    kp = head_prep(proj, OFF_K, A_KV_HEADS, kn_g, rope_full, "half64", HEAD_DIM, 1.0)
    vt = transpose_heads(proj, OFF_V, A_KV_HEADS)
    a_ctx, a_lat = _attention_pair(qp, kp, vt, s, n_ctx, kv_heads=A_KV_HEADS, groups=A_HEADS // A_KV_HEADS,
                                   window=True, sink=sink)
    w1, b1, f1, w2, b2, f2, w3 = fparams
    w1p = jnp.pad(w1, ((0, LANE - HYENA_EMB), (0, 0)))
    w3 = w3.astype(BF16)
    deltas = jnp.linspace(DECAY_MAX, DECAY_MIN, B_WIDTH, dtype=F32)
    uc3, ul3 = conv3(proj, OFF_REST, conv_w, conv_b, n_ctx)
    a = int(round(math.sqrt(2 * s)))
    consts, (mf, mi) = dft
    h_lat, tap_lat, nrm_lat = hyena_filter(feats[0], w1p, b1, f1, w2, b2, f2, w3, deltas)
    kf_lat = filter_spectrum(h_lat, tap_lat, nrm_lat, consts, a)
    b_lat = long_conv_gated(ul3, kf_lat, hy_bias, consts, a)
    h_ctx, tap_ctx, nrm_ctx = hyena_filter(feats[1], w1p, b1, f1, w2, b2, f2, w3, deltas)
    hf_ctx = ctx_filter_spectrum(h_ctx, mf)
    b_ctx = ctx_hyena(uc3, hf_ctx, tap_ctx, nrm_ctx, hy_bias, mf, mi)
    return proj, a_ctx, a_lat, b_ctx, b_lat


def _odd_layer(u, s, n_ctx, w_in, qn_g, kn_g, cq_g, ckv_g, wuq, wukv, mq_g, mk_g, rope_full, rope_half):
    d = w_in.shape[0]
    parts = jnp.split(w_in, np.cumsum([ATT_WIDTH, KV_WIDTH, KV_WIDTH, Q_LORA, KV_LORA, ROPE_DIM]).tolist(), axis=1)
    wq, wk, wv, wmq, wmkv, wmkr, wg = parts
    w_perm = jnp.concatenate([wg, wq, wk, wv, wmq, wmkv, wmkr, jnp.zeros((d, LANE - ROPE_DIM), w_in.dtype)], axis=1)
    proj = matmul(u, w_perm.astype(BF16), F32)
    qp = head_prep(proj, OFF_Q, C_HEADS, qn_g, rope_full, "half64", HEAD_DIM, HEAD_DIM ** -0.5 * LOG2E, True)
    kp = head_prep(proj, OFF_K, C_KV_HEADS, kn_g, rope_full, "half64", HEAD_DIM, 1.0)
    vt = transpose_heads(proj, OFF_V, C_KV_HEADS)
    c_ctx, c_lat = _attention_pair(qp, kp, vt, s, n_ctx, kv_heads=C_KV_HEADS, groups=C_HEADS // C_KV_HEADS,
                                   window=False)
    pad_r = LANE - ROPE_DIM
    wuq3 = wuq.reshape(Q_LORA, M_HEADS, QK_DIM)
    wuq_p = jnp.concatenate([wuq3[:, :, :NOPE_DIM].reshape(Q_LORA, -1),
                             jnp.pad(wuq3[:, :, NOPE_DIM:], ((0, 0), (0, 0), (0, pad_r))).reshape(Q_LORA, -1)], axis=1)
    wukv3 = wukv.reshape(KV_LORA, M_HEADS, NOPE_DIM + V_DIM)
    wukv_p = jnp.concatenate([wukv3[:, :, :NOPE_DIM].reshape(KV_LORA, -1),
                              wukv3[:, :, NOPE_DIM:].reshape(KV_LORA, -1)], axis=1)
    q_raw = norm_matmul(proj, OFF_MQ, cq_g, wuq_p.astype(BF16))
    kv_raw = norm_matmul(proj, OFF_MKV, ckv_g, wukv_p.astype(BF16))
    scale = QK_DIM ** -0.5 * LOG2E
    gq_r = jnp.pad(mq_g[NOPE_DIM:], (0, pad_r))
    gk_r = jnp.pad(mk_g[NOPE_DIM:], (0, pad_r))
    qn = head_prep(q_raw, 0, M_HEADS, mq_g[:NOPE_DIM], (), "none", NOPE_DIM, scale, True)
    qr = head_prep(q_raw, M_HEADS * NOPE_DIM, M_HEADS, gq_r, rope_half, "half32", ROPE_DIM, scale, True)
    kn = head_prep(kv_raw, 0, M_HEADS, mk_g[:NOPE_DIM], (), "none", NOPE_DIM, 1.0)
    kr = head_prep(proj, OFF_MKR, 1, gk_r, rope_half, "half32", ROPE_DIM, 1.0)
    vmt = transpose_heads(kv_raw, M_HEADS * NOPE_DIM, M_HEADS)
    d_ctx, d_lat = _attention_pair(qn, kn, vmt, s, n_ctx, kv_heads=M_HEADS, groups=1, window=False, q2=qr, k2=kr)
    return proj, c_ctx, c_lat, d_ctx, d_lat


def kernel(x, c, ctx, c_ctx, ada_w, ada_b, norm_g, w_out, ev_w_in, ev_qn_g, ev_kn_g, ev_sink, ev_conv_w, ev_conv_b, hy_w1, hy_b1, hy_f1, hy_w2, hy_b2, hy_f2, hy_w3, hy_bias, od_w_in, od_qn_g, od_kn_g, od_cq_g, od_ckv_g, od_wuq, od_wukv, od_mq_g, od_mk_g):
    _, s, d = x.shape
    n_ctx = ctx.shape[1]
    depth = ada_w.shape[0]
    h = jnp.concatenate([x[0], ctx[0]], axis=0)
    cc = jnp.zeros((8, d), F32).at[0].set(c[0]).at[1].set(c_ctx)
    mod = ada_mod(cc, ada_w, ada_b)
    rope_full = _rope_tables_full(s, n_ctx)
    rope_half = _rope_tables_half(s, n_ctx)
    feats = (_filter_features(s), _filter_features(n_ctx))
    to_bf16 = lambda mats: tuple(jnp.asarray(m).astype(BF16) for m in mats)
    dft = (to_bf16(_dft_consts(int(round(math.sqrt(2 * s))))), to_bf16(_dense_dft_consts(n_ctx)))
    u = norm_mod(h, norm_g[0], mod[0], n_ctx)
    for i in range(depth):
        if i % 2 == 0:
            e = i // 2
            proj, o1c, o1l, o2c, o2l = _even_layer(
                u, s, n_ctx, ev_w_in[e], ev_qn_g[e], ev_kn_g[e], ev_sink[e], ev_conv_w[e], ev_conv_b[e],
                (hy_w1[e], hy_b1[e], hy_f1[e], hy_w2[e], hy_b2[e], hy_f2[e], hy_w3[e]), hy_bias[e], rope_full, feats,
                dft)
        else:
            o = i // 2
            proj, o1c, o1l, o2c, o2l = _odd_layer(
                u, s, n_ctx, od_w_in[o], od_qn_g[o], od_kn_g[o], od_cq_g[o], od_ckv_g[o], od_wuq[o], od_wukv[o],
                od_mq_g[o], od_mk_g[o], rope_full, rope_half)
        w = w_out[i].astype(BF16)
        if i < depth - 1:
            h, u = out_proj(o1c, o1l, o2c, o2l, proj, h, mod[i], w, (norm_g[i + 1], mod[i + 1]))
        else:
            h = out_proj(o1c, o1l, o2c, o2l, proj, h, mod[i], w, None)
    return h[None]
```

```python
import functools
import math

import numpy as np
import jax
import jax.numpy as jnp
from jax import lax
from jax.experimental import pallas as pl
from jax.experimental.pallas import tpu as pltpu

F32 = jnp.float32
BF16 = jnp.bfloat16

GRID_W = 64
HEAD_DIM = 128
ROPE_THETA = 10000.0
EPS = 1e-6
NEG_INF = -1e30
A_HEADS, A_KV_HEADS, WINDOW = 8, 2, 128
B_WIDTH = 1024
HYENA_BANDS = 16
HYENA_EMB = 1 + 2 * HYENA_BANDS
DECAY_TARGET = 1e-2
DECAY_MAX = abs(math.log(DECAY_TARGET)) / 0.3
DECAY_MIN = abs(math.log(DECAY_TARGET)) / 1.5
C_HEADS, C_KV_HEADS = 8, 2
M_HEADS, Q_LORA, KV_LORA, NOPE_DIM, ROPE_DIM, V_DIM = 8, 512, 256, 128, 64, 128
QK_DIM = NOPE_DIM + ROPE_DIM
ATT_WIDTH = A_HEADS * HEAD_DIM
KV_WIDTH = A_KV_HEADS * HEAD_DIM
BRANCH = ATT_WIDTH + B_WIDTH

LOG2E = math.log2(math.e)
LANE = 128
SUBLANES = 8
ROWS_BF16 = 16
ROW_TILE = 256
VMEM_LIMIT = 48 * 1024 * 1024

OFF_GATE, OFF_Q, OFF_K, OFF_V, OFF_REST = 0, BRANCH, BRANCH + ATT_WIDTH, BRANCH + ATT_WIDTH + KV_WIDTH, BRANCH + ATT_WIDTH + 2 * KV_WIDTH
OFF_MQ, OFF_MKV, OFF_MKR = OFF_REST, OFF_REST + Q_LORA, OFF_REST + Q_LORA + KV_LORA


def _params(*sem):
    return pltpu.CompilerParams(dimension_semantics=sem, vmem_limit_bytes=VMEM_LIMIT)


def _split_bf16(x):
    hi = x.astype(BF16)
    return hi, (x - hi.astype(F32)).astype(BF16)


def _dot(a, b):
    return jnp.dot(a, b, preferred_element_type=F32)


def _dot3(a_hi, a_lo, b_hi, b_lo):
    return _dot(a_hi, b_hi) + (_dot(a_hi, b_lo) + _dot(a_lo, b_hi))


def _ada_kernel(c_ref, w_ref, b_ref, o_ref):
    c = c_ref[...]
    a = (c * jax.nn.sigmoid(c)).astype(BF16)
    o_ref[0] = _dot(a, w_ref[0].astype(BF16)) + b_ref[0]


def ada_mod(cc, ada_w, ada_b):
    depth, d, n = ada_w.shape
    tn = _pick_tile(n, 1024)
    return pl.pallas_call(
        _ada_kernel, grid=(depth, n // tn),
        in_specs=[pl.BlockSpec((8, d), lambda l, j: (0, 0)),
                  pl.BlockSpec((1, d, tn), lambda l, j: (l, 0, j)),
                  pl.BlockSpec((1, 1, tn), lambda l, j: (l, 0, j))],
        out_specs=pl.BlockSpec((1, 8, tn), lambda l, j: (l, 0, j)),
        out_shape=jax.ShapeDtypeStruct((depth, 8, n), F32),
        compiler_params=_params("parallel", "parallel"), name="ada_mod",
    )(cc, ada_w, ada_b.reshape(depth, 1, n))


def _modulated_norm(x, g, mod_ref, is_ctx):
    d = x.shape[1]
    y = x * lax.rsqrt(jnp.mean(x * x, axis=-1, keepdims=True) + EPS) * g
    sh = jnp.where(is_ctx, mod_ref[1:2, 0:d], mod_ref[0:1, 0:d])
    sc = jnp.where(is_ctx, mod_ref[1:2, d:2 * d], mod_ref[0:1, d:2 * d])
    return y * (1.0 + sc) + sh


def _norm_kernel(x_ref, g_ref, mod_ref, o_ref, *, lat_tiles):
    is_ctx = pl.program_id(0) >= lat_tiles
    o_ref[...] = _modulated_norm(x_ref[...], g_ref[...], mod_ref, is_ctx).astype(o_ref.dtype)


def norm_mod(h, g, mod, n_ctx):
    r, d = h.shape
    tm = ROW_TILE
    return pl.pallas_call(
        functools.partial(_norm_kernel, lat_tiles=(r - n_ctx) // tm), grid=(r // tm,),
        in_specs=[pl.BlockSpec((tm, d), lambda i: (i, 0)),
                  pl.BlockSpec((1, d), lambda i: (0, 0)),
                  pl.BlockSpec(mod.shape, lambda i: (0, 0))],
        out_specs=pl.BlockSpec((tm, d), lambda i: (i, 0)),
        out_shape=jax.ShapeDtypeStruct((r, d), BF16),
        compiler_params=_params("parallel"), name="norm_mod",
    )(h, g.reshape(1, d), mod)


def _mm_kernel(a_ref, b_ref, o_ref):
    o_ref[...] = _dot(a_ref[...], b_ref[...]).astype(o_ref.dtype)


def _pick_tile(n, cap, unit=LANE):
    best = unit
    for t in range(unit, cap + 1, unit):
        if n % t == 0:
            best = t
    return best


def matmul(a, b, out_dtype):
    m, k = a.shape
    n = b.shape[1]
    tm = _pick_tile(m, 1536, 8)
    tn = _pick_tile(n, 1024)
    return pl.pallas_call(
        _mm_kernel, grid=(m // tm, n // tn),
        in_specs=[pl.BlockSpec((tm, k), lambda i, j: (i, 0)),
                  pl.BlockSpec((k, tn), lambda i, j: (0, j))],
        out_specs=pl.BlockSpec((tm, tn), lambda i, j: (i, j)),
        out_shape=jax.ShapeDtypeStruct((m, n), out_dtype),
        compiler_params=_params("parallel", "parallel"), name="in_proj",
    )(a, b)


def _mm_cast_kernel(a_ref, b_ref, o_ref):
    o_ref[...] = _dot(a_ref[...], b_ref[0].astype(a_ref.dtype)).astype(o_ref.dtype)


def matmul_permuted(a, w3, layer, segments, out_dtype):
    m, k = a.shape
    n = sum(width for _, width in segments)
    tm = _pick_tile(m, 1536, 8)
    tn = LANE
    for t in range(LANE, 1024 + 1, LANE):
        if all(start % t == 0 and width % t == 0 for start, width in segments):
            tn = t
    src_tiles = [(start + off) // tn for start, width in segments for off in range(0, width, tn)]

    def src(j):
        idx = src_tiles[0]
        for t, s_t in enumerate(src_tiles[1:], 1):
            idx = jnp.where(j == t, s_t, idx)
        return idx
    return pl.pallas_call(
        _mm_cast_kernel, grid=(m // tm, n // tn),
        in_specs=[pl.BlockSpec((tm, k), lambda i, j: (i, 0)),
                  pl.BlockSpec((1, k, tn), lambda i, j: (layer, 0, src(j)))],
        out_specs=pl.BlockSpec((tm, tn), lambda i, j: (i, j)),
        out_shape=jax.ShapeDtypeStruct((m, n), out_dtype),
        compiler_params=_params("parallel", "parallel"), name="in_proj",
    )(a, w3)


def _normmm_kernel(x_ref, g_ref, w_ref, o_ref):
    x = x_ref[...].astype(F32)
    y = x * lax.rsqrt(jnp.mean(x * x, axis=-1, keepdims=True) + EPS) * g_ref[...]
    o_ref[...] = _dot(y.astype(BF16), w_ref[...]).astype(o_ref.dtype)


def norm_matmul(x, col_off, g, w):
    r = x.shape[0]
    k, n = w.shape
    tm = _pick_tile(r, 768, 8)
    cb = col_off // k
    return pl.pallas_call(
        _normmm_kernel, grid=(r // tm,),
        in_specs=[pl.BlockSpec((tm, k), lambda i: (i, cb)),
                  pl.BlockSpec((1, k), lambda i: (0, 0)),
                  pl.BlockSpec((k, n), lambda i: (0, 0))],
        out_specs=pl.BlockSpec((tm, n), lambda i: (i, 0)),
        out_shape=jax.ShapeDtypeStruct((r, n), F32),
        compiler_params=_params("parallel"), name="norm_matmul",
    )(x, g.reshape(1, k), w)


def _prep_kernel(*refs, mode, n_real, scale, n_heads, head_major):
    x_ref, g_ref = refs[0], refs[1]
    o_ref = refs[-1]
    g = g_ref[...]
    tabs = [t[...] for t in refs[2:-1]]
    for h in range(n_heads):
        x = x_ref[:, h * LANE:(h + 1) * LANE].astype(F32)
        ms = jnp.sum(x * x, axis=-1, keepdims=True) * (1.0 / n_real)
        xn = x * lax.rsqrt(ms + EPS) * g
        if mode == "none":
            y = xn
        elif mode == "half64":
            y = xn * tabs[0] + pltpu.roll(xn, 64, 1) * tabs[1]
        else:
            y = xn * tabs[0] + pltpu.roll(xn, 96, 1) * tabs[1] + pltpu.roll(xn, 32, 1) * tabs[2]
        y = (y * scale).astype(o_ref.dtype)
        if head_major:
            o_ref[h] = y
        else:
            o_ref[:, h * LANE:(h + 1) * LANE] = y


def head_prep(x, col_off, n_heads, g, tables, mode, n_real, scale, head_major=False):
    r = x.shape[0]
    w = n_heads * LANE
    tm = _pick_tile(r, 768, 8)
    tab_spec = pl.BlockSpec((tm, LANE), lambda i: (i, 0))
    if head_major:
        out_spec = pl.BlockSpec((n_heads, tm, LANE), lambda i: (0, i, 0))
        out_shape = jax.ShapeDtypeStruct((n_heads, r, LANE), BF16)
    else:
        out_spec = pl.BlockSpec((tm, w), lambda i: (i, 0))
        out_shape = jax.ShapeDtypeStruct((r, w), BF16)
    return pl.pallas_call(
        functools.partial(_prep_kernel, mode=mode, n_real=n_real, scale=scale, n_heads=n_heads,
                          head_major=head_major),
        grid=(r // tm,),
        in_specs=[pl.BlockSpec((tm, w), lambda i: (i, col_off // w)),
                  pl.BlockSpec((1, LANE), lambda i: (0, 0))] + [tab_spec] * len(tables),
        out_specs=out_spec, out_shape=out_shape,
        compiler_params=_params("parallel"), name="head_prep_" + mode,
    )(x, g.reshape(1, LANE), *tables)


def _vt_kernel(x_ref, o_ref):
    o_ref[...] = x_ref[...].astype(F32).T.astype(o_ref.dtype)


def transpose_heads(x, col_off, n_heads):
    r = x.shape[0]
    w = n_heads * LANE
    tm = _pick_tile(r, 768, LANE)
    return pl.pallas_call(
        _vt_kernel, grid=(r // tm,),
        in_specs=[pl.BlockSpec((tm, w), lambda i: (i, col_off // w))],
        out_specs=pl.BlockSpec((w, tm), lambda i: (0, i)),
        out_shape=jax.ShapeDtypeStruct((w, r), BF16),
        compiler_params=_params("parallel"), name="transpose_heads",
    )(x)


def _flash_kernel(*refs, groups, tq, tk, dv, n_kv, has_qk2, has_sink, key_chunk):
    q_ref, k_ref, kn_ref, vt_ref = refs[0], refs[1], refs[2], refs[3]
    pos = 4
    if has_qk2:
        q2_ref, k2_ref, k2n_ref = refs[4], refs[5], refs[6]
        pos = 7
    if has_sink:
        sink_ref = refs[pos]
        pos += 1
    o_ref, m_ref, l_ref, acc_ref, s_ref, mt_ref = refs[pos:pos + 6]
    j = pl.program_id(2)
    n = groups * tq
    bounds = [(r0, min(r0 + key_chunk, tk)) for r0 in range(0, tk, key_chunk)]

    q = q_ref[...].reshape(n, q_ref.shape[2])
    if has_qk2:
        q = jnp.concatenate([q, q2_ref[...].reshape(n, LANE)], axis=1)

    def scores(kr, k2r, r0, r1):
        kk = kr[r0:r1, :]
        if has_qk2:
            kk = jnp.concatenate([kk, k2r[r0:r1, :]], axis=1)
        return lax.dot_general(kk, q, (((1,), (1,)), ((), ())), preferred_element_type=F32)

    @pl.when(j == 0)
    def _():
        m_ref[...] = jnp.full(m_ref.shape, NEG_INF, F32)
        l_ref[...] = jnp.zeros(l_ref.shape, F32)
        acc_ref[...] = jnp.zeros(acc_ref.shape, F32)
        mt = jnp.full(mt_ref.shape, NEG_INF, F32)
        for r0, r1 in bounds:
            s = scores(k_ref, k2_ref if has_qk2 else None, r0, r1)
            s_ref[r0:r1, :] = s
            mt = jnp.maximum(mt, jnp.max(s, axis=0, keepdims=True))
        mt_ref[...] = mt

    def step(prefetch):
        m_prev = m_ref[...]
        m_new = jnp.maximum(m_prev, mt_ref[...])
        alpha = jnp.exp2(m_prev - m_new)
        l_new = alpha * l_ref[...]
        acc_new = alpha * acc_ref[...]
        mt = jnp.full(mt_ref.shape, NEG_INF, F32)
        for r0, r1 in bounds:
            if prefetch:
                s_next = scores(kn_ref, k2n_ref if has_qk2 else None, r0, r1)
            p = jnp.exp2(s_ref[r0:r1, :] - m_new)
            l_new = l_new + jnp.sum(p, axis=0, keepdims=True)
            acc_new = acc_new + _dot(vt_ref[:, r0:r1], p.astype(BF16))
            if prefetch:
                s_ref[r0:r1, :] = s_next
                mt = jnp.maximum(mt, jnp.max(s_next, axis=0, keepdims=True))
        l_ref[...] = l_new
        acc_ref[...] = acc_new
        m_ref[...] = m_new
        if prefetch:
            mt_ref[...] = mt

    if n_kv > 1:
        pl.when(j < n_kv - 1)(lambda: step(True))

    @pl.when(j == n_kv - 1)
    def _():
        step(False)
        m, l, acc = m_ref[...], l_ref[...], acc_ref[...]
        if has_sink:
            sk = sink_ref[0]
            m_fin = jnp.maximum(m, sk)
            a = jnp.exp2(m - m_fin)
            l = l * a + jnp.exp2(sk - m_fin)
            acc = acc * a
        o_t = acc / l
        for g in range(groups):
            o_ref[:, g * dv:(g + 1) * dv] = o_t[:, g * tq:(g + 1) * tq].T.astype(o_ref.dtype)


KEY_CHUNK = 256


def flash(q, k, vt, *, kv_heads, groups, n_q, q_row0, kv_row0, n_kv_rows, tq, tk, dv=HEAD_DIM,
          q2=None, k2=None, sink=None):
    dk = q.shape[2]
    qb0 = q_row0 // tq
    kb0 = kv_row0 // tk
    n_kv = n_kv_rows // tk

    def kv_row(i, j):
        return kb0 + j
    n = groups * tq
    nxt = lambda j: jnp.minimum(j + 1, n_kv - 1)
    in_specs = [pl.BlockSpec((groups, tq, dk), lambda h, i, j: (h, qb0 + i, 0)),
                pl.BlockSpec((tk, dk), lambda h, i, j: (kv_row(i, 0), h)),
                pl.BlockSpec((tk, dk), lambda h, i, j: (kv_row(i, nxt(j)), h)),
                pl.BlockSpec((dv, tk), lambda h, i, j: (h, kv_row(i, j)))]
    args = [q, k, k, vt]
    if q2 is not None:
        in_specs += [pl.BlockSpec((groups, tq, LANE), lambda h, i, j: (h, qb0 + i, 0)),
                     pl.BlockSpec((tk, LANE), lambda h, i, j: (kv_row(i, 0), 0)),
                     pl.BlockSpec((tk, LANE), lambda h, i, j: (kv_row(i, nxt(j)), 0))]
        args += [q2, k2, k2]
    if sink is not None:
        in_specs.append(pl.BlockSpec((1, 1, n), lambda h, i, j: (h, 0, 0)))
        sink2 = sink.astype(F32).reshape(kv_heads, groups, 1) * LOG2E
        args.append(jnp.broadcast_to(sink2, (kv_heads, groups, tq)).reshape(kv_heads, 1, n))
    kern = functools.partial(
        _flash_kernel, groups=groups, tq=tq, tk=tk, dv=dv, n_kv=n_kv, has_qk2=q2 is not None,
        has_sink=sink is not None, key_chunk=KEY_CHUNK)
    return pl.pallas_call(
        kern, grid=(kv_heads, n_q // tq, n_kv), in_specs=in_specs,
        out_specs=pl.BlockSpec((tq, groups * dv), lambda h, i, j: (i, h)),
        out_shape=jax.ShapeDtypeStruct((n_q, kv_heads * groups * dv), BF16),
        scratch_shapes=[pltpu.VMEM((1, n), F32), pltpu.VMEM((1, n), F32), pltpu.VMEM((dv, n), F32),
                        pltpu.VMEM((tk, n), F32), pltpu.VMEM((1, n), F32)],
        compiler_params=_params("parallel", "parallel", "arbitrary"),
        name="flash_dense",
    )(*args)


def _window_kernel(*refs, groups, tq, n_lat, has_sink):
    q_ref = refs[0]
    n_half = tq // WINDOW + 2
    k_refs = refs[1:2 + n_half]
    vt_refs = refs[2 + n_half:3 + 2 * n_half]
    pos = 3 + 2 * n_half
    if has_sink:
        sink_ref = refs[pos]
        pos += 1
    o_ref = refs[pos]
    i = pl.program_id(1)
    n = groups * tq
    dv = vt_refs[0].shape[0]
    q = q_ref[...].reshape(n, q_ref.shape[2])
    qpos = i * tq + lax.broadcasted_iota(jnp.int32, (1, n), 1) % tq
    scores = []
    m = sink_ref[0] if has_sink else jnp.full((1, n), NEG_INF, F32)
    for b, k_ref in enumerate(k_refs):
        s = lax.dot_general(k_ref[...], q, (((1,), (1,)), ((), ())), preferred_element_type=F32)
        if b > 0:
            kpos = i * tq + (b - 2) * WINDOW + lax.broadcasted_iota(jnp.int32, (WINDOW, 1), 0)
            valid = (jnp.abs(qpos - kpos) <= WINDOW) & (kpos >= 0) & (kpos < n_lat)
            s = jnp.where(valid, s, NEG_INF)
        scores.append(s)
        m = jnp.maximum(m, jnp.max(s, axis=0, keepdims=True))
    l = jnp.exp2(sink_ref[0] - m) if has_sink else jnp.zeros((1, n), F32)
    acc = jnp.zeros((dv, n), F32)
    for s, vt_ref in zip(scores, vt_refs):
        p = jnp.exp2(s - m)
        l = l + jnp.sum(p, axis=0, keepdims=True)
        acc = acc + _dot(vt_ref[...], p.astype(BF16))
    o_t = acc / l
    for g in range(groups):
        o_ref[:, g * dv:(g + 1) * dv] = o_t[:, g * tq:(g + 1) * tq].T.astype(o_ref.dtype)


def window_attention(q, k, vt, *, kv_heads, groups, n_lat, n_ctx, tq, sink=None, dv=HEAD_DIM):
    dk = q.shape[2]
    per = tq // WINDOW
    n_half_blocks = n_lat // WINDOW
    n = groups * tq
    ctx_blk = n_lat // n_ctx
    half = lambda b: (lambda i: jnp.clip(i * per + b - 1, 0, n_half_blocks - 1))
    halves = [half(b) for b in range(per + 2)]
    in_specs = [pl.BlockSpec((groups, tq, dk), lambda h, i: (h, i, 0)),
                pl.BlockSpec((n_ctx, dk), lambda h, i: (ctx_blk, h))]
    in_specs += [pl.BlockSpec((WINDOW, dk), lambda h, i, f=f: (f(i), h)) for f in halves]
    in_specs += [pl.BlockSpec((dv, n_ctx), lambda h, i: (h, ctx_blk))]
    in_specs += [pl.BlockSpec((dv, WINDOW), lambda h, i, f=f: (h, f(i))) for f in halves]
    args = [q] + [k] * (per + 3) + [vt] * (per + 3)
    if sink is not None:
        in_specs.append(pl.BlockSpec((1, 1, n), lambda h, i: (h, 0, 0)))
        sink2 = sink.astype(F32).reshape(kv_heads, groups, 1) * LOG2E
        args.append(jnp.broadcast_to(sink2, (kv_heads, groups, tq)).reshape(kv_heads, 1, n))
    return pl.pallas_call(
        functools.partial(_window_kernel, groups=groups, tq=tq, n_lat=n_lat, has_sink=sink is not None),
        grid=(kv_heads, n_lat // tq), in_specs=in_specs,
        out_specs=pl.BlockSpec((tq, groups * dv), lambda h, i: (i, h)),
        out_shape=jax.ShapeDtypeStruct((n_lat, kv_heads * groups * dv), BF16),
        compiler_params=_params("parallel", "parallel"), name="window_attention",
    )(*args)


def _outproj_kernel(*refs, lat_tiles, with_next):
    o1c_ref, o1l_ref, o2c_ref, o2l_ref, gate_ref, h_ref, mod_ref, w_ref = refs[:8]
    is_ctx = pl.program_id(0) >= lat_tiles
    w1 = o1l_ref.shape[1]
    d = h_ref.shape[1]
    gt = gate_ref[...].astype(F32)
    sg = gt * jax.nn.sigmoid(gt)
    o1 = jnp.where(is_ctx, o1c_ref[...].astype(F32), o1l_ref[...].astype(F32))
    o2 = jnp.where(is_ctx, o2c_ref[...].astype(F32), o2l_ref[...].astype(F32))
    t1 = (o1 * sg[:, :w1]).astype(BF16)
    t2 = (o2 * sg[:, w1:]).astype(BF16)
    y = _dot(t1, w_ref[0:w1, :]) + _dot(t2, w_ref[w1:, :])
    gm = jnp.where(is_ctx, mod_ref[1:2, 2 * d:3 * d], mod_ref[0:1, 2 * d:3 * d])
    h_new = h_ref[...] + gm * y
    if with_next:
        g_next_ref, mod_next_ref, out_ref, u_ref = refs[8:]
        u_ref[...] = _modulated_norm(h_new, g_next_ref[...], mod_next_ref, is_ctx).astype(u_ref.dtype)
    else:
        out_ref = refs[8]
    out_ref[...] = h_new


def out_proj(o1c, o1l, o2c, o2l, proj, h, mod, w, nxt):
    r, d = h.shape
    tm = ROW_TILE
    assert o1c.shape[0] == tm, "the context rows must be exactly one row tile"
    lt = o1l.shape[0] // tm
    w1, w2 = o1l.shape[1], o2l.shape[1]
    lat = lambda i: (jnp.minimum(i, lt - 1), 0)
    row = pl.BlockSpec((tm, d), lambda i: (i, 0))
    in_specs = [pl.BlockSpec((tm, w1), lambda i: (0, 0)), pl.BlockSpec((tm, w1), lat),
                pl.BlockSpec((tm, w2), lambda i: (0, 0)), pl.BlockSpec((tm, w2), lat),
                pl.BlockSpec((tm, w1 + w2), lambda i: (i, OFF_GATE // (w1 + w2))),
                row, pl.BlockSpec(mod.shape, lambda i: (0, 0)), pl.BlockSpec(w.shape, lambda i: (0, 0))]
    args = [o1c, o1l, o2c, o2l, proj, h, mod, w]
    if nxt is None:
        r = lt * tm
        out_specs, out_shape = row, jax.ShapeDtypeStruct((r, d), F32)
    else:
        in_specs += [pl.BlockSpec((1, d), lambda i: (0, 0)), pl.BlockSpec(nxt[1].shape, lambda i: (0, 0))]
        args += [nxt[0].reshape(1, d), nxt[1]]
        out_specs = [row, row]
        out_shape = [jax.ShapeDtypeStruct((r, d), F32), jax.ShapeDtypeStruct((r, d), BF16)]
    return pl.pallas_call(
        functools.partial(_outproj_kernel, lat_tiles=lt, with_next=nxt is not None), grid=(r // tm,),
        in_specs=in_specs, out_specs=out_specs, out_shape=out_shape,
        compiler_params=_params("parallel"), name="out_proj",
    )(*args)


def _conv3_kernel(x_ref, w_ref, b_ref, oc_ref, ol_ref, *, n_ctx):
    x = x_ref[...].astype(F32)
    r = x.shape[0]
    n_lat = r - n_ctx
    row = lax.broadcasted_iota(jnp.int32, (r, 1), 0)
    prev = jnp.where((row == 0) | (row == n_lat), 0.0, pltpu.roll(x, 1, 0))
    nxt = jnp.where((row == n_lat - 1) | (row == r - 1), 0.0, pltpu.roll(x, r - 1, 0))
    y = prev * w_ref[0:1, :] + x * w_ref[1:2, :] + nxt * w_ref[2:3, :] + b_ref[...]
    ol_ref[0] = y[:n_lat]
    oc_ref[0] = y[n_lat:]


def conv3(proj, col_off, w, b, n_ctx):
    r = proj.shape[0]
    cw = w.shape[1] // 3
    nb = cw // LANE
    cb = col_off // LANE
    return pl.pallas_call(
        functools.partial(_conv3_kernel, n_ctx=n_ctx), grid=(3, nb),
        in_specs=[pl.BlockSpec((r, LANE), lambda p, c: (0, cb + p * nb + c)),
                  pl.BlockSpec((3, LANE), lambda p, c: (0, p * nb + c)),
                  pl.BlockSpec((1, LANE), lambda p, c: (0, p * nb + c))],
        out_specs=[pl.BlockSpec((1, n_ctx, LANE), lambda p, c: (p, 0, c)),
                   pl.BlockSpec((1, r - n_ctx, LANE), lambda p, c: (p, 0, c))],
        out_shape=[jax.ShapeDtypeStruct((3, n_ctx, cw), F32), jax.ShapeDtypeStruct((3, r - n_ctx, cw), F32)],
        compiler_params=_params("parallel", "parallel"), name="conv3",
    )(proj, w, b.reshape(1, -1))


def _filt_kernel(z_ref, w1_ref, b1_ref, f1_ref, w2_ref, b2_ref, f2_ref, w3_ref, dl_ref, h_ref, tap_ref, nrm_ref, *,
                 tm, cw):
    i = pl.program_id(0)
    z = z_ref[...]

    def mm(a, w_r):
        return _dot3(*_split_bf16(a), *_split_bf16(w_r[...]))

    a = jnp.sin(f1_ref[...] * (mm(z, w1_ref) + b1_ref[...]))
    a = jnp.sin(f2_ref[...] * (mm(a, w2_ref) + b2_ref[...]))
    dec = jnp.exp(-z[:, 0:1] * dl_ref[...])
    h = _dot(a.astype(BF16), w3_ref[...]) * jnp.concatenate([dec] * (h_ref.shape[1] // cw), axis=1)
    h_ref[...] = h.astype(h_ref.dtype)
    row = i * tm + lax.broadcasted_iota(jnp.int32, (tm, 1), 0)
    col = lax.broadcasted_iota(jnp.int32, (1, h.shape[1]), 1)
    skip = (row == 0) & ((col // cw) % 2 == 1)
    part = jnp.sum(jnp.where(skip, 0.0, jnp.abs(h)), axis=0, keepdims=True)

    @pl.when(i == 0)
    def _():
        nrm_ref[...] = jnp.zeros(nrm_ref.shape, F32)
        tap_ref[...] = h[0:SUBLANES]

    nrm_ref[...] += jnp.broadcast_to(part, nrm_ref.shape)


def hyena_filter(z, w1p, b1, f1, w2, b2, f2, w3, deltas):
    l = z.shape[0]
    n = w3.shape[1]
    cw = deltas.shape[0]
    hid = w2.shape[0]
    tm = min(l, 256)
    full = lambda a: pl.BlockSpec(a.shape, lambda i: (0, 0))
    ins = [w1p, b1.reshape(1, hid), f1.reshape(1, hid), w2, b2.reshape(1, hid), f2.reshape(1, hid), w3,
           deltas.reshape(1, cw)]
    small = pl.BlockSpec((SUBLANES, n), lambda i: (0, 0))
    return pl.pallas_call(
        functools.partial(_filt_kernel, tm=tm, cw=cw), grid=(l // tm,),
        in_specs=[pl.BlockSpec((tm, LANE), lambda i: (i, 0))] + [full(a) for a in ins],
        out_specs=[pl.BlockSpec((tm, n), lambda i: (i, 0)), small, small],
        out_shape=[jax.ShapeDtypeStruct((l, n), BF16), jax.ShapeDtypeStruct((SUBLANES, n), F32),
                   jax.ShapeDtypeStruct((SUBLANES, n), F32)],
        compiler_params=_params("arbitrary"), name="hyena_filter",
    )(z, *ins)


@functools.lru_cache(maxsize=None)
def _dft_consts(a):
    n = a * a
    hh = a // 2
    n1 = np.arange(hh)
    k1 = np.arange(hh)
    n2 = np.arange(a)
    k2 = np.arange(a)
    ang1 = 2 * np.pi * np.outer(k1 + 0.5, n1) / a
    m1 = np.zeros((hh, 2, hh))
    m1[:, 0], m1[:, 1] = np.cos(ang1), -np.sin(ang1)
    m1 = m1.reshape(2 * hh, hh)
    phi = 2 * np.pi * (n2[None, None, :] * (k1[:, None, None] + 0.5) / n + n2[None, None, :] * k2[None, :, None] / a)
    c, s = np.cos(phi), np.sin(phi)
    g = np.zeros((hh, 2, a, 2, a))
    g[:, 0, :, 0, :], g[:, 0, :, 1, :], g[:, 1, :, 0, :], g[:, 1, :, 1, :] = c, s, -s, c
    g = g.reshape(hh, 2 * a, 2 * a)
    gt = np.ascontiguousarray(np.transpose(g, (0, 2, 1)))
    al = 2 * np.pi * np.outer(n1, k1 + 0.5) / a
    mb = np.zeros((hh, hh, 2))
    mb[:, :, 0], mb[:, :, 1] = 2 / n * np.cos(al), -2 / n * np.sin(al)
    mb = mb.reshape(hh, 2 * hh)
    eye = np.eye(ROWS_BF16)
    return tuple(x.astype(np.float32) for x in (np.kron(m1, eye), g, gt, np.kron(mb, eye)))


@functools.lru_cache(maxsize=None)
def _dense_dft_consts(l):
    n = 2 * l
    ang = 2 * np.pi * np.outer(np.arange(l) + 0.5, np.arange(l)) / n
    mf = np.concatenate([np.cos(ang), -np.sin(ang)], axis=0)
    mi = np.concatenate([2 / n * np.cos(ang.T), -2 / n * np.sin(ang.T)], axis=1)
    return tuple(x.astype(np.float32) for x in (mf, mi))


def _stage1_kernel(x_ref, a_ref, o_ref, a16_ref):
    @pl.when((pl.program_id(0) == 0) & (pl.program_id(1) == 0))
    def _():
        a16_ref[...] = a_ref[...].astype(BF16)

    _, hh, t, ct = x_ref.shape
    x = x_ref[0].reshape(hh * t, ct).astype(BF16)
    o_ref[...] = _dot(a16_ref[...], x).astype(o_ref.dtype).reshape(o_ref.shape)


def dft_stage1(x4, p, a1):
    _, hh, a, c = x4.shape
    ct = _pick_tile(c, 512)
    t = ROWS_BF16
    return pl.pallas_call(
        _stage1_kernel, grid=(a // t, c // ct),
        in_specs=[pl.BlockSpec((1, hh, t, ct), lambda j, q: (p, 0, j, q)),
                  pl.BlockSpec(a1.shape, lambda j, q: (0, 0))],
        out_specs=pl.BlockSpec((2 * hh, t, ct), lambda j, q: (0, j, q)),
        out_shape=jax.ShapeDtypeStruct((2 * hh, a, c), BF16),
        scratch_shapes=[pltpu.VMEM(a1.shape, BF16)],
        compiler_params=_params("arbitrary", "arbitrary"), name="dft_stage1",
    )(x4, a1)


def _spectrum_product(x, h0, h1, tap0, n0, n1):
    half = x.shape[0] // 2
    inv = 1.0 / (n0 + n1)
    kr = (h0[:half] + h1[:half] - tap0) * inv
    ki = (h0[half:] - h1[half:]) * inv
    xr, xi = x[:half], x[half:]
    return jnp.concatenate([xr * kr - xi * ki, xr * ki + xi * kr], axis=0)


K1_PER_STEP = 2


def _mid_filter_kernel(y0_ref, y1_ref, g_ref, tap_ref, n0_ref, n1_ref, o_ref):
    inv = 1.0 / (n0_ref[0:1, :] + n1_ref[0:1, :])
    tap = tap_ref[0:1, :]
    for b in range(y0_ref.shape[0]):
        g = g_ref[b].astype(BF16)
        x0 = _dot(g, y0_ref[b])
        x1 = _dot(g, y1_ref[b])
        half = x0.shape[0] // 2
        kr = (x0[:half] + x1[:half] - tap) * inv
        ki = (x0[half:] - x1[half:]) * inv
        o_ref[b] = jnp.concatenate([kr, ki], axis=0).astype(o_ref.dtype)


def filter_spectrum(h, taps, nrm, consts, a):
    a1, g, _, _ = consts
    l, n = h.shape
    c = n // 4
    hh = a // 2
    y = dft_stage1(h.reshape(1, hh, a, n), 0, a1).reshape(hh, 2 * a, n)
    kb = K1_PER_STEP
    slab = lambda side: pl.BlockSpec((kb, 2 * a, c), lambda k, o: (k, 0, 2 * o + side))
    row = lambda side: pl.BlockSpec((SUBLANES, c), lambda k, o: (0, 2 * o + side))
    return pl.pallas_call(
        _mid_filter_kernel, grid=(hh // kb, 2),
        in_specs=[slab(0), slab(1), pl.BlockSpec((kb, 2 * a, 2 * a), lambda k, o: (k, 0, 0)), row(1), row(0), row(1)],
        out_specs=pl.BlockSpec((kb, 2 * a, c), lambda k, o: (k, 0, o)),
        out_shape=jax.ShapeDtypeStruct((hh, 2 * a, 2 * c), BF16),
        compiler_params=_params("parallel", "parallel"), name="dft_mid_filter",
    )(y, y, g, taps, nrm, nrm)


def _mid_conv_kernel(y_ref, g_ref, gt_ref, kf_ref, o_ref):
    for b in range(y_ref.shape[0]):
        x = _dot(g_ref[b].astype(BF16), y_ref[b])
        kf = kf_ref[b].astype(F32)
        half = x.shape[0] // 2
        xr, xi, kr, ki = x[:half], x[half:], kf[:half], kf[half:]
        z = jnp.concatenate([xr * kr - xi * ki, xr * ki + xi * kr], axis=0).astype(BF16)
        o_ref[b] = _dot(gt_ref[b].astype(BF16), z).astype(o_ref.dtype)


def dft_mid_conv(y, g, gt, kf, order):
    hh, a2, c = y.shape
    kb = K1_PER_STEP
    slab = pl.BlockSpec((kb, a2, c), lambda k: (k, 0, 0))
    mat = pl.BlockSpec((kb, a2, a2), lambda k: (k, 0, 0))
    return pl.pallas_call(
        _mid_conv_kernel, grid=(hh // kb,),
        in_specs=[slab, mat, mat, pl.BlockSpec((kb, a2, c), lambda k: (k, 0, order))],
        out_specs=slab, out_shape=jax.ShapeDtypeStruct((hh, a2, c), BF16),
        compiler_params=_params("parallel"), name="dft_mid_conv",
    )(y, g, gt, kf)


def _last_kernel(b_ref, a_ref, xa_ref, zb_ref, bias_ref, o_ref, a16_ref):
    @pl.when((pl.program_id(0) == 0) & (pl.program_id(1) == 0))
    def _():
        a16_ref[...] = a_ref[...].astype(BF16)

    rows, t, ct = b_ref.shape
    conv = _dot(a16_ref[...], b_ref[...].reshape(rows * t, ct)).reshape(rows // 2, t, ct)
    o_ref[0] = (xa_ref[0] * (conv + bias_ref[...] * zb_ref[0])).astype(o_ref.dtype)


def dft_last(b, a2m, xa4, pa, zb4, pb, bias_row):
    rows, a, c = b.shape
    hh = rows // 2
    ct = _pick_tile(c, 512)
    t = ROWS_BF16
    sig = lambda pp: pl.BlockSpec((1, hh, t, ct), lambda j, q: (pp, 0, j, q))
    return pl.pallas_call(
        _last_kernel, grid=(a // t, c // ct),
        in_specs=[pl.BlockSpec((rows, t, ct), lambda j, q: (0, j, q)),
                  pl.BlockSpec(a2m.shape, lambda j, q: (0, 0)),
                  sig(pa), sig(pb), pl.BlockSpec((1, ct), lambda j, q: (0, q))],
        out_specs=sig(0), out_shape=jax.ShapeDtypeStruct((1, hh, a, c), F32),
        scratch_shapes=[pltpu.VMEM(a2m.shape, BF16)],
        compiler_params=_params("arbitrary", "arbitrary"), name="dft_last",
    )(b, a2m, xa4, zb4, bias_row)


def long_conv_gated(u3, kf, bias, consts, a):
    a1, g, gt, a2m = consts
    _, l, c = u3.shape
    hh = a // 2
    u4 = u3.reshape(3, hh, a, c)

    def conv_gate(src4, p_src, xa_p, order):
        y = dft_stage1(src4, p_src, a1).reshape(hh, 2 * a, c)
        bb = dft_mid_conv(y, g, gt, kf, order).reshape(2 * hh, a, c)
        return dft_last(bb, a2m, u4, xa_p, src4, p_src, bias[order].reshape(1, c))

    z2 = conv_gate(u4, 2, 0, 0)
    return conv_gate(z2, 0, 1, 1).reshape(l, c)


def _ctx_spec_kernel(h_ref, mf_ref, o_ref):
    o_ref[...] = _dot(mf_ref[...].astype(BF16), h_ref[...])


def ctx_filter_spectrum(h, mf):
    l, n = h.shape
    tn = _pick_tile(n, 1024)
    return pl.pallas_call(
        _ctx_spec_kernel, grid=(n // tn,),
        in_specs=[pl.BlockSpec((l, tn), lambda j: (0, j)), pl.BlockSpec(mf.shape, lambda j: (0, 0))],
        out_specs=pl.BlockSpec((2 * l, tn), lambda j: (0, j)),
        out_shape=jax.ShapeDtypeStruct((2 * l, n), F32),
        compiler_params=_params("parallel"), name="ctx_filter_spectrum",
    )(h, mf)


def _ctx_hyena_kernel(u_ref, mf_ref, mi_ref, hf0a, hf1a, hf0b, hf1b, tap0, tap1, n0a, n1a, n0b, n1b, bias_ref, o_ref):
    x1, x2, z = u_ref[0], u_ref[1], u_ref[2]

    mf, mi = mf_ref[...].astype(BF16), mi_ref[...].astype(BF16)

    def conv(sig, h0_ref, h1_ref, tap_ref, n0_ref, n1_ref):
        x = _dot(mf, sig.astype(BF16))
        zz = _spectrum_product(x, h0_ref[...], h1_ref[...], tap_ref[0:1, :], n0_ref[0:1, :], n1_ref[0:1, :])
        return _dot(mi, zz.astype(BF16))

    z2 = x1 * (conv(z, hf0a, hf1a, tap0, n0a, n1a) + bias_ref[0:1, :] * z)
    o_ref[...] = (x2 * (conv(z2, hf0b, hf1b, tap1, n0b, n1b) + bias_ref[1:2, :] * z2)).astype(o_ref.dtype)


def ctx_hyena(u3, hf, taps, nrm, bias, mf, mi):
    _, l, c = u3.shape
    tc = _pick_tile(c, 512)
    nb = c // tc
    const = lambda a: pl.BlockSpec(a.shape, lambda j: (0, 0))
    col = lambda rows, q: pl.BlockSpec((rows, tc), lambda j: (0, q * nb + j))
    mats = [mf, mi]
    return pl.pallas_call(
        _ctx_hyena_kernel, grid=(nb,),
        in_specs=[pl.BlockSpec((3, l, tc), lambda j: (0, 0, j))] + [const(m) for m in mats]
        + [col(2 * l, 0), col(2 * l, 1), col(2 * l, 2), col(2 * l, 3), col(8, 1), col(8, 3),
           col(8, 0), col(8, 1), col(8, 2), col(8, 3), pl.BlockSpec((2, tc), lambda j: (0, j))],
        out_specs=pl.BlockSpec((l, tc), lambda j: (0, j)),
        out_shape=jax.ShapeDtypeStruct((l, c), BF16),
        compiler_params=_params("parallel"), name="ctx_hyena",
    )(u3, *mats, hf, hf, hf, hf, taps, taps, nrm, nrm, nrm, nrm, bias)


def _axial_tables(n_tokens, n_rot):
    rows = n_tokens // GRID_W
    row = jnp.broadcast_to(jnp.arange(rows)[:, None], (rows, GRID_W)).reshape(-1).astype(F32)
    col = jnp.broadcast_to(jnp.arange(GRID_W)[None, :], (rows, GRID_W)).reshape(-1).astype(F32)
    n_freq = n_rot // 4
    inv = ROPE_THETA ** (-jnp.arange(n_freq, dtype=F32) / n_freq)
    ang = jnp.concatenate([row[:, None] * inv, col[:, None] * inv], axis=-1)
    return jnp.cos(ang), jnp.sin(ang)


def _rope_tables_full(s, n_ctx):
    cos, sin = _axial_tables(s, HEAD_DIM)
    t0 = jnp.concatenate([jnp.concatenate([cos, cos], axis=1), jnp.ones((n_ctx, LANE), F32)], axis=0)
    t1 = jnp.concatenate([jnp.concatenate([-sin, sin], axis=1), jnp.zeros((n_ctx, LANE), F32)], axis=0)
    return t0, t1


def _rope_tables_half(s, n_ctx):
    cos, sin = _axial_tables(s, ROPE_DIM)
    q = ROPE_DIM // 2
    zq = jnp.zeros((s, q), F32)
    z2 = jnp.zeros((s, LANE - ROPE_DIM), F32)
    t0 = jnp.concatenate([cos, cos, z2], axis=1)
    t1 = jnp.concatenate([-sin, zq, z2], axis=1)
    t2 = jnp.concatenate([zq, sin, z2], axis=1)
    c0 = jnp.concatenate([jnp.ones((n_ctx, ROPE_DIM), F32), jnp.zeros((n_ctx, LANE - ROPE_DIM), F32)], axis=1)
    cz = jnp.zeros((n_ctx, LANE), F32)
    return jnp.concatenate([t0, c0], axis=0), jnp.concatenate([t1, cz], axis=0), jnp.concatenate([t2, cz], axis=0)


def _filter_features(l):
    pos = jnp.arange(l, dtype=F32)
    t = pos / max(l - 1, 1)
    bands = jnp.linspace(1e-4, HYENA_BANDS - 1, HYENA_BANDS, dtype=F32)
    ang = (2.0 * math.pi / l) * pos[:, None] * bands[None, :]
    z = jnp.concatenate([t[:, None], jnp.cos(ang), -jnp.sin(ang)], axis=-1)
    return jnp.pad(z, ((0, 0), (0, LANE - HYENA_EMB)))


Q_STACK = 1024
KV_TILE_CAP = 1408


def _attention_pair(qp, kp, vt, s, n_ctx, *, kv_heads, groups, window, sink=None, q2=None, k2=None):
    common = dict(kv_heads=kv_heads, groups=groups, q2=q2, k2=k2, sink=sink)
    o_ctx = flash(qp, kp, vt, n_q=n_ctx, q_row0=s, kv_row0=s, n_kv_rows=n_ctx, tq=n_ctx, tk=n_ctx, **common)
    if window:
        o_lat = window_attention(qp, kp, vt, kv_heads=kv_heads, groups=groups, n_lat=s, n_ctx=n_ctx, tq=ROW_TILE,
                                 sink=sink)
    else:
        o_lat = flash(qp, kp, vt, n_q=s, q_row0=0, kv_row0=0, n_kv_rows=n_ctx + s, tq=Q_STACK // groups,
                      tk=_pick_tile(n_ctx + s, KV_TILE_CAP), **common)
    return o_ctx, o_lat


def _even_layer(u, s, n_ctx, w_in, qn_g, kn_g, sink, conv_w, conv_b, fparams, hy_bias, rope_full, feats, dft):
    w_all, e = w_in
    src_gate = ATT_WIDTH + 2 * KV_WIDTH + 3 * B_WIDTH
    proj = matmul_permuted(u, w_all, e, [(src_gate, BRANCH), (0, ATT_WIDTH), (ATT_WIDTH, 2 * KV_WIDTH),
                                         (ATT_WIDTH + 2 * KV_WIDTH, 3 * B_WIDTH)], F32)
    qp = head_prep(proj, OFF_Q, A_HEADS, qn_g, rope_full, "half64", HEAD_DIM, HEAD_DIM ** -0.5 * LOG2E, True)
    kp = head_prep(proj, OFF_K, A_KV_HEADS, kn_g, rope_full, "half64", HEAD_DIM, 1.0)
    vt = transpose_heads(proj, OFF_V, A_KV_HEADS)
    a_ctx, a_lat = _attention_pair(qp, kp, vt, s, n_ctx, kv_heads=A_KV_HEADS, groups=A_HEADS // A_KV_HEADS,
                                   window=True, sink=sink)
    w1, b1, f1, w2, b2, f2, w3 = fparams
    w1p = jnp.pad(w1, ((0, LANE - HYENA_EMB), (0, 0)))
    w3 = w3.astype(BF16)
    deltas = jnp.linspace(DECAY_MAX, DECAY_MIN, B_WIDTH, dtype=F32)
    uc3, ul3 = conv3(proj, OFF_REST, conv_w, conv_b, n_ctx)
    a = int(round(math.sqrt(2 * s)))
    consts, (mf, mi) = dft
    h_lat, tap_lat, nrm_lat = hyena_filter(feats[0], w1p, b1, f1, w2, b2, f2, w3, deltas)
    kf_lat = filter_spectrum(h_lat, tap_lat, nrm_lat, consts, a)
    b_lat = long_conv_gated(ul3, kf_lat, hy_bias, consts, a)
    h_ctx, tap_ctx, nrm_ctx = hyena_filter(feats[1], w1p, b1, f1, w2, b2, f2, w3, deltas)
    hf_ctx = ctx_filter_spectrum(h_ctx, mf)
    b_ctx = ctx_hyena(uc3, hf_ctx, tap_ctx, nrm_ctx, hy_bias, mf, mi)
    return proj, a_ctx, a_lat, b_ctx, b_lat


def _odd_layer(u, s, n_ctx, w_in, qn_g, kn_g, cq_g, ckv_g, wuq, wukv, mq_g, mk_g, rope_full, rope_half):
    d = w_in.shape[0]
    parts = jnp.split(w_in, np.cumsum([ATT_WIDTH, KV_WIDTH, KV_WIDTH, Q_LORA, KV_LORA, ROPE_DIM]).tolist(), axis=1)
    wq, wk, wv, wmq, wmkv, wmkr, wg = parts
    w_perm = jnp.concatenate([wg, wq, wk, wv, wmq, wmkv, wmkr, jnp.zeros((d, LANE - ROPE_DIM), w_in.dtype)], axis=1)
    proj = matmul(u, w_perm.astype(BF16), F32)
    qp = head_prep(proj, OFF_Q, C_HEADS, qn_g, rope_full, "half64", HEAD_DIM, HEAD_DIM ** -0.5 * LOG2E, True)
    kp = head_prep(proj, OFF_K, C_KV_HEADS, kn_g, rope_full, "half64", HEAD_DIM, 1.0)
    vt = transpose_heads(proj, OFF_V, C_KV_HEADS)
    c_ctx, c_lat = _attention_pair(qp, kp, vt, s, n_ctx, kv_heads=C_KV_HEADS, groups=C_HEADS // C_KV_HEADS,
                                   window=False)
    pad_r = LANE - ROPE_DIM
    wuq3 = wuq.reshape(Q_LORA, M_HEADS, QK_DIM)
    wuq_p = jnp.concatenate([wuq3[:, :, :NOPE_DIM].reshape(Q_LORA, -1),
                             jnp.pad(wuq3[:, :, NOPE_DIM:], ((0, 0), (0, 0), (0, pad_r))).reshape(Q_LORA, -1)], axis=1)
    wukv3 = wukv.reshape(KV_LORA, M_HEADS, NOPE_DIM + V_DIM)
    wukv_p = jnp.concatenate([wukv3[:, :, :NOPE_DIM].reshape(KV_LORA, -1),
                              wukv3[:, :, NOPE_DIM:].reshape(KV_LORA, -1)], axis=1)
    q_raw = norm_matmul(proj, OFF_MQ, cq_g, wuq_p.astype(BF16))
    kv_raw = norm_matmul(proj, OFF_MKV, ckv_g, wukv_p.astype(BF16))
    scale = QK_DIM ** -0.5 * LOG2E
    gq_r = jnp.pad(mq_g[NOPE_DIM:], (0, pad_r))
    gk_r = jnp.pad(mk_g[NOPE_DIM:], (0, pad_r))
    qn = head_prep(q_raw, 0, M_HEADS, mq_g[:NOPE_DIM], (), "none", NOPE_DIM, scale, True)
    qr = head_prep(q_raw, M_HEADS * NOPE_DIM, M_HEADS, gq_r, rope_half, "half32", ROPE_DIM, scale, True)
    kn = head_prep(kv_raw, 0, M_HEADS, mk_g[:NOPE_DIM], (), "none", NOPE_DIM, 1.0)
    kr = head_prep(proj, OFF_MKR, 1, gk_r, rope_half, "half32", ROPE_DIM, 1.0)
    vmt = transpose_heads(kv_raw, M_HEADS * NOPE_DIM, M_HEADS)
    d_ctx, d_lat = _attention_pair(qn, kn, vmt, s, n_ctx, kv_heads=M_HEADS, groups=1, window=False, q2=qr, k2=kr)
    return proj, c_ctx, c_lat, d_ctx, d_lat


def kernel(x, c, ctx, c_ctx, ada_w, ada_b, norm_g, w_out, ev_w_in, ev_qn_g, ev_kn_g, ev_sink, ev_conv_w, ev_conv_b, hy_w1, hy_b1, hy_f1, hy_w2, hy_b2, hy_f2, hy_w3, hy_bias, od_w_in, od_qn_g, od_kn_g, od_cq_g, od_ckv_g, od_wuq, od_wukv, od_mq_g, od_mk_g):
    _, s, d = x.shape
    n_ctx = ctx.shape[1]
    depth = ada_w.shape[0]
    h = jnp.concatenate([x[0], ctx[0]], axis=0)
    cc = jnp.zeros((8, d), F32).at[0].set(c[0]).at[1].set(c_ctx)
    mod = ada_mod(cc, ada_w, ada_b)
    rope_full = _rope_tables_full(s, n_ctx)
    rope_half = _rope_tables_half(s, n_ctx)
    feats = (_filter_features(s), _filter_features(n_ctx))
    as_arrays = lambda mats: tuple(jnp.asarray(m) for m in mats)
    dft = (as_arrays(_dft_consts(int(round(math.sqrt(2 * s))))), as_arrays(_dense_dft_consts(n_ctx)))
    u = norm_mod(h, norm_g[0], mod[0], n_ctx)
    for i in range(depth):
        if i % 2 == 0:
            e = i // 2
            proj, o1c, o1l, o2c, o2l = _even_layer(
                u, s, n_ctx, (ev_w_in, e), ev_qn_g[e], ev_kn_g[e], ev_sink[e], ev_conv_w[e], ev_conv_b[e],
                (hy_w1[e], hy_b1[e], hy_f1[e], hy_w2[e], hy_b2[e], hy_f2[e], hy_w3[e]), hy_bias[e], rope_full, feats,
                dft)
        else:
            o = i // 2
            proj, o1c, o1l, o2c, o2l = _odd_layer(
                u, s, n_ctx, od_w_in[o], od_qn_g[o], od_kn_g[o], od_cq_g[o], od_ckv_g[o], od_wuq[o], od_wukv[o],
                od_mq_g[o], od_mk_g[o], rope_full, rope_half)
        w = w_out[i].astype(BF16)
        if i < depth - 1:
            h, u = out_proj(o1c, o1l, o2c, o2l, proj, h, mod[i], w, (norm_g[i + 1], mod[i + 1]))
        else:
            h = out_proj(o1c, o1l, o2c, o2l, proj, h, mod[i], w, None)
    return h[None]
```

```python
import functools
import math

import numpy as np
import jax
import jax.numpy as jnp
from jax import lax
from jax.experimental import pallas as pl
from jax.experimental.pallas import tpu as pltpu

F32 = jnp.float32
BF16 = jnp.bfloat16

GRID_W = 64
HEAD_DIM = 128
ROPE_THETA = 10000.0
EPS = 1e-6
NEG_INF = -1e30
A_HEADS, A_KV_HEADS, WINDOW = 8, 2, 128
B_WIDTH = 1024
HYENA_BANDS = 16
HYENA_EMB = 1 + 2 * HYENA_BANDS
DECAY_TARGET = 1e-2
DECAY_MAX = abs(math.log(DECAY_TARGET)) / 0.3
DECAY_MIN = abs(math.log(DECAY_TARGET)) / 1.5
C_HEADS, C_KV_HEADS = 8, 2
M_HEADS, Q_LORA, KV_LORA, NOPE_DIM, ROPE_DIM, V_DIM = 8, 512, 256, 128, 64, 128
QK_DIM = NOPE_DIM + ROPE_DIM
ATT_WIDTH = A_HEADS * HEAD_DIM
KV_WIDTH = A_KV_HEADS * HEAD_DIM
BRANCH = ATT_WIDTH + B_WIDTH

LOG2E = math.log2(math.e)
LANE = 128
SUBLANES = 8
ROWS_BF16 = 16
ROW_TILE = 256
VMEM_LIMIT = 48 * 1024 * 1024

OFF_GATE, OFF_Q, OFF_K, OFF_V, OFF_REST = 0, BRANCH, BRANCH + ATT_WIDTH, BRANCH + ATT_WIDTH + KV_WIDTH, BRANCH + ATT_WIDTH + 2 * KV_WIDTH
OFF_MQ, OFF_MKV, OFF_MKR = OFF_REST, OFF_REST + Q_LORA, OFF_REST + Q_LORA + KV_LORA


def _params(*sem):
    return pltpu.CompilerParams(dimension_semantics=sem, vmem_limit_bytes=VMEM_LIMIT)


def _split_bf16(x):
    hi = x.astype(BF16)
    return hi, (x - hi.astype(F32)).astype(BF16)


def _dot(a, b):
    return jnp.dot(a, b, preferred_element_type=F32)


def _dot3(a_hi, a_lo, b_hi, b_lo):
    return _dot(a_hi, b_hi) + (_dot(a_hi, b_lo) + _dot(a_lo, b_hi))


def _ada_kernel(c_ref, w_ref, b_ref, o_ref):
    c = c_ref[...]
    a = (c * jax.nn.sigmoid(c)).astype(BF16)
    o_ref[0] = _dot(a, w_ref[0].astype(BF16)) + b_ref[0]


def ada_mod(cc, ada_w, ada_b):
    depth, d, n = ada_w.shape
    tn = _pick_tile(n, 1024)
    return pl.pallas_call(
        _ada_kernel, grid=(depth, n // tn),
        in_specs=[pl.BlockSpec((8, d), lambda l, j: (0, 0)),
                  pl.BlockSpec((1, d, tn), lambda l, j: (l, 0, j)),
                  pl.BlockSpec((1, 1, tn), lambda l, j: (l, 0, j))],
        out_specs=pl.BlockSpec((1, 8, tn), lambda l, j: (l, 0, j)),
        out_shape=jax.ShapeDtypeStruct((depth, 8, n), F32),
        compiler_params=_params("parallel", "parallel"), name="ada_mod",
    )(cc, ada_w, ada_b.reshape(depth, 1, n))


def _modulated_norm(x, g, mod_ref, is_ctx):
    d = x.shape[1]
    y = x * lax.rsqrt(jnp.mean(x * x, axis=-1, keepdims=True) + EPS) * g
    sh = jnp.where(is_ctx, mod_ref[1:2, 0:d], mod_ref[0:1, 0:d])
    sc = jnp.where(is_ctx, mod_ref[1:2, d:2 * d], mod_ref[0:1, d:2 * d])
    return y * (1.0 + sc) + sh


def _norm_kernel(x_ref, g_ref, mod_ref, o_ref, *, lat_tiles):
    is_ctx = pl.program_id(0) >= lat_tiles
    o_ref[...] = _modulated_norm(x_ref[...], g_ref[...], mod_ref, is_ctx).astype(o_ref.dtype)


def norm_mod(h, g, mod, n_ctx):
    r, d = h.shape
    tm = ROW_TILE
    return pl.pallas_call(
        functools.partial(_norm_kernel, lat_tiles=(r - n_ctx) // tm), grid=(r // tm,),
        in_specs=[pl.BlockSpec((tm, d), lambda i: (i, 0)),
                  pl.BlockSpec((1, d), lambda i: (0, 0)),
                  pl.BlockSpec(mod.shape, lambda i: (0, 0))],
        out_specs=pl.BlockSpec((tm, d), lambda i: (i, 0)),
        out_shape=jax.ShapeDtypeStruct((r, d), BF16),
        compiler_params=_params("parallel"), name="norm_mod",
    )(h, g.reshape(1, d), mod)


def _mm_kernel(a_ref, b_ref, o_ref):
    o_ref[...] = _dot(a_ref[...], b_ref[...]).astype(o_ref.dtype)


def _pick_tile(n, cap, unit=LANE):
    best = unit
    for t in range(unit, cap + 1, unit):
        if n % t == 0:
            best = t
    return best


def matmul(a, b, out_dtype):
    m, k = a.shape
    n = b.shape[1]
    tm = _pick_tile(m, 1536, 8)
    tn = _pick_tile(n, 1024)
    return pl.pallas_call(
        _mm_kernel, grid=(m // tm, n // tn),
        in_specs=[pl.BlockSpec((tm, k), lambda i, j: (i, 0)),
                  pl.BlockSpec((k, tn), lambda i, j: (0, j))],
        out_specs=pl.BlockSpec((tm, tn), lambda i, j: (i, j)),
        out_shape=jax.ShapeDtypeStruct((m, n), out_dtype),
        compiler_params=_params("parallel", "parallel"), name="in_proj",
    )(a, b)


def _mm_cast_kernel(a_ref, b_ref, o_ref):
    o_ref[...] = _dot(a_ref[...], b_ref[0].astype(a_ref.dtype)).astype(o_ref.dtype)


def matmul_permuted(a, w3, layer, segments, out_dtype):
    m, k = a.shape
    n = sum(width for _, width in segments)
    tm = _pick_tile(m, 1536, 8)
    tn = LANE
    for t in range(LANE, 1024 + 1, LANE):
        if all(start % t == 0 and width % t == 0 for start, width in segments):
            tn = t
    src_tiles = [(start + off) // tn for start, width in segments for off in range(0, width, tn)]

    def src(j):
        idx = src_tiles[0]
        for t, s_t in enumerate(src_tiles[1:], 1):
            idx = jnp.where(j == t, s_t, idx)
        return idx
    return pl.pallas_call(
        _mm_cast_kernel, grid=(m // tm, n // tn),
        in_specs=[pl.BlockSpec((tm, k), lambda i, j: (i, 0)),
                  pl.BlockSpec((1, k, tn), lambda i, j: (layer, 0, src(j)))],
        out_specs=pl.BlockSpec((tm, tn), lambda i, j: (i, j)),
        out_shape=jax.ShapeDtypeStruct((m, n), out_dtype),
        compiler_params=_params("parallel", "parallel"), name="in_proj",
    )(a, w3)


def _normmm_kernel(x_ref, g_ref, w_ref, o_ref):
    x = x_ref[...].astype(F32)
    y = x * lax.rsqrt(jnp.mean(x * x, axis=-1, keepdims=True) + EPS) * g_ref[...]
    o_ref[...] = _dot(y.astype(BF16), w_ref[...]).astype(o_ref.dtype)


def norm_matmul(x, col_off, g, w):
    r = x.shape[0]
    k, n = w.shape
    tm = _pick_tile(r, 768, 8)
    cb = col_off // k
    return pl.pallas_call(
        _normmm_kernel, grid=(r // tm,),
        in_specs=[pl.BlockSpec((tm, k), lambda i: (i, cb)),
                  pl.BlockSpec((1, k), lambda i: (0, 0)),
                  pl.BlockSpec((k, n), lambda i: (0, 0))],
        out_specs=pl.BlockSpec((tm, n), lambda i: (i, 0)),
        out_shape=jax.ShapeDtypeStruct((r, n), F32),
        compiler_params=_params("parallel"), name="norm_matmul",
    )(x, g.reshape(1, k), w)


def _prep_kernel(*refs, mode, n_real, scale, n_heads, head_major):
    x_ref, g_ref = refs[0], refs[1]
    o_ref = refs[-1]
    g = g_ref[...]
    tabs = [t[...] for t in refs[2:-1]]
    for h in range(n_heads):
        x = x_ref[:, h * LANE:(h + 1) * LANE].astype(F32)
        ms = jnp.sum(x * x, axis=-1, keepdims=True) * (1.0 / n_real)
        xn = x * lax.rsqrt(ms + EPS) * g
        if mode == "none":
            y = xn
        elif mode == "half64":
            y = xn * tabs[0] + pltpu.roll(xn, 64, 1) * tabs[1]
        else:
            y = xn * tabs[0] + pltpu.roll(xn, 96, 1) * tabs[1] + pltpu.roll(xn, 32, 1) * tabs[2]
        y = (y * scale).astype(o_ref.dtype)
        if head_major:
            o_ref[h] = y
        else:
            o_ref[:, h * LANE:(h + 1) * LANE] = y


def head_prep(x, col_off, n_heads, g, tables, mode, n_real, scale, head_major=False):
    r = x.shape[0]
    w = n_heads * LANE
    tm = _pick_tile(r, 768, 8)
    tab_spec = pl.BlockSpec((tm, LANE), lambda i: (i, 0))
    if head_major:
        out_spec = pl.BlockSpec((n_heads, tm, LANE), lambda i: (0, i, 0))
        out_shape = jax.ShapeDtypeStruct((n_heads, r, LANE), BF16)
    else:
        out_spec = pl.BlockSpec((tm, w), lambda i: (i, 0))
        out_shape = jax.ShapeDtypeStruct((r, w), BF16)
    return pl.pallas_call(
        functools.partial(_prep_kernel, mode=mode, n_real=n_real, scale=scale, n_heads=n_heads,
                          head_major=head_major),
        grid=(r // tm,),
        in_specs=[pl.BlockSpec((tm, w), lambda i: (i, col_off // w)),
                  pl.BlockSpec((1, LANE), lambda i: (0, 0))] + [tab_spec] * len(tables),
        out_specs=out_spec, out_shape=out_shape,
        compiler_params=_params("parallel"), name="head_prep_" + mode,
    )(x, g.reshape(1, LANE), *tables)


def _vt_kernel(x_ref, o_ref):
    o_ref[...] = x_ref[...].astype(F32).T.astype(o_ref.dtype)


def transpose_heads(x, col_off, n_heads):
    r = x.shape[0]
    w = n_heads * LANE
    tm = _pick_tile(r, 768, LANE)
    return pl.pallas_call(
        _vt_kernel, grid=(r // tm,),
        in_specs=[pl.BlockSpec((tm, w), lambda i: (i, col_off // w))],
        out_specs=pl.BlockSpec((w, tm), lambda i: (0, i)),
        out_shape=jax.ShapeDtypeStruct((w, r), BF16),
        compiler_params=_params("parallel"), name="transpose_heads",
    )(x)


def _flash_kernel(*refs, groups, tq, tk, dv, n_kv, chain, has_qk2, has_sink, key_chunk):
    q_ref, qn_ref, k_ref, kn_ref, vt_ref = refs[0:5]
    pos = 5
    q2_ref = q2n_ref = k2_ref = k2n_ref = None
    if has_qk2:
        q2_ref, q2n_ref, k2_ref, k2n_ref = refs[5:9]
        pos = 9
    if has_sink:
        sink_ref = refs[pos]
        pos += 1
    o_ref, m_ref, l_ref, acc_ref, s_ref, mt_ref = refs[pos:pos + 6]
    i, j = pl.program_id(1), pl.program_id(2)
    n = groups * tq
    bounds = [(r0, min(r0 + key_chunk, tk)) for r0 in range(0, tk, key_chunk)]

    def stacked(qr, q2r):
        q = qr[...].reshape(n, qr.shape[2])
        if has_qk2:
            q = jnp.concatenate([q, q2r[...].reshape(n, LANE)], axis=1)
        return q

    def scores(kr, k2r, q, r0, r1):
        kk = kr[r0:r1, :]
        if has_qk2:
            kk = jnp.concatenate([kk, k2r[r0:r1, :]], axis=1)
        return lax.dot_general(kk, q, (((1,), (1,)), ((), ())), preferred_element_type=F32)

    @pl.when(j == 0)
    def _():
        m_ref[...] = jnp.full(m_ref.shape, NEG_INF, F32)
        l_ref[...] = jnp.zeros(l_ref.shape, F32)
        acc_ref[...] = jnp.zeros(acc_ref.shape, F32)

    @pl.when((j == 0) & (i == 0) if chain else j == 0)
    def _():
        q = stacked(q_ref, q2_ref)
        mt = jnp.full(mt_ref.shape, NEG_INF, F32)
        for r0, r1 in bounds:
            s = scores(k_ref, k2_ref, q, r0, r1)
            s_ref[r0:r1, :] = s
            mt = jnp.maximum(mt, jnp.max(s, axis=0, keepdims=True))
        mt_ref[...] = mt

    def step(kr, k2r, qr, q2r):
        prefetch = kr is not None
        if prefetch:
            q = stacked(qr, q2r)
        m_prev = m_ref[...]
        m_new = jnp.maximum(m_prev, mt_ref[...])
        alpha = jnp.exp2(m_prev - m_new)
        l_new = alpha * l_ref[...]
        acc_new = alpha * acc_ref[...]
        mt = jnp.full(mt_ref.shape, NEG_INF, F32)
        for r0, r1 in bounds:
            if prefetch:
                s_next = scores(kr, k2r, q, r0, r1)
            p = jnp.exp2(s_ref[r0:r1, :] - m_new)
            l_new = l_new + jnp.sum(p, axis=0, keepdims=True)
            acc_new = acc_new + _dot(vt_ref[:, r0:r1], p.astype(BF16))
            if prefetch:
                s_ref[r0:r1, :] = s_next
                mt = jnp.maximum(mt, jnp.max(s_next, axis=0, keepdims=True))
        l_ref[...] = l_new
        acc_ref[...] = acc_new
        m_ref[...] = m_new
        if prefetch:
            mt_ref[...] = mt

    if n_kv > 1:
        pl.when(j < n_kv - 1)(lambda: step(kn_ref, k2n_ref, q_ref, q2_ref))

    @pl.when(j == n_kv - 1)
    def _():
        if chain:
            step(k_ref, k2_ref, qn_ref, q2n_ref)
        else:
            step(None, None, None, None)
        m, l, acc = m_ref[...], l_ref[...], acc_ref[...]
        if has_sink:
            sk = sink_ref[0]
            m_fin = jnp.maximum(m, sk)
            a = jnp.exp2(m - m_fin)
            l = l * a + jnp.exp2(sk - m_fin)
            acc = acc * a
        o_t = acc / l
        for g in range(groups):
            o_ref[:, g * dv:(g + 1) * dv] = o_t[:, g * tq:(g + 1) * tq].T.astype(o_ref.dtype)


KEY_CHUNK = 256


def flash(q, k, vt, *, kv_heads, groups, n_q, q_row0, kv_row0, n_kv_rows, tq, tk, dv=HEAD_DIM,
          q2=None, k2=None, sink=None):
    dk = q.shape[2]
    qb0 = q_row0 // tq
    kb0 = kv_row0 // tk
    n_kv = n_kv_rows // tk

    def kv_row(i, j):
        return kb0 + j
    n = groups * tq
    n_qb = n_q // tq
    nxt = lambda j: jnp.minimum(j + 1, n_kv - 1)
    q_nxt = lambda i: qb0 + jnp.minimum(i + 1, n_qb - 1)
    in_specs = [pl.BlockSpec((groups, tq, dk), lambda h, i, j: (h, qb0 + i, 0)),
                pl.BlockSpec((groups, tq, dk), lambda h, i, j: (h, q_nxt(i), 0)),
                pl.BlockSpec((tk, dk), lambda h, i, j: (kv_row(i, 0), h)),
                pl.BlockSpec((tk, dk), lambda h, i, j: (kv_row(i, nxt(j)), h)),
                pl.BlockSpec((dv, tk), lambda h, i, j: (h, kv_row(i, j)))]
    args = [q, q, k, k, vt]
    if q2 is not None:
        in_specs += [pl.BlockSpec((groups, tq, LANE), lambda h, i, j: (h, qb0 + i, 0)),
                     pl.BlockSpec((groups, tq, LANE), lambda h, i, j: (h, q_nxt(i), 0)),
                     pl.BlockSpec((tk, LANE), lambda h, i, j: (kv_row(i, 0), 0)),
                     pl.BlockSpec((tk, LANE), lambda h, i, j: (kv_row(i, nxt(j)), 0))]
        args += [q2, q2, k2, k2]
    if sink is not None:
        in_specs.append(pl.BlockSpec((1, 1, n), lambda h, i, j: (h, 0, 0)))
        sink2 = sink.astype(F32).reshape(kv_heads, groups, 1) * LOG2E
        args.append(jnp.broadcast_to(sink2, (kv_heads, groups, tq)).reshape(kv_heads, 1, n))
    kern = functools.partial(
        _flash_kernel, groups=groups, tq=tq, tk=tk, dv=dv, n_kv=n_kv, chain=n_qb > 1, has_qk2=q2 is not None,
        has_sink=sink is not None, key_chunk=KEY_CHUNK)
    return pl.pallas_call(
        kern, grid=(kv_heads, n_q // tq, n_kv), in_specs=in_specs,
        out_specs=pl.BlockSpec((tq, groups * dv), lambda h, i, j: (i, h)),
        out_shape=jax.ShapeDtypeStruct((n_q, kv_heads * groups * dv), BF16),
        scratch_shapes=[pltpu.VMEM((1, n), F32), pltpu.VMEM((1, n), F32), pltpu.VMEM((dv, n), F32),
                        pltpu.VMEM((tk, n), F32), pltpu.VMEM((1, n), F32)],
        compiler_params=_params("parallel", "arbitrary", "arbitrary"),
        name="flash_dense",
    )(*args)


def _window_kernel(*refs, groups, tq, n_lat, has_sink):
    q_ref = refs[0]
    n_half = tq // WINDOW + 2
    k_refs = refs[1:2 + n_half]
    vt_refs = refs[2 + n_half:3 + 2 * n_half]
    pos = 3 + 2 * n_half
    if has_sink:
        sink_ref = refs[pos]
        pos += 1
    o_ref = refs[pos]
    i = pl.program_id(1)
    n = groups * tq
    dv = vt_refs[0].shape[0]
    q = q_ref[...].reshape(n, q_ref.shape[2])
    qpos = i * tq + lax.broadcasted_iota(jnp.int32, (1, n), 1) % tq
    scores = []
    m = sink_ref[0] if has_sink else jnp.full((1, n), NEG_INF, F32)
    for b, k_ref in enumerate(k_refs):
        s = lax.dot_general(k_ref[...], q, (((1,), (1,)), ((), ())), preferred_element_type=F32)
        if b > 0:
            kpos = i * tq + (b - 2) * WINDOW + lax.broadcasted_iota(jnp.int32, (WINDOW, 1), 0)
            valid = (jnp.abs(qpos - kpos) <= WINDOW) & (kpos >= 0) & (kpos < n_lat)
            s = jnp.where(valid, s, NEG_INF)
        scores.append(s)
        m = jnp.maximum(m, jnp.max(s, axis=0, keepdims=True))
    l = jnp.exp2(sink_ref[0] - m) if has_sink else jnp.zeros((1, n), F32)
    acc = jnp.zeros((dv, n), F32)
    for s, vt_ref in zip(scores, vt_refs):
        p = jnp.exp2(s - m)
        l = l + jnp.sum(p, axis=0, keepdims=True)
        acc = acc + _dot(vt_ref[...], p.astype(BF16))
    o_t = acc / l
    for g in range(groups):
        o_ref[:, g * dv:(g + 1) * dv] = o_t[:, g * tq:(g + 1) * tq].T.astype(o_ref.dtype)


def window_attention(q, k, vt, *, kv_heads, groups, n_lat, n_ctx, tq, sink=None, dv=HEAD_DIM):
    dk = q.shape[2]
    per = tq // WINDOW
    n_half_blocks = n_lat // WINDOW
    n = groups * tq
    ctx_blk = n_lat // n_ctx
    half = lambda b: (lambda i: jnp.clip(i * per + b - 1, 0, n_half_blocks - 1))
    halves = [half(b) for b in range(per + 2)]
    in_specs = [pl.BlockSpec((groups, tq, dk), lambda h, i: (h, i, 0)),
                pl.BlockSpec((n_ctx, dk), lambda h, i: (ctx_blk, h))]
    in_specs += [pl.BlockSpec((WINDOW, dk), lambda h, i, f=f: (f(i), h)) for f in halves]
    in_specs += [pl.BlockSpec((dv, n_ctx), lambda h, i: (h, ctx_blk))]
    in_specs += [pl.BlockSpec((dv, WINDOW), lambda h, i, f=f: (h, f(i))) for f in halves]
    args = [q] + [k] * (per + 3) + [vt] * (per + 3)
    if sink is not None:
        in_specs.append(pl.BlockSpec((1, 1, n), lambda h, i: (h, 0, 0)))
        sink2 = sink.astype(F32).reshape(kv_heads, groups, 1) * LOG2E
        args.append(jnp.broadcast_to(sink2, (kv_heads, groups, tq)).reshape(kv_heads, 1, n))
    return pl.pallas_call(
        functools.partial(_window_kernel, groups=groups, tq=tq, n_lat=n_lat, has_sink=sink is not None),
        grid=(kv_heads, n_lat // tq), in_specs=in_specs,
        out_specs=pl.BlockSpec((tq, groups * dv), lambda h, i: (i, h)),
        out_shape=jax.ShapeDtypeStruct((n_lat, kv_heads * groups * dv), BF16),
        compiler_params=_params("parallel", "parallel"), name="window_attention",
    )(*args)


def _outproj_kernel(*refs, lat_tiles, with_next):
    o1c_ref, o1l_ref, o2c_ref, o2l_ref, gate_ref, h_ref, mod_ref, w_ref = refs[:8]
    is_ctx = pl.program_id(0) >= lat_tiles
    w1 = o1l_ref.shape[1]
    d = h_ref.shape[1]
    gt = gate_ref[...].astype(F32)
    sg = gt * jax.nn.sigmoid(gt)
    o1 = jnp.where(is_ctx, o1c_ref[...].astype(F32), o1l_ref[...].astype(F32))
    o2 = jnp.where(is_ctx, o2c_ref[...].astype(F32), o2l_ref[...].astype(F32))
    t1 = (o1 * sg[:, :w1]).astype(BF16)
    t2 = (o2 * sg[:, w1:]).astype(BF16)
    y = _dot(t1, w_ref[0:w1, :]) + _dot(t2, w_ref[w1:, :])
    gm = jnp.where(is_ctx, mod_ref[1:2, 2 * d:3 * d], mod_ref[0:1, 2 * d:3 * d])
    h_new = h_ref[...] + gm * y
    if with_next:
        g_next_ref, mod_next_ref, out_ref, u_ref = refs[8:]
        u_ref[...] = _modulated_norm(h_new, g_next_ref[...], mod_next_ref, is_ctx).astype(u_ref.dtype)
    else:
        out_ref = refs[8]
    out_ref[...] = h_new


def out_proj(o1c, o1l, o2c, o2l, proj, h, mod, w, nxt):
    r, d = h.shape
    tm = ROW_TILE
    assert o1c.shape[0] == tm, "the context rows must be exactly one row tile"
    lt = o1l.shape[0] // tm
    w1, w2 = o1l.shape[1], o2l.shape[1]
    lat = lambda i: (jnp.minimum(i, lt - 1), 0)
    row = pl.BlockSpec((tm, d), lambda i: (i, 0))
    in_specs = [pl.BlockSpec((tm, w1), lambda i: (0, 0)), pl.BlockSpec((tm, w1), lat),
                pl.BlockSpec((tm, w2), lambda i: (0, 0)), pl.BlockSpec((tm, w2), lat),
                pl.BlockSpec((tm, w1 + w2), lambda i: (i, OFF_GATE // (w1 + w2))),
                row, pl.BlockSpec(mod.shape, lambda i: (0, 0)), pl.BlockSpec(w.shape, lambda i: (0, 0))]
    args = [o1c, o1l, o2c, o2l, proj, h, mod, w]
    if nxt is None:
        r = lt * tm
        out_specs, out_shape = row, jax.ShapeDtypeStruct((r, d), F32)
    else:
        in_specs += [pl.BlockSpec((1, d), lambda i: (0, 0)), pl.BlockSpec(nxt[1].shape, lambda i: (0, 0))]
        args += [nxt[0].reshape(1, d), nxt[1]]
        out_specs = [row, row]
        out_shape = [jax.ShapeDtypeStruct((r, d), F32), jax.ShapeDtypeStruct((r, d), BF16)]
    return pl.pallas_call(
        functools.partial(_outproj_kernel, lat_tiles=lt, with_next=nxt is not None), grid=(r // tm,),
        in_specs=in_specs, out_specs=out_specs, out_shape=out_shape,
        compiler_params=_params("parallel"), name="out_proj",
    )(*args)


def _conv3_kernel(x_ref, w_ref, b_ref, oc_ref, ol_ref, *, n_ctx):
    x = x_ref[...].astype(F32)
    r = x.shape[0]
    n_lat = r - n_ctx
    row = lax.broadcasted_iota(jnp.int32, (r, 1), 0)
    prev = jnp.where((row == 0) | (row == n_lat), 0.0, pltpu.roll(x, 1, 0))
    nxt = jnp.where((row == n_lat - 1) | (row == r - 1), 0.0, pltpu.roll(x, r - 1, 0))
    y = prev * w_ref[0:1, :] + x * w_ref[1:2, :] + nxt * w_ref[2:3, :] + b_ref[...]
    ol_ref[0] = y[:n_lat]
    oc_ref[0] = y[n_lat:]


def conv3(proj, col_off, w, b, n_ctx):
    r = proj.shape[0]
    cw = w.shape[1] // 3
    nb = cw // LANE
    cb = col_off // LANE
    return pl.pallas_call(
        functools.partial(_conv3_kernel, n_ctx=n_ctx), grid=(3, nb),
        in_specs=[pl.BlockSpec((r, LANE), lambda p, c: (0, cb + p * nb + c)),
                  pl.BlockSpec((3, LANE), lambda p, c: (0, p * nb + c)),
                  pl.BlockSpec((1, LANE), lambda p, c: (0, p * nb + c))],
        out_specs=[pl.BlockSpec((1, n_ctx, LANE), lambda p, c: (p, 0, c)),
                   pl.BlockSpec((1, r - n_ctx, LANE), lambda p, c: (p, 0, c))],
        out_shape=[jax.ShapeDtypeStruct((3, n_ctx, cw), F32), jax.ShapeDtypeStruct((3, r - n_ctx, cw), F32)],
        compiler_params=_params("parallel", "parallel"), name="conv3",
    )(proj, w, b.reshape(1, -1))


def _filt_kernel(z_ref, w1_ref, b1_ref, f1_ref, w2_ref, b2_ref, f2_ref, w3_ref, dl_ref, h_ref, tap_ref, nrm_ref, *,
                 tm, cw):
    i = pl.program_id(0)
    z = z_ref[...]

    def mm(a, w_r):
        return _dot3(*_split_bf16(a), *_split_bf16(w_r[...]))

    a = jnp.sin(f1_ref[...] * (mm(z, w1_ref) + b1_ref[...]))
    a = jnp.sin(f2_ref[...] * (mm(a, w2_ref) + b2_ref[...]))
    dec = jnp.exp(-z[:, 0:1] * dl_ref[...])
    h = _dot(a.astype(BF16), w3_ref[...]) * jnp.concatenate([dec] * (h_ref.shape[1] // cw), axis=1)
    h_ref[...] = h.astype(h_ref.dtype)
    row = i * tm + lax.broadcasted_iota(jnp.int32, (tm, 1), 0)
    col = lax.broadcasted_iota(jnp.int32, (1, h.shape[1]), 1)
    skip = (row == 0) & ((col // cw) % 2 == 1)
    part = jnp.sum(jnp.where(skip, 0.0, jnp.abs(h)), axis=0, keepdims=True)

    @pl.when(i == 0)
    def _():
        nrm_ref[...] = jnp.zeros(nrm_ref.shape, F32)
        tap_ref[...] = h[0:SUBLANES]

    nrm_ref[...] += jnp.broadcast_to(part, nrm_ref.shape)


def hyena_filter(z, w1p, b1, f1, w2, b2, f2, w3, deltas):
    l = z.shape[0]
    n = w3.shape[1]
    cw = deltas.shape[0]
    hid = w2.shape[0]
    tm = min(l, 256)
    full = lambda a: pl.BlockSpec(a.shape, lambda i: (0, 0))
    ins = [w1p, b1.reshape(1, hid), f1.reshape(1, hid), w2, b2.reshape(1, hid), f2.reshape(1, hid), w3,
           deltas.reshape(1, cw)]
    small = pl.BlockSpec((SUBLANES, n), lambda i: (0, 0))
    return pl.pallas_call(
        functools.partial(_filt_kernel, tm=tm, cw=cw), grid=(l // tm,),
        in_specs=[pl.BlockSpec((tm, LANE), lambda i: (i, 0))] + [full(a) for a in ins],
        out_specs=[pl.BlockSpec((tm, n), lambda i: (i, 0)), small, small],
        out_shape=[jax.ShapeDtypeStruct((l, n), BF16), jax.ShapeDtypeStruct((SUBLANES, n), F32),
                   jax.ShapeDtypeStruct((SUBLANES, n), F32)],
        compiler_params=_params("arbitrary"), name="hyena_filter",
    )(z, *ins)


@functools.lru_cache(maxsize=None)
def _dft_consts(a):
    n = a * a
    hh = a // 2
    n1 = np.arange(hh)
    k1 = np.arange(hh)
    n2 = np.arange(a)
    k2 = np.arange(a)
    ang1 = 2 * np.pi * np.outer(k1 + 0.5, n1) / a
    m1 = np.zeros((hh, 2, hh))
    m1[:, 0], m1[:, 1] = np.cos(ang1), -np.sin(ang1)
    m1 = m1.reshape(2 * hh, hh)
    phi = 2 * np.pi * (n2[None, None, :] * (k1[:, None, None] + 0.5) / n + n2[None, None, :] * k2[None, :, None] / a)
    c, s = np.cos(phi), np.sin(phi)
    g = np.zeros((hh, 2, a, 2, a))
    g[:, 0, :, 0, :], g[:, 0, :, 1, :], g[:, 1, :, 0, :], g[:, 1, :, 1, :] = c, s, -s, c
    g = g.reshape(hh, 2 * a, 2 * a)
    gt = np.ascontiguousarray(np.transpose(g, (0, 2, 1)))
    al = 2 * np.pi * np.outer(n1, k1 + 0.5) / a
    mb = np.zeros((hh, hh, 2))
    mb[:, :, 0], mb[:, :, 1] = 2 / n * np.cos(al), -2 / n * np.sin(al)
    mb = mb.reshape(hh, 2 * hh)
    eye = np.eye(ROWS_BF16)
    return tuple(x.astype(np.float32) for x in (np.kron(m1, eye), g, gt, np.kron(mb, eye)))


@functools.lru_cache(maxsize=None)
def _dense_dft_consts(l):
    n = 2 * l
    ang = 2 * np.pi * np.outer(np.arange(l) + 0.5, np.arange(l)) / n
    mf = np.concatenate([np.cos(ang), -np.sin(ang)], axis=0)
    mi = np.concatenate([2 / n * np.cos(ang.T), -2 / n * np.sin(ang.T)], axis=1)
    return tuple(x.astype(np.float32) for x in (mf, mi))


def _stage1_kernel(x_ref, a_ref, o_ref, a16_ref):
    @pl.when((pl.program_id(0) == 0) & (pl.program_id(1) == 0))
    def _():
        a16_ref[...] = a_ref[...].astype(BF16)

    _, hh, t, ct = x_ref.shape
    x = x_ref[0].reshape(hh * t, ct).astype(BF16)
    o_ref[...] = _dot(a16_ref[...], x).astype(o_ref.dtype).reshape(o_ref.shape)


def dft_stage1(x4, p, a1):
    _, hh, a, c = x4.shape
    ct = _pick_tile(c, 512)
    t = ROWS_BF16
    return pl.pallas_call(
        _stage1_kernel, grid=(a // t, c // ct),
        in_specs=[pl.BlockSpec((1, hh, t, ct), lambda j, q: (p, 0, j, q)),
                  pl.BlockSpec(a1.shape, lambda j, q: (0, 0))],
        out_specs=pl.BlockSpec((2 * hh, t, ct), lambda j, q: (0, j, q)),
        out_shape=jax.ShapeDtypeStruct((2 * hh, a, c), BF16),
        scratch_shapes=[pltpu.VMEM(a1.shape, BF16)],
        compiler_params=_params("arbitrary", "arbitrary"), name="dft_stage1",
    )(x4, a1)


def _spectrum_product(x, h0, h1, tap0, n0, n1):
    half = x.shape[0] // 2
    inv = 1.0 / (n0 + n1)
    kr = (h0[:half] + h1[:half] - tap0) * inv
    ki = (h0[half:] - h1[half:]) * inv
    xr, xi = x[:half], x[half:]
    return jnp.concatenate([xr * kr - xi * ki, xr * ki + xi * kr], axis=0)


K1_PER_STEP = 4


def _mid_filter_kernel(y0_ref, y1_ref, g_ref, tap_ref, n0_ref, n1_ref, o_ref):
    inv = 1.0 / (n0_ref[0:1, :] + n1_ref[0:1, :])
    tap = tap_ref[0:1, :]
    for b in range(y0_ref.shape[0]):
        g = g_ref[b].astype(BF16)
        x0 = _dot(g, y0_ref[b])
        x1 = _dot(g, y1_ref[b])
        half = x0.shape[0] // 2
        kr = (x0[:half] + x1[:half] - tap) * inv
        ki = (x0[half:] - x1[half:]) * inv
        o_ref[b] = jnp.concatenate([kr, ki], axis=0).astype(o_ref.dtype)


def filter_spectrum(h, taps, nrm, consts, a):
    a1, g, _, _ = consts
    l, n = h.shape
    c = n // 4
    hh = a // 2
    y = dft_stage1(h.reshape(1, hh, a, n), 0, a1).reshape(hh, 2 * a, n)
    kb = K1_PER_STEP
    slab = lambda side: pl.BlockSpec((kb, 2 * a, c), lambda k, o: (k, 0, 2 * o + side))
    row = lambda side: pl.BlockSpec((SUBLANES, c), lambda k, o: (0, 2 * o + side))
    return pl.pallas_call(
        _mid_filter_kernel, grid=(hh // kb, 2),
        in_specs=[slab(0), slab(1), pl.BlockSpec((kb, 2 * a, 2 * a), lambda k, o: (k, 0, 0)), row(1), row(0), row(1)],
        out_specs=pl.BlockSpec((kb, 2 * a, c), lambda k, o: (k, 0, o)),
        out_shape=jax.ShapeDtypeStruct((hh, 2 * a, 2 * c), BF16),
        compiler_params=_params("parallel", "parallel"), name="dft_mid_filter",
    )(y, y, g, taps, nrm, nrm)


def _mid_conv_kernel(y_ref, g_ref, gt_ref, kf_ref, o_ref):
    for b in range(y_ref.shape[0]):
        x = _dot(g_ref[b].astype(BF16), y_ref[b])
        kf = kf_ref[b].astype(F32)
        half = x.shape[0] // 2
        xr, xi, kr, ki = x[:half], x[half:], kf[:half], kf[half:]
        z = jnp.concatenate([xr * kr - xi * ki, xr * ki + xi * kr], axis=0).astype(BF16)
        o_ref[b] = _dot(gt_ref[b].astype(BF16), z).astype(o_ref.dtype)


def dft_mid_conv(y, g, gt, kf, order):
    hh, a2, c = y.shape
    kb = K1_PER_STEP
    slab = pl.BlockSpec((kb, a2, c), lambda k: (k, 0, 0))
    mat = pl.BlockSpec((kb, a2, a2), lambda k: (k, 0, 0))
    return pl.pallas_call(
        _mid_conv_kernel, grid=(hh // kb,),
        in_specs=[slab, mat, mat, pl.BlockSpec((kb, a2, c), lambda k: (k, 0, order))],
        out_specs=slab, out_shape=jax.ShapeDtypeStruct((hh, a2, c), BF16),
        compiler_params=_params("parallel"), name="dft_mid_conv",
    )(y, g, gt, kf)


def _last_kernel(b_ref, a_ref, xa_ref, zb_ref, bias_ref, o_ref, a16_ref):
    @pl.when((pl.program_id(0) == 0) & (pl.program_id(1) == 0))
    def _():
        a16_ref[...] = a_ref[...].astype(BF16)

    rows, t, ct = b_ref.shape
    conv = _dot(a16_ref[...], b_ref[...].reshape(rows * t, ct)).reshape(rows // 2, t, ct)
    o_ref[0] = (xa_ref[0] * (conv + bias_ref[...] * zb_ref[0])).astype(o_ref.dtype)


def dft_last(b, a2m, xa4, pa, zb4, pb, bias_row):
    rows, a, c = b.shape
    hh = rows // 2
    ct = _pick_tile(c, 512)
    t = ROWS_BF16
    sig = lambda pp: pl.BlockSpec((1, hh, t, ct), lambda j, q: (pp, 0, j, q))
    return pl.pallas_call(
        _last_kernel, grid=(a // t, c // ct),
        in_specs=[pl.BlockSpec((rows, t, ct), lambda j, q: (0, j, q)),
                  pl.BlockSpec(a2m.shape, lambda j, q: (0, 0)),
                  sig(pa), sig(pb), pl.BlockSpec((1, ct), lambda j, q: (0, q))],
        out_specs=sig(0), out_shape=jax.ShapeDtypeStruct((1, hh, a, c), F32),
        scratch_shapes=[pltpu.VMEM(a2m.shape, BF16)],
        compiler_params=_params("arbitrary", "arbitrary"), name="dft_last",
    )(b, a2m, xa4, zb4, bias_row)


def long_conv_gated(u3, kf, bias, consts, a):
    a1, g, gt, a2m = consts
    _, l, c = u3.shape
    hh = a // 2
    u4 = u3.reshape(3, hh, a, c)

    def conv_gate(src4, p_src, xa_p, order):
        y = dft_stage1(src4, p_src, a1).reshape(hh, 2 * a, c)
        bb = dft_mid_conv(y, g, gt, kf, order).reshape(2 * hh, a, c)
        return dft_last(bb, a2m, u4, xa_p, src4, p_src, bias[order].reshape(1, c))

    z2 = conv_gate(u4, 2, 0, 0)
    return conv_gate(z2, 0, 1, 1).reshape(l, c)


def _ctx_spec_kernel(h_ref, mf_ref, o_ref):
    o_ref[...] = _dot(mf_ref[...].astype(BF16), h_ref[...])


def ctx_filter_spectrum(h, mf):
    l, n = h.shape
    tn = _pick_tile(n, 1024)
    return pl.pallas_call(
        _ctx_spec_kernel, grid=(n // tn,),
        in_specs=[pl.BlockSpec((l, tn), lambda j: (0, j)), pl.BlockSpec(mf.shape, lambda j: (0, 0))],
        out_specs=pl.BlockSpec((2 * l, tn), lambda j: (0, j)),
        out_shape=jax.ShapeDtypeStruct((2 * l, n), F32),
        compiler_params=_params("parallel"), name="ctx_filter_spectrum",
    )(h, mf)


def _ctx_hyena_kernel(u_ref, mf_ref, mi_ref, hf0a, hf1a, hf0b, hf1b, tap0, tap1, n0a, n1a, n0b, n1b, bias_ref, o_ref):
    x1, x2, z = u_ref[0], u_ref[1], u_ref[2]

    mf, mi = mf_ref[...].astype(BF16), mi_ref[...].astype(BF16)

    def conv(sig, h0_ref, h1_ref, tap_ref, n0_ref, n1_ref):
        x = _dot(mf, sig.astype(BF16))
        zz = _spectrum_product(x, h0_ref[...], h1_ref[...], tap_ref[0:1, :], n0_ref[0:1, :], n1_ref[0:1, :])
        return _dot(mi, zz.astype(BF16))

    z2 = x1 * (conv(z, hf0a, hf1a, tap0, n0a, n1a) + bias_ref[0:1, :] * z)
    o_ref[...] = (x2 * (conv(z2, hf0b, hf1b, tap1, n0b, n1b) + bias_ref[1:2, :] * z2)).astype(o_ref.dtype)


def ctx_hyena(u3, hf, taps, nrm, bias, mf, mi):
    _, l, c = u3.shape
    tc = _pick_tile(c, 512)
    nb = c // tc
    const = lambda a: pl.BlockSpec(a.shape, lambda j: (0, 0))
    col = lambda rows, q: pl.BlockSpec((rows, tc), lambda j: (0, q * nb + j))
    mats = [mf, mi]
    return pl.pallas_call(
        _ctx_hyena_kernel, grid=(nb,),
        in_specs=[pl.BlockSpec((3, l, tc), lambda j: (0, 0, j))] + [const(m) for m in mats]
        + [col(2 * l, 0), col(2 * l, 1), col(2 * l, 2), col(2 * l, 3), col(8, 1), col(8, 3),
           col(8, 0), col(8, 1), col(8, 2), col(8, 3), pl.BlockSpec((2, tc), lambda j: (0, j))],
        out_specs=pl.BlockSpec((l, tc), lambda j: (0, j)),
        out_shape=jax.ShapeDtypeStruct((l, c), BF16),
        compiler_params=_params("parallel"), name="ctx_hyena",
    )(u3, *mats, hf, hf, hf, hf, taps, taps, nrm, nrm, nrm, nrm, bias)


def _axial_tables(n_tokens, n_rot):
    rows = n_tokens // GRID_W
    row = jnp.broadcast_to(jnp.arange(rows)[:, None], (rows, GRID_W)).reshape(-1).astype(F32)
    col = jnp.broadcast_to(jnp.arange(GRID_W)[None, :], (rows, GRID_W)).reshape(-1).astype(F32)
    n_freq = n_rot // 4
    inv = ROPE_THETA ** (-jnp.arange(n_freq, dtype=F32) / n_freq)
    ang = jnp.concatenate([row[:, None] * inv, col[:, None] * inv], axis=-1)
    return jnp.cos(ang), jnp.sin(ang)


def _rope_tables_full(s, n_ctx):
    cos, sin = _axial_tables(s, HEAD_DIM)
    t0 = jnp.concatenate([jnp.concatenate([cos, cos], axis=1), jnp.ones((n_ctx, LANE), F32)], axis=0)
    t1 = jnp.concatenate([jnp.concatenate([-sin, sin], axis=1), jnp.zeros((n_ctx, LANE), F32)], axis=0)
    return t0, t1


def _rope_tables_half(s, n_ctx):
    cos, sin = _axial_tables(s, ROPE_DIM)
    q = ROPE_DIM // 2
    zq = jnp.zeros((s, q), F32)
    z2 = jnp.zeros((s, LANE - ROPE_DIM), F32)
    t0 = jnp.concatenate([cos, cos, z2], axis=1)
    t1 = jnp.concatenate([-sin, zq, z2], axis=1)
    t2 = jnp.concatenate([zq, sin, z2], axis=1)
    c0 = jnp.concatenate([jnp.ones((n_ctx, ROPE_DIM), F32), jnp.zeros((n_ctx, LANE - ROPE_DIM), F32)], axis=1)
    cz = jnp.zeros((n_ctx, LANE), F32)
    return jnp.concatenate([t0, c0], axis=0), jnp.concatenate([t1, cz], axis=0), jnp.concatenate([t2, cz], axis=0)


def _filter_features(l):
    pos = jnp.arange(l, dtype=F32)
    t = pos / max(l - 1, 1)
    bands = jnp.linspace(1e-4, HYENA_BANDS - 1, HYENA_BANDS, dtype=F32)
    ang = (2.0 * math.pi / l) * pos[:, None] * bands[None, :]
    z = jnp.concatenate([t[:, None], jnp.cos(ang), -jnp.sin(ang)], axis=-1)
    return jnp.pad(z, ((0, 0), (0, LANE - HYENA_EMB)))


Q_STACK = 1024
KV_TILE_CAP = 1408


def _attention_pair(qp, kp, vt, s, n_ctx, *, kv_heads, groups, window, sink=None, q2=None, k2=None):
    common = dict(kv_heads=kv_heads, groups=groups, q2=q2, k2=k2, sink=sink)
    o_ctx = flash(qp, kp, vt, n_q=n_ctx, q_row0=s, kv_row0=s, n_kv_rows=n_ctx, tq=n_ctx, tk=n_ctx, **common)
    if window:
        o_lat = window_attention(qp, kp, vt, kv_heads=kv_heads, groups=groups, n_lat=s, n_ctx=n_ctx, tq=ROW_TILE,
                                 sink=sink)
    else:
        o_lat = flash(qp, kp, vt, n_q=s, q_row0=0, kv_row0=0, n_kv_rows=n_ctx + s, tq=Q_STACK // groups,
                      tk=_pick_tile(n_ctx + s, KV_TILE_CAP), **common)
    return o_ctx, o_lat


def _even_layer(u, s, n_ctx, w_in, qn_g, kn_g, sink, conv_w, conv_b, fparams, hy_bias, rope_full, feats, dft):
    w_all, e = w_in
    src_gate = ATT_WIDTH + 2 * KV_WIDTH + 3 * B_WIDTH
    proj = matmul_permuted(u, w_all, e, [(src_gate, BRANCH), (0, ATT_WIDTH), (ATT_WIDTH, 2 * KV_WIDTH),
                                         (ATT_WIDTH + 2 * KV_WIDTH, 3 * B_WIDTH)], F32)
    qp = head_prep(proj, OFF_Q, A_HEADS, qn_g, rope_full, "half64", HEAD_DIM, HEAD_DIM ** -0.5 * LOG2E, True)
    kp = head_prep(proj, OFF_K, A_KV_HEADS, kn_g, rope_full, "half64", HEAD_DIM, 1.0)
    vt = transpose_heads(proj, OFF_V, A_KV_HEADS)
    a_ctx, a_lat = _attention_pair(qp, kp, vt, s, n_ctx, kv_heads=A_KV_HEADS, groups=A_HEADS // A_KV_HEADS,
                                   window=True, sink=sink)
    w1, b1, f1, w2, b2, f2, w3 = fparams
    w1p = jnp.pad(w1, ((0, LANE - HYENA_EMB), (0, 0)))
    w3 = w3.astype(BF16)
    deltas = jnp.linspace(DECAY_MAX, DECAY_MIN, B_WIDTH, dtype=F32)
    uc3, ul3 = conv3(proj, OFF_REST, conv_w, conv_b, n_ctx)
    a = int(round(math.sqrt(2 * s)))
    consts, (mf, mi) = dft
    h_lat, tap_lat, nrm_lat = hyena_filter(feats[0], w1p, b1, f1, w2, b2, f2, w3, deltas)
    kf_lat = filter_spectrum(h_lat, tap_lat, nrm_lat, consts, a)
    b_lat = long_conv_gated(ul3, kf_lat, hy_bias, consts, a)
    h_ctx, tap_ctx, nrm_ctx = hyena_filter(feats[1], w1p, b1, f1, w2, b2, f2, w3, deltas)
    hf_ctx = ctx_filter_spectrum(h_ctx, mf)
    b_ctx = ctx_hyena(uc3, hf_ctx, tap_ctx, nrm_ctx, hy_bias, mf, mi)
    return proj, a_ctx, a_lat, b_ctx, b_lat


def _odd_layer(u, s, n_ctx, w_in, qn_g, kn_g, cq_g, ckv_g, wuq, wukv, mq_g, mk_g, rope_full, rope_half):
    d = w_in.shape[0]
    parts = jnp.split(w_in, np.cumsum([ATT_WIDTH, KV_WIDTH, KV_WIDTH, Q_LORA, KV_LORA, ROPE_DIM]).tolist(), axis=1)
    wq, wk, wv, wmq, wmkv, wmkr, wg = parts
    w_perm = jnp.concatenate([wg, wq, wk, wv, wmq, wmkv, wmkr, jnp.zeros((d, LANE - ROPE_DIM), w_in.dtype)], axis=1)
    proj = matmul(u, w_perm.astype(BF16), F32)
    qp = head_prep(proj, OFF_Q, C_HEADS, qn_g, rope_full, "half64", HEAD_DIM, HEAD_DIM ** -0.5 * LOG2E, True)
    kp = head_prep(proj, OFF_K, C_KV_HEADS, kn_g, rope_full, "half64", HEAD_DIM, 1.0)
    vt = transpose_heads(proj, OFF_V, C_KV_HEADS)
    c_ctx, c_lat = _attention_pair(qp, kp, vt, s, n_ctx, kv_heads=C_KV_HEADS, groups=C_HEADS // C_KV_HEADS,
                                   window=False)
    pad_r = LANE - ROPE_DIM
    wuq3 = wuq.reshape(Q_LORA, M_HEADS, QK_DIM)
    wuq_p = jnp.concatenate([wuq3[:, :, :NOPE_DIM].reshape(Q_LORA, -1),
                             jnp.pad(wuq3[:, :, NOPE_DIM:], ((0, 0), (0, 0), (0, pad_r))).reshape(Q_LORA, -1)], axis=1)
    wukv3 = wukv.reshape(KV_LORA, M_HEADS, NOPE_DIM + V_DIM)
    wukv_p = jnp.concatenate([wukv3[:, :, :NOPE_DIM].reshape(KV_LORA, -1),
                              wukv3[:, :, NOPE_DIM:].reshape(KV_LORA, -1)], axis=1)
    q_raw = norm_matmul(proj, OFF_MQ, cq_g, wuq_p.astype(BF16))
    kv_raw = norm_matmul(proj, OFF_MKV, ckv_g, wukv_p.astype(BF16))
    scale = QK_DIM ** -0.5 * LOG2E
    gq_r = jnp.pad(mq_g[NOPE_DIM:], (0, pad_r))
    gk_r = jnp.pad(mk_g[NOPE_DIM:], (0, pad_r))
    qn = head_prep(q_raw, 0, M_HEADS, mq_g[:NOPE_DIM], (), "none", NOPE_DIM, scale, True)
    qr = head_prep(q_raw, M_HEADS * NOPE_DIM, M_HEADS, gq_r, rope_half, "half32", ROPE_DIM, scale, True)
    kn = head_prep(kv_raw, 0, M_HEADS, mk_g[:NOPE_DIM], (), "none", NOPE_DIM, 1.0)
    kr = head_prep(proj, OFF_MKR, 1, gk_r, rope_half, "half32", ROPE_DIM, 1.0)
    vmt = transpose_heads(kv_raw, M_HEADS * NOPE_DIM, M_HEADS)
    d_ctx, d_lat = _attention_pair(qn, kn, vmt, s, n_ctx, kv_heads=M_HEADS, groups=1, window=False, q2=qr, k2=kr)
    return proj, c_ctx, c_lat, d_ctx, d_lat


def kernel(x, c, ctx, c_ctx, ada_w, ada_b, norm_g, w_out, ev_w_in, ev_qn_g, ev_kn_g, ev_sink, ev_conv_w, ev_conv_b, hy_w1, hy_b1, hy_f1, hy_w2, hy_b2, hy_f2, hy_w3, hy_bias, od_w_in, od_qn_g, od_kn_g, od_cq_g, od_ckv_g, od_wuq, od_wukv, od_mq_g, od_mk_g):
    _, s, d = x.shape
    n_ctx = ctx.shape[1]
    depth = ada_w.shape[0]
    h = jnp.concatenate([x[0], ctx[0]], axis=0)
    cc = jnp.zeros((8, d), F32).at[0].set(c[0]).at[1].set(c_ctx)
    mod = ada_mod(cc, ada_w, ada_b)
    rope_full = _rope_tables_full(s, n_ctx)
    rope_half = _rope_tables_half(s, n_ctx)
    feats = (_filter_features(s), _filter_features(n_ctx))
    as_arrays = lambda mats: tuple(jnp.asarray(m) for m in mats)
    dft = (as_arrays(_dft_consts(int(round(math.sqrt(2 * s))))), as_arrays(_dense_dft_consts(n_ctx)))
    u = norm_mod(h, norm_g[0], mod[0], n_ctx)
    for i in range(depth):
        if i % 2 == 0:
            e = i // 2
            proj, o1c, o1l, o2c, o2l = _even_layer(
                u, s, n_ctx, (ev_w_in, e), ev_qn_g[e], ev_kn_g[e], ev_sink[e], ev_conv_w[e], ev_conv_b[e],
                (hy_w1[e], hy_b1[e], hy_f1[e], hy_w2[e], hy_b2[e], hy_f2[e], hy_w3[e]), hy_bias[e], rope_full, feats,
                dft)
        else:
            o = i // 2
            proj, o1c, o1l, o2c, o2l = _odd_layer(
                u, s, n_ctx, od_w_in[o], od_qn_g[o], od_kn_g[o], od_cq_g[o], od_ckv_g[o], od_wuq[o], od_wukv[o],
                od_mq_g[o], od_mk_g[o], rope_full, rope_half)
        w = w_out[i].astype(BF16)
        if i < depth - 1:
            h, u = out_proj(o1c, o1l, o2c, o2l, proj, h, mod[i], w, (norm_g[i + 1], mod[i + 1]))
        else:
            h = out_proj(o1c, o1l, o2c, o2l, proj, h, mod[i], w, None)
    return h[None]
```

```python
import functools
import math

import numpy as np
import jax
import jax.numpy as jnp
from jax import lax
from jax.experimental import pallas as pl
from jax.experimental.pallas import tpu as pltpu

F32 = jnp.float32
BF16 = jnp.bfloat16

GRID_W = 64
HEAD_DIM = 128
ROPE_THETA = 10000.0
EPS = 1e-6
NEG_INF = -1e30
A_HEADS, A_KV_HEADS, WINDOW = 8, 2, 128
B_WIDTH = 1024
HYENA_BANDS = 16
HYENA_EMB = 1 + 2 * HYENA_BANDS
DECAY_TARGET = 1e-2
DECAY_MAX = abs(math.log(DECAY_TARGET)) / 0.3
DECAY_MIN = abs(math.log(DECAY_TARGET)) / 1.5
C_HEADS, C_KV_HEADS = 8, 2
M_HEADS, Q_LORA, KV_LORA, NOPE_DIM, ROPE_DIM, V_DIM = 8, 512, 256, 128, 64, 128
QK_DIM = NOPE_DIM + ROPE_DIM
ATT_WIDTH = A_HEADS * HEAD_DIM
KV_WIDTH = A_KV_HEADS * HEAD_DIM
BRANCH = ATT_WIDTH + B_WIDTH

LOG2E = math.log2(math.e)
LANE = 128
SUBLANES = 8
ROWS_BF16 = 16
ROW_TILE = 256
VMEM_LIMIT = 48 * 1024 * 1024

OFF_GATE, OFF_Q, OFF_K, OFF_V, OFF_REST = 0, BRANCH, BRANCH + ATT_WIDTH, BRANCH + ATT_WIDTH + KV_WIDTH, BRANCH + ATT_WIDTH + 2 * KV_WIDTH


def _params(*sem):
    return pltpu.CompilerParams(dimension_semantics=sem, vmem_limit_bytes=VMEM_LIMIT)


def _split_bf16(x):
    hi = x.astype(BF16)
    return hi, (x - hi.astype(F32)).astype(BF16)


def _dot(a, b):
    return jnp.dot(a, b, preferred_element_type=F32)


def _dot3(a_hi, a_lo, b_hi, b_lo):
    return _dot(a_hi, b_hi) + (_dot(a_hi, b_lo) + _dot(a_lo, b_hi))


def _ada_kernel(c_ref, w_ref, b_ref, o_ref):
    c = c_ref[...]
    a = (c * jax.nn.sigmoid(c)).astype(BF16)
    o_ref[0] = _dot(a, w_ref[0].astype(BF16)) + b_ref[0]


def ada_mod(cc, ada_w, ada_b):
    depth, d, n = ada_w.shape
    tn = _pick_tile(n, 1024)
    return pl.pallas_call(
        _ada_kernel, grid=(depth, n // tn),
        in_specs=[pl.BlockSpec((8, d), lambda l, j: (0, 0)),
                  pl.BlockSpec((1, d, tn), lambda l, j: (l, 0, j)),
                  pl.BlockSpec((1, 1, tn), lambda l, j: (l, 0, j))],
        out_specs=pl.BlockSpec((1, 8, tn), lambda l, j: (l, 0, j)),
        out_shape=jax.ShapeDtypeStruct((depth, 8, n), F32),
        compiler_params=_params("parallel", "parallel"), name="ada_mod",
    )(cc, ada_w, ada_b.reshape(depth, 1, n))


def _modulated_norm(x, g, mod_ref, is_ctx):
    d = x.shape[1]
    y = x * lax.rsqrt(jnp.mean(x * x, axis=-1, keepdims=True) + EPS) * g
    sh = jnp.where(is_ctx, mod_ref[1:2, 0:d], mod_ref[0:1, 0:d])
    sc = jnp.where(is_ctx, mod_ref[1:2, d:2 * d], mod_ref[0:1, d:2 * d])
    return y * (1.0 + sc) + sh


def _norm_kernel(x_ref, g_ref, mod_ref, o_ref, *, lat_tiles):
    is_ctx = pl.program_id(0) >= lat_tiles
    o_ref[...] = _modulated_norm(x_ref[...], g_ref[...], mod_ref, is_ctx).astype(o_ref.dtype)


def norm_mod(h, g, mod, n_ctx):
    r, d = h.shape
    tm = ROW_TILE
    return pl.pallas_call(
        functools.partial(_norm_kernel, lat_tiles=(r - n_ctx) // tm), grid=(r // tm,),
        in_specs=[pl.BlockSpec((tm, d), lambda i: (i, 0)),
                  pl.BlockSpec((1, d), lambda i: (0, 0)),
                  pl.BlockSpec(mod.shape, lambda i: (0, 0))],
        out_specs=pl.BlockSpec((tm, d), lambda i: (i, 0)),
        out_shape=jax.ShapeDtypeStruct((r, d), BF16),
        compiler_params=_params("parallel"), name="norm_mod",
    )(h, g.reshape(1, d), mod)


def _pick_tile(n, cap, unit=LANE):
    best = unit
    for t in range(unit, cap + 1, unit):
        if n % t == 0:
            best = t
    return best


def _mm_cast_kernel(a_ref, b_ref, o_ref):
    o_ref[...] = _dot(a_ref[...], b_ref[0].astype(a_ref.dtype)).astype(o_ref.dtype)


def matmul_permuted(a, w3, layer, segments, out_dtype):
    m, k = a.shape
    n = sum(width for _, width in segments)
    tm = _pick_tile(m, 1536, 8)
    tn = LANE
    for t in range(LANE, 1024 + 1, LANE):
        if all(start % t == 0 and width % t == 0 for start, width in segments):
            tn = t
    src_tiles = [(start + off) // tn for start, width in segments for off in range(0, width, tn)]

    def src(j):
        idx = src_tiles[0]
        for t, s_t in enumerate(src_tiles[1:], 1):
            idx = jnp.where(j == t, s_t, idx)
        return idx
    return pl.pallas_call(
        _mm_cast_kernel, grid=(m // tm, n // tn),
        in_specs=[pl.BlockSpec((tm, k), lambda i, j: (i, 0)),
                  pl.BlockSpec((1, k, tn), lambda i, j: (layer, 0, src(j)))],
        out_specs=pl.BlockSpec((tm, tn), lambda i, j: (i, j)),
        out_shape=jax.ShapeDtypeStruct((m, n), out_dtype),
        compiler_params=_params("parallel", "parallel"), name="in_proj",
    )(a, w3)


def _normmm_kernel(x_ref, g_ref, w_ref, o_ref):
    x = x_ref[...].astype(F32)
    y = x * lax.rsqrt(jnp.mean(x * x, axis=-1, keepdims=True) + EPS) * g_ref[...]
    o_ref[...] = _dot(y.astype(BF16), w_ref[...]).astype(o_ref.dtype)


def norm_matmul(x, col_off, g, w):
    r = x.shape[0]
    k, n = w.shape
    tm = _pick_tile(r, 768, 8)
    cb = col_off // k
    return pl.pallas_call(
        _normmm_kernel, grid=(r // tm,),
        in_specs=[pl.BlockSpec((tm, k), lambda i: (i, cb)),
                  pl.BlockSpec((1, k), lambda i: (0, 0)),
                  pl.BlockSpec((k, n), lambda i: (0, 0))],
        out_specs=pl.BlockSpec((tm, n), lambda i: (i, 0)),
        out_shape=jax.ShapeDtypeStruct((r, n), F32),
        compiler_params=_params("parallel"), name="norm_matmul",
    )(x, g.reshape(1, k), w)


def _prep_kernel(*refs, mode, n_real, scale, n_heads, head_major):
    x_ref, g_ref = refs[0], refs[1]
    o_ref = refs[-1]
    g = g_ref[...]
    tabs = [t[...] for t in refs[2:-1]]
    for h in range(n_heads):
        x = x_ref[:, h * LANE:(h + 1) * LANE].astype(F32)
        if n_real < LANE:
            x = jnp.where(lax.broadcasted_iota(jnp.int32, (1, LANE), 1) < n_real, x, 0.0)
        ms = jnp.sum(x * x, axis=-1, keepdims=True) * (1.0 / n_real)
        xn = x * lax.rsqrt(ms + EPS) * g
        if mode == "none":
            y = xn
        elif mode == "half64":
            y = xn * tabs[0] + pltpu.roll(xn, 64, 1) * tabs[1]
        else:
            y = xn * tabs[0] + pltpu.roll(xn, 96, 1) * tabs[1] + pltpu.roll(xn, 32, 1) * tabs[2]
        y = (y * scale).astype(o_ref.dtype)
        if head_major:
            o_ref[h] = y
        else:
            o_ref[:, h * LANE:(h + 1) * LANE] = y


def head_prep(x, col_off, n_heads, g, tables, mode, n_real, scale, head_major=False):
    r = x.shape[0]
    w = n_heads * LANE
    tm = _pick_tile(r, 768, 8)
    tab_spec = pl.BlockSpec((tm, LANE), lambda i: (i, 0))
    if head_major:
        out_spec = pl.BlockSpec((n_heads, tm, LANE), lambda i: (0, i, 0))
        out_shape = jax.ShapeDtypeStruct((n_heads, r, LANE), BF16)
    else:
        out_spec = pl.BlockSpec((tm, w), lambda i: (i, 0))
        out_shape = jax.ShapeDtypeStruct((r, w), BF16)
    return pl.pallas_call(
        functools.partial(_prep_kernel, mode=mode, n_real=n_real, scale=scale, n_heads=n_heads,
                          head_major=head_major),
        grid=(r // tm,),
        in_specs=[pl.BlockSpec((tm, w), lambda i: (i, col_off // w)),
                  pl.BlockSpec((1, LANE), lambda i: (0, 0))] + [tab_spec] * len(tables),
        out_specs=out_spec, out_shape=out_shape,
        compiler_params=_params("parallel"), name="head_prep_" + mode,
    )(x, g.reshape(1, LANE), *tables)


def _vt_kernel(x_ref, o_ref):
    o_ref[...] = x_ref[...].astype(F32).T.astype(o_ref.dtype)


def transpose_heads(x, col_off, n_heads):
    r = x.shape[0]
    w = n_heads * LANE
    tm = _pick_tile(r, 768, LANE)
    return pl.pallas_call(
        _vt_kernel, grid=(r // tm,),
        in_specs=[pl.BlockSpec((tm, w), lambda i: (i, col_off // w))],
        out_specs=pl.BlockSpec((w, tm), lambda i: (0, i)),
        out_shape=jax.ShapeDtypeStruct((w, r), BF16),
        compiler_params=_params("parallel"), name="transpose_heads",
    )(x)


def _flash_kernel(*refs, groups, tq, tk, dv, n_kv, chain, has_qk2, has_sink, key_chunk):
    q_ref, qn_ref, k_ref, kn_ref, vt_ref = refs[0:5]
    pos = 5
    q2_ref = q2n_ref = k2_ref = k2n_ref = None
    if has_qk2:
        q2_ref, q2n_ref, k2_ref, k2n_ref = refs[5:9]
        pos = 9
    if has_sink:
        sink_ref = refs[pos]
        pos += 1
    o_ref, m_ref, l_ref, acc_ref, s_ref, mt_ref = refs[pos:pos + 6]
    i, j = pl.program_id(1), pl.program_id(2)
    n = groups * tq
    bounds = [(r0, min(r0 + key_chunk, tk)) for r0 in range(0, tk, key_chunk)]

    def stacked(qr, q2r):
        q = qr[...].reshape(n, qr.shape[2])
        if has_qk2:
            q = jnp.concatenate([q, q2r[...].reshape(n, LANE)], axis=1)
        return q

    def scores(kr, k2r, q, r0, r1):
        kk = kr[r0:r1, :]
        if has_qk2:
            kk = jnp.concatenate([kk, k2r[r0:r1, :]], axis=1)
        return lax.dot_general(kk, q, (((1,), (1,)), ((), ())), preferred_element_type=F32)

    @pl.when(j == 0)
    def _():
        m_ref[...] = jnp.full(m_ref.shape, NEG_INF, F32)
        l_ref[...] = jnp.zeros(l_ref.shape, F32)
        acc_ref[...] = jnp.zeros(acc_ref.shape, F32)

    @pl.when((j == 0) & (i == 0) if chain else j == 0)
    def _():
        q = stacked(q_ref, q2_ref)
        mt = jnp.full(mt_ref.shape, NEG_INF, F32)
        for r0, r1 in bounds:
            s = scores(k_ref, k2_ref, q, r0, r1)
            s_ref[r0:r1, :] = s
            mt = jnp.maximum(mt, jnp.max(s, axis=0, keepdims=True))
        mt_ref[...] = mt

    def step(kr, k2r, qr, q2r):
        prefetch = kr is not None
        if prefetch:
            q = stacked(qr, q2r)
        m_prev = m_ref[...]
        m_new = jnp.maximum(m_prev, mt_ref[...])
        alpha = jnp.exp2(m_prev - m_new)
        l_new = alpha * l_ref[...]
        acc_new = alpha * acc_ref[...]
        mt = jnp.full(mt_ref.shape, NEG_INF, F32)
        for r0, r1 in bounds:
            if prefetch:
                s_next = scores(kr, k2r, q, r0, r1)
            p = jnp.exp2(s_ref[r0:r1, :] - m_new)
            l_new = l_new + jnp.sum(p, axis=0, keepdims=True)
            acc_new = acc_new + _dot(vt_ref[:, r0:r1], p.astype(BF16))
            if prefetch:
                s_ref[r0:r1, :] = s_next
                mt = jnp.maximum(mt, jnp.max(s_next, axis=0, keepdims=True))
        l_ref[...] = l_new
        acc_ref[...] = acc_new
        m_ref[...] = m_new
        if prefetch:
            mt_ref[...] = mt

    if n_kv > 1:
        pl.when(j < n_kv - 1)(lambda: step(kn_ref, k2n_ref, q_ref, q2_ref))

    @pl.when(j == n_kv - 1)
    def _():
        if chain:
            step(k_ref, k2_ref, qn_ref, q2n_ref)
        else:
            step(None, None, None, None)
        m, l, acc = m_ref[...], l_ref[...], acc_ref[...]
        if has_sink:
            sk = sink_ref[0]
            m_fin = jnp.maximum(m, sk)
            a = jnp.exp2(m - m_fin)
            l = l * a + jnp.exp2(sk - m_fin)
            acc = acc * a
        o_t = acc / l
        for g in range(groups):
            o_ref[:, g * dv:(g + 1) * dv] = o_t[:, g * tq:(g + 1) * tq].T.astype(o_ref.dtype)


KEY_CHUNK = 256


def flash(q, k, vt, *, kv_heads, groups, n_q, q_row0, kv_row0, n_kv_rows, tq, tk, dv=HEAD_DIM,
          q2=None, k2=None, sink=None):
    dk = q.shape[2]
    qb0 = q_row0 // tq
    kb0 = kv_row0 // tk
    n_kv = n_kv_rows // tk

    def kv_row(i, j):
        return kb0 + j
    n = groups * tq
    n_qb = n_q // tq
    nxt = lambda j: jnp.minimum(j + 1, n_kv - 1)
    q_nxt = lambda i: qb0 + jnp.minimum(i + 1, n_qb - 1)
    in_specs = [pl.BlockSpec((groups, tq, dk), lambda h, i, j: (h, qb0 + i, 0)),
                pl.BlockSpec((groups, tq, dk), lambda h, i, j: (h, q_nxt(i), 0)),
                pl.BlockSpec((tk, dk), lambda h, i, j: (kv_row(i, 0), h)),
                pl.BlockSpec((tk, dk), lambda h, i, j: (kv_row(i, nxt(j)), h)),
                pl.BlockSpec((dv, tk), lambda h, i, j: (h, kv_row(i, j)))]
    args = [q, q, k, k, vt]
    if q2 is not None:
        in_specs += [pl.BlockSpec((groups, tq, LANE), lambda h, i, j: (h, qb0 + i, 0)),
                     pl.BlockSpec((groups, tq, LANE), lambda h, i, j: (h, q_nxt(i), 0)),
                     pl.BlockSpec((tk, LANE), lambda h, i, j: (kv_row(i, 0), 0)),
                     pl.BlockSpec((tk, LANE), lambda h, i, j: (kv_row(i, nxt(j)), 0))]
        args += [q2, q2, k2, k2]
    if sink is not None:
        in_specs.append(pl.BlockSpec((1, 1, n), lambda h, i, j: (h, 0, 0)))
        sink2 = sink.astype(F32).reshape(kv_heads, groups, 1) * LOG2E
        args.append(jnp.broadcast_to(sink2, (kv_heads, groups, tq)).reshape(kv_heads, 1, n))
    kern = functools.partial(
        _flash_kernel, groups=groups, tq=tq, tk=tk, dv=dv, n_kv=n_kv, chain=n_qb > 1, has_qk2=q2 is not None,
        has_sink=sink is not None, key_chunk=KEY_CHUNK)
    return pl.pallas_call(
        kern, grid=(kv_heads, n_q // tq, n_kv), in_specs=in_specs,
        out_specs=pl.BlockSpec((tq, groups * dv), lambda h, i, j: (i, h)),
        out_shape=jax.ShapeDtypeStruct((n_q, kv_heads * groups * dv), BF16),
        scratch_shapes=[pltpu.VMEM((1, n), F32), pltpu.VMEM((1, n), F32), pltpu.VMEM((dv, n), F32),
                        pltpu.VMEM((tk, n), F32), pltpu.VMEM((1, n), F32)],
        compiler_params=_params("parallel", "arbitrary", "arbitrary"),
        name="flash_dense",
    )(*args)


def _window_kernel(*refs, groups, tq, n_lat, has_sink):
    q_ref = refs[0]
    n_half = tq // WINDOW + 2
    k_refs = refs[1:2 + n_half]
    vt_refs = refs[2 + n_half:3 + 2 * n_half]
    pos = 3 + 2 * n_half
    if has_sink:
        sink_ref = refs[pos]
        pos += 1
    o_ref = refs[pos]
    i = pl.program_id(1)
    n = groups * tq
    dv = vt_refs[0].shape[0]
    q = q_ref[...].reshape(n, q_ref.shape[2])
    qpos = i * tq + lax.broadcasted_iota(jnp.int32, (1, n), 1) % tq
    scores = []
    m = sink_ref[0] if has_sink else jnp.full((1, n), NEG_INF, F32)
    for b, k_ref in enumerate(k_refs):
        s = lax.dot_general(k_ref[...], q, (((1,), (1,)), ((), ())), preferred_element_type=F32)
        if b > 0:
            kpos = i * tq + (b - 2) * WINDOW + lax.broadcasted_iota(jnp.int32, (WINDOW, 1), 0)
            valid = (jnp.abs(qpos - kpos) <= WINDOW) & (kpos >= 0) & (kpos < n_lat)
            s = jnp.where(valid, s, NEG_INF)
        scores.append(s)
        m = jnp.maximum(m, jnp.max(s, axis=0, keepdims=True))
    l = jnp.exp2(sink_ref[0] - m) if has_sink else jnp.zeros((1, n), F32)
    acc = jnp.zeros((dv, n), F32)
    for s, vt_ref in zip(scores, vt_refs):
        p = jnp.exp2(s - m)
        l = l + jnp.sum(p, axis=0, keepdims=True)
        acc = acc + _dot(vt_ref[...], p.astype(BF16))
    o_t = acc / l
    for g in range(groups):
        o_ref[:, g * dv:(g + 1) * dv] = o_t[:, g * tq:(g + 1) * tq].T.astype(o_ref.dtype)


def window_attention(q, k, vt, *, kv_heads, groups, n_lat, n_ctx, tq, sink=None, dv=HEAD_DIM):
    dk = q.shape[2]
    per = tq // WINDOW
    n_half_blocks = n_lat // WINDOW
    n = groups * tq
    ctx_blk = n_lat // n_ctx
    half = lambda b: (lambda i: jnp.clip(i * per + b - 1, 0, n_half_blocks - 1))
    halves = [half(b) for b in range(per + 2)]
    in_specs = [pl.BlockSpec((groups, tq, dk), lambda h, i: (h, i, 0)),
                pl.BlockSpec((n_ctx, dk), lambda h, i: (ctx_blk, h))]
    in_specs += [pl.BlockSpec((WINDOW, dk), lambda h, i, f=f: (f(i), h)) for f in halves]
    in_specs += [pl.BlockSpec((dv, n_ctx), lambda h, i: (h, ctx_blk))]
    in_specs += [pl.BlockSpec((dv, WINDOW), lambda h, i, f=f: (h, f(i))) for f in halves]
    args = [q] + [k] * (per + 3) + [vt] * (per + 3)
    if sink is not None:
        in_specs.append(pl.BlockSpec((1, 1, n), lambda h, i: (h, 0, 0)))
        sink2 = sink.astype(F32).reshape(kv_heads, groups, 1) * LOG2E
        args.append(jnp.broadcast_to(sink2, (kv_heads, groups, tq)).reshape(kv_heads, 1, n))
    return pl.pallas_call(
        functools.partial(_window_kernel, groups=groups, tq=tq, n_lat=n_lat, has_sink=sink is not None),
        grid=(kv_heads, n_lat // tq), in_specs=in_specs,
        out_specs=pl.BlockSpec((tq, groups * dv), lambda h, i: (i, h)),
        out_shape=jax.ShapeDtypeStruct((n_lat, kv_heads * groups * dv), BF16),
        compiler_params=_params("parallel", "parallel"), name="window_attention",
    )(*args)


def _outproj_kernel(*refs, lat_tiles, with_next):
    o1c_ref, o1l_ref, o2c_ref, o2l_ref, gate_ref, h_ref, mod_ref, w_ref = refs[:8]
    is_ctx = pl.program_id(0) >= lat_tiles
    w1 = o1l_ref.shape[1]
    d = h_ref.shape[1]
    gt = gate_ref[...].astype(F32)
    sg = gt * jax.nn.sigmoid(gt)
    o1 = jnp.where(is_ctx, o1c_ref[...].astype(F32), o1l_ref[...].astype(F32))
    o2 = jnp.where(is_ctx, o2c_ref[...].astype(F32), o2l_ref[...].astype(F32))
    t1 = (o1 * sg[:, :w1]).astype(BF16)
    t2 = (o2 * sg[:, w1:]).astype(BF16)
    y = _dot(t1, w_ref[0:w1, :]) + _dot(t2, w_ref[w1:, :])
    gm = jnp.where(is_ctx, mod_ref[1:2, 2 * d:3 * d], mod_ref[0:1, 2 * d:3 * d])
    h_new = h_ref[...] + gm * y
    if with_next:
        g_next_ref, mod_next_ref, out_ref, u_ref = refs[8:]
        u_ref[...] = _modulated_norm(h_new, g_next_ref[...], mod_next_ref, is_ctx).astype(u_ref.dtype)
    else:
        out_ref = refs[8]
    out_ref[...] = h_new


def out_proj(o1c, o1l, o2c, o2l, proj, h, mod, w, nxt):
    r, d = h.shape
    tm = ROW_TILE
    assert o1c.shape[0] == tm, "the context rows must be exactly one row tile"
    lt = o1l.shape[0] // tm
    w1, w2 = o1l.shape[1], o2l.shape[1]
    lat = lambda i: (jnp.minimum(i, lt - 1), 0)
    row = pl.BlockSpec((tm, d), lambda i: (i, 0))
    in_specs = [pl.BlockSpec((tm, w1), lambda i: (0, 0)), pl.BlockSpec((tm, w1), lat),
                pl.BlockSpec((tm, w2), lambda i: (0, 0)), pl.BlockSpec((tm, w2), lat),
                pl.BlockSpec((tm, w1 + w2), lambda i: (i, OFF_GATE // (w1 + w2))),
                row, pl.BlockSpec(mod.shape, lambda i: (0, 0)), pl.BlockSpec(w.shape, lambda i: (0, 0))]
    args = [o1c, o1l, o2c, o2l, proj, h, mod, w]
    if nxt is None:
        r = lt * tm
        out_specs, out_shape = row, jax.ShapeDtypeStruct((r, d), F32)
    else:
        in_specs += [pl.BlockSpec((1, d), lambda i: (0, 0)), pl.BlockSpec(nxt[1].shape, lambda i: (0, 0))]
        args += [nxt[0].reshape(1, d), nxt[1]]
        out_specs = [row, row]
        out_shape = [jax.ShapeDtypeStruct((r, d), F32), jax.ShapeDtypeStruct((r, d), BF16)]
    return pl.pallas_call(
        functools.partial(_outproj_kernel, lat_tiles=lt, with_next=nxt is not None), grid=(r // tm,),
        in_specs=in_specs, out_specs=out_specs, out_shape=out_shape,
        compiler_params=_params("parallel"), name="out_proj",
    )(*args)


def _conv3_kernel(x_ref, w_ref, b_ref, oc_ref, ol_ref, *, n_ctx):
    x = x_ref[...].astype(F32)
    r = x.shape[0]
    n_lat = r - n_ctx
    row = lax.broadcasted_iota(jnp.int32, (r, 1), 0)
    prev = jnp.where((row == 0) | (row == n_lat), 0.0, pltpu.roll(x, 1, 0))
    nxt = jnp.where((row == n_lat - 1) | (row == r - 1), 0.0, pltpu.roll(x, r - 1, 0))
    y = prev * w_ref[0:1, :] + x * w_ref[1:2, :] + nxt * w_ref[2:3, :] + b_ref[...]
    ol_ref[0] = y[:n_lat]
    oc_ref[0] = y[n_lat:]


def conv3(proj, col_off, w, b, n_ctx):
    r = proj.shape[0]
    cw = w.shape[1] // 3
    nb = cw // LANE
    cb = col_off // LANE
    return pl.pallas_call(
        functools.partial(_conv3_kernel, n_ctx=n_ctx), grid=(3, nb),
        in_specs=[pl.BlockSpec((r, LANE), lambda p, c: (0, cb + p * nb + c)),
                  pl.BlockSpec((3, LANE), lambda p, c: (0, p * nb + c)),
                  pl.BlockSpec((1, LANE), lambda p, c: (0, p * nb + c))],
        out_specs=[pl.BlockSpec((1, n_ctx, LANE), lambda p, c: (p, 0, c)),
                   pl.BlockSpec((1, r - n_ctx, LANE), lambda p, c: (p, 0, c))],
        out_shape=[jax.ShapeDtypeStruct((3, n_ctx, cw), F32), jax.ShapeDtypeStruct((3, r - n_ctx, cw), F32)],
        compiler_params=_params("parallel", "parallel"), name="conv3",
    )(proj, w, b.reshape(1, -1))


def _filt_kernel(z_ref, w1_ref, b1_ref, f1_ref, w2_ref, b2_ref, f2_ref, w3_ref, dl_ref, h_ref, tap_ref, nrm_ref, *,
                 tm, cw):
    i = pl.program_id(0)
    z = z_ref[...]

    def mm(a, w_r):
        return _dot3(*_split_bf16(a), *_split_bf16(w_r[...]))

    a = jnp.sin(f1_ref[...] * (mm(z, w1_ref) + b1_ref[...]))
    a = jnp.sin(f2_ref[...] * (mm(a, w2_ref) + b2_ref[...]))
    dec = jnp.exp(-z[:, 0:1] * dl_ref[...])
    h = _dot(a.astype(BF16), w3_ref[...]) * jnp.concatenate([dec] * (h_ref.shape[1] // cw), axis=1)
    h_ref[...] = h.astype(h_ref.dtype)
    row = i * tm + lax.broadcasted_iota(jnp.int32, (tm, 1), 0)
    col = lax.broadcasted_iota(jnp.int32, (1, h.shape[1]), 1)
    skip = (row == 0) & ((col // cw) % 2 == 1)
    part = jnp.sum(jnp.where(skip, 0.0, jnp.abs(h)), axis=0, keepdims=True)

    @pl.when(i == 0)
    def _():
        nrm_ref[...] = jnp.zeros(nrm_ref.shape, F32)
        tap_ref[...] = h[0:SUBLANES]

    nrm_ref[...] += jnp.broadcast_to(part, nrm_ref.shape)


def hyena_filter(z, w1p, b1, f1, w2, b2, f2, w3, deltas):
    l = z.shape[0]
    n = w3.shape[1]
    cw = deltas.shape[0]
    hid = w2.shape[0]
    tm = min(l, 256)
    full = lambda a: pl.BlockSpec(a.shape, lambda i: (0, 0))
    ins = [w1p, b1.reshape(1, hid), f1.reshape(1, hid), w2, b2.reshape(1, hid), f2.reshape(1, hid), w3,
           deltas.reshape(1, cw)]
    small = pl.BlockSpec((SUBLANES, n), lambda i: (0, 0))
    return pl.pallas_call(
        functools.partial(_filt_kernel, tm=tm, cw=cw), grid=(l // tm,),
        in_specs=[pl.BlockSpec((tm, LANE), lambda i: (i, 0))] + [full(a) for a in ins],
        out_specs=[pl.BlockSpec((tm, n), lambda i: (i, 0)), small, small],
        out_shape=[jax.ShapeDtypeStruct((l, n), BF16), jax.ShapeDtypeStruct((SUBLANES, n), F32),
                   jax.ShapeDtypeStruct((SUBLANES, n), F32)],
        compiler_params=_params("arbitrary"), name="hyena_filter",
    )(z, *ins)


@functools.lru_cache(maxsize=None)
def _dft_consts(a):
    n = a * a
    hh = a // 2
    n1 = np.arange(hh)
    k1 = np.arange(hh)
    n2 = np.arange(a)
    k2 = np.arange(a)
    ang1 = 2 * np.pi * np.outer(k1 + 0.5, n1) / a
    m1 = np.zeros((hh, 2, hh))
    m1[:, 0], m1[:, 1] = np.cos(ang1), -np.sin(ang1)
    m1 = m1.reshape(2 * hh, hh)
    phi = 2 * np.pi * (n2[None, None, :] * (k1[:, None, None] + 0.5) / n + n2[None, None, :] * k2[None, :, None] / a)
    c, s = np.cos(phi), np.sin(phi)
    g = np.zeros((hh, 2, a, 2, a))
    g[:, 0, :, 0, :], g[:, 0, :, 1, :], g[:, 1, :, 0, :], g[:, 1, :, 1, :] = c, s, -s, c
    g = g.reshape(hh, 2 * a, 2 * a)
    gt = np.ascontiguousarray(np.transpose(g, (0, 2, 1)))
    al = 2 * np.pi * np.outer(n1, k1 + 0.5) / a
    mb = np.zeros((hh, hh, 2))
    mb[:, :, 0], mb[:, :, 1] = 2 / n * np.cos(al), -2 / n * np.sin(al)
    mb = mb.reshape(hh, 2 * hh)
    eye = np.eye(ROWS_BF16)
    return tuple(x.astype(np.float32) for x in (np.kron(m1, eye), g, gt, np.kron(mb, eye)))


@functools.lru_cache(maxsize=None)
def _dense_dft_consts(l):
    n = 2 * l
    ang = 2 * np.pi * np.outer(np.arange(l) + 0.5, np.arange(l)) / n
    mf = np.concatenate([np.cos(ang), -np.sin(ang)], axis=0)
    mi = np.concatenate([2 / n * np.cos(ang.T), -2 / n * np.sin(ang.T)], axis=1)
    return tuple(x.astype(np.float32) for x in (mf, mi))


def _stage1_kernel(x_ref, a_ref, o_ref, a16_ref):
    @pl.when((pl.program_id(0) == 0) & (pl.program_id(1) == 0))
    def _():
        a16_ref[...] = a_ref[...].astype(BF16)

    _, hh, t, ct = x_ref.shape
    x = x_ref[0].reshape(hh * t, ct).astype(BF16)
    o_ref[...] = _dot(a16_ref[...], x).astype(o_ref.dtype).reshape(o_ref.shape)


def dft_stage1(x4, p, a1):
    _, hh, a, c = x4.shape
    ct = _pick_tile(c, 512)
    t = ROWS_BF16
    return pl.pallas_call(
        _stage1_kernel, grid=(a // t, c // ct),
        in_specs=[pl.BlockSpec((1, hh, t, ct), lambda j, q: (p, 0, j, q)),
                  pl.BlockSpec(a1.shape, lambda j, q: (0, 0))],
        out_specs=pl.BlockSpec((2 * hh, t, ct), lambda j, q: (0, j, q)),
        out_shape=jax.ShapeDtypeStruct((2 * hh, a, c), BF16),
        scratch_shapes=[pltpu.VMEM(a1.shape, BF16)],
        compiler_params=_params("arbitrary", "arbitrary"), name="dft_stage1",
    )(x4, a1)


def _spectrum_product(x, h0, h1, tap0, n0, n1):
    half = x.shape[0] // 2
    inv = 1.0 / (n0 + n1)
    kr = (h0[:half] + h1[:half] - tap0) * inv
    ki = (h0[half:] - h1[half:]) * inv
    xr, xi = x[:half], x[half:]
    return jnp.concatenate([xr * kr - xi * ki, xr * ki + xi * kr], axis=0)


K1_PER_STEP = 4


def _mid_filter_kernel(y0_ref, y1_ref, g_ref, tap_ref, n0_ref, n1_ref, o_ref):
    inv = 1.0 / (n0_ref[0:1, :] + n1_ref[0:1, :])
    tap = tap_ref[0:1, :]
    for b in range(y0_ref.shape[0]):
        g = g_ref[b].astype(BF16)
        x0 = _dot(g, y0_ref[b])
        x1 = _dot(g, y1_ref[b])
        half = x0.shape[0] // 2
        kr = (x0[:half] + x1[:half] - tap) * inv
        ki = (x0[half:] - x1[half:]) * inv
        o_ref[b] = jnp.concatenate([kr, ki], axis=0).astype(o_ref.dtype)


def filter_spectrum(h, taps, nrm, consts, a):
    a1, g, _, _ = consts
    l, n = h.shape
    c = n // 4
    hh = a // 2
    y = dft_stage1(h.reshape(1, hh, a, n), 0, a1).reshape(hh, 2 * a, n)
    kb = K1_PER_STEP
    slab = lambda side: pl.BlockSpec((kb, 2 * a, c), lambda k, o: (k, 0, 2 * o + side))
    row = lambda side: pl.BlockSpec((SUBLANES, c), lambda k, o: (0, 2 * o + side))
    return pl.pallas_call(
        _mid_filter_kernel, grid=(hh // kb, 2),
        in_specs=[slab(0), slab(1), pl.BlockSpec((kb, 2 * a, 2 * a), lambda k, o: (k, 0, 0)), row(1), row(0), row(1)],
        out_specs=pl.BlockSpec((kb, 2 * a, c), lambda k, o: (k, 0, o)),
        out_shape=jax.ShapeDtypeStruct((hh, 2 * a, 2 * c), BF16),
        compiler_params=_params("parallel", "parallel"), name="dft_mid_filter",
    )(y, y, g, taps, nrm, nrm)


def _mid_conv_kernel(y_ref, g_ref, gt_ref, kf_ref, o_ref):
    for b in range(y_ref.shape[0]):
        x = _dot(g_ref[b].astype(BF16), y_ref[b])
        kf = kf_ref[b].astype(F32)
        half = x.shape[0] // 2
        xr, xi, kr, ki = x[:half], x[half:], kf[:half], kf[half:]
        z = jnp.concatenate([xr * kr - xi * ki, xr * ki + xi * kr], axis=0).astype(BF16)
        o_ref[b] = _dot(gt_ref[b].astype(BF16), z).astype(o_ref.dtype)


def dft_mid_conv(y, g, gt, kf, order):
    hh, a2, c = y.shape
    kb = K1_PER_STEP
    slab = pl.BlockSpec((kb, a2, c), lambda k: (k, 0, 0))
    mat = pl.BlockSpec((kb, a2, a2), lambda k: (k, 0, 0))
    return pl.pallas_call(
        _mid_conv_kernel, grid=(hh // kb,),
        in_specs=[slab, mat, mat, pl.BlockSpec((kb, a2, c), lambda k: (k, 0, order))],
        out_specs=slab, out_shape=jax.ShapeDtypeStruct((hh, a2, c), BF16),
        compiler_params=_params("parallel"), name="dft_mid_conv",
    )(y, g, gt, kf)


def _last_kernel(b_ref, a_ref, xa_ref, zb_ref, bias_ref, o_ref, a16_ref):
    @pl.when((pl.program_id(0) == 0) & (pl.program_id(1) == 0))
    def _():
        a16_ref[...] = a_ref[...].astype(BF16)

    rows, t, ct = b_ref.shape
    conv = _dot(a16_ref[...], b_ref[...].reshape(rows * t, ct)).reshape(rows // 2, t, ct)
    o_ref[0] = (xa_ref[0] * (conv + bias_ref[...] * zb_ref[0])).astype(o_ref.dtype)


def dft_last(b, a2m, xa4, pa, zb4, pb, bias_row):
    rows, a, c = b.shape
    hh = rows // 2
    ct = _pick_tile(c, 512)
    t = ROWS_BF16
    sig = lambda pp: pl.BlockSpec((1, hh, t, ct), lambda j, q: (pp, 0, j, q))
    return pl.pallas_call(
        _last_kernel, grid=(a // t, c // ct),
        in_specs=[pl.BlockSpec((rows, t, ct), lambda j, q: (0, j, q)),
                  pl.BlockSpec(a2m.shape, lambda j, q: (0, 0)),
                  sig(pa), sig(pb), pl.BlockSpec((1, ct), lambda j, q: (0, q))],
        out_specs=sig(0), out_shape=jax.ShapeDtypeStruct((1, hh, a, c), F32),
        scratch_shapes=[pltpu.VMEM(a2m.shape, BF16)],
        compiler_params=_params("arbitrary", "arbitrary"), name="dft_last",
    )(b, a2m, xa4, zb4, bias_row)


def long_conv_gated(u3, kf, bias, consts, a):
    a1, g, gt, a2m = consts
    _, l, c = u3.shape
    hh = a // 2
    u4 = u3.reshape(3, hh, a, c)

    def conv_gate(src4, p_src, xa_p, order):
        y = dft_stage1(src4, p_src, a1).reshape(hh, 2 * a, c)
        bb = dft_mid_conv(y, g, gt, kf, order).reshape(2 * hh, a, c)
        return dft_last(bb, a2m, u4, xa_p, src4, p_src, bias[order].reshape(1, c))

    z2 = conv_gate(u4, 2, 0, 0)
    return conv_gate(z2, 0, 1, 1).reshape(l, c)


def _ctx_spec_kernel(h_ref, mf_ref, o_ref):
    o_ref[...] = _dot(mf_ref[...].astype(BF16), h_ref[...])


def ctx_filter_spectrum(h, mf):
    l, n = h.shape
    tn = _pick_tile(n, 1024)
    return pl.pallas_call(
        _ctx_spec_kernel, grid=(n // tn,),
        in_specs=[pl.BlockSpec((l, tn), lambda j: (0, j)), pl.BlockSpec(mf.shape, lambda j: (0, 0))],
        out_specs=pl.BlockSpec((2 * l, tn), lambda j: (0, j)),
        out_shape=jax.ShapeDtypeStruct((2 * l, n), F32),
        compiler_params=_params("parallel"), name="ctx_filter_spectrum",
    )(h, mf)


def _ctx_hyena_kernel(u_ref, mf_ref, mi_ref, hf0a, hf1a, hf0b, hf1b, tap0, tap1, n0a, n1a, n0b, n1b, bias_ref, o_ref):
    x1, x2, z = u_ref[0], u_ref[1], u_ref[2]

    mf, mi = mf_ref[...].astype(BF16), mi_ref[...].astype(BF16)

    def conv(sig, h0_ref, h1_ref, tap_ref, n0_ref, n1_ref):
        x = _dot(mf, sig.astype(BF16))
        zz = _spectrum_product(x, h0_ref[...], h1_ref[...], tap_ref[0:1, :], n0_ref[0:1, :], n1_ref[0:1, :])
        return _dot(mi, zz.astype(BF16))

    z2 = x1 * (conv(z, hf0a, hf1a, tap0, n0a, n1a) + bias_ref[0:1, :] * z)
    o_ref[...] = (x2 * (conv(z2, hf0b, hf1b, tap1, n0b, n1b) + bias_ref[1:2, :] * z2)).astype(o_ref.dtype)


def ctx_hyena(u3, hf, taps, nrm, bias, mf, mi):
    _, l, c = u3.shape
    tc = _pick_tile(c, 512)
    nb = c // tc
    const = lambda a: pl.BlockSpec(a.shape, lambda j: (0, 0))
    col = lambda rows, q: pl.BlockSpec((rows, tc), lambda j: (0, q * nb + j))
    mats = [mf, mi]
    return pl.pallas_call(
        _ctx_hyena_kernel, grid=(nb,),
        in_specs=[pl.BlockSpec((3, l, tc), lambda j: (0, 0, j))] + [const(m) for m in mats]
        + [col(2 * l, 0), col(2 * l, 1), col(2 * l, 2), col(2 * l, 3), col(8, 1), col(8, 3),
           col(8, 0), col(8, 1), col(8, 2), col(8, 3), pl.BlockSpec((2, tc), lambda j: (0, j))],
        out_specs=pl.BlockSpec((l, tc), lambda j: (0, j)),
        out_shape=jax.ShapeDtypeStruct((l, c), BF16),
        compiler_params=_params("parallel"), name="ctx_hyena",
    )(u3, *mats, hf, hf, hf, hf, taps, taps, nrm, nrm, nrm, nrm, bias)


def _axial_tables(n_tokens, n_rot):
    rows = n_tokens // GRID_W
    row = jnp.broadcast_to(jnp.arange(rows)[:, None], (rows, GRID_W)).reshape(-1).astype(F32)
    col = jnp.broadcast_to(jnp.arange(GRID_W)[None, :], (rows, GRID_W)).reshape(-1).astype(F32)
    n_freq = n_rot // 4
    inv = ROPE_THETA ** (-jnp.arange(n_freq, dtype=F32) / n_freq)
    ang = jnp.concatenate([row[:, None] * inv, col[:, None] * inv], axis=-1)
    return jnp.cos(ang), jnp.sin(ang)


def _rope_tables_full(s, n_ctx):
    cos, sin = _axial_tables(s, HEAD_DIM)
    t0 = jnp.concatenate([jnp.concatenate([cos, cos], axis=1), jnp.ones((n_ctx, LANE), F32)], axis=0)
    t1 = jnp.concatenate([jnp.concatenate([-sin, sin], axis=1), jnp.zeros((n_ctx, LANE), F32)], axis=0)
    return t0, t1


def _rope_tables_half(s, n_ctx):
    cos, sin = _axial_tables(s, ROPE_DIM)
    q = ROPE_DIM // 2
    zq = jnp.zeros((s, q), F32)
    z2 = jnp.zeros((s, LANE - ROPE_DIM), F32)
    t0 = jnp.concatenate([cos, cos, z2], axis=1)
    t1 = jnp.concatenate([-sin, zq, z2], axis=1)
    t2 = jnp.concatenate([zq, sin, z2], axis=1)
    c0 = jnp.concatenate([jnp.ones((n_ctx, ROPE_DIM), F32), jnp.zeros((n_ctx, LANE - ROPE_DIM), F32)], axis=1)
    cz = jnp.zeros((n_ctx, LANE), F32)
    return jnp.concatenate([t0, c0], axis=0), jnp.concatenate([t1, cz], axis=0), jnp.concatenate([t2, cz], axis=0)


def _filter_features(l):
    pos = jnp.arange(l, dtype=F32)
    t = pos / max(l - 1, 1)
    bands = jnp.linspace(1e-4, HYENA_BANDS - 1, HYENA_BANDS, dtype=F32)
    ang = (2.0 * math.pi / l) * pos[:, None] * bands[None, :]
    z = jnp.concatenate([t[:, None], jnp.cos(ang), -jnp.sin(ang)], axis=-1)
    return jnp.pad(z, ((0, 0), (0, LANE - HYENA_EMB)))


Q_STACK = 1024
KV_TILE_CAP = 1408


def _attention_pair(qp, kp, vt, s, n_ctx, *, kv_heads, groups, window, sink=None, q2=None, k2=None):
    common = dict(kv_heads=kv_heads, groups=groups, q2=q2, k2=k2, sink=sink)
    o_ctx = flash(qp, kp, vt, n_q=n_ctx, q_row0=s, kv_row0=s, n_kv_rows=n_ctx, tq=n_ctx, tk=n_ctx, **common)
    if window:
        o_lat = window_attention(qp, kp, vt, kv_heads=kv_heads, groups=groups, n_lat=s, n_ctx=n_ctx, tq=ROW_TILE,
                                 sink=sink)
    else:
        o_lat = flash(qp, kp, vt, n_q=s, q_row0=0, kv_row0=0, n_kv_rows=n_ctx + s, tq=Q_STACK // groups,
                      tk=_pick_tile(n_ctx + s, KV_TILE_CAP), **common)
    return o_ctx, o_lat


def _even_layer(u, s, n_ctx, w_in, qn_g, kn_g, sink, conv_w, conv_b, fparams, hy_bias, rope_full, feats, dft):
    w_all, e = w_in
    src_gate = ATT_WIDTH + 2 * KV_WIDTH + 3 * B_WIDTH
    proj = matmul_permuted(u, w_all, e, [(src_gate, BRANCH), (0, ATT_WIDTH), (ATT_WIDTH, 2 * KV_WIDTH),
                                         (ATT_WIDTH + 2 * KV_WIDTH, 3 * B_WIDTH)], F32)
    qp = head_prep(proj, OFF_Q, A_HEADS, qn_g, rope_full, "half64", HEAD_DIM, HEAD_DIM ** -0.5 * LOG2E, True)
    kp = head_prep(proj, OFF_K, A_KV_HEADS, kn_g, rope_full, "half64", HEAD_DIM, 1.0)
    vt = transpose_heads(proj, OFF_V, A_KV_HEADS)
    a_ctx, a_lat = _attention_pair(qp, kp, vt, s, n_ctx, kv_heads=A_KV_HEADS, groups=A_HEADS // A_KV_HEADS,
                                   window=True, sink=sink)
    w1, b1, f1, w2, b2, f2, w3 = fparams
    w1p = jnp.pad(w1, ((0, LANE - HYENA_EMB), (0, 0)))
    w3 = w3.astype(BF16)
    deltas = jnp.linspace(DECAY_MAX, DECAY_MIN, B_WIDTH, dtype=F32)
    uc3, ul3 = conv3(proj, OFF_REST, conv_w, conv_b, n_ctx)
    a = int(round(math.sqrt(2 * s)))
    consts, (mf, mi) = dft
    h_lat, tap_lat, nrm_lat = hyena_filter(feats[0], w1p, b1, f1, w2, b2, f2, w3, deltas)
    kf_lat = filter_spectrum(h_lat, tap_lat, nrm_lat, consts, a)
    b_lat = long_conv_gated(ul3, kf_lat, hy_bias, consts, a)
    h_ctx, tap_ctx, nrm_ctx = hyena_filter(feats[1], w1p, b1, f1, w2, b2, f2, w3, deltas)
    hf_ctx = ctx_filter_spectrum(h_ctx, mf)
    b_ctx = ctx_hyena(uc3, hf_ctx, tap_ctx, nrm_ctx, hy_bias, mf, mi)
    return proj, a_ctx, a_lat, b_ctx, b_lat


def _odd_layer(u, s, n_ctx, w_in, qn_g, kn_g, cq_g, ckv_g, wuq, wukv, mq_g, mk_g, rope_full, rope_half):
    w_all, o = w_in
    off_k, off_v, off_mq = ATT_WIDTH, ATT_WIDTH + KV_WIDTH, ATT_WIDTH + 2 * KV_WIDTH
    off_mkv, off_mkr = off_mq + Q_LORA, off_mq + Q_LORA + KV_LORA
    off_gate = off_mkr + ROPE_DIM
    proj = matmul_permuted(u, w_all, o, [(0, off_mkr + 2 * LANE)], F32)
    proj_gate = matmul_permuted(u, w_all[o][None, :, off_gate:], 0, [(0, BRANCH)], F32)
    qp = head_prep(proj, 0, C_HEADS, qn_g, rope_full, "half64", HEAD_DIM, HEAD_DIM ** -0.5 * LOG2E, True)
    kp = head_prep(proj, off_k, C_KV_HEADS, kn_g, rope_full, "half64", HEAD_DIM, 1.0)
    vt = transpose_heads(proj, off_v, C_KV_HEADS)
    c_ctx, c_lat = _attention_pair(qp, kp, vt, s, n_ctx, kv_heads=C_KV_HEADS, groups=C_HEADS // C_KV_HEADS,
                                   window=False)
    pad_r = LANE - ROPE_DIM
    wuq3 = wuq.reshape(Q_LORA, M_HEADS, QK_DIM)
    wuq_p = jnp.concatenate([wuq3[:, :, :NOPE_DIM].reshape(Q_LORA, -1),
                             jnp.pad(wuq3[:, :, NOPE_DIM:], ((0, 0), (0, 0), (0, pad_r))).reshape(Q_LORA, -1)], axis=1)
    wukv3 = wukv.reshape(KV_LORA, M_HEADS, NOPE_DIM + V_DIM)
    wukv_p = jnp.concatenate([wukv3[:, :, :NOPE_DIM].reshape(KV_LORA, -1),
                              wukv3[:, :, NOPE_DIM:].reshape(KV_LORA, -1)], axis=1)
    q_raw = norm_matmul(proj, off_mq, cq_g, wuq_p.astype(BF16))
    kv_raw = norm_matmul(proj, off_mkv, ckv_g, wukv_p.astype(BF16))
    scale = QK_DIM ** -0.5 * LOG2E
    gq_r = jnp.pad(mq_g[NOPE_DIM:], (0, pad_r))
    gk_r = jnp.pad(mk_g[NOPE_DIM:], (0, pad_r))
    qn = head_prep(q_raw, 0, M_HEADS, mq_g[:NOPE_DIM], (), "none", NOPE_DIM, scale, True)
    qr = head_prep(q_raw, M_HEADS * NOPE_DIM, M_HEADS, gq_r, rope_half, "half32", ROPE_DIM, scale, True)
    kn = head_prep(kv_raw, 0, M_HEADS, mk_g[:NOPE_DIM], (), "none", NOPE_DIM, 1.0)
    kr = head_prep(proj, off_mkr, 1, gk_r, rope_half, "half32", ROPE_DIM, 1.0)
    vmt = transpose_heads(kv_raw, M_HEADS * NOPE_DIM, M_HEADS)
    d_ctx, d_lat = _attention_pair(qn, kn, vmt, s, n_ctx, kv_heads=M_HEADS, groups=1, window=False, q2=qr, k2=kr)
    return proj_gate, c_ctx, c_lat, d_ctx, d_lat


def kernel(x, c, ctx, c_ctx, ada_w, ada_b, norm_g, w_out, ev_w_in, ev_qn_g, ev_kn_g, ev_sink, ev_conv_w, ev_conv_b, hy_w1, hy_b1, hy_f1, hy_w2, hy_b2, hy_f2, hy_w3, hy_bias, od_w_in, od_qn_g, od_kn_g, od_cq_g, od_ckv_g, od_wuq, od_wukv, od_mq_g, od_mk_g):
    _, s, d = x.shape
    n_ctx = ctx.shape[1]
    depth = ada_w.shape[0]
    h = jnp.concatenate([x[0], ctx[0]], axis=0)
    cc = jnp.zeros((8, d), F32).at[0].set(c[0]).at[1].set(c_ctx)
    mod = ada_mod(cc, ada_w, ada_b)
    rope_full = _rope_tables_full(s, n_ctx)
    rope_half = _rope_tables_half(s, n_ctx)
    feats = (_filter_features(s), _filter_features(n_ctx))
    as_arrays = lambda mats: tuple(jnp.asarray(m) for m in mats)
    dft = (as_arrays(_dft_consts(int(round(math.sqrt(2 * s))))), as_arrays(_dense_dft_consts(n_ctx)))
    u = norm_mod(h, norm_g[0], mod[0], n_ctx)
    for i in range(depth):
        if i % 2 == 0:
            e = i // 2
            proj, o1c, o1l, o2c, o2l = _even_layer(
                u, s, n_ctx, (ev_w_in, e), ev_qn_g[e], ev_kn_g[e], ev_sink[e], ev_conv_w[e], ev_conv_b[e],
                (hy_w1[e], hy_b1[e], hy_f1[e], hy_w2[e], hy_b2[e], hy_f2[e], hy_w3[e]), hy_bias[e], rope_full, feats,
                dft)
        else:
            o = i // 2
            proj, o1c, o1l, o2c, o2l = _odd_layer(
                u, s, n_ctx, (od_w_in, o), od_qn_g[o], od_kn_g[o], od_cq_g[o], od_ckv_g[o], od_wuq[o], od_wukv[o],
                od_mq_g[o], od_mk_g[o], rope_full, rope_half)
        w = w_out[i].astype(BF16)
        if i < depth - 1:
            h, u = out_proj(o1c, o1l, o2c, o2l, proj, h, mod[i], w, (norm_g[i + 1], mod[i + 1]))
        else:
            h = out_proj(o1c, o1l, o2c, o2l, proj, h, mod[i], w, None)
    return h[None]
```

```python
import functools
import math

import numpy as np
import jax
import jax.numpy as jnp
from jax import lax
from jax.experimental import pallas as pl
from jax.experimental.pallas import tpu as pltpu

F32 = jnp.float32
BF16 = jnp.bfloat16

GRID_W = 64
HEAD_DIM = 128
ROPE_THETA = 10000.0
EPS = 1e-6
NEG_INF = -1e30
A_HEADS, A_KV_HEADS, WINDOW = 8, 2, 128
B_WIDTH = 1024
HYENA_BANDS = 16
HYENA_EMB = 1 + 2 * HYENA_BANDS
DECAY_TARGET = 1e-2
DECAY_MAX = abs(math.log(DECAY_TARGET)) / 0.3
DECAY_MIN = abs(math.log(DECAY_TARGET)) / 1.5
C_HEADS, C_KV_HEADS = 8, 2
M_HEADS, Q_LORA, KV_LORA, NOPE_DIM, ROPE_DIM, V_DIM = 8, 512, 256, 128, 64, 128
QK_DIM = NOPE_DIM + ROPE_DIM
ATT_WIDTH = A_HEADS * HEAD_DIM
KV_WIDTH = A_KV_HEADS * HEAD_DIM
BRANCH = ATT_WIDTH + B_WIDTH

LOG2E = math.log2(math.e)
LANE = 128
SUBLANES = 8
ROWS_BF16 = 16
ROW_TILE = 256
VMEM_LIMIT = 48 * 1024 * 1024

OFF_GATE, OFF_Q, OFF_K, OFF_V, OFF_REST = 0, BRANCH, BRANCH + ATT_WIDTH, BRANCH + ATT_WIDTH + KV_WIDTH, BRANCH + ATT_WIDTH + 2 * KV_WIDTH
OFF_MQ, OFF_MKV, OFF_MKR = OFF_REST, OFF_REST + Q_LORA, OFF_REST + Q_LORA + KV_LORA


def _params(*sem):
    return pltpu.CompilerParams(dimension_semantics=sem, vmem_limit_bytes=VMEM_LIMIT)


def _split_bf16(x):
    hi = x.astype(BF16)
    return hi, (x - hi.astype(F32)).astype(BF16)


def _dot(a, b):
    return jnp.dot(a, b, preferred_element_type=F32)


def _dot3(a_hi, a_lo, b_hi, b_lo):
    return _dot(a_hi, b_hi) + (_dot(a_hi, b_lo) + _dot(a_lo, b_hi))


def _ada_kernel(c_ref, w_ref, b_ref, o_ref):
    c = c_ref[...]
    a = (c * jax.nn.sigmoid(c)).astype(BF16)
    o_ref[0] = _dot(a, w_ref[0].astype(BF16)) + b_ref[0]


def ada_mod(cc, ada_w, ada_b):
    depth, d, n = ada_w.shape
    tn = _pick_tile(n, 1024)
    return pl.pallas_call(
        _ada_kernel, grid=(depth, n // tn),
        in_specs=[pl.BlockSpec((8, d), lambda l, j: (0, 0)),
                  pl.BlockSpec((1, d, tn), lambda l, j: (l, 0, j)),
                  pl.BlockSpec((1, 1, tn), lambda l, j: (l, 0, j))],
        out_specs=pl.BlockSpec((1, 8, tn), lambda l, j: (l, 0, j)),
        out_shape=jax.ShapeDtypeStruct((depth, 8, n), F32),
        compiler_params=_params("parallel", "parallel"), name="ada_mod",
    )(cc, ada_w, ada_b.reshape(depth, 1, n))


def _modulated_norm(x, g, mod_ref, is_ctx):
    d = x.shape[1]
    y = x * lax.rsqrt(jnp.mean(x * x, axis=-1, keepdims=True) + EPS) * g
    sh = jnp.where(is_ctx, mod_ref[1:2, 0:d], mod_ref[0:1, 0:d])
    sc = jnp.where(is_ctx, mod_ref[1:2, d:2 * d], mod_ref[0:1, d:2 * d])
    return y * (1.0 + sc) + sh


def _norm_kernel(x_ref, g_ref, mod_ref, o_ref, *, lat_tiles):
    is_ctx = pl.program_id(0) >= lat_tiles
    o_ref[...] = _modulated_norm(x_ref[...], g_ref[...], mod_ref, is_ctx).astype(o_ref.dtype)


def norm_mod(h, g, mod, n_ctx):
    r, d = h.shape
    tm = ROW_TILE
    return pl.pallas_call(
        functools.partial(_norm_kernel, lat_tiles=(r - n_ctx) // tm), grid=(r // tm,),
        in_specs=[pl.BlockSpec((tm, d), lambda i: (i, 0)),
                  pl.BlockSpec((1, d), lambda i: (0, 0)),
                  pl.BlockSpec(mod.shape, lambda i: (0, 0))],
        out_specs=pl.BlockSpec((tm, d), lambda i: (i, 0)),
        out_shape=jax.ShapeDtypeStruct((r, d), BF16),
        compiler_params=_params("parallel"), name="norm_mod",
    )(h, g.reshape(1, d), mod)


def _mm_kernel(a_ref, b_ref, o_ref):
    o_ref[...] = _dot(a_ref[...], b_ref[...]).astype(o_ref.dtype)


def _pick_tile(n, cap, unit=LANE):
    best = unit
    for t in range(unit, cap + 1, unit):
        if n % t == 0:
            best = t
    return best


def matmul(a, b, out_dtype):
    m, k = a.shape
    n = b.shape[1]
    tm = _pick_tile(m, 1536, 8)
    tn = _pick_tile(n, 1024)
    return pl.pallas_call(
        _mm_kernel, grid=(m // tm, n // tn),
        in_specs=[pl.BlockSpec((tm, k), lambda i, j: (i, 0)),
                  pl.BlockSpec((k, tn), lambda i, j: (0, j))],
        out_specs=pl.BlockSpec((tm, tn), lambda i, j: (i, j)),
        out_shape=jax.ShapeDtypeStruct((m, n), out_dtype),
        compiler_params=_params("parallel", "parallel"), name="in_proj",
    )(a, b)


def _mm_cast_kernel(a_ref, b_ref, o_ref):
    o_ref[...] = _dot(a_ref[...], b_ref[0].astype(a_ref.dtype)).astype(o_ref.dtype)


def matmul_permuted(a, w3, layer, segments, out_dtype):
    m, k = a.shape
    n = sum(width for _, width in segments)
    tm = _pick_tile(m, 1536, 8)
    tn = LANE
    for t in range(LANE, 1024 + 1, LANE):
        if all(start % t == 0 and width % t == 0 for start, width in segments):
            tn = t
    src_tiles = [(start + off) // tn for start, width in segments for off in range(0, width, tn)]

    def src(j):
        idx = src_tiles[0]
        for t, s_t in enumerate(src_tiles[1:], 1):
            idx = jnp.where(j == t, s_t, idx)
        return idx
    return pl.pallas_call(
        _mm_cast_kernel, grid=(m // tm, n // tn),
        in_specs=[pl.BlockSpec((tm, k), lambda i, j: (i, 0)),
                  pl.BlockSpec((1, k, tn), lambda i, j: (layer, 0, src(j)))],
        out_specs=pl.BlockSpec((tm, tn), lambda i, j: (i, j)),
        out_shape=jax.ShapeDtypeStruct((m, n), out_dtype),
        compiler_params=_params("parallel", "parallel"), name="in_proj",
    )(a, w3)


def _normmm_kernel(x_ref, g_ref, w_ref, o_ref):
    x = x_ref[...].astype(F32)
    y = x * lax.rsqrt(jnp.mean(x * x, axis=-1, keepdims=True) + EPS) * g_ref[...]
    o_ref[...] = _dot(y.astype(BF16), w_ref[...]).astype(o_ref.dtype)


def norm_matmul(x, col_off, g, w):
    r = x.shape[0]
    k, n = w.shape
    tm = _pick_tile(r, 768, 8)
    cb = col_off // k
    return pl.pallas_call(
        _normmm_kernel, grid=(r // tm,),
        in_specs=[pl.BlockSpec((tm, k), lambda i: (i, cb)),
                  pl.BlockSpec((1, k), lambda i: (0, 0)),
                  pl.BlockSpec((k, n), lambda i: (0, 0))],
        out_specs=pl.BlockSpec((tm, n), lambda i: (i, 0)),
        out_shape=jax.ShapeDtypeStruct((r, n), F32),
        compiler_params=_params("parallel"), name="norm_matmul",
    )(x, g.reshape(1, k), w)


def _prep_kernel(x_ref, g_ref, t0_ref, t1_ref, o_ref, *, scale, n_heads, head_major):
    g, t0, t1 = g_ref[...], t0_ref[...], t1_ref[...]
    for h in range(n_heads):
        x = x_ref[:, h * LANE:(h + 1) * LANE].astype(F32)
        xn = x * lax.rsqrt(jnp.mean(x * x, axis=-1, keepdims=True) + EPS) * g
        y = ((xn * t0 + pltpu.roll(xn, HEAD_DIM // 2, 1) * t1) * scale).astype(o_ref.dtype)
        if head_major:
            o_ref[h] = y
        else:
            o_ref[:, h * LANE:(h + 1) * LANE] = y


def head_prep(x, col_off, n_heads, g, tables, scale, head_major=False):
    r = x.shape[0]
    w = n_heads * LANE
    tm = _pick_tile(r, 768, 8)
    tab_spec = pl.BlockSpec((tm, LANE), lambda i: (i, 0))
    if head_major:
        out_spec = pl.BlockSpec((n_heads, tm, LANE), lambda i: (0, i, 0))
        out_shape = jax.ShapeDtypeStruct((n_heads, r, LANE), BF16)
    else:
        out_spec = pl.BlockSpec((tm, w), lambda i: (i, 0))
        out_shape = jax.ShapeDtypeStruct((r, w), BF16)
    return pl.pallas_call(
        functools.partial(_prep_kernel, scale=scale, n_heads=n_heads, head_major=head_major),
        grid=(r // tm,),
        in_specs=[pl.BlockSpec((tm, w), lambda i: (i, col_off // w)),
                  pl.BlockSpec((1, LANE), lambda i: (0, 0)), tab_spec, tab_spec],
        out_specs=out_spec, out_shape=out_shape,
        compiler_params=_params("parallel"), name="head_prep",
    )(x, g.reshape(1, LANE), *tables)


def _mla_prep_kernel(xn_ref, xr_ref, gn_ref, gr_ref, t0_ref, t1_ref, t2_ref, o_ref, *, n_heads, shared_rope, scale,
                     head_major):
    gn, gr = gn_ref[...], gr_ref[...]
    t0, t1, t2 = t0_ref[...], t1_ref[...], t2_ref[...]

    def norm(x, n_real, g):
        return x * lax.rsqrt(jnp.sum(x * x, axis=-1, keepdims=True) * (1.0 / n_real) + EPS) * g

    def rope_part(x):
        xn = norm(x.astype(F32), ROPE_DIM, gr)
        return xn * t0 + pltpu.roll(xn, 96, 1) * t1 + pltpu.roll(xn, 32, 1) * t2

    if shared_rope:
        shared = rope_part(xr_ref[...])
    for h in range(n_heads):
        cols = slice(h * LANE, (h + 1) * LANE)
        a = norm(xn_ref[:, cols].astype(F32), NOPE_DIM, gn)
        b = shared if shared_rope else rope_part(xr_ref[:, cols])
        y = (jnp.concatenate([a, b], axis=1) * scale).astype(o_ref.dtype)
        if head_major:
            o_ref[h] = y
        else:
            o_ref[:, 2 * h * LANE:2 * (h + 1) * LANE] = y


def mla_prep(xn, xn_off, xr, xr_off, shared_rope, n_heads, gn, gr, tables, scale, head_major):
    r = xn.shape[0]
    w = n_heads * LANE
    wr = LANE if shared_rope else w
    tm = _pick_tile(r, 768, 8)
    row = lambda width: pl.BlockSpec((1, width), lambda i: (0, 0))
    tab = pl.BlockSpec((tm, LANE), lambda i: (i, 0))
    if head_major:
        out_spec = pl.BlockSpec((n_heads, tm, 2 * LANE), lambda i: (0, i, 0))
        out_shape = jax.ShapeDtypeStruct((n_heads, r, 2 * LANE), BF16)
    else:
        out_spec = pl.BlockSpec((tm, 2 * w), lambda i: (i, 0))
        out_shape = jax.ShapeDtypeStruct((r, 2 * w), BF16)
    return pl.pallas_call(
        functools.partial(_mla_prep_kernel, n_heads=n_heads, shared_rope=shared_rope, scale=scale,
                          head_major=head_major),
        grid=(r // tm,),
        in_specs=[pl.BlockSpec((tm, w), lambda i: (i, xn_off // w)), pl.BlockSpec((tm, wr), lambda i: (i, xr_off // wr)),
                  row(LANE), row(LANE), tab, tab, tab],
        out_specs=out_spec, out_shape=out_shape,
        compiler_params=_params("parallel"), name="mla_prep",
    )(xn, xr, gn.reshape(1, LANE), gr.reshape(1, LANE), *tables)


def _vt_kernel(x_ref, o_ref):
    o_ref[...] = x_ref[...].astype(F32).T.astype(o_ref.dtype)


def transpose_heads(x, col_off, n_heads):
    r = x.shape[0]
    w = n_heads * LANE
    tm = _pick_tile(r, 768, LANE)
    return pl.pallas_call(
        _vt_kernel, grid=(r // tm,),
        in_specs=[pl.BlockSpec((tm, w), lambda i: (i, col_off // w))],
        out_specs=pl.BlockSpec((w, tm), lambda i: (0, i)),
        out_shape=jax.ShapeDtypeStruct((w, r), BF16),
        compiler_params=_params("parallel"), name="transpose_heads",
    )(x)


def _flash_kernel(*refs, groups, tq, tk, dv, n_kv, chain, has_sink, key_chunk):
    q_ref, qn_ref, k_ref, kn_ref, vt_ref = refs[0:5]
    pos = 5
    if has_sink:
        sink_ref = refs[pos]
        pos += 1
    o_ref, m_ref, l_ref, acc_ref, s_ref, mt_ref = refs[pos:pos + 6]
    i, j = pl.program_id(1), pl.program_id(2)
    n = groups * tq
    bounds = [(r0, min(r0 + key_chunk, tk)) for r0 in range(0, tk, key_chunk)]

    def stacked(qr):
        return qr[...].reshape(n, qr.shape[2])

    def scores(kr, q, r0, r1):
        return lax.dot_general(kr[r0:r1, :], q, (((1,), (1,)), ((), ())), preferred_element_type=F32)

    @pl.when(j == 0)
    def _():
        m_ref[...] = jnp.full(m_ref.shape, NEG_INF, F32)
        l_ref[...] = jnp.zeros(l_ref.shape, F32)
        acc_ref[...] = jnp.zeros(acc_ref.shape, F32)

    @pl.when((j == 0) & (i == 0) if chain else j == 0)
    def _():
        q = stacked(q_ref)
        mt = jnp.full(mt_ref.shape, NEG_INF, F32)
        for r0, r1 in bounds:
            s = scores(k_ref, q, r0, r1)
            s_ref[r0:r1, :] = s
            mt = jnp.maximum(mt, jnp.max(s, axis=0, keepdims=True))
        mt_ref[...] = mt

    def step(kr, qr):
        prefetch = kr is not None
        if prefetch:
            q = stacked(qr)
        m_prev = m_ref[...]
        m_new = jnp.maximum(m_prev, mt_ref[...])
        alpha = jnp.exp2(m_prev - m_new)
        l_new = alpha * l_ref[...]
        acc_new = alpha * acc_ref[...]
        mt = jnp.full(mt_ref.shape, NEG_INF, F32)
        for r0, r1 in bounds:
            if prefetch:
                s_next = scores(kr, q, r0, r1)
            p = jnp.exp2(s_ref[r0:r1, :] - m_new)
            l_new = l_new + jnp.sum(p, axis=0, keepdims=True)
            acc_new = acc_new + _dot(vt_ref[:, r0:r1], p.astype(BF16))
            if prefetch:
                s_ref[r0:r1, :] = s_next
                mt = jnp.maximum(mt, jnp.max(s_next, axis=0, keepdims=True))
        l_ref[...] = l_new
        acc_ref[...] = acc_new
        m_ref[...] = m_new
        if prefetch:
            mt_ref[...] = mt

    if n_kv > 1:
        pl.when(j < n_kv - 1)(lambda: step(kn_ref, q_ref))

    @pl.when(j == n_kv - 1)
    def _():
        if chain:
            step(k_ref, qn_ref)
        else:
            step(None, None)
        m, l, acc = m_ref[...], l_ref[...], acc_ref[...]
        if has_sink:
            sk = sink_ref[0]
            m_fin = jnp.maximum(m, sk)
            a = jnp.exp2(m - m_fin)
            l = l * a + jnp.exp2(sk - m_fin)
            acc = acc * a
        o_t = acc / l
        for g in range(groups):
            o_ref[:, g * dv:(g + 1) * dv] = o_t[:, g * tq:(g + 1) * tq].T.astype(o_ref.dtype)


KEY_CHUNK = 256


def flash(q, k, vt, *, kv_heads, groups, n_q, q_row0, kv_row0, n_kv_rows, tq, tk, dv=HEAD_DIM, sink=None):
    dk = q.shape[2]
    qb0 = q_row0 // tq
    kb0 = kv_row0 // tk
    n_kv = n_kv_rows // tk

    def kv_row(i, j):
        return kb0 + j
    n = groups * tq
    n_qb = n_q // tq
    nxt = lambda j: jnp.minimum(j + 1, n_kv - 1)
    q_nxt = lambda i: qb0 + jnp.minimum(i + 1, n_qb - 1)
    in_specs = [pl.BlockSpec((groups, tq, dk), lambda h, i, j: (h, qb0 + i, 0)),
                pl.BlockSpec((groups, tq, dk), lambda h, i, j: (h, q_nxt(i), 0)),
                pl.BlockSpec((tk, dk), lambda h, i, j: (kv_row(i, 0), h)),
                pl.BlockSpec((tk, dk), lambda h, i, j: (kv_row(i, nxt(j)), h)),
                pl.BlockSpec((dv, tk), lambda h, i, j: (h, kv_row(i, j)))]
    args = [q, q, k, k, vt]
    if sink is not None:
        in_specs.append(pl.BlockSpec((1, 1, n), lambda h, i, j: (h, 0, 0)))
        sink2 = sink.astype(F32).reshape(kv_heads, groups, 1) * LOG2E
        args.append(jnp.broadcast_to(sink2, (kv_heads, groups, tq)).reshape(kv_heads, 1, n))
    kern = functools.partial(
        _flash_kernel, groups=groups, tq=tq, tk=tk, dv=dv, n_kv=n_kv, chain=n_qb > 1, has_sink=sink is not None,
        key_chunk=KEY_CHUNK)
    return pl.pallas_call(
        kern, grid=(kv_heads, n_q // tq, n_kv), in_specs=in_specs,
        out_specs=pl.BlockSpec((tq, groups * dv), lambda h, i, j: (i, h)),
        out_shape=jax.ShapeDtypeStruct((n_q, kv_heads * groups * dv), BF16),
        scratch_shapes=[pltpu.VMEM((1, n), F32), pltpu.VMEM((1, n), F32), pltpu.VMEM((dv, n), F32),
                        pltpu.VMEM((tk, n), F32), pltpu.VMEM((1, n), F32)],
        compiler_params=_params("parallel", "arbitrary", "arbitrary"),
        name="flash_dense",
    )(*args)


def _window_kernel(*refs, groups, tq, n_lat, has_sink):
    q_ref = refs[0]
    n_half = tq // WINDOW + 2
    k_refs = refs[1:2 + n_half]
    vt_refs = refs[2 + n_half:3 + 2 * n_half]
    pos = 3 + 2 * n_half
    if has_sink:
        sink_ref = refs[pos]
        pos += 1
    o_ref = refs[pos]
    i = pl.program_id(1)
    n = groups * tq
    dv = vt_refs[0].shape[0]
    q = q_ref[...].reshape(n, q_ref.shape[2])
    qpos = i * tq + lax.broadcasted_iota(jnp.int32, (1, n), 1) % tq
    scores = []
    m = sink_ref[0] if has_sink else jnp.full((1, n), NEG_INF, F32)
    for b, k_ref in enumerate(k_refs):
        s = lax.dot_general(k_ref[...], q, (((1,), (1,)), ((), ())), preferred_element_type=F32)
        if b > 0:
            kpos = i * tq + (b - 2) * WINDOW + lax.broadcasted_iota(jnp.int32, (WINDOW, 1), 0)
            valid = (jnp.abs(qpos - kpos) <= WINDOW) & (kpos >= 0) & (kpos < n_lat)
            s = jnp.where(valid, s, NEG_INF)
        scores.append(s)
        m = jnp.maximum(m, jnp.max(s, axis=0, keepdims=True))
    l = jnp.exp2(sink_ref[0] - m) if has_sink else jnp.zeros((1, n), F32)
    acc = jnp.zeros((dv, n), F32)
    for s, vt_ref in zip(scores, vt_refs):
        p = jnp.exp2(s - m)
        l = l + jnp.sum(p, axis=0, keepdims=True)
        acc = acc + _dot(vt_ref[...], p.astype(BF16))
    o_t = acc / l
    for g in range(groups):
        o_ref[:, g * dv:(g + 1) * dv] = o_t[:, g * tq:(g + 1) * tq].T.astype(o_ref.dtype)


def window_attention(q, k, vt, *, kv_heads, groups, n_lat, n_ctx, tq, sink=None, dv=HEAD_DIM):
    dk = q.shape[2]
    per = tq // WINDOW
    n_half_blocks = n_lat // WINDOW
    n = groups * tq
    ctx_blk = n_lat // n_ctx
    half = lambda b: (lambda i: jnp.clip(i * per + b - 1, 0, n_half_blocks - 1))
    halves = [half(b) for b in range(per + 2)]
    in_specs = [pl.BlockSpec((groups, tq, dk), lambda h, i: (h, i, 0)),
                pl.BlockSpec((n_ctx, dk), lambda h, i: (ctx_blk, h))]
    in_specs += [pl.BlockSpec((WINDOW, dk), lambda h, i, f=f: (f(i), h)) for f in halves]
    in_specs += [pl.BlockSpec((dv, n_ctx), lambda h, i: (h, ctx_blk))]
    in_specs += [pl.BlockSpec((dv, WINDOW), lambda h, i, f=f: (h, f(i))) for f in halves]
    args = [q] + [k] * (per + 3) + [vt] * (per + 3)
    if sink is not None:
        in_specs.append(pl.BlockSpec((1, 1, n), lambda h, i: (h, 0, 0)))
        sink2 = sink.astype(F32).reshape(kv_heads, groups, 1) * LOG2E
        args.append(jnp.broadcast_to(sink2, (kv_heads, groups, tq)).reshape(kv_heads, 1, n))
    return pl.pallas_call(
        functools.partial(_window_kernel, groups=groups, tq=tq, n_lat=n_lat, has_sink=sink is not None),
        grid=(kv_heads, n_lat // tq), in_specs=in_specs,
        out_specs=pl.BlockSpec((tq, groups * dv), lambda h, i: (i, h)),
        out_shape=jax.ShapeDtypeStruct((n_lat, kv_heads * groups * dv), BF16),
        compiler_params=_params("parallel", "parallel"), name="window_attention",
    )(*args)


def _outproj_kernel(*refs, lat_tiles, with_next):
    o1c_ref, o1l_ref, o2c_ref, o2l_ref, gate_ref, h_ref, mod_ref, w_ref = refs[:8]
    is_ctx = pl.program_id(0) >= lat_tiles
    w1 = o1l_ref.shape[1]
    d = h_ref.shape[1]
    gt = gate_ref[...].astype(F32)
    sg = gt * jax.nn.sigmoid(gt)
    o1 = jnp.where(is_ctx, o1c_ref[...].astype(F32), o1l_ref[...].astype(F32))
    o2 = jnp.where(is_ctx, o2c_ref[...].astype(F32), o2l_ref[...].astype(F32))
    t1 = (o1 * sg[:, :w1]).astype(BF16)
    t2 = (o2 * sg[:, w1:]).astype(BF16)
    y = _dot(t1, w_ref[0:w1, :]) + _dot(t2, w_ref[w1:, :])
    gm = jnp.where(is_ctx, mod_ref[1:2, 2 * d:3 * d], mod_ref[0:1, 2 * d:3 * d])
    h_new = h_ref[...] + gm * y
    if with_next:
        g_next_ref, mod_next_ref, out_ref, u_ref = refs[8:]
        u_ref[...] = _modulated_norm(h_new, g_next_ref[...], mod_next_ref, is_ctx).astype(u_ref.dtype)
    else:
        out_ref = refs[8]
    out_ref[...] = h_new


def out_proj(o1c, o1l, o2c, o2l, proj, h, mod, w, nxt):
    r, d = h.shape
    tm = ROW_TILE
    assert o1c.shape[0] == tm, "the context rows must be exactly one row tile"
    lt = o1l.shape[0] // tm
    w1, w2 = o1l.shape[1], o2l.shape[1]
    lat = lambda i: (jnp.minimum(i, lt - 1), 0)
    row = pl.BlockSpec((tm, d), lambda i: (i, 0))
    in_specs = [pl.BlockSpec((tm, w1), lambda i: (0, 0)), pl.BlockSpec((tm, w1), lat),
                pl.BlockSpec((tm, w2), lambda i: (0, 0)), pl.BlockSpec((tm, w2), lat),
                pl.BlockSpec((tm, w1 + w2), lambda i: (i, OFF_GATE // (w1 + w2))),
                row, pl.BlockSpec(mod.shape, lambda i: (0, 0)), pl.BlockSpec(w.shape, lambda i: (0, 0))]
    args = [o1c, o1l, o2c, o2l, proj, h, mod, w]
    if nxt is None:
        r = lt * tm
        out_specs, out_shape = row, jax.ShapeDtypeStruct((r, d), F32)
    else:
        in_specs += [pl.BlockSpec((1, d), lambda i: (0, 0)), pl.BlockSpec(nxt[1].shape, lambda i: (0, 0))]
        args += [nxt[0].reshape(1, d), nxt[1]]
        out_specs = [row, row]
        out_shape = [jax.ShapeDtypeStruct((r, d), F32), jax.ShapeDtypeStruct((r, d), BF16)]
    return pl.pallas_call(
        functools.partial(_outproj_kernel, lat_tiles=lt, with_next=nxt is not None), grid=(r // tm,),
        in_specs=in_specs, out_specs=out_specs, out_shape=out_shape,
        compiler_params=_params("parallel"), name="out_proj",
    )(*args)


def _conv3_kernel(x_ref, w_ref, b_ref, oc_ref, ol_ref, *, n_ctx):
    x = x_ref[...].astype(F32)
    r = x.shape[0]
    n_lat = r - n_ctx
    row = lax.broadcasted_iota(jnp.int32, (r, 1), 0)
    prev = jnp.where((row == 0) | (row == n_lat), 0.0, pltpu.roll(x, 1, 0))
    nxt = jnp.where((row == n_lat - 1) | (row == r - 1), 0.0, pltpu.roll(x, r - 1, 0))
    y = prev * w_ref[0:1, :] + x * w_ref[1:2, :] + nxt * w_ref[2:3, :] + b_ref[...]
    ol_ref[0] = y[:n_lat]
    oc_ref[0] = y[n_lat:]


def conv3(proj, col_off, w, b, n_ctx):
    r = proj.shape[0]
    cw = w.shape[1] // 3
    nb = cw // LANE
    cb = col_off // LANE
    return pl.pallas_call(
        functools.partial(_conv3_kernel, n_ctx=n_ctx), grid=(3, nb),
        in_specs=[pl.BlockSpec((r, LANE), lambda p, c: (0, cb + p * nb + c)),
                  pl.BlockSpec((3, LANE), lambda p, c: (0, p * nb + c)),
                  pl.BlockSpec((1, LANE), lambda p, c: (0, p * nb + c))],
        out_specs=[pl.BlockSpec((1, n_ctx, LANE), lambda p, c: (p, 0, c)),
                   pl.BlockSpec((1, r - n_ctx, LANE), lambda p, c: (p, 0, c))],
        out_shape=[jax.ShapeDtypeStruct((3, n_ctx, cw), F32), jax.ShapeDtypeStruct((3, r - n_ctx, cw), F32)],
        compiler_params=_params("parallel", "parallel"), name="conv3",
    )(proj, w, b.reshape(1, -1))


def _filt_kernel(z_ref, w1_ref, b1_ref, f1_ref, w2_ref, b2_ref, f2_ref, w3_ref, dl_ref, h_ref, tap_ref, nrm_ref, *,
                 tm, cw):
    i = pl.program_id(0)
    z = z_ref[...]

    def mm(a, w_r):
        return _dot3(*_split_bf16(a), *_split_bf16(w_r[...]))

    a = jnp.sin(f1_ref[...] * (mm(z, w1_ref) + b1_ref[...]))
    a = jnp.sin(f2_ref[...] * (mm(a, w2_ref) + b2_ref[...]))
    dec = jnp.exp(-z[:, 0:1] * dl_ref[...])
    h = _dot(a.astype(BF16), w3_ref[...]) * jnp.concatenate([dec] * (h_ref.shape[1] // cw), axis=1)
    h_ref[...] = h.astype(h_ref.dtype)
    row = i * tm + lax.broadcasted_iota(jnp.int32, (tm, 1), 0)
    col = lax.broadcasted_iota(jnp.int32, (1, h.shape[1]), 1)
    skip = (row == 0) & ((col // cw) % 2 == 1)
    part = jnp.sum(jnp.where(skip, 0.0, jnp.abs(h)), axis=0, keepdims=True)

    @pl.when(i == 0)
    def _():
        nrm_ref[...] = jnp.zeros(nrm_ref.shape, F32)
        tap_ref[...] = h[0:SUBLANES]

    nrm_ref[...] += jnp.broadcast_to(part, nrm_ref.shape)


def hyena_filter(z, w1p, b1, f1, w2, b2, f2, w3, deltas):
    l = z.shape[0]
    n = w3.shape[1]
    cw = deltas.shape[0]
    hid = w2.shape[0]
    tm = min(l, 256)
    full = lambda a: pl.BlockSpec(a.shape, lambda i: (0, 0))
    ins = [w1p, b1.reshape(1, hid), f1.reshape(1, hid), w2, b2.reshape(1, hid), f2.reshape(1, hid), w3,
           deltas.reshape(1, cw)]
    small = pl.BlockSpec((SUBLANES, n), lambda i: (0, 0))
    return pl.pallas_call(
        functools.partial(_filt_kernel, tm=tm, cw=cw), grid=(l // tm,),
        in_specs=[pl.BlockSpec((tm, LANE), lambda i: (i, 0))] + [full(a) for a in ins],
        out_specs=[pl.BlockSpec((tm, n), lambda i: (i, 0)), small, small],
        out_shape=[jax.ShapeDtypeStruct((l, n), BF16), jax.ShapeDtypeStruct((SUBLANES, n), F32),
                   jax.ShapeDtypeStruct((SUBLANES, n), F32)],
        compiler_params=_params("arbitrary"), name="hyena_filter",
    )(z, *ins)


@functools.lru_cache(maxsize=None)
def _dft_consts(a):
    n = a * a
    hh = a // 2
    n1 = np.arange(hh)
    k1 = np.arange(hh)
    n2 = np.arange(a)
    k2 = np.arange(a)
    ang1 = 2 * np.pi * np.outer(k1 + 0.5, n1) / a
    m1 = np.zeros((hh, 2, hh))
    m1[:, 0], m1[:, 1] = np.cos(ang1), -np.sin(ang1)
    m1 = m1.reshape(2 * hh, hh)
    phi = 2 * np.pi * (n2[None, None, :] * (k1[:, None, None] + 0.5) / n + n2[None, None, :] * k2[None, :, None] / a)
    c, s = np.cos(phi), np.sin(phi)
    g = np.zeros((hh, 2, a, 2, a))
    g[:, 0, :, 0, :], g[:, 0, :, 1, :], g[:, 1, :, 0, :], g[:, 1, :, 1, :] = c, s, -s, c
    g = g.reshape(hh, 2 * a, 2 * a)
    gt = np.ascontiguousarray(np.transpose(g, (0, 2, 1)))
    al = 2 * np.pi * np.outer(n1, k1 + 0.5) / a
    mb = np.zeros((hh, hh, 2))
    mb[:, :, 0], mb[:, :, 1] = 2 / n * np.cos(al), -2 / n * np.sin(al)
    mb = mb.reshape(hh, 2 * hh)
    eye = np.eye(ROWS_BF16)
    return tuple(x.astype(np.float32) for x in (np.kron(m1, eye), g, gt, np.kron(mb, eye)))


@functools.lru_cache(maxsize=None)
def _dense_dft_consts(l):
    n = 2 * l
    ang = 2 * np.pi * np.outer(np.arange(l) + 0.5, np.arange(l)) / n
    mf = np.concatenate([np.cos(ang), -np.sin(ang)], axis=0)
    mi = np.concatenate([2 / n * np.cos(ang.T), -2 / n * np.sin(ang.T)], axis=1)
    return tuple(x.astype(np.float32) for x in (mf, mi))


def _stage1_kernel(x_ref, a_ref, o_ref, a16_ref):
    @pl.when((pl.program_id(0) == 0) & (pl.program_id(1) == 0))
    def _():
        a16_ref[...] = a_ref[...].astype(BF16)

    _, hh, t, ct = x_ref.shape
    x = x_ref[0].reshape(hh * t, ct).astype(BF16)
    o_ref[...] = _dot(a16_ref[...], x).astype(o_ref.dtype).reshape(o_ref.shape)


def dft_stage1(x4, p, a1):
    _, hh, a, c = x4.shape
    ct = _pick_tile(c, 512)
    t = ROWS_BF16
    return pl.pallas_call(
        _stage1_kernel, grid=(a // t, c // ct),
        in_specs=[pl.BlockSpec((1, hh, t, ct), lambda j, q: (p, 0, j, q)),
                  pl.BlockSpec(a1.shape, lambda j, q: (0, 0))],
        out_specs=pl.BlockSpec((2 * hh, t, ct), lambda j, q: (0, j, q)),
        out_shape=jax.ShapeDtypeStruct((2 * hh, a, c), BF16),
        scratch_shapes=[pltpu.VMEM(a1.shape, BF16)],
        compiler_params=_params("arbitrary", "arbitrary"), name="dft_stage1",
    )(x4, a1)


def _spectrum_product(x, h0, h1, tap0, n0, n1):
    half = x.shape[0] // 2
    inv = 1.0 / (n0 + n1)
    kr = (h0[:half] + h1[:half] - tap0) * inv
    ki = (h0[half:] - h1[half:]) * inv
    xr, xi = x[:half], x[half:]
    return jnp.concatenate([xr * kr - xi * ki, xr * ki + xi * kr], axis=0)


K1_PER_STEP = 4


def _mid_filter_kernel(y0_ref, y1_ref, g_ref, tap_ref, n0_ref, n1_ref, o_ref):
    inv = 1.0 / (n0_ref[0:1, :] + n1_ref[0:1, :])
    tap = tap_ref[0:1, :]
    for b in range(y0_ref.shape[0]):
        g = g_ref[b].astype(BF16)
        x0 = _dot(g, y0_ref[b])
        x1 = _dot(g, y1_ref[b])
        half = x0.shape[0] // 2
        kr = (x0[:half] + x1[:half] - tap) * inv
        ki = (x0[half:] - x1[half:]) * inv
        o_ref[b] = jnp.concatenate([kr, ki], axis=0).astype(o_ref.dtype)


def filter_spectrum(h, taps, nrm, consts, a):
    a1, g, _, _ = consts
    l, n = h.shape
    c = n // 4
    hh = a // 2
    y = dft_stage1(h.reshape(1, hh, a, n), 0, a1).reshape(hh, 2 * a, n)
    kb = K1_PER_STEP
    slab = lambda side: pl.BlockSpec((kb, 2 * a, c), lambda k, o: (k, 0, 2 * o + side))
    row = lambda side: pl.BlockSpec((SUBLANES, c), lambda k, o: (0, 2 * o + side))
    return pl.pallas_call(
        _mid_filter_kernel, grid=(hh // kb, 2),
        in_specs=[slab(0), slab(1), pl.BlockSpec((kb, 2 * a, 2 * a), lambda k, o: (k, 0, 0)), row(1), row(0), row(1)],
        out_specs=pl.BlockSpec((kb, 2 * a, c), lambda k, o: (k, 0, o)),
        out_shape=jax.ShapeDtypeStruct((hh, 2 * a, 2 * c), BF16),
        compiler_params=_params("parallel", "parallel"), name="dft_mid_filter",
    )(y, y, g, taps, nrm, nrm)


def _mid_conv_kernel(y_ref, g_ref, gt_ref, kf_ref, o_ref):
    for b in range(y_ref.shape[0]):
        x = _dot(g_ref[b].astype(BF16), y_ref[b])
        kf = kf_ref[b].astype(F32)
        half = x.shape[0] // 2
        xr, xi, kr, ki = x[:half], x[half:], kf[:half], kf[half:]
        z = jnp.concatenate([xr * kr - xi * ki, xr * ki + xi * kr], axis=0).astype(BF16)
        o_ref[b] = _dot(gt_ref[b].astype(BF16), z).astype(o_ref.dtype)


def dft_mid_conv(y, g, gt, kf, order):
    hh, a2, c = y.shape
    kb = K1_PER_STEP
    slab = pl.BlockSpec((kb, a2, c), lambda k: (k, 0, 0))
    mat = pl.BlockSpec((kb, a2, a2), lambda k: (k, 0, 0))
    return pl.pallas_call(
        _mid_conv_kernel, grid=(hh // kb,),
        in_specs=[slab, mat, mat, pl.BlockSpec((kb, a2, c), lambda k: (k, 0, order))],
        out_specs=slab, out_shape=jax.ShapeDtypeStruct((hh, a2, c), BF16),
        compiler_params=_params("parallel"), name="dft_mid_conv",
    )(y, g, gt, kf)


def _last_kernel(b_ref, a_ref, xa_ref, zb_ref, bias_ref, o_ref, a16_ref):
    @pl.when((pl.program_id(0) == 0) & (pl.program_id(1) == 0))
    def _():
        a16_ref[...] = a_ref[...].astype(BF16)

    rows, t, ct = b_ref.shape
    conv = _dot(a16_ref[...], b_ref[...].reshape(rows * t, ct)).reshape(rows // 2, t, ct)
    o_ref[0] = (xa_ref[0] * (conv + bias_ref[...] * zb_ref[0])).astype(o_ref.dtype)


def dft_last(b, a2m, xa4, pa, zb4, pb, bias_row):
    rows, a, c = b.shape
    hh = rows // 2
    ct = _pick_tile(c, 512)
    t = ROWS_BF16
    sig = lambda pp: pl.BlockSpec((1, hh, t, ct), lambda j, q: (pp, 0, j, q))
    return pl.pallas_call(
        _last_kernel, grid=(a // t, c // ct),
        in_specs=[pl.BlockSpec((rows, t, ct), lambda j, q: (0, j, q)),
                  pl.BlockSpec(a2m.shape, lambda j, q: (0, 0)),
                  sig(pa), sig(pb), pl.BlockSpec((1, ct), lambda j, q: (0, q))],
        out_specs=sig(0), out_shape=jax.ShapeDtypeStruct((1, hh, a, c), F32),
        scratch_shapes=[pltpu.VMEM(a2m.shape, BF16)],
        compiler_params=_params("arbitrary", "arbitrary"), name="dft_last",
    )(b, a2m, xa4, zb4, bias_row)


def long_conv_gated(u3, kf, bias, consts, a):
    a1, g, gt, a2m = consts
    _, l, c = u3.shape
    hh = a // 2
    u4 = u3.reshape(3, hh, a, c)

    def conv_gate(src4, p_src, xa_p, order):
        y = dft_stage1(src4, p_src, a1).reshape(hh, 2 * a, c)
        bb = dft_mid_conv(y, g, gt, kf, order).reshape(2 * hh, a, c)
        return dft_last(bb, a2m, u4, xa_p, src4, p_src, bias[order].reshape(1, c))

    z2 = conv_gate(u4, 2, 0, 0)
    return conv_gate(z2, 0, 1, 1).reshape(l, c)


def _ctx_spec_kernel(h_ref, mf_ref, o_ref):
    o_ref[...] = _dot(mf_ref[...].astype(BF16), h_ref[...])


def ctx_filter_spectrum(h, mf):
    l, n = h.shape
    tn = _pick_tile(n, 1024)
    return pl.pallas_call(
        _ctx_spec_kernel, grid=(n // tn,),
        in_specs=[pl.BlockSpec((l, tn), lambda j: (0, j)), pl.BlockSpec(mf.shape, lambda j: (0, 0))],
        out_specs=pl.BlockSpec((2 * l, tn), lambda j: (0, j)),
        out_shape=jax.ShapeDtypeStruct((2 * l, n), F32),
        compiler_params=_params("parallel"), name="ctx_filter_spectrum",
    )(h, mf)


def _ctx_hyena_kernel(u_ref, mf_ref, mi_ref, hf0a, hf1a, hf0b, hf1b, tap0, tap1, n0a, n1a, n0b, n1b, bias_ref, o_ref):
    x1, x2, z = u_ref[0], u_ref[1], u_ref[2]

    mf, mi = mf_ref[...].astype(BF16), mi_ref[...].astype(BF16)

    def conv(sig, h0_ref, h1_ref, tap_ref, n0_ref, n1_ref):
        x = _dot(mf, sig.astype(BF16))
        zz = _spectrum_product(x, h0_ref[...], h1_ref[...], tap_ref[0:1, :], n0_ref[0:1, :], n1_ref[0:1, :])
        return _dot(mi, zz.astype(BF16))

    z2 = x1 * (conv(z, hf0a, hf1a, tap0, n0a, n1a) + bias_ref[0:1, :] * z)
    o_ref[...] = (x2 * (conv(z2, hf0b, hf1b, tap1, n0b, n1b) + bias_ref[1:2, :] * z2)).astype(o_ref.dtype)


def ctx_hyena(u3, hf, taps, nrm, bias, mf, mi):
    _, l, c = u3.shape
    tc = _pick_tile(c, 512)
    nb = c // tc
    const = lambda a: pl.BlockSpec(a.shape, lambda j: (0, 0))
    col = lambda rows, q: pl.BlockSpec((rows, tc), lambda j: (0, q * nb + j))
    mats = [mf, mi]
    return pl.pallas_call(
        _ctx_hyena_kernel, grid=(nb,),
        in_specs=[pl.BlockSpec((3, l, tc), lambda j: (0, 0, j))] + [const(m) for m in mats]
        + [col(2 * l, 0), col(2 * l, 1), col(2 * l, 2), col(2 * l, 3), col(8, 1), col(8, 3),
           col(8, 0), col(8, 1), col(8, 2), col(8, 3), pl.BlockSpec((2, tc), lambda j: (0, j))],
        out_specs=pl.BlockSpec((l, tc), lambda j: (0, j)),
        out_shape=jax.ShapeDtypeStruct((l, c), BF16),
        compiler_params=_params("parallel"), name="ctx_hyena",
    )(u3, *mats, hf, hf, hf, hf, taps, taps, nrm, nrm, nrm, nrm, bias)


def _axial_tables(n_tokens, n_rot):
    rows = n_tokens // GRID_W
    row = jnp.broadcast_to(jnp.arange(rows)[:, None], (rows, GRID_W)).reshape(-1).astype(F32)
    col = jnp.broadcast_to(jnp.arange(GRID_W)[None, :], (rows, GRID_W)).reshape(-1).astype(F32)
    n_freq = n_rot // 4
    inv = ROPE_THETA ** (-jnp.arange(n_freq, dtype=F32) / n_freq)
    ang = jnp.concatenate([row[:, None] * inv, col[:, None] * inv], axis=-1)
    return jnp.cos(ang), jnp.sin(ang)


def _rope_tables_full(s, n_ctx):
    cos, sin = _axial_tables(s, HEAD_DIM)
    t0 = jnp.concatenate([jnp.concatenate([cos, cos], axis=1), jnp.ones((n_ctx, LANE), F32)], axis=0)
    t1 = jnp.concatenate([jnp.concatenate([-sin, sin], axis=1), jnp.zeros((n_ctx, LANE), F32)], axis=0)
    return t0, t1


def _rope_tables_half(s, n_ctx):
    cos, sin = _axial_tables(s, ROPE_DIM)
    q = ROPE_DIM // 2
    zq = jnp.zeros((s, q), F32)
    z2 = jnp.zeros((s, LANE - ROPE_DIM), F32)
    t0 = jnp.concatenate([cos, cos, z2], axis=1)
    t1 = jnp.concatenate([-sin, zq, z2], axis=1)
    t2 = jnp.concatenate([zq, sin, z2], axis=1)
    c0 = jnp.concatenate([jnp.ones((n_ctx, ROPE_DIM), F32), jnp.zeros((n_ctx, LANE - ROPE_DIM), F32)], axis=1)
    cz = jnp.zeros((n_ctx, LANE), F32)
    return jnp.concatenate([t0, c0], axis=0), jnp.concatenate([t1, cz], axis=0), jnp.concatenate([t2, cz], axis=0)


def _filter_features(l):
    pos = jnp.arange(l, dtype=F32)
    t = pos / max(l - 1, 1)
    bands = jnp.linspace(1e-4, HYENA_BANDS - 1, HYENA_BANDS, dtype=F32)
    ang = (2.0 * math.pi / l) * pos[:, None] * bands[None, :]
    z = jnp.concatenate([t[:, None], jnp.cos(ang), -jnp.sin(ang)], axis=-1)
    return jnp.pad(z, ((0, 0), (0, LANE - HYENA_EMB)))


Q_STACK = 1024
KV_TILE_CAP = 1408


def _attention_pair(qp, kp, vt, s, n_ctx, *, kv_heads, groups, window, sink=None):
    common = dict(kv_heads=kv_heads, groups=groups, sink=sink)
    o_ctx = flash(qp, kp, vt, n_q=n_ctx, q_row0=s, kv_row0=s, n_kv_rows=n_ctx, tq=n_ctx, tk=n_ctx, **common)
    if window:
        o_lat = window_attention(qp, kp, vt, kv_heads=kv_heads, groups=groups, n_lat=s, n_ctx=n_ctx, tq=ROW_TILE,
                                 sink=sink)
    else:
        o_lat = flash(qp, kp, vt, n_q=s, q_row0=0, kv_row0=0, n_kv_rows=n_ctx + s, tq=Q_STACK // groups,
                      tk=_pick_tile(n_ctx + s, KV_TILE_CAP), **common)
    return o_ctx, o_lat


def _even_layer(u, s, n_ctx, w_in, qn_g, kn_g, sink, conv_w, conv_b, fparams, hy_bias, rope_full, feats, dft):
    w_all, e = w_in
    src_gate = ATT_WIDTH + 2 * KV_WIDTH + 3 * B_WIDTH
    proj = matmul_permuted(u, w_all, e, [(src_gate, BRANCH), (0, ATT_WIDTH), (ATT_WIDTH, 2 * KV_WIDTH),
                                         (ATT_WIDTH + 2 * KV_WIDTH, 3 * B_WIDTH)], F32)
    qp = head_prep(proj, OFF_Q, A_HEADS, qn_g, rope_full, HEAD_DIM ** -0.5 * LOG2E, True)
    kp = head_prep(proj, OFF_K, A_KV_HEADS, kn_g, rope_full, 1.0)
    vt = transpose_heads(proj, OFF_V, A_KV_HEADS)
    a_ctx, a_lat = _attention_pair(qp, kp, vt, s, n_ctx, kv_heads=A_KV_HEADS, groups=A_HEADS // A_KV_HEADS,
                                   window=True, sink=sink)
    w1, b1, f1, w2, b2, f2, w3 = fparams
    w1p = jnp.pad(w1, ((0, LANE - HYENA_EMB), (0, 0)))
    w3 = w3.astype(BF16)
    deltas = jnp.linspace(DECAY_MAX, DECAY_MIN, B_WIDTH, dtype=F32)
    uc3, ul3 = conv3(proj, OFF_REST, conv_w, conv_b, n_ctx)
    a = int(round(math.sqrt(2 * s)))
    consts, (mf, mi) = dft
    h_lat, tap_lat, nrm_lat = hyena_filter(feats[0], w1p, b1, f1, w2, b2, f2, w3, deltas)
    kf_lat = filter_spectrum(h_lat, tap_lat, nrm_lat, consts, a)
    b_lat = long_conv_gated(ul3, kf_lat, hy_bias, consts, a)
    h_ctx, tap_ctx, nrm_ctx = hyena_filter(feats[1], w1p, b1, f1, w2, b2, f2, w3, deltas)
    hf_ctx = ctx_filter_spectrum(h_ctx, mf)
    b_ctx = ctx_hyena(uc3, hf_ctx, tap_ctx, nrm_ctx, hy_bias, mf, mi)
    return proj, a_ctx, a_lat, b_ctx, b_lat


def _odd_layer(u, s, n_ctx, w_in, qn_g, kn_g, cq_g, ckv_g, wuq, wukv, mq_g, mk_g, rope_full, rope_half):
    d = w_in.shape[0]
    parts = jnp.split(w_in, np.cumsum([ATT_WIDTH, KV_WIDTH, KV_WIDTH, Q_LORA, KV_LORA, ROPE_DIM]).tolist(), axis=1)
    wq, wk, wv, wmq, wmkv, wmkr, wg = parts
    w_perm = jnp.concatenate([wg, wq, wk, wv, wmq, wmkv, wmkr, jnp.zeros((d, LANE - ROPE_DIM), w_in.dtype)], axis=1)
    proj = matmul(u, w_perm.astype(BF16), F32)
    qp = head_prep(proj, OFF_Q, C_HEADS, qn_g, rope_full, HEAD_DIM ** -0.5 * LOG2E, True)
    kp = head_prep(proj, OFF_K, C_KV_HEADS, kn_g, rope_full, 1.0)
    vt = transpose_heads(proj, OFF_V, C_KV_HEADS)
    c_ctx, c_lat = _attention_pair(qp, kp, vt, s, n_ctx, kv_heads=C_KV_HEADS, groups=C_HEADS // C_KV_HEADS,
                                   window=False)
    pad_r = LANE - ROPE_DIM
    wuq3 = wuq.reshape(Q_LORA, M_HEADS, QK_DIM)
    wuq_p = jnp.concatenate([wuq3[:, :, :NOPE_DIM].reshape(Q_LORA, -1),
                             jnp.pad(wuq3[:, :, NOPE_DIM:], ((0, 0), (0, 0), (0, pad_r))).reshape(Q_LORA, -1)], axis=1)
    wukv3 = wukv.reshape(KV_LORA, M_HEADS, NOPE_DIM + V_DIM)
    wukv_p = jnp.concatenate([wukv3[:, :, :NOPE_DIM].reshape(KV_LORA, -1),
                              wukv3[:, :, NOPE_DIM:].reshape(KV_LORA, -1)], axis=1)
    q_raw = norm_matmul(proj, OFF_MQ, cq_g, wuq_p.astype(BF16))
    kv_raw = norm_matmul(proj, OFF_MKV, ckv_g, wukv_p.astype(BF16))
    scale = QK_DIM ** -0.5 * LOG2E
    gq_r = jnp.pad(mq_g[NOPE_DIM:], (0, pad_r))
    gk_r = jnp.pad(mk_g[NOPE_DIM:], (0, pad_r))
    q_full = mla_prep(q_raw, 0, q_raw, M_HEADS * NOPE_DIM, False, M_HEADS, mq_g[:NOPE_DIM], gq_r, rope_half, scale,
                      True)
    k_full = mla_prep(kv_raw, 0, proj, OFF_MKR, True, M_HEADS, mk_g[:NOPE_DIM], gk_r, rope_half, 1.0, False)
    vmt = transpose_heads(kv_raw, M_HEADS * NOPE_DIM, M_HEADS)
    d_ctx, d_lat = _attention_pair(q_full, k_full, vmt, s, n_ctx, kv_heads=M_HEADS, groups=1, window=False)
    return proj, c_ctx, c_lat, d_ctx, d_lat


def kernel(x, c, ctx, c_ctx, ada_w, ada_b, norm_g, w_out, ev_w_in, ev_qn_g, ev_kn_g, ev_sink, ev_conv_w, ev_conv_b, hy_w1, hy_b1, hy_f1, hy_w2, hy_b2, hy_f2, hy_w3, hy_bias, od_w_in, od_qn_g, od_kn_g, od_cq_g, od_ckv_g, od_wuq, od_wukv, od_mq_g, od_mk_g):
    _, s, d = x.shape
    n_ctx = ctx.shape[1]
    depth = ada_w.shape[0]
    h = jnp.concatenate([x[0], ctx[0]], axis=0)
    cc = jnp.zeros((8, d), F32).at[0].set(c[0]).at[1].set(c_ctx)
    mod = ada_mod(cc, ada_w, ada_b)
    rope_full = _rope_tables_full(s, n_ctx)
    rope_half = _rope_tables_half(s, n_ctx)
    feats = (_filter_features(s), _filter_features(n_ctx))
    as_arrays = lambda mats: tuple(jnp.asarray(m) for m in mats)
    dft = (as_arrays(_dft_consts(int(round(math.sqrt(2 * s))))), as_arrays(_dense_dft_consts(n_ctx)))
    u = norm_mod(h, norm_g[0], mod[0], n_ctx)
    for i in range(depth):
        if i % 2 == 0:
            e = i // 2
            proj, o1c, o1l, o2c, o2l = _even_layer(
                u, s, n_ctx, (ev_w_in, e), ev_qn_g[e], ev_kn_g[e], ev_sink[e], ev_conv_w[e], ev_conv_b[e],
                (hy_w1[e], hy_b1[e], hy_f1[e], hy_w2[e], hy_b2[e], hy_f2[e], hy_w3[e]), hy_bias[e], rope_full, feats,
                dft)
        else:
            o = i // 2
            proj, o1c, o1l, o2c, o2l = _odd_layer(
                u, s, n_ctx, od_w_in[o], od_qn_g[o], od_kn_g[o], od_cq_g[o], od_ckv_g[o], od_wuq[o], od_wukv[o],
                od_mq_g[o], od_mk_g[o], rope_full, rope_half)
        w = w_out[i].astype(BF16)
        if i < depth - 1:
            h, u = out_proj(o1c, o1l, o2c, o2l, proj, h, mod[i], w, (norm_g[i + 1], mod[i + 1]))
        else:
            h = out_proj(o1c, o1l, o2c, o2l, proj, h, mod[i], w, None)
    return h[None]
```

```python
import functools
import math

import numpy as np
import jax
import jax.numpy as jnp
from jax import lax
from jax.experimental import pallas as pl
from jax.experimental.pallas import tpu as pltpu

F32 = jnp.float32
BF16 = jnp.bfloat16

GRID_W = 64
HEAD_DIM = 128
ROPE_THETA = 10000.0
EPS = 1e-6
NEG_INF = -1e30
A_HEADS, A_KV_HEADS, WINDOW = 8, 2, 128
B_WIDTH = 1024
HYENA_BANDS = 16
HYENA_EMB = 1 + 2 * HYENA_BANDS
DECAY_TARGET = 1e-2
DECAY_MAX = abs(math.log(DECAY_TARGET)) / 0.3
DECAY_MIN = abs(math.log(DECAY_TARGET)) / 1.5
C_HEADS, C_KV_HEADS = 8, 2
M_HEADS, Q_LORA, KV_LORA, NOPE_DIM, ROPE_DIM, V_DIM = 8, 512, 256, 128, 64, 128
QK_DIM = NOPE_DIM + ROPE_DIM
ATT_WIDTH = A_HEADS * HEAD_DIM
KV_WIDTH = A_KV_HEADS * HEAD_DIM
BRANCH = ATT_WIDTH + B_WIDTH

LOG2E = math.log2(math.e)
LANE = 128
SUBLANES = 8
ROWS_BF16 = 16
ROW_TILE = 256
VMEM_LIMIT = 48 * 1024 * 1024

OFF_GATE, OFF_Q, OFF_K, OFF_V, OFF_REST = 0, BRANCH, BRANCH + ATT_WIDTH, BRANCH + ATT_WIDTH + KV_WIDTH, BRANCH + ATT_WIDTH + 2 * KV_WIDTH
OFF_MQ, OFF_MKV, OFF_MKR = OFF_REST, OFF_REST + Q_LORA, OFF_REST + Q_LORA + KV_LORA


def _params(*sem):
    return pltpu.CompilerParams(dimension_semantics=sem, vmem_limit_bytes=VMEM_LIMIT)


def _split_bf16(x):
    hi = x.astype(BF16)
    return hi, (x - hi.astype(F32)).astype(BF16)


def _dot(a, b):
    return jnp.dot(a, b, preferred_element_type=F32)


def _dot3(a_hi, a_lo, b_hi, b_lo):
    return _dot(a_hi, b_hi) + (_dot(a_hi, b_lo) + _dot(a_lo, b_hi))


def _ada_kernel(c_ref, w_ref, b_ref, o_ref):
    c = c_ref[...]
    a = (c * jax.nn.sigmoid(c)).astype(BF16)
    o_ref[0] = _dot(a, w_ref[0].astype(BF16)) + b_ref[0]


def ada_mod(cc, ada_w, ada_b):
    depth, d, n = ada_w.shape
    tn = _pick_tile(n, 1024)
    return pl.pallas_call(
        _ada_kernel, grid=(depth, n // tn),
        in_specs=[pl.BlockSpec((8, d), lambda l, j: (0, 0)),
                  pl.BlockSpec((1, d, tn), lambda l, j: (l, 0, j)),
                  pl.BlockSpec((1, 1, tn), lambda l, j: (l, 0, j))],
        out_specs=pl.BlockSpec((1, 8, tn), lambda l, j: (l, 0, j)),
        out_shape=jax.ShapeDtypeStruct((depth, 8, n), F32),
        compiler_params=_params("parallel", "parallel"), name="ada_mod",
    )(cc, ada_w, ada_b.reshape(depth, 1, n))


def _modulated_norm(x, g, mod_ref, is_ctx):
    d = x.shape[1]
    y = x * lax.rsqrt(jnp.mean(x * x, axis=-1, keepdims=True) + EPS) * g
    sh = jnp.where(is_ctx, mod_ref[1:2, 0:d], mod_ref[0:1, 0:d])
    sc = jnp.where(is_ctx, mod_ref[1:2, d:2 * d], mod_ref[0:1, d:2 * d])
    return y * (1.0 + sc) + sh


def _norm_kernel(x_ref, g_ref, mod_ref, o_ref, *, lat_tiles):
    is_ctx = pl.program_id(0) >= lat_tiles
    o_ref[...] = _modulated_norm(x_ref[...], g_ref[...], mod_ref, is_ctx).astype(o_ref.dtype)


def norm_mod(h, g, mod, n_ctx):
    r, d = h.shape
    tm = ROW_TILE
    return pl.pallas_call(
        functools.partial(_norm_kernel, lat_tiles=(r - n_ctx) // tm), grid=(r // tm,),
        in_specs=[pl.BlockSpec((tm, d), lambda i: (i, 0)),
                  pl.BlockSpec((1, d), lambda i: (0, 0)),
                  pl.BlockSpec(mod.shape, lambda i: (0, 0))],
        out_specs=pl.BlockSpec((tm, d), lambda i: (i, 0)),
        out_shape=jax.ShapeDtypeStruct((r, d), BF16),
        compiler_params=_params("parallel"), name="norm_mod",
    )(h, g.reshape(1, d), mod)


def _mm_kernel(a_ref, b_ref, o_ref):
    o_ref[...] = _dot(a_ref[...], b_ref[...]).astype(o_ref.dtype)


def _pick_tile(n, cap, unit=LANE):
    best = unit
    for t in range(unit, cap + 1, unit):
        if n % t == 0:
            best = t
    return best


def matmul(a, b, out_dtype):
    m, k = a.shape
    n = b.shape[1]
    tm = _pick_tile(m, 1536, 8)
    tn = _pick_tile(n, 1024)
    return pl.pallas_call(
        _mm_kernel, grid=(m // tm, n // tn),
        in_specs=[pl.BlockSpec((tm, k), lambda i, j: (i, 0)),
                  pl.BlockSpec((k, tn), lambda i, j: (0, j))],
        out_specs=pl.BlockSpec((tm, tn), lambda i, j: (i, j)),
        out_shape=jax.ShapeDtypeStruct((m, n), out_dtype),
        compiler_params=_params("parallel", "parallel"), name="in_proj",
    )(a, b)


def _mm_cast_kernel(a_ref, b_ref, o_ref):
    o_ref[...] = _dot(a_ref[...], b_ref[0].astype(a_ref.dtype)).astype(o_ref.dtype)


def matmul_permuted(a, w3, layer, segments, out_dtype):
    m, k = a.shape
    n = sum(width for _, width in segments)
    tm = _pick_tile(m, 2112, 8)
    tn = LANE
    for t in range(LANE, 1024 + 1, LANE):
        if all(start % t == 0 and width % t == 0 for start, width in segments):
            tn = t
    src_tiles = [(start + off) // tn for start, width in segments for off in range(0, width, tn)]

    def src(j):
        idx = src_tiles[0]
        for t, s_t in enumerate(src_tiles[1:], 1):
            idx = jnp.where(j == t, s_t, idx)
        return idx
    return pl.pallas_call(
        _mm_cast_kernel, grid=(m // tm, n // tn),
        in_specs=[pl.BlockSpec((tm, k), lambda i, j: (i, 0)),
                  pl.BlockSpec((1, k, tn), lambda i, j: (layer, 0, src(j)))],
        out_specs=pl.BlockSpec((tm, tn), lambda i, j: (i, j)),
        out_shape=jax.ShapeDtypeStruct((m, n), out_dtype),
        compiler_params=_params("parallel", "parallel"), name="in_proj",
    )(a, w3)


def _normmm_kernel(x_ref, g_ref, w_ref, o_ref):
    x = x_ref[...].astype(F32)
    y = x * lax.rsqrt(jnp.mean(x * x, axis=-1, keepdims=True) + EPS) * g_ref[...]
    o_ref[...] = _dot(y.astype(BF16), w_ref[...]).astype(o_ref.dtype)


def norm_matmul(x, col_off, g, w):
    r = x.shape[0]
    k, n = w.shape
    tm = _pick_tile(r, 768, 8)
    cb = col_off // k
    return pl.pallas_call(
        _normmm_kernel, grid=(r // tm,),
        in_specs=[pl.BlockSpec((tm, k), lambda i: (i, cb)),
                  pl.BlockSpec((1, k), lambda i: (0, 0)),
                  pl.BlockSpec((k, n), lambda i: (0, 0))],
        out_specs=pl.BlockSpec((tm, n), lambda i: (i, 0)),
        out_shape=jax.ShapeDtypeStruct((r, n), F32),
        compiler_params=_params("parallel"), name="norm_matmul",
    )(x, g.reshape(1, k), w)


def _prep_kernel(x_ref, g_ref, t0_ref, t1_ref, o_ref, *, scale, n_heads, head_major):
    g, t0, t1 = g_ref[...], t0_ref[...], t1_ref[...]
    for h in range(n_heads):
        x = x_ref[:, h * LANE:(h + 1) * LANE].astype(F32)
        xn = x * lax.rsqrt(jnp.mean(x * x, axis=-1, keepdims=True) + EPS) * g
        y = ((xn * t0 + pltpu.roll(xn, HEAD_DIM // 2, 1) * t1) * scale).astype(o_ref.dtype)
        if head_major:
            o_ref[h] = y
        else:
            o_ref[:, h * LANE:(h + 1) * LANE] = y


def head_prep(x, col_off, n_heads, g, tables, scale, head_major=False):
    r = x.shape[0]
    w = n_heads * LANE
    tm = _pick_tile(r, 768, 8)
    tab_spec = pl.BlockSpec((tm, LANE), lambda i: (i, 0))
    if head_major:
        out_spec = pl.BlockSpec((n_heads, tm, LANE), lambda i: (0, i, 0))
        out_shape = jax.ShapeDtypeStruct((n_heads, r, LANE), BF16)
    else:
        out_spec = pl.BlockSpec((tm, w), lambda i: (i, 0))
        out_shape = jax.ShapeDtypeStruct((r, w), BF16)
    return pl.pallas_call(
        functools.partial(_prep_kernel, scale=scale, n_heads=n_heads, head_major=head_major),
        grid=(r // tm,),
        in_specs=[pl.BlockSpec((tm, w), lambda i: (i, col_off // w)),
                  pl.BlockSpec((1, LANE), lambda i: (0, 0)), tab_spec, tab_spec],
        out_specs=out_spec, out_shape=out_shape,
        compiler_params=_params("parallel"), name="head_prep",
    )(x, g.reshape(1, LANE), *tables)


def _mla_prep_kernel(xn_ref, xr_ref, gn_ref, gr_ref, t0_ref, t1_ref, t2_ref, o_ref, *, n_heads, shared_rope, scale,
                     head_major):
    gn, gr = gn_ref[...], gr_ref[...]
    t0, t1, t2 = t0_ref[...], t1_ref[...], t2_ref[...]

    def norm(x, n_real, g):
        return x * lax.rsqrt(jnp.sum(x * x, axis=-1, keepdims=True) * (1.0 / n_real) + EPS) * g

    def rope_part(x):
        xn = norm(x.astype(F32), ROPE_DIM, gr)
        return xn * t0 + pltpu.roll(xn, 96, 1) * t1 + pltpu.roll(xn, 32, 1) * t2

    if shared_rope:
        shared = rope_part(xr_ref[...])
    for h in range(n_heads):
        cols = slice(h * LANE, (h + 1) * LANE)
        a = norm(xn_ref[:, cols].astype(F32), NOPE_DIM, gn)
        b = shared if shared_rope else rope_part(xr_ref[:, cols])
        y = (jnp.concatenate([a, b], axis=1) * scale).astype(o_ref.dtype)
        if head_major:
            o_ref[h] = y
        else:
            o_ref[:, 2 * h * LANE:2 * (h + 1) * LANE] = y


def mla_prep(xn, xn_off, xr, xr_off, shared_rope, n_heads, gn, gr, tables, scale, head_major):
    r = xn.shape[0]
    w = n_heads * LANE
    wr = LANE if shared_rope else w
    tm = _pick_tile(r, 768, 8)
    row = lambda width: pl.BlockSpec((1, width), lambda i: (0, 0))
    tab = pl.BlockSpec((tm, LANE), lambda i: (i, 0))
    if head_major:
        out_spec = pl.BlockSpec((n_heads, tm, 2 * LANE), lambda i: (0, i, 0))
        out_shape = jax.ShapeDtypeStruct((n_heads, r, 2 * LANE), BF16)
    else:
        out_spec = pl.BlockSpec((tm, 2 * w), lambda i: (i, 0))
        out_shape = jax.ShapeDtypeStruct((r, 2 * w), BF16)
    return pl.pallas_call(
        functools.partial(_mla_prep_kernel, n_heads=n_heads, shared_rope=shared_rope, scale=scale,
                          head_major=head_major),
        grid=(r // tm,),
        in_specs=[pl.BlockSpec((tm, w), lambda i: (i, xn_off // w)), pl.BlockSpec((tm, wr), lambda i: (i, xr_off // wr)),
                  row(LANE), row(LANE), tab, tab, tab],
        out_specs=out_spec, out_shape=out_shape,
        compiler_params=_params("parallel"), name="mla_prep",
    )(xn, xr, gn.reshape(1, LANE), gr.reshape(1, LANE), *tables)


def _vt_kernel(x_ref, o_ref):
    o_ref[...] = x_ref[...].astype(F32).T.astype(o_ref.dtype)


def transpose_heads(x, col_off, n_heads):
    r = x.shape[0]
    w = n_heads * LANE
    tm = _pick_tile(r, 768, LANE)
    return pl.pallas_call(
        _vt_kernel, grid=(r // tm,),
        in_specs=[pl.BlockSpec((tm, w), lambda i: (i, col_off // w))],
        out_specs=pl.BlockSpec((w, tm), lambda i: (0, i)),
        out_shape=jax.ShapeDtypeStruct((w, r), BF16),
        compiler_params=_params("parallel"), name="transpose_heads",
    )(x)


def _flash_kernel(*refs, groups, tq, tk, dv, n_kv, chain, has_sink, key_chunk):
    q_ref, qn_ref, k_ref, kn_ref, vt_ref = refs[0:5]
    pos = 5
    if has_sink:
        sink_ref = refs[pos]
        pos += 1
    o_ref, m_ref, l_ref, acc_ref, s_ref, mt_ref = refs[pos:pos + 6]
    i, j = pl.program_id(1), pl.program_id(2)
    n = groups * tq
    bounds = [(r0, min(r0 + key_chunk, tk)) for r0 in range(0, tk, key_chunk)]

    def stacked(qr):
        return qr[...].reshape(n, qr.shape[2])

    def scores(kr, q, r0, r1):
        return lax.dot_general(kr[r0:r1, :], q, (((1,), (1,)), ((), ())), preferred_element_type=F32)

    @pl.when(j == 0)
    def _():
        m_ref[...] = jnp.full(m_ref.shape, NEG_INF, F32)
        l_ref[...] = jnp.zeros(l_ref.shape, F32)
        acc_ref[...] = jnp.zeros(acc_ref.shape, F32)

    @pl.when((j == 0) & (i == 0) if chain else j == 0)
    def _():
        q = stacked(q_ref)
        mt = jnp.full(mt_ref.shape, NEG_INF, F32)
        for r0, r1 in bounds:
            s = scores(k_ref, q, r0, r1)
            s_ref[r0:r1, :] = s
            mt = jnp.maximum(mt, jnp.max(s, axis=0, keepdims=True))
        mt_ref[...] = mt

    def step(kr, qr):
        prefetch = kr is not None
        if prefetch:
            q = stacked(qr)
        m_prev = m_ref[...]
        m_new = jnp.maximum(m_prev, mt_ref[...])
        alpha = jnp.exp2(m_prev - m_new)
        l_new = alpha * l_ref[...]
        acc_new = alpha * acc_ref[...]
        mt = jnp.full(mt_ref.shape, NEG_INF, F32)
        for r0, r1 in bounds:
            if prefetch:
                s_next = scores(kr, q, r0, r1)
            p = jnp.exp2(s_ref[r0:r1, :] - m_new)
            l_new = l_new + jnp.sum(p, axis=0, keepdims=True)
            acc_new = acc_new + _dot(vt_ref[:, r0:r1], p.astype(BF16))
            if prefetch:
                s_ref[r0:r1, :] = s_next
                mt = jnp.maximum(mt, jnp.max(s_next, axis=0, keepdims=True))
        l_ref[...] = l_new
        acc_ref[...] = acc_new
        m_ref[...] = m_new
        if prefetch:
            mt_ref[...] = mt

    if n_kv > 1:
        pl.when(j < n_kv - 1)(lambda: step(kn_ref, q_ref))

    @pl.when(j == n_kv - 1)
    def _():
        if chain:
            step(k_ref, qn_ref)
        else:
            step(None, None)
        m, l, acc = m_ref[...], l_ref[...], acc_ref[...]
        if has_sink:
            sk = sink_ref[0]
            m_fin = jnp.maximum(m, sk)
            a = jnp.exp2(m - m_fin)
            l = l * a + jnp.exp2(sk - m_fin)
            acc = acc * a
        o_t = acc / l
        for g in range(groups):
            o_ref[:, g * dv:(g + 1) * dv] = o_t[:, g * tq:(g + 1) * tq].T.astype(o_ref.dtype)


KEY_CHUNK = 256


def flash(q, k, vt, *, kv_heads, groups, n_q, q_row0, kv_row0, n_kv_rows, tq, tk, dv=HEAD_DIM, sink=None):
    dk = q.shape[2]
    qb0 = q_row0 // tq
    kb0 = kv_row0 // tk
    n_kv = n_kv_rows // tk

    def kv_row(i, j):
        return kb0 + j
    n = groups * tq
    n_qb = n_q // tq
    nxt = lambda j: jnp.minimum(j + 1, n_kv - 1)
    q_nxt = lambda i: qb0 + jnp.minimum(i + 1, n_qb - 1)
    in_specs = [pl.BlockSpec((groups, tq, dk), lambda h, i, j: (h, qb0 + i, 0)),
                pl.BlockSpec((groups, tq, dk), lambda h, i, j: (h, q_nxt(i), 0)),
                pl.BlockSpec((tk, dk), lambda h, i, j: (kv_row(i, 0), h)),
                pl.BlockSpec((tk, dk), lambda h, i, j: (kv_row(i, nxt(j)), h)),
                pl.BlockSpec((dv, tk), lambda h, i, j: (h, kv_row(i, j)))]
    args = [q, q, k, k, vt]
    if sink is not None:
        in_specs.append(pl.BlockSpec((1, 1, n), lambda h, i, j: (h, 0, 0)))
        sink2 = sink.astype(F32).reshape(kv_heads, groups, 1) * LOG2E
        args.append(jnp.broadcast_to(sink2, (kv_heads, groups, tq)).reshape(kv_heads, 1, n))
    kern = functools.partial(
        _flash_kernel, groups=groups, tq=tq, tk=tk, dv=dv, n_kv=n_kv, chain=n_qb > 1, has_sink=sink is not None,
        key_chunk=KEY_CHUNK)
    return pl.pallas_call(
        kern, grid=(kv_heads, n_q // tq, n_kv), in_specs=in_specs,
        out_specs=pl.BlockSpec((tq, groups * dv), lambda h, i, j: (i, h)),
        out_shape=jax.ShapeDtypeStruct((n_q, kv_heads * groups * dv), BF16),
        scratch_shapes=[pltpu.VMEM((1, n), F32), pltpu.VMEM((1, n), F32), pltpu.VMEM((dv, n), F32),
                        pltpu.VMEM((tk, n), F32), pltpu.VMEM((1, n), F32)],
        compiler_params=_params("parallel", "arbitrary", "arbitrary"),
        name="flash_dense",
    )(*args)


def _window_kernel(*refs, groups, tq, n_lat, has_sink):
    q_ref = refs[0]
    n_half = tq // WINDOW + 2
    k_refs = refs[1:2 + n_half]
    vt_refs = refs[2 + n_half:3 + 2 * n_half]
    pos = 3 + 2 * n_half
    if has_sink:
        sink_ref = refs[pos]
        pos += 1
    o_ref = refs[pos]
    i = pl.program_id(1)
    n = groups * tq
    dv = vt_refs[0].shape[0]
    q = q_ref[...].reshape(n, q_ref.shape[2])
    qpos = i * tq + lax.broadcasted_iota(jnp.int32, (1, n), 1) % tq
    scores = []
    m = sink_ref[0] if has_sink else jnp.full((1, n), NEG_INF, F32)
    for b, k_ref in enumerate(k_refs):
        s = lax.dot_general(k_ref[...], q, (((1,), (1,)), ((), ())), preferred_element_type=F32)
        if b > 0:
            kpos = i * tq + (b - 2) * WINDOW + lax.broadcasted_iota(jnp.int32, (WINDOW, 1), 0)
            valid = (jnp.abs(qpos - kpos) <= WINDOW) & (kpos >= 0) & (kpos < n_lat)
            s = jnp.where(valid, s, NEG_INF)
        scores.append(s)
        m = jnp.maximum(m, jnp.max(s, axis=0, keepdims=True))
    l = jnp.exp2(sink_ref[0] - m) if has_sink else jnp.zeros((1, n), F32)
    acc = jnp.zeros((dv, n), F32)
    for s, vt_ref in zip(scores, vt_refs):
        p = jnp.exp2(s - m)
        l = l + jnp.sum(p, axis=0, keepdims=True)
        acc = acc + _dot(vt_ref[...], p.astype(BF16))
    o_t = acc / l
    for g in range(groups):
        o_ref[:, g * dv:(g + 1) * dv] = o_t[:, g * tq:(g + 1) * tq].T.astype(o_ref.dtype)


def window_attention(q, k, vt, *, kv_heads, groups, n_lat, n_ctx, tq, sink=None, dv=HEAD_DIM):
    dk = q.shape[2]
    per = tq // WINDOW
    n_half_blocks = n_lat // WINDOW
    n = groups * tq
    ctx_blk = n_lat // n_ctx
    half = lambda b: (lambda i: jnp.clip(i * per + b - 1, 0, n_half_blocks - 1))
    halves = [half(b) for b in range(per + 2)]
    in_specs = [pl.BlockSpec((groups, tq, dk), lambda h, i: (h, i, 0)),
                pl.BlockSpec((n_ctx, dk), lambda h, i: (ctx_blk, h))]
    in_specs += [pl.BlockSpec((WINDOW, dk), lambda h, i, f=f: (f(i), h)) for f in halves]
    in_specs += [pl.BlockSpec((dv, n_ctx), lambda h, i: (h, ctx_blk))]
    in_specs += [pl.BlockSpec((dv, WINDOW), lambda h, i, f=f: (h, f(i))) for f in halves]
    args = [q] + [k] * (per + 3) + [vt] * (per + 3)
    if sink is not None:
        in_specs.append(pl.BlockSpec((1, 1, n), lambda h, i: (h, 0, 0)))
        sink2 = sink.astype(F32).reshape(kv_heads, groups, 1) * LOG2E
        args.append(jnp.broadcast_to(sink2, (kv_heads, groups, tq)).reshape(kv_heads, 1, n))
    return pl.pallas_call(
        functools.partial(_window_kernel, groups=groups, tq=tq, n_lat=n_lat, has_sink=sink is not None),
        grid=(kv_heads, n_lat // tq), in_specs=in_specs,
        out_specs=pl.BlockSpec((tq, groups * dv), lambda h, i: (i, h)),
        out_shape=jax.ShapeDtypeStruct((n_lat, kv_heads * groups * dv), BF16),
        compiler_params=_params("parallel", "parallel"), name="window_attention",
    )(*args)


def _outproj_kernel(*refs, lat_tiles, with_next):
    o1c_ref, o1l_ref, o2c_ref, o2l_ref, gate_ref, h_ref, mod_ref, w_ref = refs[:8]
    is_ctx = pl.program_id(0) >= lat_tiles
    w1 = o1l_ref.shape[1]
    d = h_ref.shape[1]
    gt = gate_ref[...].astype(F32)
    sg = gt * jax.nn.sigmoid(gt)
    o1 = jnp.where(is_ctx, o1c_ref[...].astype(F32), o1l_ref[...].astype(F32))
    o2 = jnp.where(is_ctx, o2c_ref[...].astype(F32), o2l_ref[...].astype(F32))
    t1 = (o1 * sg[:, :w1]).astype(BF16)
    t2 = (o2 * sg[:, w1:]).astype(BF16)
    y = _dot(t1, w_ref[0:w1, :]) + _dot(t2, w_ref[w1:, :])
    gm = jnp.where(is_ctx, mod_ref[1:2, 2 * d:3 * d], mod_ref[0:1, 2 * d:3 * d])
    h_new = h_ref[...] + gm * y
    if with_next:
        g_next_ref, mod_next_ref, out_ref, u_ref = refs[8:]
        u_ref[...] = _modulated_norm(h_new, g_next_ref[...], mod_next_ref, is_ctx).astype(u_ref.dtype)
    else:
        out_ref = refs[8]
    out_ref[...] = h_new


def out_proj(o1c, o1l, o2c, o2l, proj, h, mod, w, nxt):
    r, d = h.shape
    tm = ROW_TILE
    assert o1c.shape[0] == tm, "the context rows must be exactly one row tile"
    lt = o1l.shape[0] // tm
    w1, w2 = o1l.shape[1], o2l.shape[1]
    lat = lambda i: (jnp.minimum(i, lt - 1), 0)
    row = pl.BlockSpec((tm, d), lambda i: (i, 0))
    in_specs = [pl.BlockSpec((tm, w1), lambda i: (0, 0)), pl.BlockSpec((tm, w1), lat),
                pl.BlockSpec((tm, w2), lambda i: (0, 0)), pl.BlockSpec((tm, w2), lat),
                pl.BlockSpec((tm, w1 + w2), lambda i: (i, OFF_GATE // (w1 + w2))),
                row, pl.BlockSpec(mod.shape, lambda i: (0, 0)), pl.BlockSpec(w.shape, lambda i: (0, 0))]
    args = [o1c, o1l, o2c, o2l, proj, h, mod, w]
    if nxt is None:
        r = lt * tm
        out_specs, out_shape = row, jax.ShapeDtypeStruct((r, d), F32)
    else:
        in_specs += [pl.BlockSpec((1, d), lambda i: (0, 0)), pl.BlockSpec(nxt[1].shape, lambda i: (0, 0))]
        args += [nxt[0].reshape(1, d), nxt[1]]
        out_specs = [row, row]
        out_shape = [jax.ShapeDtypeStruct((r, d), F32), jax.ShapeDtypeStruct((r, d), BF16)]
    return pl.pallas_call(
        functools.partial(_outproj_kernel, lat_tiles=lt, with_next=nxt is not None), grid=(r // tm,),
        in_specs=in_specs, out_specs=out_specs, out_shape=out_shape,
        compiler_params=_params("parallel"), name="out_proj",
    )(*args)


def _conv3_kernel(x_ref, w_ref, b_ref, oc_ref, ol_ref, *, n_ctx):
    x = x_ref[...].astype(F32)
    r = x.shape[0]
    n_lat = r - n_ctx
    row = lax.broadcasted_iota(jnp.int32, (r, 1), 0)
    prev = jnp.where((row == 0) | (row == n_lat), 0.0, pltpu.roll(x, 1, 0))
    nxt = jnp.where((row == n_lat - 1) | (row == r - 1), 0.0, pltpu.roll(x, r - 1, 0))
    y = prev * w_ref[0:1, :] + x * w_ref[1:2, :] + nxt * w_ref[2:3, :] + b_ref[...]
    ol_ref[0] = y[:n_lat]
    oc_ref[0] = y[n_lat:]


def conv3(proj, col_off, w, b, n_ctx):
    r = proj.shape[0]
    cw = w.shape[1] // 3
    nb = cw // LANE
    cb = col_off // LANE
    return pl.pallas_call(
        functools.partial(_conv3_kernel, n_ctx=n_ctx), grid=(3, nb),
        in_specs=[pl.BlockSpec((r, LANE), lambda p, c: (0, cb + p * nb + c)),
                  pl.BlockSpec((3, LANE), lambda p, c: (0, p * nb + c)),
                  pl.BlockSpec((1, LANE), lambda p, c: (0, p * nb + c))],
        out_specs=[pl.BlockSpec((1, n_ctx, LANE), lambda p, c: (p, 0, c)),
                   pl.BlockSpec((1, r - n_ctx, LANE), lambda p, c: (p, 0, c))],
        out_shape=[jax.ShapeDtypeStruct((3, n_ctx, cw), F32), jax.ShapeDtypeStruct((3, r - n_ctx, cw), F32)],
        compiler_params=_params("parallel", "parallel"), name="conv3",
    )(proj, w, b.reshape(1, -1))


def _filt_kernel(z_ref, w1_ref, b1_ref, f1_ref, w2_ref, b2_ref, f2_ref, w3_ref, dl_ref, h_ref, tap_ref, nrm_ref, *,
                 tm, cw):
    i = pl.program_id(0)
    z = z_ref[...]

    def mm(a, w_r):
        return _dot3(*_split_bf16(a), *_split_bf16(w_r[...]))

    a = jnp.sin(f1_ref[...] * (mm(z, w1_ref) + b1_ref[...]))
    a = jnp.sin(f2_ref[...] * (mm(a, w2_ref) + b2_ref[...]))
    dec = jnp.exp(-z[:, 0:1] * dl_ref[...])
    h = _dot(a.astype(BF16), w3_ref[...]) * jnp.concatenate([dec] * (h_ref.shape[1] // cw), axis=1)
    h_ref[...] = h.astype(h_ref.dtype)
    row = i * tm + lax.broadcasted_iota(jnp.int32, (tm, 1), 0)
    col = lax.broadcasted_iota(jnp.int32, (1, h.shape[1]), 1)
    skip = (row == 0) & ((col // cw) % 2 == 1)
    part = jnp.sum(jnp.where(skip, 0.0, jnp.abs(h)), axis=0, keepdims=True)

    @pl.when(i == 0)
    def _():
        nrm_ref[...] = jnp.zeros(nrm_ref.shape, F32)
        tap_ref[...] = h[0:SUBLANES]

    nrm_ref[...] += jnp.broadcast_to(part, nrm_ref.shape)


def hyena_filter(z, w1p, b1, f1, w2, b2, f2, w3, deltas):
    l = z.shape[0]
    n = w3.shape[1]
    cw = deltas.shape[0]
    hid = w2.shape[0]
    tm = min(l, 512)
    full = lambda a: pl.BlockSpec(a.shape, lambda i: (0, 0))
    ins = [w1p, b1.reshape(1, hid), f1.reshape(1, hid), w2, b2.reshape(1, hid), f2.reshape(1, hid), w3,
           deltas.reshape(1, cw)]
    small = pl.BlockSpec((SUBLANES, n), lambda i: (0, 0))
    return pl.pallas_call(
        functools.partial(_filt_kernel, tm=tm, cw=cw), grid=(l // tm,),
        in_specs=[pl.BlockSpec((tm, LANE), lambda i: (i, 0))] + [full(a) for a in ins],
        out_specs=[pl.BlockSpec((tm, n), lambda i: (i, 0)), small, small],
        out_shape=[jax.ShapeDtypeStruct((l, n), BF16), jax.ShapeDtypeStruct((SUBLANES, n), F32),
                   jax.ShapeDtypeStruct((SUBLANES, n), F32)],
        compiler_params=_params("arbitrary"), name="hyena_filter",
    )(z, *ins)


@functools.lru_cache(maxsize=None)
def _dft_consts(a):
    n = a * a
    hh = a // 2
    n1 = np.arange(hh)
    k1 = np.arange(hh)
    n2 = np.arange(a)
    k2 = np.arange(a)
    ang1 = 2 * np.pi * np.outer(k1 + 0.5, n1) / a
    m1 = np.zeros((hh, 2, hh))
    m1[:, 0], m1[:, 1] = np.cos(ang1), -np.sin(ang1)
    m1 = m1.reshape(2 * hh, hh)
    phi = 2 * np.pi * (n2[None, None, :] * (k1[:, None, None] + 0.5) / n + n2[None, None, :] * k2[None, :, None] / a)
    c, s = np.cos(phi), np.sin(phi)
    g = np.zeros((hh, 2, a, 2, a))
    g[:, 0, :, 0, :], g[:, 0, :, 1, :], g[:, 1, :, 0, :], g[:, 1, :, 1, :] = c, s, -s, c
    g = g.reshape(hh, 2 * a, 2 * a)
    gt = np.ascontiguousarray(np.transpose(g, (0, 2, 1)))
    al = 2 * np.pi * np.outer(n1, k1 + 0.5) / a
    mb = np.zeros((hh, hh, 2))
    mb[:, :, 0], mb[:, :, 1] = 2 / n * np.cos(al), -2 / n * np.sin(al)
    mb = mb.reshape(hh, 2 * hh)
    eye = np.eye(ROWS_BF16)
    return tuple(x.astype(np.float32) for x in (np.kron(m1, eye), g, gt, np.kron(mb, eye)))


@functools.lru_cache(maxsize=None)
def _dense_dft_consts(l):
    n = 2 * l
    ang = 2 * np.pi * np.outer(np.arange(l) + 0.5, np.arange(l)) / n
    mf = np.concatenate([np.cos(ang), -np.sin(ang)], axis=0)
    mi = np.concatenate([2 / n * np.cos(ang.T), -2 / n * np.sin(ang.T)], axis=1)
    return tuple(x.astype(np.float32) for x in (mf, mi))


def _stage1_kernel(x_ref, a_ref, o_ref, a16_ref):
    @pl.when((pl.program_id(0) == 0) & (pl.program_id(1) == 0))
    def _():
        a16_ref[...] = a_ref[...].astype(BF16)

    _, hh, t, ct = x_ref.shape
    x = x_ref[0].reshape(hh * t, ct).astype(BF16)
    o_ref[...] = _dot(a16_ref[...], x).astype(o_ref.dtype).reshape(o_ref.shape)


def dft_stage1(x4, p, a1):
    _, hh, a, c = x4.shape
    ct = _pick_tile(c, 512)
    t = ROWS_BF16
    return pl.pallas_call(
        _stage1_kernel, grid=(a // t, c // ct),
        in_specs=[pl.BlockSpec((1, hh, t, ct), lambda j, q: (p, 0, j, q)),
                  pl.BlockSpec(a1.shape, lambda j, q: (0, 0))],
        out_specs=pl.BlockSpec((2 * hh, t, ct), lambda j, q: (0, j, q)),
        out_shape=jax.ShapeDtypeStruct((2 * hh, a, c), BF16),
        scratch_shapes=[pltpu.VMEM(a1.shape, BF16)],
        compiler_params=_params("arbitrary", "arbitrary"), name="dft_stage1",
    )(x4, a1)


def _spectrum_product(x, h0, h1, tap0, n0, n1):
    half = x.shape[0] // 2
    inv = 1.0 / (n0 + n1)
    kr = (h0[:half] + h1[:half] - tap0) * inv
    ki = (h0[half:] - h1[half:]) * inv
    xr, xi = x[:half], x[half:]
    return jnp.concatenate([xr * kr - xi * ki, xr * ki + xi * kr], axis=0)


K1_PER_STEP = 4


def _mid_filter_kernel(y0_ref, y1_ref, g_ref, tap_ref, n0_ref, n1_ref, o_ref):
    inv = 1.0 / (n0_ref[0:1, :] + n1_ref[0:1, :])
    tap = tap_ref[0:1, :]
    for b in range(y0_ref.shape[0]):
        g = g_ref[b].astype(BF16)
        x0 = _dot(g, y0_ref[b])
        x1 = _dot(g, y1_ref[b])
        half = x0.shape[0] // 2
        kr = (x0[:half] + x1[:half] - tap) * inv
        ki = (x0[half:] - x1[half:]) * inv
        o_ref[b] = jnp.concatenate([kr, ki], axis=0).astype(o_ref.dtype)


def filter_spectrum(h, taps, nrm, consts, a):
    a1, g, _, _ = consts
    l, n = h.shape
    c = n // 4
    hh = a // 2
    y = dft_stage1(h.reshape(1, hh, a, n), 0, a1).reshape(hh, 2 * a, n)
    kb = K1_PER_STEP
    slab = lambda side: pl.BlockSpec((kb, 2 * a, c), lambda k, o: (k, 0, 2 * o + side))
    row = lambda side: pl.BlockSpec((SUBLANES, c), lambda k, o: (0, 2 * o + side))
    return pl.pallas_call(
        _mid_filter_kernel, grid=(hh // kb, 2),
        in_specs=[slab(0), slab(1), pl.BlockSpec((kb, 2 * a, 2 * a), lambda k, o: (k, 0, 0)), row(1), row(0), row(1)],
        out_specs=pl.BlockSpec((kb, 2 * a, c), lambda k, o: (k, 0, o)),
        out_shape=jax.ShapeDtypeStruct((hh, 2 * a, 2 * c), BF16),
        compiler_params=_params("parallel", "parallel"), name="dft_mid_filter",
    )(y, y, g, taps, nrm, nrm)


def _mid_conv_kernel(y_ref, g_ref, gt_ref, kf_ref, o_ref):
    for b in range(y_ref.shape[0]):
        x = _dot(g_ref[b].astype(BF16), y_ref[b])
        kf = kf_ref[b].astype(F32)
        half = x.shape[0] // 2
        xr, xi, kr, ki = x[:half], x[half:], kf[:half], kf[half:]
        z = jnp.concatenate([xr * kr - xi * ki, xr * ki + xi * kr], axis=0).astype(BF16)
        o_ref[b] = _dot(gt_ref[b].astype(BF16), z).astype(o_ref.dtype)


def dft_mid_conv(y, g, gt, kf, order):
    hh, a2, c = y.shape
    kb = K1_PER_STEP
    slab = pl.BlockSpec((kb, a2, c), lambda k: (k, 0, 0))
    mat = pl.BlockSpec((kb, a2, a2), lambda k: (k, 0, 0))
    return pl.pallas_call(
        _mid_conv_kernel, grid=(hh // kb,),
        in_specs=[slab, mat, mat, pl.BlockSpec((kb, a2, c), lambda k: (k, 0, order))],
        out_specs=slab, out_shape=jax.ShapeDtypeStruct((hh, a2, c), BF16),
        compiler_params=_params("parallel"), name="dft_mid_conv",
    )(y, g, gt, kf)


def _last_kernel(b_ref, a_ref, xa_ref, zb_ref, bias_ref, o_ref, a16_ref):
    @pl.when((pl.program_id(0) == 0) & (pl.program_id(1) == 0))
    def _():
        a16_ref[...] = a_ref[...].astype(BF16)

    rows, t, ct = b_ref.shape
    conv = _dot(a16_ref[...], b_ref[...].reshape(rows * t, ct)).reshape(rows // 2, t, ct)
    o_ref[0] = (xa_ref[0] * (conv + bias_ref[...] * zb_ref[0])).astype(o_ref.dtype)


def dft_last(b, a2m, xa4, pa, zb4, pb, bias_row):
    rows, a, c = b.shape
    hh = rows // 2
    ct = _pick_tile(c, 512)
    t = ROWS_BF16
    sig = lambda pp: pl.BlockSpec((1, hh, t, ct), lambda j, q: (pp, 0, j, q))
    return pl.pallas_call(
        _last_kernel, grid=(a // t, c // ct),
        in_specs=[pl.BlockSpec((rows, t, ct), lambda j, q: (0, j, q)),
                  pl.BlockSpec(a2m.shape, lambda j, q: (0, 0)),
                  sig(pa), sig(pb), pl.BlockSpec((1, ct), lambda j, q: (0, q))],
        out_specs=sig(0), out_shape=jax.ShapeDtypeStruct((1, hh, a, c), F32),
        scratch_shapes=[pltpu.VMEM(a2m.shape, BF16)],
        compiler_params=_params("arbitrary", "arbitrary"), name="dft_last",
    )(b, a2m, xa4, zb4, bias_row)


def long_conv_gated(u3, kf, bias, consts, a):
    a1, g, gt, a2m = consts
    _, l, c = u3.shape
    hh = a // 2
    u4 = u3.reshape(3, hh, a, c)

    def conv_gate(src4, p_src, xa_p, order):
        y = dft_stage1(src4, p_src, a1).reshape(hh, 2 * a, c)
        bb = dft_mid_conv(y, g, gt, kf, order).reshape(2 * hh, a, c)
        return dft_last(bb, a2m, u4, xa_p, src4, p_src, bias[order].reshape(1, c))

    z2 = conv_gate(u4, 2, 0, 0)
    return conv_gate(z2, 0, 1, 1).reshape(l, c)


def _ctx_spec_kernel(h_ref, mf_ref, o_ref):
    o_ref[...] = _dot(mf_ref[...].astype(BF16), h_ref[...])


def ctx_filter_spectrum(h, mf):
    l, n = h.shape
    tn = _pick_tile(n, 1024)
    return pl.pallas_call(
        _ctx_spec_kernel, grid=(n // tn,),
        in_specs=[pl.BlockSpec((l, tn), lambda j: (0, j)), pl.BlockSpec(mf.shape, lambda j: (0, 0))],
        out_specs=pl.BlockSpec((2 * l, tn), lambda j: (0, j)),
        out_shape=jax.ShapeDtypeStruct((2 * l, n), F32),
        compiler_params=_params("parallel"), name="ctx_filter_spectrum",
    )(h, mf)


def _ctx_hyena_kernel(u_ref, mf_ref, mi_ref, hf0a, hf1a, hf0b, hf1b, tap0, tap1, n0a, n1a, n0b, n1b, bias_ref, o_ref):
    x1, x2, z = u_ref[0], u_ref[1], u_ref[2]

    mf, mi = mf_ref[...].astype(BF16), mi_ref[...].astype(BF16)

    def conv(sig, h0_ref, h1_ref, tap_ref, n0_ref, n1_ref):
        x = _dot(mf, sig.astype(BF16))
        zz = _spectrum_product(x, h0_ref[...], h1_ref[...], tap_ref[0:1, :], n0_ref[0:1, :], n1_ref[0:1, :])
        return _dot(mi, zz.astype(BF16))

    z2 = x1 * (conv(z, hf0a, hf1a, tap0, n0a, n1a) + bias_ref[0:1, :] * z)
    o_ref[...] = (x2 * (conv(z2, hf0b, hf1b, tap1, n0b, n1b) + bias_ref[1:2, :] * z2)).astype(o_ref.dtype)


def ctx_hyena(u3, hf, taps, nrm, bias, mf, mi):
    _, l, c = u3.shape
    tc = _pick_tile(c, 512)
    nb = c // tc
    const = lambda a: pl.BlockSpec(a.shape, lambda j: (0, 0))
    col = lambda rows, q: pl.BlockSpec((rows, tc), lambda j: (0, q * nb + j))
    mats = [mf, mi]
    return pl.pallas_call(
        _ctx_hyena_kernel, grid=(nb,),
        in_specs=[pl.BlockSpec((3, l, tc), lambda j: (0, 0, j))] + [const(m) for m in mats]
        + [col(2 * l, 0), col(2 * l, 1), col(2 * l, 2), col(2 * l, 3), col(8, 1), col(8, 3),
           col(8, 0), col(8, 1), col(8, 2), col(8, 3), pl.BlockSpec((2, tc), lambda j: (0, j))],
        out_specs=pl.BlockSpec((l, tc), lambda j: (0, j)),
        out_shape=jax.ShapeDtypeStruct((l, c), BF16),
        compiler_params=_params("parallel"), name="ctx_hyena",
    )(u3, *mats, hf, hf, hf, hf, taps, taps, nrm, nrm, nrm, nrm, bias)


def _axial_tables(n_tokens, n_rot):
    rows = n_tokens // GRID_W
    row = jnp.broadcast_to(jnp.arange(rows)[:, None], (rows, GRID_W)).reshape(-1).astype(F32)
    col = jnp.broadcast_to(jnp.arange(GRID_W)[None, :], (rows, GRID_W)).reshape(-1).astype(F32)
    n_freq = n_rot // 4
    inv = ROPE_THETA ** (-jnp.arange(n_freq, dtype=F32) / n_freq)
    ang = jnp.concatenate([row[:, None] * inv, col[:, None] * inv], axis=-1)
    return jnp.cos(ang), jnp.sin(ang)


def _rope_tables_full(s, n_ctx):
    cos, sin = _axial_tables(s, HEAD_DIM)
    t0 = jnp.concatenate([jnp.concatenate([cos, cos], axis=1), jnp.ones((n_ctx, LANE), F32)], axis=0)
    t1 = jnp.concatenate([jnp.concatenate([-sin, sin], axis=1), jnp.zeros((n_ctx, LANE), F32)], axis=0)
    return t0, t1


def _rope_tables_half(s, n_ctx):
    cos, sin = _axial_tables(s, ROPE_DIM)
    q = ROPE_DIM // 2
    zq = jnp.zeros((s, q), F32)
    z2 = jnp.zeros((s, LANE - ROPE_DIM), F32)
    t0 = jnp.concatenate([cos, cos, z2], axis=1)
    t1 = jnp.concatenate([-sin, zq, z2], axis=1)
    t2 = jnp.concatenate([zq, sin, z2], axis=1)
    c0 = jnp.concatenate([jnp.ones((n_ctx, ROPE_DIM), F32), jnp.zeros((n_ctx, LANE - ROPE_DIM), F32)], axis=1)
    cz = jnp.zeros((n_ctx, LANE), F32)
    return jnp.concatenate([t0, c0], axis=0), jnp.concatenate([t1, cz], axis=0), jnp.concatenate([t2, cz], axis=0)


def _filter_features(l):
    pos = jnp.arange(l, dtype=F32)
    t = pos / max(l - 1, 1)
    bands = jnp.linspace(1e-4, HYENA_BANDS - 1, HYENA_BANDS, dtype=F32)
    ang = (2.0 * math.pi / l) * pos[:, None] * bands[None, :]
    z = jnp.concatenate([t[:, None], jnp.cos(ang), -jnp.sin(ang)], axis=-1)
    return jnp.pad(z, ((0, 0), (0, LANE - HYENA_EMB)))


Q_STACK = 1024
KV_TILE_CAP = 1408


def _attention_pair(qp, kp, vt, s, n_ctx, *, kv_heads, groups, window, sink=None):
    common = dict(kv_heads=kv_heads, groups=groups, sink=sink)
    o_ctx = flash(qp, kp, vt, n_q=n_ctx, q_row0=s, kv_row0=s, n_kv_rows=n_ctx, tq=n_ctx, tk=n_ctx, **common)
    if window:
        o_lat = window_attention(qp, kp, vt, kv_heads=kv_heads, groups=groups, n_lat=s, n_ctx=n_ctx, tq=ROW_TILE,
                                 sink=sink)
    else:
        o_lat = flash(qp, kp, vt, n_q=s, q_row0=0, kv_row0=0, n_kv_rows=n_ctx + s, tq=Q_STACK // groups,
                      tk=_pick_tile(n_ctx + s, KV_TILE_CAP), **common)
    return o_ctx, o_lat


def _even_layer(u, s, n_ctx, w_in, qn_g, kn_g, sink, conv_w, conv_b, fparams, hy_bias, rope_full, feats, dft):
    w_all, e = w_in
    src_gate = ATT_WIDTH + 2 * KV_WIDTH + 3 * B_WIDTH
    proj = matmul_permuted(u, w_all, e, [(src_gate, BRANCH), (0, ATT_WIDTH), (ATT_WIDTH, 2 * KV_WIDTH),
                                         (ATT_WIDTH + 2 * KV_WIDTH, 3 * B_WIDTH)], F32)
    qp = head_prep(proj, OFF_Q, A_HEADS, qn_g, rope_full, HEAD_DIM ** -0.5 * LOG2E, True)
    kp = head_prep(proj, OFF_K, A_KV_HEADS, kn_g, rope_full, 1.0)
    vt = transpose_heads(proj, OFF_V, A_KV_HEADS)
    a_ctx, a_lat = _attention_pair(qp, kp, vt, s, n_ctx, kv_heads=A_KV_HEADS, groups=A_HEADS // A_KV_HEADS,
                                   window=True, sink=sink)
    w1, b1, f1, w2, b2, f2, w3 = fparams
    w1p = jnp.pad(w1, ((0, LANE - HYENA_EMB), (0, 0)))
    w3 = w3.astype(BF16)
    deltas = jnp.linspace(DECAY_MAX, DECAY_MIN, B_WIDTH, dtype=F32)
    uc3, ul3 = conv3(proj, OFF_REST, conv_w, conv_b, n_ctx)
    a = int(round(math.sqrt(2 * s)))
    consts, (mf, mi) = dft
    h_lat, tap_lat, nrm_lat = hyena_filter(feats[0], w1p, b1, f1, w2, b2, f2, w3, deltas)
    kf_lat = filter_spectrum(h_lat, tap_lat, nrm_lat, consts, a)
    b_lat = long_conv_gated(ul3, kf_lat, hy_bias, consts, a)
    h_ctx, tap_ctx, nrm_ctx = hyena_filter(feats[1], w1p, b1, f1, w2, b2, f2, w3, deltas)
    hf_ctx = ctx_filter_spectrum(h_ctx, mf)
    b_ctx = ctx_hyena(uc3, hf_ctx, tap_ctx, nrm_ctx, hy_bias, mf, mi)
    return proj, a_ctx, a_lat, b_ctx, b_lat


def _odd_layer(u, s, n_ctx, w_in, qn_g, kn_g, cq_g, ckv_g, wuq, wukv, mq_g, mk_g, rope_full, rope_half):
    d = w_in.shape[0]
    parts = jnp.split(w_in, np.cumsum([ATT_WIDTH, KV_WIDTH, KV_WIDTH, Q_LORA, KV_LORA, ROPE_DIM]).tolist(), axis=1)
    wq, wk, wv, wmq, wmkv, wmkr, wg = parts
    w_perm = jnp.concatenate([wg, wq, wk, wv, wmq, wmkv, wmkr, jnp.zeros((d, LANE - ROPE_DIM), w_in.dtype)], axis=1)
    proj = matmul(u, w_perm.astype(BF16), F32)
    qp = head_prep(proj, OFF_Q, C_HEADS, qn_g, rope_full, HEAD_DIM ** -0.5 * LOG2E, True)
    kp = head_prep(proj, OFF_K, C_KV_HEADS, kn_g, rope_full, 1.0)
    vt = transpose_heads(proj, OFF_V, C_KV_HEADS)
    c_ctx, c_lat = _attention_pair(qp, kp, vt, s, n_ctx, kv_heads=C_KV_HEADS, groups=C_HEADS // C_KV_HEADS,
                                   window=False)
    pad_r = LANE - ROPE_DIM
    wuq3 = wuq.reshape(Q_LORA, M_HEADS, QK_DIM)
    wuq_p = jnp.concatenate([wuq3[:, :, :NOPE_DIM].reshape(Q_LORA, -1),
                             jnp.pad(wuq3[:, :, NOPE_DIM:], ((0, 0), (0, 0), (0, pad_r))).reshape(Q_LORA, -1)], axis=1)
    wukv3 = wukv.reshape(KV_LORA, M_HEADS, NOPE_DIM + V_DIM)
    wukv_p = jnp.concatenate([wukv3[:, :, :NOPE_DIM].reshape(KV_LORA, -1),
                              wukv3[:, :, NOPE_DIM:].reshape(KV_LORA, -1)], axis=1)
    q_raw = norm_matmul(proj, OFF_MQ, cq_g, wuq_p.astype(BF16))
    kv_raw = norm_matmul(proj, OFF_MKV, ckv_g, wukv_p.astype(BF16))
    scale = QK_DIM ** -0.5 * LOG2E
    gq_r = jnp.pad(mq_g[NOPE_DIM:], (0, pad_r))
    gk_r = jnp.pad(mk_g[NOPE_DIM:], (0, pad_r))
    q_full = mla_prep(q_raw, 0, q_raw, M_HEADS * NOPE_DIM, False, M_HEADS, mq_g[:NOPE_DIM], gq_r, rope_half, scale,
                      True)
    k_full = mla_prep(kv_raw, 0, proj, OFF_MKR, True, M_HEADS, mk_g[:NOPE_DIM], gk_r, rope_half, 1.0, False)
    vmt = transpose_heads(kv_raw, M_HEADS * NOPE_DIM, M_HEADS)
    d_ctx, d_lat = _attention_pair(q_full, k_full, vmt, s, n_ctx, kv_heads=M_HEADS, groups=1, window=False)
    return proj, c_ctx, c_lat, d_ctx, d_lat


def kernel(x, c, ctx, c_ctx, ada_w, ada_b, norm_g, w_out, ev_w_in, ev_qn_g, ev_kn_g, ev_sink, ev_conv_w, ev_conv_b, hy_w1, hy_b1, hy_f1, hy_w2, hy_b2, hy_f2, hy_w3, hy_bias, od_w_in, od_qn_g, od_kn_g, od_cq_g, od_ckv_g, od_wuq, od_wukv, od_mq_g, od_mk_g):
    _, s, d = x.shape
    n_ctx = ctx.shape[1]
    depth = ada_w.shape[0]
    h = jnp.concatenate([x[0], ctx[0]], axis=0)
    cc = jnp.zeros((8, d), F32).at[0].set(c[0]).at[1].set(c_ctx)
    mod = ada_mod(cc, ada_w, ada_b)
    rope_full = _rope_tables_full(s, n_ctx)
    rope_half = _rope_tables_half(s, n_ctx)
    feats = (_filter_features(s), _filter_features(n_ctx))
    as_arrays = lambda mats: tuple(jnp.asarray(m) for m in mats)
    dft = (as_arrays(_dft_consts(int(round(math.sqrt(2 * s))))), as_arrays(_dense_dft_consts(n_ctx)))
    u = norm_mod(h, norm_g[0], mod[0], n_ctx)
    for i in range(depth):
        if i % 2 == 0:
            e = i // 2
            proj, o1c, o1l, o2c, o2l = _even_layer(
                u, s, n_ctx, (ev_w_in, e), ev_qn_g[e], ev_kn_g[e], ev_sink[e], ev_conv_w[e], ev_conv_b[e],
                (hy_w1[e], hy_b1[e], hy_f1[e], hy_w2[e], hy_b2[e], hy_f2[e], hy_w3[e]), hy_bias[e], rope_full, feats,
                dft)
        else:
            o = i // 2
            proj, o1c, o1l, o2c, o2l = _odd_layer(
                u, s, n_ctx, od_w_in[o], od_qn_g[o], od_kn_g[o], od_cq_g[o], od_ckv_g[o], od_wuq[o], od_wukv[o],
                od_mq_g[o], od_mk_g[o], rope_full, rope_half)
        w = w_out[i].astype(BF16)
        if i < depth - 1:
            h, u = out_proj(o1c, o1l, o2c, o2l, proj, h, mod[i], w, (norm_g[i + 1], mod[i + 1]))
        else:
            h = out_proj(o1c, o1l, o2c, o2l, proj, h, mod[i], w, None)
    return h[None]
```

```python
import functools
import math

import numpy as np
import jax
import jax.numpy as jnp
from jax import lax
from jax.experimental import pallas as pl
from jax.experimental.pallas import tpu as pltpu

F32 = jnp.float32
BF16 = jnp.bfloat16

GRID_W = 64
HEAD_DIM = 128
ROPE_THETA = 10000.0
EPS = 1e-6
NEG_INF = -1e30
A_HEADS, A_KV_HEADS, WINDOW = 8, 2, 128
B_WIDTH = 1024
HYENA_BANDS = 16
HYENA_EMB = 1 + 2 * HYENA_BANDS
DECAY_TARGET = 1e-2
DECAY_MAX = abs(math.log(DECAY_TARGET)) / 0.3
DECAY_MIN = abs(math.log(DECAY_TARGET)) / 1.5
C_HEADS, C_KV_HEADS = 8, 2
M_HEADS, Q_LORA, KV_LORA, NOPE_DIM, ROPE_DIM, V_DIM = 8, 512, 256, 128, 64, 128
QK_DIM = NOPE_DIM + ROPE_DIM
ATT_WIDTH = A_HEADS * HEAD_DIM
KV_WIDTH = A_KV_HEADS * HEAD_DIM
BRANCH = ATT_WIDTH + B_WIDTH

LOG2E = math.log2(math.e)
LANE = 128
SUBLANES = 8
ROWS_BF16 = 16
ROW_TILE = 256
VMEM_LIMIT = 48 * 1024 * 1024
COL_CAP = 1024
PROJ_ROW_CAP = 2112
PLAIN_ROW_CAP = 1536
PREP_ROW_CAP = 768
FILTER_ROWS = 512
DFT_COL_CAP = 512

OFF_GATE, OFF_Q, OFF_K, OFF_V, OFF_REST = 0, BRANCH, BRANCH + ATT_WIDTH, BRANCH + ATT_WIDTH + KV_WIDTH, BRANCH + ATT_WIDTH + 2 * KV_WIDTH
OFF_MQ, OFF_MKV, OFF_MKR = OFF_REST, OFF_REST + Q_LORA, OFF_REST + Q_LORA + KV_LORA


def _params(*sem):
    return pltpu.CompilerParams(dimension_semantics=sem, vmem_limit_bytes=VMEM_LIMIT)


def _split_bf16(x):
    hi = x.astype(BF16)
    return hi, (x - hi.astype(F32)).astype(BF16)


def _dot(a, b):
    return jnp.dot(a, b, preferred_element_type=F32)


def _dot3(a_hi, a_lo, b_hi, b_lo):
    return _dot(a_hi, b_hi) + (_dot(a_hi, b_lo) + _dot(a_lo, b_hi))


def _ada_kernel(c_ref, w_ref, b_ref, o_ref):
    c = c_ref[...]
    a = (c * jax.nn.sigmoid(c)).astype(BF16)
    o_ref[0] = _dot(a, w_ref[0].astype(BF16)) + b_ref[0]


def ada_mod(cc, ada_w, ada_b):
    depth, d, n = ada_w.shape
    tn = _pick_tile(n, COL_CAP)
    return pl.pallas_call(
        _ada_kernel, grid=(depth, n // tn),
        in_specs=[pl.BlockSpec((8, d), lambda l, j: (0, 0)),
                  pl.BlockSpec((1, d, tn), lambda l, j: (l, 0, j)),
                  pl.BlockSpec((1, 1, tn), lambda l, j: (l, 0, j))],
        out_specs=pl.BlockSpec((1, 8, tn), lambda l, j: (l, 0, j)),
        out_shape=jax.ShapeDtypeStruct((depth, 8, n), F32),
        compiler_params=_params("parallel", "parallel"), name="ada_mod",
    )(cc, ada_w, ada_b.reshape(depth, 1, n))


def _modulated_norm(x, g, mod_ref, is_ctx):
    d = x.shape[1]
    y = x * lax.rsqrt(jnp.mean(x * x, axis=-1, keepdims=True) + EPS) * g
    sh = jnp.where(is_ctx, mod_ref[1:2, 0:d], mod_ref[0:1, 0:d])
    sc = jnp.where(is_ctx, mod_ref[1:2, d:2 * d], mod_ref[0:1, d:2 * d])
    return y * (1.0 + sc) + sh


def _norm_kernel(x_ref, g_ref, mod_ref, o_ref, *, lat_tiles):
    is_ctx = pl.program_id(0) >= lat_tiles
    o_ref[...] = _modulated_norm(x_ref[...], g_ref[...], mod_ref, is_ctx).astype(o_ref.dtype)


def norm_mod(h, g, mod, n_ctx):
    r, d = h.shape
    tm = ROW_TILE
    return pl.pallas_call(
        functools.partial(_norm_kernel, lat_tiles=(r - n_ctx) // tm), grid=(r // tm,),
        in_specs=[pl.BlockSpec((tm, d), lambda i: (i, 0)),
                  pl.BlockSpec((1, d), lambda i: (0, 0)),
                  pl.BlockSpec(mod.shape, lambda i: (0, 0))],
        out_specs=pl.BlockSpec((tm, d), lambda i: (i, 0)),
        out_shape=jax.ShapeDtypeStruct((r, d), BF16),
        compiler_params=_params("parallel"), name="norm_mod",
    )(h, g.reshape(1, d), mod)


def _mm_kernel(a_ref, b_ref, o_ref):
    o_ref[...] = _dot(a_ref[...], b_ref[...]).astype(o_ref.dtype)


def _pick_tile(n, cap, unit=LANE):
    best = unit
    for t in range(unit, cap + 1, unit):
        if n % t == 0:
            best = t
    return best


def matmul(a, b, out_dtype):
    m, k = a.shape
    n = b.shape[1]
    tm = _pick_tile(m, PLAIN_ROW_CAP, SUBLANES)
    tn = _pick_tile(n, COL_CAP)
    return pl.pallas_call(
        _mm_kernel, grid=(m // tm, n // tn),
        in_specs=[pl.BlockSpec((tm, k), lambda i, j: (i, 0)),
                  pl.BlockSpec((k, tn), lambda i, j: (0, j))],
        out_specs=pl.BlockSpec((tm, tn), lambda i, j: (i, j)),
        out_shape=jax.ShapeDtypeStruct((m, n), out_dtype),
        compiler_params=_params("parallel", "parallel"), name="in_proj",
    )(a, b)


def _mm_cast_kernel(a_ref, b_ref, o_ref):
    o_ref[...] = _dot(a_ref[...], b_ref[0].astype(a_ref.dtype)).astype(o_ref.dtype)


def matmul_permuted(a, w3, layer, segments, out_dtype):
    m, k = a.shape
    n = sum(width for _, width in segments)
    tm = _pick_tile(m, PROJ_ROW_CAP, SUBLANES)
    tn = LANE
    for t in range(LANE, COL_CAP + 1, LANE):
        if all(start % t == 0 and width % t == 0 for start, width in segments):
            tn = t
    src_tiles = [(start + off) // tn for start, width in segments for off in range(0, width, tn)]

    def src(j):
        idx = src_tiles[0]
        for t, s_t in enumerate(src_tiles[1:], 1):
            idx = jnp.where(j == t, s_t, idx)
        return idx
    return pl.pallas_call(
        _mm_cast_kernel, grid=(m // tm, n // tn),
        in_specs=[pl.BlockSpec((tm, k), lambda i, j: (i, 0)),
                  pl.BlockSpec((1, k, tn), lambda i, j: (layer, 0, src(j)))],
        out_specs=pl.BlockSpec((tm, tn), lambda i, j: (i, j)),
        out_shape=jax.ShapeDtypeStruct((m, n), out_dtype),
        compiler_params=_params("parallel", "parallel"), name="in_proj",
    )(a, w3)


def _normmm_kernel(x_ref, g_ref, w_ref, o_ref):
    x = x_ref[...].astype(F32)
    y = x * lax.rsqrt(jnp.mean(x * x, axis=-1, keepdims=True) + EPS) * g_ref[...]
    o_ref[...] = _dot(y.astype(BF16), w_ref[...]).astype(o_ref.dtype)


def norm_matmul(x, col_off, g, w):
    r = x.shape[0]
    k, n = w.shape
    tm = _pick_tile(r, PREP_ROW_CAP, SUBLANES)
    cb = col_off // k
    return pl.pallas_call(
        _normmm_kernel, grid=(r // tm,),
        in_specs=[pl.BlockSpec((tm, k), lambda i: (i, cb)),
                  pl.BlockSpec((1, k), lambda i: (0, 0)),
                  pl.BlockSpec((k, n), lambda i: (0, 0))],
        out_specs=pl.BlockSpec((tm, n), lambda i: (i, 0)),
        out_shape=jax.ShapeDtypeStruct((r, n), F32),
        compiler_params=_params("parallel"), name="norm_matmul",
    )(x, g.reshape(1, k), w)


def _prep_kernel(x_ref, g_ref, t0_ref, t1_ref, o_ref, *, scale, n_heads, head_major):
    g, t0, t1 = g_ref[...], t0_ref[...], t1_ref[...]
    for h in range(n_heads):
        x = x_ref[:, h * LANE:(h + 1) * LANE].astype(F32)
        xn = x * lax.rsqrt(jnp.mean(x * x, axis=-1, keepdims=True) + EPS) * g
        y = ((xn * t0 + pltpu.roll(xn, HEAD_DIM // 2, 1) * t1) * scale).astype(o_ref.dtype)
        if head_major:
            o_ref[h] = y
        else:
            o_ref[:, h * LANE:(h + 1) * LANE] = y


def head_prep(x, col_off, n_heads, g, tables, scale, head_major=False):
    r = x.shape[0]
    w = n_heads * LANE
    tm = _pick_tile(r, PREP_ROW_CAP, SUBLANES)
    tab_spec = pl.BlockSpec((tm, LANE), lambda i: (i, 0))
    if head_major:
        out_spec = pl.BlockSpec((n_heads, tm, LANE), lambda i: (0, i, 0))
        out_shape = jax.ShapeDtypeStruct((n_heads, r, LANE), BF16)
    else:
        out_spec = pl.BlockSpec((tm, w), lambda i: (i, 0))
        out_shape = jax.ShapeDtypeStruct((r, w), BF16)
    return pl.pallas_call(
        functools.partial(_prep_kernel, scale=scale, n_heads=n_heads, head_major=head_major),
        grid=(r // tm,),
        in_specs=[pl.BlockSpec((tm, w), lambda i: (i, col_off // w)),
                  pl.BlockSpec((1, LANE), lambda i: (0, 0)), tab_spec, tab_spec],
        out_specs=out_spec, out_shape=out_shape,
        compiler_params=_params("parallel"), name="head_prep",
    )(x, g.reshape(1, LANE), *tables)


def _mla_prep_kernel(xn_ref, xr_ref, gn_ref, gr_ref, t0_ref, t1_ref, t2_ref, o_ref, *, n_heads, shared_rope, scale,
                     head_major):
    gn, gr = gn_ref[...], gr_ref[...]
    t0, t1, t2 = t0_ref[...], t1_ref[...], t2_ref[...]

    def norm(x, n_real, g):
        return x * lax.rsqrt(jnp.sum(x * x, axis=-1, keepdims=True) * (1.0 / n_real) + EPS) * g

    def rope_part(x):
        xn = norm(x.astype(F32), ROPE_DIM, gr)
        return xn * t0 + pltpu.roll(xn, 96, 1) * t1 + pltpu.roll(xn, 32, 1) * t2

    if shared_rope:
        shared = rope_part(xr_ref[...])
    for h in range(n_heads):
        cols = slice(h * LANE, (h + 1) * LANE)
        a = norm(xn_ref[:, cols].astype(F32), NOPE_DIM, gn)
        b = shared if shared_rope else rope_part(xr_ref[:, cols])
        y = (jnp.concatenate([a, b], axis=1) * scale).astype(o_ref.dtype)
        if head_major:
            o_ref[h] = y
        else:
            o_ref[:, 2 * h * LANE:2 * (h + 1) * LANE] = y


def mla_prep(xn, xn_off, xr, xr_off, shared_rope, n_heads, gn, gr, tables, scale, head_major):
    r = xn.shape[0]
    w = n_heads * LANE
    wr = LANE if shared_rope else w
    tm = _pick_tile(r, PREP_ROW_CAP, SUBLANES)
    row = lambda width: pl.BlockSpec((1, width), lambda i: (0, 0))
    tab = pl.BlockSpec((tm, LANE), lambda i: (i, 0))
    if head_major:
        out_spec = pl.BlockSpec((n_heads, tm, 2 * LANE), lambda i: (0, i, 0))
        out_shape = jax.ShapeDtypeStruct((n_heads, r, 2 * LANE), BF16)
    else:
        out_spec = pl.BlockSpec((tm, 2 * w), lambda i: (i, 0))
        out_shape = jax.ShapeDtypeStruct((r, 2 * w), BF16)
    return pl.pallas_call(
        functools.partial(_mla_prep_kernel, n_heads=n_heads, shared_rope=shared_rope, scale=scale,
                          head_major=head_major),
        grid=(r // tm,),
        in_specs=[pl.BlockSpec((tm, w), lambda i: (i, xn_off // w)), pl.BlockSpec((tm, wr), lambda i: (i, xr_off // wr)),
                  row(LANE), row(LANE), tab, tab, tab],
        out_specs=out_spec, out_shape=out_shape,
        compiler_params=_params("parallel"), name="mla_prep",
    )(xn, xr, gn.reshape(1, LANE), gr.reshape(1, LANE), *tables)


def _vt_kernel(x_ref, o_ref):
    o_ref[...] = x_ref[...].astype(F32).T.astype(o_ref.dtype)


def transpose_heads(x, col_off, n_heads):
    r = x.shape[0]
    w = n_heads * LANE
    tm = _pick_tile(r, PREP_ROW_CAP, LANE)
    return pl.pallas_call(
        _vt_kernel, grid=(r // tm,),
        in_specs=[pl.BlockSpec((tm, w), lambda i: (i, col_off // w))],
        out_specs=pl.BlockSpec((w, tm), lambda i: (0, i)),
        out_shape=jax.ShapeDtypeStruct((w, r), BF16),
        compiler_params=_params("parallel"), name="transpose_heads",
    )(x)


def _flash_kernel(*refs, groups, tq, tk, dv, n_kv, chain, has_sink, key_chunk):
    q_ref, qn_ref, k_ref, kn_ref, vt_ref = refs[0:5]
    pos = 5
    if has_sink:
        sink_ref = refs[pos]
        pos += 1
    o_ref, m_ref, l_ref, acc_ref, s_ref, mt_ref = refs[pos:pos + 6]
    i, j = pl.program_id(1), pl.program_id(2)
    n = groups * tq
    bounds = [(r0, min(r0 + key_chunk, tk)) for r0 in range(0, tk, key_chunk)]

    def stacked(qr):
        return qr[...].reshape(n, qr.shape[2])

    def scores(kr, q, r0, r1):
        return lax.dot_general(kr[r0:r1, :], q, (((1,), (1,)), ((), ())), preferred_element_type=F32)

    @pl.when(j == 0)
    def _():
        m_ref[...] = jnp.full(m_ref.shape, NEG_INF, F32)
        l_ref[...] = jnp.zeros(l_ref.shape, F32)
        acc_ref[...] = jnp.zeros(acc_ref.shape, F32)

    @pl.when((j == 0) & (i == 0) if chain else j == 0)
    def _():
        q = stacked(q_ref)
        mt = jnp.full(mt_ref.shape, NEG_INF, F32)
        for r0, r1 in bounds:
            s = scores(k_ref, q, r0, r1)
            s_ref[r0:r1, :] = s
            mt = jnp.maximum(mt, jnp.max(s, axis=0, keepdims=True))
        mt_ref[...] = mt

    def step(kr, qr):
        prefetch = kr is not None
        if prefetch:
            q = stacked(qr)
        m_prev = m_ref[...]
        m_new = jnp.maximum(m_prev, mt_ref[...])
        alpha = jnp.exp2(m_prev - m_new)
        l_new = alpha * l_ref[...]
        acc_new = alpha * acc_ref[...]
        mt = jnp.full(mt_ref.shape, NEG_INF, F32)
        for r0, r1 in bounds:
            if prefetch:
                s_next = scores(kr, q, r0, r1)
            p = jnp.exp2(s_ref[r0:r1, :] - m_new)
            l_new = l_new + jnp.sum(p, axis=0, keepdims=True)
            acc_new = acc_new + _dot(vt_ref[:, r0:r1], p.astype(BF16))
            if prefetch:
                s_ref[r0:r1, :] = s_next
                mt = jnp.maximum(mt, jnp.max(s_next, axis=0, keepdims=True))
        l_ref[...] = l_new
        acc_ref[...] = acc_new
        m_ref[...] = m_new
        if prefetch:
            mt_ref[...] = mt

    if n_kv > 1:
        pl.when(j < n_kv - 1)(lambda: step(kn_ref, q_ref))

    @pl.when(j == n_kv - 1)
    def _():
        if chain:
            step(k_ref, qn_ref)
        else:
            step(None, None)
        m, l, acc = m_ref[...], l_ref[...], acc_ref[...]
        if has_sink:
            sk = sink_ref[0]
            m_fin = jnp.maximum(m, sk)
            a = jnp.exp2(m - m_fin)
            l = l * a + jnp.exp2(sk - m_fin)
            acc = acc * a
        o_t = acc / l
        for g in range(groups):
            o_ref[:, g * dv:(g + 1) * dv] = o_t[:, g * tq:(g + 1) * tq].T.astype(o_ref.dtype)


KEY_CHUNK = 256


def flash(q, k, vt, *, kv_heads, groups, n_q, q_row0, kv_row0, n_kv_rows, tq, tk, dv=HEAD_DIM, sink=None):
    dk = q.shape[2]
    qb0 = q_row0 // tq
    kb0 = kv_row0 // tk
    n_kv = n_kv_rows // tk

    def kv_row(i, j):
        return kb0 + j
    n = groups * tq
    n_qb = n_q // tq
    nxt = lambda j: jnp.minimum(j + 1, n_kv - 1)
    q_nxt = lambda i: qb0 + jnp.minimum(i + 1, n_qb - 1)
    in_specs = [pl.BlockSpec((groups, tq, dk), lambda h, i, j: (h, qb0 + i, 0)),
                pl.BlockSpec((groups, tq, dk), lambda h, i, j: (h, q_nxt(i), 0)),
                pl.BlockSpec((tk, dk), lambda h, i, j: (kv_row(i, 0), h)),
                pl.BlockSpec((tk, dk), lambda h, i, j: (kv_row(i, nxt(j)), h)),
                pl.BlockSpec((dv, tk), lambda h, i, j: (h, kv_row(i, j)))]
    args = [q, q, k, k, vt]
    if sink is not None:
        in_specs.append(pl.BlockSpec((1, 1, n), lambda h, i, j: (h, 0, 0)))
        sink2 = sink.astype(F32).reshape(kv_heads, groups, 1) * LOG2E
        args.append(jnp.broadcast_to(sink2, (kv_heads, groups, tq)).reshape(kv_heads, 1, n))
    kern = functools.partial(
        _flash_kernel, groups=groups, tq=tq, tk=tk, dv=dv, n_kv=n_kv, chain=n_qb > 1, has_sink=sink is not None,
        key_chunk=KEY_CHUNK)
    return pl.pallas_call(
        kern, grid=(kv_heads, n_q // tq, n_kv), in_specs=in_specs,
        out_specs=pl.BlockSpec((tq, groups * dv), lambda h, i, j: (i, h)),
        out_shape=jax.ShapeDtypeStruct((n_q, kv_heads * groups * dv), BF16),
        scratch_shapes=[pltpu.VMEM((1, n), F32), pltpu.VMEM((1, n), F32), pltpu.VMEM((dv, n), F32),
                        pltpu.VMEM((tk, n), F32), pltpu.VMEM((1, n), F32)],
        compiler_params=_params("parallel", "arbitrary", "arbitrary"),
        name="flash_dense",
    )(*args)


def _window_kernel(*refs, groups, tq, n_lat, has_sink):
    q_ref = refs[0]
    n_half = tq // WINDOW + 2
    k_refs = refs[1:2 + n_half]
    vt_refs = refs[2 + n_half:3 + 2 * n_half]
    pos = 3 + 2 * n_half
    if has_sink:
        sink_ref = refs[pos]
        pos += 1
    o_ref = refs[pos]
    i = pl.program_id(1)
    n = groups * tq
    dv = vt_refs[0].shape[0]
    q = q_ref[...].reshape(n, q_ref.shape[2])
    qpos = i * tq + lax.broadcasted_iota(jnp.int32, (1, n), 1) % tq
    scores = []
    m = sink_ref[0] if has_sink else jnp.full((1, n), NEG_INF, F32)
    for b, k_ref in enumerate(k_refs):
        s = lax.dot_general(k_ref[...], q, (((1,), (1,)), ((), ())), preferred_element_type=F32)
        if b > 0:
            kpos = i * tq + (b - 2) * WINDOW + lax.broadcasted_iota(jnp.int32, (WINDOW, 1), 0)
            valid = (jnp.abs(qpos - kpos) <= WINDOW) & (kpos >= 0) & (kpos < n_lat)
            s = jnp.where(valid, s, NEG_INF)
        scores.append(s)
        m = jnp.maximum(m, jnp.max(s, axis=0, keepdims=True))
    l = jnp.exp2(sink_ref[0] - m) if has_sink else jnp.zeros((1, n), F32)
    acc = jnp.zeros((dv, n), F32)
    for s, vt_ref in zip(scores, vt_refs):
        p = jnp.exp2(s - m)
        l = l + jnp.sum(p, axis=0, keepdims=True)
        acc = acc + _dot(vt_ref[...], p.astype(BF16))
    o_t = acc / l
    for g in range(groups):
        o_ref[:, g * dv:(g + 1) * dv] = o_t[:, g * tq:(g + 1) * tq].T.astype(o_ref.dtype)


def window_attention(q, k, vt, *, kv_heads, groups, n_lat, n_ctx, tq, sink=None, dv=HEAD_DIM):
    dk = q.shape[2]
    per = tq // WINDOW
    n_half_blocks = n_lat // WINDOW
    n = groups * tq
    ctx_blk = n_lat // n_ctx
    half = lambda b: (lambda i: jnp.clip(i * per + b - 1, 0, n_half_blocks - 1))
    halves = [half(b) for b in range(per + 2)]
    in_specs = [pl.BlockSpec((groups, tq, dk), lambda h, i: (h, i, 0)),
                pl.BlockSpec((n_ctx, dk), lambda h, i: (ctx_blk, h))]
    in_specs += [pl.BlockSpec((WINDOW, dk), lambda h, i, f=f: (f(i), h)) for f in halves]
    in_specs += [pl.BlockSpec((dv, n_ctx), lambda h, i: (h, ctx_blk))]
    in_specs += [pl.BlockSpec((dv, WINDOW), lambda h, i, f=f: (h, f(i))) for f in halves]
    args = [q] + [k] * (per + 3) + [vt] * (per + 3)
    if sink is not None:
        in_specs.append(pl.BlockSpec((1, 1, n), lambda h, i: (h, 0, 0)))
        sink2 = sink.astype(F32).reshape(kv_heads, groups, 1) * LOG2E
        args.append(jnp.broadcast_to(sink2, (kv_heads, groups, tq)).reshape(kv_heads, 1, n))
    return pl.pallas_call(
        functools.partial(_window_kernel, groups=groups, tq=tq, n_lat=n_lat, has_sink=sink is not None),
        grid=(kv_heads, n_lat // tq), in_specs=in_specs,
        out_specs=pl.BlockSpec((tq, groups * dv), lambda h, i: (i, h)),
        out_shape=jax.ShapeDtypeStruct((n_lat, kv_heads * groups * dv), BF16),
        compiler_params=_params("parallel", "parallel"), name="window_attention",
    )(*args)


def _outproj_kernel(*refs, lat_tiles, with_next):
    o1c_ref, o1l_ref, o2c_ref, o2l_ref, gate_ref, h_ref, mod_ref, w_ref = refs[:8]
    is_ctx = pl.program_id(0) >= lat_tiles
    w1 = o1l_ref.shape[1]
    d = h_ref.shape[1]
    gt = gate_ref[...].astype(F32)
    sg = gt * jax.nn.sigmoid(gt)
    o1 = jnp.where(is_ctx, o1c_ref[...].astype(F32), o1l_ref[...].astype(F32))
    o2 = jnp.where(is_ctx, o2c_ref[...].astype(F32), o2l_ref[...].astype(F32))
    t1 = (o1 * sg[:, :w1]).astype(BF16)
    t2 = (o2 * sg[:, w1:]).astype(BF16)
    y = _dot(t1, w_ref[0:w1, :]) + _dot(t2, w_ref[w1:, :])
    gm = jnp.where(is_ctx, mod_ref[1:2, 2 * d:3 * d], mod_ref[0:1, 2 * d:3 * d])
    h_new = h_ref[...] + gm * y
    if with_next:
        g_next_ref, mod_next_ref, out_ref, u_ref = refs[8:]
        u_ref[...] = _modulated_norm(h_new, g_next_ref[...], mod_next_ref, is_ctx).astype(u_ref.dtype)
    else:
        out_ref = refs[8]
    out_ref[...] = h_new


def out_proj(o1c, o1l, o2c, o2l, proj, h, mod, w, nxt):
    r, d = h.shape
    tm = ROW_TILE
    assert o1c.shape[0] == tm, "the context rows must be exactly one row tile"
    lt = o1l.shape[0] // tm
    w1, w2 = o1l.shape[1], o2l.shape[1]
    lat = lambda i: (jnp.minimum(i, lt - 1), 0)
    row = pl.BlockSpec((tm, d), lambda i: (i, 0))
    in_specs = [pl.BlockSpec((tm, w1), lambda i: (0, 0)), pl.BlockSpec((tm, w1), lat),
                pl.BlockSpec((tm, w2), lambda i: (0, 0)), pl.BlockSpec((tm, w2), lat),
                pl.BlockSpec((tm, w1 + w2), lambda i: (i, OFF_GATE // (w1 + w2))),
                row, pl.BlockSpec(mod.shape, lambda i: (0, 0)), pl.BlockSpec(w.shape, lambda i: (0, 0))]
    args = [o1c, o1l, o2c, o2l, proj, h, mod, w]
    if nxt is None:
        r = lt * tm
        out_specs, out_shape = row, jax.ShapeDtypeStruct((r, d), F32)
    else:
        in_specs += [pl.BlockSpec((1, d), lambda i: (0, 0)), pl.BlockSpec(nxt[1].shape, lambda i: (0, 0))]
        args += [nxt[0].reshape(1, d), nxt[1]]
        out_specs = [row, row]
        out_shape = [jax.ShapeDtypeStruct((r, d), F32), jax.ShapeDtypeStruct((r, d), BF16)]
    return pl.pallas_call(
        functools.partial(_outproj_kernel, lat_tiles=lt, with_next=nxt is not None), grid=(r // tm,),
        in_specs=in_specs, out_specs=out_specs, out_shape=out_shape,
        compiler_params=_params("parallel"), name="out_proj",
    )(*args)


def _conv3_kernel(x_ref, w_ref, b_ref, oc_ref, ol_ref, *, n_ctx):
    x = x_ref[...].astype(F32)
    r = x.shape[0]
    n_lat = r - n_ctx
    row = lax.broadcasted_iota(jnp.int32, (r, 1), 0)
    prev = jnp.where((row == 0) | (row == n_lat), 0.0, pltpu.roll(x, 1, 0))
    nxt = jnp.where((row == n_lat - 1) | (row == r - 1), 0.0, pltpu.roll(x, r - 1, 0))
    y = prev * w_ref[0:1, :] + x * w_ref[1:2, :] + nxt * w_ref[2:3, :] + b_ref[...]
    ol_ref[0] = y[:n_lat]
    oc_ref[0] = y[n_lat:]


def conv3(proj, col_off, w, b, n_ctx):
    r = proj.shape[0]
    cw = w.shape[1] // 3
    nb = cw // LANE
    cb = col_off // LANE
    return pl.pallas_call(
        functools.partial(_conv3_kernel, n_ctx=n_ctx), grid=(3, nb),
        in_specs=[pl.BlockSpec((r, LANE), lambda p, c: (0, cb + p * nb + c)),
                  pl.BlockSpec((3, LANE), lambda p, c: (0, p * nb + c)),
                  pl.BlockSpec((1, LANE), lambda p, c: (0, p * nb + c))],
        out_specs=[pl.BlockSpec((1, n_ctx, LANE), lambda p, c: (p, 0, c)),
                   pl.BlockSpec((1, r - n_ctx, LANE), lambda p, c: (p, 0, c))],
        out_shape=[jax.ShapeDtypeStruct((3, n_ctx, cw), F32), jax.ShapeDtypeStruct((3, r - n_ctx, cw), F32)],
        compiler_params=_params("parallel", "parallel"), name="conv3",
    )(proj, w, b.reshape(1, -1))


def _filt_kernel(z_ref, w1_ref, b1_ref, f1_ref, w2_ref, b2_ref, f2_ref, w3_ref, dl_ref, h_ref, tap_ref, nrm_ref, *,
                 tm, cw):
    i = pl.program_id(0)
    z = z_ref[...]

    def mm(a, w_r):
        return _dot3(*_split_bf16(a), *_split_bf16(w_r[...]))

    a = jnp.sin(f1_ref[...] * (mm(z, w1_ref) + b1_ref[...]))
    a = jnp.sin(f2_ref[...] * (mm(a, w2_ref) + b2_ref[...]))
    dec = jnp.exp(-z[:, 0:1] * dl_ref[...])
    h = _dot(a.astype(BF16), w3_ref[...]) * jnp.concatenate([dec] * (h_ref.shape[1] // cw), axis=1)
    h_ref[...] = h.astype(h_ref.dtype)
    row = i * tm + lax.broadcasted_iota(jnp.int32, (tm, 1), 0)
    col = lax.broadcasted_iota(jnp.int32, (1, h.shape[1]), 1)
    skip = (row == 0) & ((col // cw) % 2 == 1)
    part = jnp.sum(jnp.where(skip, 0.0, jnp.abs(h)), axis=0, keepdims=True)

    @pl.when(i == 0)
    def _():
        nrm_ref[...] = jnp.zeros(nrm_ref.shape, F32)
        tap_ref[...] = h[0:SUBLANES]

    nrm_ref[...] += jnp.broadcast_to(part, nrm_ref.shape)


def hyena_filter(z, w1p, b1, f1, w2, b2, f2, w3, deltas):
    l = z.shape[0]
    n = w3.shape[1]
    cw = deltas.shape[0]
    hid = w2.shape[0]
    tm = min(l, FILTER_ROWS)
    full = lambda a: pl.BlockSpec(a.shape, lambda i: (0, 0))
    ins = [w1p, b1.reshape(1, hid), f1.reshape(1, hid), w2, b2.reshape(1, hid), f2.reshape(1, hid), w3,
           deltas.reshape(1, cw)]
    small = pl.BlockSpec((SUBLANES, n), lambda i: (0, 0))
    return pl.pallas_call(
        functools.partial(_filt_kernel, tm=tm, cw=cw), grid=(l // tm,),
        in_specs=[pl.BlockSpec((tm, LANE), lambda i: (i, 0))] + [full(a) for a in ins],
        out_specs=[pl.BlockSpec((tm, n), lambda i: (i, 0)), small, small],
        out_shape=[jax.ShapeDtypeStruct((l, n), BF16), jax.ShapeDtypeStruct((SUBLANES, n), F32),
                   jax.ShapeDtypeStruct((SUBLANES, n), F32)],
        compiler_params=_params("arbitrary"), name="hyena_filter",
    )(z, *ins)


@functools.lru_cache(maxsize=None)
def _dft_consts(a):
    n = a * a
    hh = a // 2
    n1 = np.arange(hh)
    k1 = np.arange(hh)
    n2 = np.arange(a)
    k2 = np.arange(a)
    ang1 = 2 * np.pi * np.outer(k1 + 0.5, n1) / a
    m1 = np.zeros((hh, 2, hh))
    m1[:, 0], m1[:, 1] = np.cos(ang1), -np.sin(ang1)
    m1 = m1.reshape(2 * hh, hh)
    phi = 2 * np.pi * (n2[None, None, :] * (k1[:, None, None] + 0.5) / n + n2[None, None, :] * k2[None, :, None] / a)
    c, s = np.cos(phi), np.sin(phi)
    g = np.zeros((hh, 2, a, 2, a))
    g[:, 0, :, 0, :], g[:, 0, :, 1, :], g[:, 1, :, 0, :], g[:, 1, :, 1, :] = c, s, -s, c
    g = g.reshape(hh, 2 * a, 2 * a)
    gt = np.ascontiguousarray(np.transpose(g, (0, 2, 1)))
    al = 2 * np.pi * np.outer(n1, k1 + 0.5) / a
    mb = np.zeros((hh, hh, 2))
    mb[:, :, 0], mb[:, :, 1] = 2 / n * np.cos(al), -2 / n * np.sin(al)
    mb = mb.reshape(hh, 2 * hh)
    eye = np.eye(ROWS_BF16)
    return tuple(x.astype(np.float32) for x in (np.kron(m1, eye), g, gt, np.kron(mb, eye)))


@functools.lru_cache(maxsize=None)
def _dense_dft_consts(l):
    n = 2 * l
    ang = 2 * np.pi * np.outer(np.arange(l) + 0.5, np.arange(l)) / n
    mf = np.concatenate([np.cos(ang), -np.sin(ang)], axis=0)
    mi = np.concatenate([2 / n * np.cos(ang.T), -2 / n * np.sin(ang.T)], axis=1)
    return tuple(x.astype(np.float32) for x in (mf, mi))


def _stage1_kernel(x_ref, a_ref, o_ref, a16_ref):
    @pl.when((pl.program_id(0) == 0) & (pl.program_id(1) == 0))
    def _():
        a16_ref[...] = a_ref[...].astype(BF16)

    _, hh, t, ct = x_ref.shape
    x = x_ref[0].reshape(hh * t, ct).astype(BF16)
    o_ref[...] = _dot(a16_ref[...], x).astype(o_ref.dtype).reshape(o_ref.shape)


def dft_stage1(x4, p, a1):
    _, hh, a, c = x4.shape
    ct = _pick_tile(c, DFT_COL_CAP)
    t = ROWS_BF16
    return pl.pallas_call(
        _stage1_kernel, grid=(a // t, c // ct),
        in_specs=[pl.BlockSpec((1, hh, t, ct), lambda j, q: (p, 0, j, q)),
                  pl.BlockSpec(a1.shape, lambda j, q: (0, 0))],
        out_specs=pl.BlockSpec((2 * hh, t, ct), lambda j, q: (0, j, q)),
        out_shape=jax.ShapeDtypeStruct((2 * hh, a, c), BF16),
        scratch_shapes=[pltpu.VMEM(a1.shape, BF16)],
        compiler_params=_params("arbitrary", "arbitrary"), name="dft_stage1",
    )(x4, a1)


def _spectrum_product(x, h0, h1, tap0, n0, n1):
    half = x.shape[0] // 2
    inv = 1.0 / (n0 + n1)
    kr = (h0[:half] + h1[:half] - tap0) * inv
    ki = (h0[half:] - h1[half:]) * inv
    xr, xi = x[:half], x[half:]
    return jnp.concatenate([xr * kr - xi * ki, xr * ki + xi * kr], axis=0)


K1_PER_STEP = 4


def _mid_filter_kernel(y0_ref, y1_ref, g_ref, tap_ref, n0_ref, n1_ref, o_ref):
    inv = 1.0 / (n0_ref[0:1, :] + n1_ref[0:1, :])
    tap = tap_ref[0:1, :]
    for b in range(y0_ref.shape[0]):
        g = g_ref[b].astype(BF16)
        x0 = _dot(g, y0_ref[b])
        x1 = _dot(g, y1_ref[b])
        half = x0.shape[0] // 2
        kr = (x0[:half] + x1[:half] - tap) * inv
        ki = (x0[half:] - x1[half:]) * inv
        o_ref[b] = jnp.concatenate([kr, ki], axis=0).astype(o_ref.dtype)


def filter_spectrum(h, taps, nrm, consts, a):
    a1, g, _, _ = consts
    l, n = h.shape
    c = n // 4
    hh = a // 2
    y = dft_stage1(h.reshape(1, hh, a, n), 0, a1).reshape(hh, 2 * a, n)
    kb = K1_PER_STEP
    slab = lambda side: pl.BlockSpec((kb, 2 * a, c), lambda k, o: (k, 0, 2 * o + side))
    row = lambda side: pl.BlockSpec((SUBLANES, c), lambda k, o: (0, 2 * o + side))
    return pl.pallas_call(
        _mid_filter_kernel, grid=(hh // kb, 2),
        in_specs=[slab(0), slab(1), pl.BlockSpec((kb, 2 * a, 2 * a), lambda k, o: (k, 0, 0)), row(1), row(0), row(1)],
        out_specs=pl.BlockSpec((kb, 2 * a, c), lambda k, o: (k, 0, o)),
        out_shape=jax.ShapeDtypeStruct((hh, 2 * a, 2 * c), BF16),
        compiler_params=_params("parallel", "parallel"), name="dft_mid_filter",
    )(y, y, g, taps, nrm, nrm)


def _mid_conv_kernel(y_ref, g_ref, gt_ref, kf_ref, o_ref):
    for b in range(y_ref.shape[0]):
        x = _dot(g_ref[b].astype(BF16), y_ref[b])
        kf = kf_ref[b].astype(F32)
        half = x.shape[0] // 2
        xr, xi, kr, ki = x[:half], x[half:], kf[:half], kf[half:]
        z = jnp.concatenate([xr * kr - xi * ki, xr * ki + xi * kr], axis=0).astype(BF16)
        o_ref[b] = _dot(gt_ref[b].astype(BF16), z).astype(o_ref.dtype)


def dft_mid_conv(y, g, gt, kf, order):
    hh, a2, c = y.shape
    kb = K1_PER_STEP
    slab = pl.BlockSpec((kb, a2, c), lambda k: (k, 0, 0))
    mat = pl.BlockSpec((kb, a2, a2), lambda k: (k, 0, 0))
    return pl.pallas_call(
        _mid_conv_kernel, grid=(hh // kb,),
        in_specs=[slab, mat, mat, pl.BlockSpec((kb, a2, c), lambda k: (k, 0, order))],
        out_specs=slab, out_shape=jax.ShapeDtypeStruct((hh, a2, c), BF16),
        compiler_params=_params("parallel"), name="dft_mid_conv",
    )(y, g, gt, kf)


def _last_kernel(b_ref, a_ref, xa_ref, zb_ref, bias_ref, o_ref, a16_ref):
    @pl.when((pl.program_id(0) == 0) & (pl.program_id(1) == 0))
    def _():
        a16_ref[...] = a_ref[...].astype(BF16)

    rows, t, ct = b_ref.shape
    conv = _dot(a16_ref[...], b_ref[...].reshape(rows * t, ct)).reshape(rows // 2, t, ct)
    o_ref[0] = (xa_ref[0] * (conv + bias_ref[...] * zb_ref[0])).astype(o_ref.dtype)


def dft_last(b, a2m, xa4, pa, zb4, pb, bias_row):
    rows, a, c = b.shape
    hh = rows // 2
    ct = _pick_tile(c, DFT_COL_CAP)
    t = ROWS_BF16
    sig = lambda pp: pl.BlockSpec((1, hh, t, ct), lambda j, q: (pp, 0, j, q))
    return pl.pallas_call(
        _last_kernel, grid=(a // t, c // ct),
        in_specs=[pl.BlockSpec((rows, t, ct), lambda j, q: (0, j, q)),
                  pl.BlockSpec(a2m.shape, lambda j, q: (0, 0)),
                  sig(pa), sig(pb), pl.BlockSpec((1, ct), lambda j, q: (0, q))],
        out_specs=sig(0), out_shape=jax.ShapeDtypeStruct((1, hh, a, c), F32),
        scratch_shapes=[pltpu.VMEM(a2m.shape, BF16)],
        compiler_params=_params("arbitrary", "arbitrary"), name="dft_last",
    )(b, a2m, xa4, zb4, bias_row)


def long_conv_gated(u3, kf, bias, consts, a):
    a1, g, gt, a2m = consts
    _, l, c = u3.shape
    hh = a // 2
    u4 = u3.reshape(3, hh, a, c)

    def conv_gate(src4, p_src, xa_p, order):
        y = dft_stage1(src4, p_src, a1).reshape(hh, 2 * a, c)
        bb = dft_mid_conv(y, g, gt, kf, order).reshape(2 * hh, a, c)
        return dft_last(bb, a2m, u4, xa_p, src4, p_src, bias[order].reshape(1, c))

    z2 = conv_gate(u4, 2, 0, 0)
    return conv_gate(z2, 0, 1, 1).reshape(l, c)


def _ctx_spec_kernel(h_ref, mf_ref, o_ref):
    o_ref[...] = _dot(mf_ref[...].astype(BF16), h_ref[...])


def ctx_filter_spectrum(h, mf):
    l, n = h.shape
    tn = _pick_tile(n, COL_CAP)
    return pl.pallas_call(
        _ctx_spec_kernel, grid=(n // tn,),
        in_specs=[pl.BlockSpec((l, tn), lambda j: (0, j)), pl.BlockSpec(mf.shape, lambda j: (0, 0))],
        out_specs=pl.BlockSpec((2 * l, tn), lambda j: (0, j)),
        out_shape=jax.ShapeDtypeStruct((2 * l, n), F32),
        compiler_params=_params("parallel"), name="ctx_filter_spectrum",
    )(h, mf)


def _ctx_hyena_kernel(u_ref, mf_ref, mi_ref, hf0a, hf1a, hf0b, hf1b, tap0, tap1, n0a, n1a, n0b, n1b, bias_ref, o_ref):
    x1, x2, z = u_ref[0], u_ref[1], u_ref[2]

    mf, mi = mf_ref[...].astype(BF16), mi_ref[...].astype(BF16)

    def conv(sig, h0_ref, h1_ref, tap_ref, n0_ref, n1_ref):
        x = _dot(mf, sig.astype(BF16))
        zz = _spectrum_product(x, h0_ref[...], h1_ref[...], tap_ref[0:1, :], n0_ref[0:1, :], n1_ref[0:1, :])
        return _dot(mi, zz.astype(BF16))

    z2 = x1 * (conv(z, hf0a, hf1a, tap0, n0a, n1a) + bias_ref[0:1, :] * z)
    o_ref[...] = (x2 * (conv(z2, hf0b, hf1b, tap1, n0b, n1b) + bias_ref[1:2, :] * z2)).astype(o_ref.dtype)


def ctx_hyena(u3, hf, taps, nrm, bias, mf, mi):
    _, l, c = u3.shape
    tc = _pick_tile(c, DFT_COL_CAP)
    nb = c // tc
    const = lambda a: pl.BlockSpec(a.shape, lambda j: (0, 0))
    col = lambda rows, q: pl.BlockSpec((rows, tc), lambda j: (0, q * nb + j))
    mats = [mf, mi]
    return pl.pallas_call(
        _ctx_hyena_kernel, grid=(nb,),
        in_specs=[pl.BlockSpec((3, l, tc), lambda j: (0, 0, j))] + [const(m) for m in mats]
        + [col(2 * l, 0), col(2 * l, 1), col(2 * l, 2), col(2 * l, 3), col(8, 1), col(8, 3),
           col(8, 0), col(8, 1), col(8, 2), col(8, 3), pl.BlockSpec((2, tc), lambda j: (0, j))],
        out_specs=pl.BlockSpec((l, tc), lambda j: (0, j)),
        out_shape=jax.ShapeDtypeStruct((l, c), BF16),
        compiler_params=_params("parallel"), name="ctx_hyena",
    )(u3, *mats, hf, hf, hf, hf, taps, taps, nrm, nrm, nrm, nrm, bias)


def _axial_tables(n_tokens, n_rot):
    rows = n_tokens // GRID_W
    row = jnp.broadcast_to(jnp.arange(rows)[:, None], (rows, GRID_W)).reshape(-1).astype(F32)
    col = jnp.broadcast_to(jnp.arange(GRID_W)[None, :], (rows, GRID_W)).reshape(-1).astype(F32)
    n_freq = n_rot // 4
    inv = ROPE_THETA ** (-jnp.arange(n_freq, dtype=F32) / n_freq)
    ang = jnp.concatenate([row[:, None] * inv, col[:, None] * inv], axis=-1)
    return jnp.cos(ang), jnp.sin(ang)


def _rope_tables_full(s, n_ctx):
    cos, sin = _axial_tables(s, HEAD_DIM)
    t0 = jnp.concatenate([jnp.concatenate([cos, cos], axis=1), jnp.ones((n_ctx, LANE), F32)], axis=0)
    t1 = jnp.concatenate([jnp.concatenate([-sin, sin], axis=1), jnp.zeros((n_ctx, LANE), F32)], axis=0)
    return t0, t1


def _rope_tables_half(s, n_ctx):
    cos, sin = _axial_tables(s, ROPE_DIM)
    q = ROPE_DIM // 2
    zq = jnp.zeros((s, q), F32)
    z2 = jnp.zeros((s, LANE - ROPE_DIM), F32)
    t0 = jnp.concatenate([cos, cos, z2], axis=1)
    t1 = jnp.concatenate([-sin, zq, z2], axis=1)
    t2 = jnp.concatenate([zq, sin, z2], axis=1)
    c0 = jnp.concatenate([jnp.ones((n_ctx, ROPE_DIM), F32), jnp.zeros((n_ctx, LANE - ROPE_DIM), F32)], axis=1)
    cz = jnp.zeros((n_ctx, LANE), F32)
    return jnp.concatenate([t0, c0], axis=0), jnp.concatenate([t1, cz], axis=0), jnp.concatenate([t2, cz], axis=0)


def _filter_features(l):
    pos = jnp.arange(l, dtype=F32)
    t = pos / max(l - 1, 1)
    bands = jnp.linspace(1e-4, HYENA_BANDS - 1, HYENA_BANDS, dtype=F32)
    ang = (2.0 * math.pi / l) * pos[:, None] * bands[None, :]
    z = jnp.concatenate([t[:, None], jnp.cos(ang), -jnp.sin(ang)], axis=-1)
    return jnp.pad(z, ((0, 0), (0, LANE - HYENA_EMB)))


Q_STACK = 1024
KV_TILE_CAP = 1408


def _attention_pair(qp, kp, vt, s, n_ctx, *, kv_heads, groups, window, sink=None):
    common = dict(kv_heads=kv_heads, groups=groups, sink=sink)
    o_ctx = flash(qp, kp, vt, n_q=n_ctx, q_row0=s, kv_row0=s, n_kv_rows=n_ctx, tq=n_ctx, tk=n_ctx, **common)
    if window:
        o_lat = window_attention(qp, kp, vt, kv_heads=kv_heads, groups=groups, n_lat=s, n_ctx=n_ctx, tq=ROW_TILE,
                                 sink=sink)
    else:
        o_lat = flash(qp, kp, vt, n_q=s, q_row0=0, kv_row0=0, n_kv_rows=n_ctx + s, tq=Q_STACK // groups,
                      tk=_pick_tile(n_ctx + s, KV_TILE_CAP), **common)
    return o_ctx, o_lat


def _even_layer(u, s, n_ctx, w_in, qn_g, kn_g, sink, conv_w, conv_b, fparams, hy_bias, rope_full, feats, dft):
    w_all, e = w_in
    src_gate = ATT_WIDTH + 2 * KV_WIDTH + 3 * B_WIDTH
    proj = matmul_permuted(u, w_all, e, [(src_gate, BRANCH), (0, ATT_WIDTH), (ATT_WIDTH, 2 * KV_WIDTH),
                                         (ATT_WIDTH + 2 * KV_WIDTH, 3 * B_WIDTH)], F32)
    qp = head_prep(proj, OFF_Q, A_HEADS, qn_g, rope_full, HEAD_DIM ** -0.5 * LOG2E, True)
    kp = head_prep(proj, OFF_K, A_KV_HEADS, kn_g, rope_full, 1.0)
    vt = transpose_heads(proj, OFF_V, A_KV_HEADS)
    a_ctx, a_lat = _attention_pair(qp, kp, vt, s, n_ctx, kv_heads=A_KV_HEADS, groups=A_HEADS // A_KV_HEADS,
                                   window=True, sink=sink)
    w1, b1, f1, w2, b2, f2, w3 = fparams
    w1p = jnp.pad(w1, ((0, LANE - HYENA_EMB), (0, 0)))
    w3 = w3.astype(BF16)
    deltas = jnp.linspace(DECAY_MAX, DECAY_MIN, B_WIDTH, dtype=F32)
    uc3, ul3 = conv3(proj, OFF_REST, conv_w, conv_b, n_ctx)
    a = int(round(math.sqrt(2 * s)))
    consts, (mf, mi) = dft
    h_lat, tap_lat, nrm_lat = hyena_filter(feats[0], w1p, b1, f1, w2, b2, f2, w3, deltas)
    kf_lat = filter_spectrum(h_lat, tap_lat, nrm_lat, consts, a)
    b_lat = long_conv_gated(ul3, kf_lat, hy_bias, consts, a)
    h_ctx, tap_ctx, nrm_ctx = hyena_filter(feats[1], w1p, b1, f1, w2, b2, f2, w3, deltas)
    hf_ctx = ctx_filter_spectrum(h_ctx, mf)
    b_ctx = ctx_hyena(uc3, hf_ctx, tap_ctx, nrm_ctx, hy_bias, mf, mi)
    return proj, a_ctx, a_lat, b_ctx, b_lat


def _odd_layer(u, s, n_ctx, w_in, qn_g, kn_g, cq_g, ckv_g, wuq, wukv, mq_g, mk_g, rope_full, rope_half):
    d = w_in.shape[0]
    parts = jnp.split(w_in, np.cumsum([ATT_WIDTH, KV_WIDTH, KV_WIDTH, Q_LORA, KV_LORA, ROPE_DIM]).tolist(), axis=1)
    wq, wk, wv, wmq, wmkv, wmkr, wg = parts
    w_perm = jnp.concatenate([wg, wq, wk, wv, wmq, wmkv, wmkr, jnp.zeros((d, LANE - ROPE_DIM), w_in.dtype)], axis=1)
    proj = matmul(u, w_perm.astype(BF16), F32)
    qp = head_prep(proj, OFF_Q, C_HEADS, qn_g, rope_full, HEAD_DIM ** -0.5 * LOG2E, True)
    kp = head_prep(proj, OFF_K, C_KV_HEADS, kn_g, rope_full, 1.0)
    vt = transpose_heads(proj, OFF_V, C_KV_HEADS)
    c_ctx, c_lat = _attention_pair(qp, kp, vt, s, n_ctx, kv_heads=C_KV_HEADS, groups=C_HEADS // C_KV_HEADS,
                                   window=False)
    pad_r = LANE - ROPE_DIM
    wuq3 = wuq.reshape(Q_LORA, M_HEADS, QK_DIM)
    wuq_p = jnp.concatenate([wuq3[:, :, :NOPE_DIM].reshape(Q_LORA, -1),
                             jnp.pad(wuq3[:, :, NOPE_DIM:], ((0, 0), (0, 0), (0, pad_r))).reshape(Q_LORA, -1)], axis=1)
    wukv3 = wukv.reshape(KV_LORA, M_HEADS, NOPE_DIM + V_DIM)
    wukv_p = jnp.concatenate([wukv3[:, :, :NOPE_DIM].reshape(KV_LORA, -1),
                              wukv3[:, :, NOPE_DIM:].reshape(KV_LORA, -1)], axis=1)
    q_raw = norm_matmul(proj, OFF_MQ, cq_g, wuq_p.astype(BF16))
    kv_raw = norm_matmul(proj, OFF_MKV, ckv_g, wukv_p.astype(BF16))
    scale = QK_DIM ** -0.5 * LOG2E
    gq_r = jnp.pad(mq_g[NOPE_DIM:], (0, pad_r))
    gk_r = jnp.pad(mk_g[NOPE_DIM:], (0, pad_r))
    q_full = mla_prep(q_raw, 0, q_raw, M_HEADS * NOPE_DIM, False, M_HEADS, mq_g[:NOPE_DIM], gq_r, rope_half, scale,
                      True)
    k_full = mla_prep(kv_raw, 0, proj, OFF_MKR, True, M_HEADS, mk_g[:NOPE_DIM], gk_r, rope_half, 1.0, False)
    vmt = transpose_heads(kv_raw, M_HEADS * NOPE_DIM, M_HEADS)
    d_ctx, d_lat = _attention_pair(q_full, k_full, vmt, s, n_ctx, kv_heads=M_HEADS, groups=1, window=False)
    return proj, c_ctx, c_lat, d_ctx, d_lat


def kernel(x, c, ctx, c_ctx, ada_w, ada_b, norm_g, w_out, ev_w_in, ev_qn_g, ev_kn_g, ev_sink, ev_conv_w, ev_conv_b, hy_w1, hy_b1, hy_f1, hy_w2, hy_b2, hy_f2, hy_w3, hy_bias, od_w_in, od_qn_g, od_kn_g, od_cq_g, od_ckv_g, od_wuq, od_wukv, od_mq_g, od_mk_g):
    _, s, d = x.shape
    n_ctx = ctx.shape[1]
    depth = ada_w.shape[0]
    h = jnp.concatenate([x[0], ctx[0]], axis=0)
    cc = jnp.zeros((8, d), F32).at[0].set(c[0]).at[1].set(c_ctx)
    mod = ada_mod(cc, ada_w, ada_b)
    rope_full = _rope_tables_full(s, n_ctx)
    rope_half = _rope_tables_half(s, n_ctx)
    feats = (_filter_features(s), _filter_features(n_ctx))
    as_arrays = lambda mats: tuple(jnp.asarray(m) for m in mats)
    dft = (as_arrays(_dft_consts(int(round(math.sqrt(2 * s))))), as_arrays(_dense_dft_consts(n_ctx)))
    u = norm_mod(h, norm_g[0], mod[0], n_ctx)
    for i in range(depth):
        if i % 2 == 0:
            e = i // 2
            proj, o1c, o1l, o2c, o2l = _even_layer(
                u, s, n_ctx, (ev_w_in, e), ev_qn_g[e], ev_kn_g[e], ev_sink[e], ev_conv_w[e], ev_conv_b[e],
                (hy_w1[e], hy_b1[e], hy_f1[e], hy_w2[e], hy_b2[e], hy_f2[e], hy_w3[e]), hy_bias[e], rope_full, feats,
                dft)
        else:
            o = i // 2
            proj, o1c, o1l, o2c, o2l = _odd_layer(
                u, s, n_ctx, od_w_in[o], od_qn_g[o], od_kn_g[o], od_cq_g[o], od_ckv_g[o], od_wuq[o], od_wukv[o],
                od_mq_g[o], od_mk_g[o], rope_full, rope_half)
        w = w_out[i].astype(BF16)
        if i < depth - 1:
            h, u = out_proj(o1c, o1l, o2c, o2l, proj, h, mod[i], w, (norm_g[i + 1], mod[i + 1]))
        else:
            h = out_proj(o1c, o1l, o2c, o2l, proj, h, mod[i], w, None)
    return h[None]
```

```python
import functools
import math

import numpy as np
import jax
import jax.numpy as jnp
from jax import lax
from jax.experimental import pallas as pl
from jax.experimental.pallas import tpu as pltpu

F32 = jnp.float32
BF16 = jnp.bfloat16

GRID_W = 64
HEAD_DIM = 128
ROPE_THETA = 10000.0
EPS = 1e-6
NEG_INF = -1e30
A_HEADS, A_KV_HEADS, WINDOW = 8, 2, 128
B_WIDTH = 1024
HYENA_BANDS = 16
HYENA_EMB = 1 + 2 * HYENA_BANDS
DECAY_TARGET = 1e-2
DECAY_MAX = abs(math.log(DECAY_TARGET)) / 0.3
DECAY_MIN = abs(math.log(DECAY_TARGET)) / 1.5
C_HEADS, C_KV_HEADS = 8, 2
M_HEADS, Q_LORA, KV_LORA, NOPE_DIM, ROPE_DIM, V_DIM = 8, 512, 256, 128, 64, 128
QK_DIM = NOPE_DIM + ROPE_DIM
ATT_WIDTH = A_HEADS * HEAD_DIM
KV_WIDTH = A_KV_HEADS * HEAD_DIM
BRANCH = ATT_WIDTH + B_WIDTH

LOG2E = math.log2(math.e)
LANE = 128
SUBLANES = 8
ROWS_BF16 = 16
ROW_TILE = 256
VMEM_LIMIT = 48 * 1024 * 1024
COL_CAP = 1024
PROJ_ROW_CAP = 2112
PLAIN_ROW_CAP = 1536
PREP_ROW_CAP = 768
FILTER_ROWS = 512
DFT_COL_CAP = 1024

OFF_GATE, OFF_Q, OFF_K, OFF_V, OFF_REST = 0, BRANCH, BRANCH + ATT_WIDTH, BRANCH + ATT_WIDTH + KV_WIDTH, BRANCH + ATT_WIDTH + 2 * KV_WIDTH
OFF_MQ, OFF_MKV, OFF_MKR = OFF_REST, OFF_REST + Q_LORA, OFF_REST + Q_LORA + KV_LORA


def _params(*sem):
    return pltpu.CompilerParams(dimension_semantics=sem, vmem_limit_bytes=VMEM_LIMIT)


def _split_bf16(x):
    hi = x.astype(BF16)
    return hi, (x - hi.astype(F32)).astype(BF16)


def _dot(a, b):
    return jnp.dot(a, b, preferred_element_type=F32)


def _dot3(a_hi, a_lo, b_hi, b_lo):
    return _dot(a_hi, b_hi) + (_dot(a_hi, b_lo) + _dot(a_lo, b_hi))


def _ada_kernel(c_ref, w_ref, b_ref, o_ref):
    c = c_ref[...]
    a = (c * jax.nn.sigmoid(c)).astype(BF16)
    o_ref[0] = _dot(a, w_ref[0].astype(BF16)) + b_ref[0]


def ada_mod(cc, ada_w, ada_b):
    depth, d, n = ada_w.shape
    tn = _pick_tile(n, COL_CAP)
    return pl.pallas_call(
        _ada_kernel, grid=(depth, n // tn),
        in_specs=[pl.BlockSpec((8, d), lambda l, j: (0, 0)),
                  pl.BlockSpec((1, d, tn), lambda l, j: (l, 0, j)),
                  pl.BlockSpec((1, 1, tn), lambda l, j: (l, 0, j))],
        out_specs=pl.BlockSpec((1, 8, tn), lambda l, j: (l, 0, j)),
        out_shape=jax.ShapeDtypeStruct((depth, 8, n), F32),
        compiler_params=_params("parallel", "parallel"), name="ada_mod",
    )(cc, ada_w, ada_b.reshape(depth, 1, n))


def _modulated_norm(x, g, mod_ref, is_ctx):
    d = x.shape[1]
    y = x * lax.rsqrt(jnp.mean(x * x, axis=-1, keepdims=True) + EPS) * g
    sh = jnp.where(is_ctx, mod_ref[1:2, 0:d], mod_ref[0:1, 0:d])
    sc = jnp.where(is_ctx, mod_ref[1:2, d:2 * d], mod_ref[0:1, d:2 * d])
    return y * (1.0 + sc) + sh


def _norm_kernel(x_ref, g_ref, mod_ref, o_ref, *, lat_tiles):
    is_ctx = pl.program_id(0) >= lat_tiles
    o_ref[...] = _modulated_norm(x_ref[...], g_ref[...], mod_ref, is_ctx).astype(o_ref.dtype)


def norm_mod(h, g, mod, n_ctx):
    r, d = h.shape
    tm = ROW_TILE
    return pl.pallas_call(
        functools.partial(_norm_kernel, lat_tiles=(r - n_ctx) // tm), grid=(r // tm,),
        in_specs=[pl.BlockSpec((tm, d), lambda i: (i, 0)),
                  pl.BlockSpec((1, d), lambda i: (0, 0)),
                  pl.BlockSpec(mod.shape, lambda i: (0, 0))],
        out_specs=pl.BlockSpec((tm, d), lambda i: (i, 0)),
        out_shape=jax.ShapeDtypeStruct((r, d), BF16),
        compiler_params=_params("parallel"), name="norm_mod",
    )(h, g.reshape(1, d), mod)


def _mm_kernel(a_ref, b_ref, o_ref):
    o_ref[...] = _dot(a_ref[...], b_ref[...]).astype(o_ref.dtype)


def _pick_tile(n, cap, unit=LANE):
    best = unit
    for t in range(unit, cap + 1, unit):
        if n % t == 0:
            best = t
    return best


def matmul(a, b, out_dtype):
    m, k = a.shape
    n = b.shape[1]
    tm = _pick_tile(m, PLAIN_ROW_CAP, SUBLANES)
    tn = _pick_tile(n, COL_CAP)
    return pl.pallas_call(
        _mm_kernel, grid=(m // tm, n // tn),
        in_specs=[pl.BlockSpec((tm, k), lambda i, j: (i, 0)),
                  pl.BlockSpec((k, tn), lambda i, j: (0, j))],
        out_specs=pl.BlockSpec((tm, tn), lambda i, j: (i, j)),
        out_shape=jax.ShapeDtypeStruct((m, n), out_dtype),
        compiler_params=_params("parallel", "parallel"), name="in_proj",
    )(a, b)


def _mm_cast_kernel(a_ref, b_ref, o_ref):
    o_ref[...] = _dot(a_ref[...], b_ref[0].astype(a_ref.dtype)).astype(o_ref.dtype)


def matmul_permuted(a, w3, layer, segments, out_dtype):
    m, k = a.shape
    n = sum(width for _, width in segments)
    tm = _pick_tile(m, PROJ_ROW_CAP, SUBLANES)
    tn = LANE
    for t in range(LANE, COL_CAP + 1, LANE):
        if all(start % t == 0 and width % t == 0 for start, width in segments):
            tn = t
    src_tiles = [(start + off) // tn for start, width in segments for off in range(0, width, tn)]

    def src(j):
        idx = src_tiles[0]
        for t, s_t in enumerate(src_tiles[1:], 1):
            idx = jnp.where(j == t, s_t, idx)
        return idx
    return pl.pallas_call(
        _mm_cast_kernel, grid=(m // tm, n // tn),
        in_specs=[pl.BlockSpec((tm, k), lambda i, j: (i, 0)),
                  pl.BlockSpec((1, k, tn), lambda i, j: (layer, 0, src(j)))],
        out_specs=pl.BlockSpec((tm, tn), lambda i, j: (i, j)),
        out_shape=jax.ShapeDtypeStruct((m, n), out_dtype),
        compiler_params=_params("parallel", "parallel"), name="in_proj",
    )(a, w3)


def _normmm_kernel(x_ref, g_ref, w_ref, o_ref):
    x = x_ref[...].astype(F32)
    y = x * lax.rsqrt(jnp.mean(x * x, axis=-1, keepdims=True) + EPS) * g_ref[...]
    o_ref[...] = _dot(y.astype(BF16), w_ref[...]).astype(o_ref.dtype)


def norm_matmul(x, col_off, g, w):
    r = x.shape[0]
    k, n = w.shape
    tm = _pick_tile(r, PREP_ROW_CAP, SUBLANES)
    cb = col_off // k
    return pl.pallas_call(
        _normmm_kernel, grid=(r // tm,),
        in_specs=[pl.BlockSpec((tm, k), lambda i: (i, cb)),
                  pl.BlockSpec((1, k), lambda i: (0, 0)),
                  pl.BlockSpec((k, n), lambda i: (0, 0))],
        out_specs=pl.BlockSpec((tm, n), lambda i: (i, 0)),
        out_shape=jax.ShapeDtypeStruct((r, n), F32),
        compiler_params=_params("parallel"), name="norm_matmul",
    )(x, g.reshape(1, k), w)


def _prep_kernel(x_ref, g_ref, t0_ref, t1_ref, o_ref, *, scale, n_heads, head_major):
    g, t0, t1 = g_ref[...], t0_ref[...], t1_ref[...]
    for h in range(n_heads):
        x = x_ref[:, h * LANE:(h + 1) * LANE].astype(F32)
        xn = x * lax.rsqrt(jnp.mean(x * x, axis=-1, keepdims=True) + EPS) * g
        y = ((xn * t0 + pltpu.roll(xn, HEAD_DIM // 2, 1) * t1) * scale).astype(o_ref.dtype)
        if head_major:
            o_ref[h] = y
        else:
            o_ref[:, h * LANE:(h + 1) * LANE] = y


def head_prep(x, col_off, n_heads, g, tables, scale, head_major=False):
    r = x.shape[0]
    w = n_heads * LANE
    tm = _pick_tile(r, PREP_ROW_CAP, SUBLANES)
    tab_spec = pl.BlockSpec((tm, LANE), lambda i: (i, 0))
    if head_major:
        out_spec = pl.BlockSpec((n_heads, tm, LANE), lambda i: (0, i, 0))
        out_shape = jax.ShapeDtypeStruct((n_heads, r, LANE), BF16)
    else:
        out_spec = pl.BlockSpec((tm, w), lambda i: (i, 0))
        out_shape = jax.ShapeDtypeStruct((r, w), BF16)
    return pl.pallas_call(
        functools.partial(_prep_kernel, scale=scale, n_heads=n_heads, head_major=head_major),
        grid=(r // tm,),
        in_specs=[pl.BlockSpec((tm, w), lambda i: (i, col_off // w)),
                  pl.BlockSpec((1, LANE), lambda i: (0, 0)), tab_spec, tab_spec],
        out_specs=out_spec, out_shape=out_shape,
        compiler_params=_params("parallel"), name="head_prep",
    )(x, g.reshape(1, LANE), *tables)


def _mla_prep_kernel(xn_ref, xr_ref, gn_ref, gr_ref, t0_ref, t1_ref, t2_ref, o_ref, *, n_heads, shared_rope, scale,
                     head_major):
    gn, gr = gn_ref[...], gr_ref[...]
    t0, t1, t2 = t0_ref[...], t1_ref[...], t2_ref[...]

    def norm(x, n_real, g):
        return x * lax.rsqrt(jnp.sum(x * x, axis=-1, keepdims=True) * (1.0 / n_real) + EPS) * g

    def rope_part(x):
        xn = norm(x.astype(F32), ROPE_DIM, gr)
        return xn * t0 + pltpu.roll(xn, 96, 1) * t1 + pltpu.roll(xn, 32, 1) * t2

    if shared_rope:
        shared = rope_part(xr_ref[...])
    for h in range(n_heads):
        cols = slice(h * LANE, (h + 1) * LANE)
        a = norm(xn_ref[:, cols].astype(F32), NOPE_DIM, gn)
        b = shared if shared_rope else rope_part(xr_ref[:, cols])
        y = (jnp.concatenate([a, b], axis=1) * scale).astype(o_ref.dtype)
        if head_major:
            o_ref[h] = y
        else:
            o_ref[:, 2 * h * LANE:2 * (h + 1) * LANE] = y


def mla_prep(xn, xn_off, xr, xr_off, shared_rope, n_heads, gn, gr, tables, scale, head_major):
    r = xn.shape[0]
    w = n_heads * LANE
    wr = LANE if shared_rope else w
    tm = _pick_tile(r, PREP_ROW_CAP, SUBLANES)
    row = lambda width: pl.BlockSpec((1, width), lambda i: (0, 0))
    tab = pl.BlockSpec((tm, LANE), lambda i: (i, 0))
    if head_major:
        out_spec = pl.BlockSpec((n_heads, tm, 2 * LANE), lambda i: (0, i, 0))
        out_shape = jax.ShapeDtypeStruct((n_heads, r, 2 * LANE), BF16)
    else:
        out_spec = pl.BlockSpec((tm, 2 * w), lambda i: (i, 0))
        out_shape = jax.ShapeDtypeStruct((r, 2 * w), BF16)
    return pl.pallas_call(
        functools.partial(_mla_prep_kernel, n_heads=n_heads, shared_rope=shared_rope, scale=scale,
                          head_major=head_major),
        grid=(r // tm,),
        in_specs=[pl.BlockSpec((tm, w), lambda i: (i, xn_off // w)), pl.BlockSpec((tm, wr), lambda i: (i, xr_off // wr)),
                  row(LANE), row(LANE), tab, tab, tab],
        out_specs=out_spec, out_shape=out_shape,
        compiler_params=_params("parallel"), name="mla_prep",
    )(xn, xr, gn.reshape(1, LANE), gr.reshape(1, LANE), *tables)


def _vt_kernel(x_ref, o_ref):
    o_ref[...] = x_ref[...].astype(F32).T.astype(o_ref.dtype)


def transpose_heads(x, col_off, n_heads):
    r = x.shape[0]
    w = n_heads * LANE
    tm = _pick_tile(r, PREP_ROW_CAP, LANE)
    return pl.pallas_call(
        _vt_kernel, grid=(r // tm,),
        in_specs=[pl.BlockSpec((tm, w), lambda i: (i, col_off // w))],
        out_specs=pl.BlockSpec((w, tm), lambda i: (0, i)),
        out_shape=jax.ShapeDtypeStruct((w, r), BF16),
        compiler_params=_params("parallel"), name="transpose_heads",
    )(x)


def _flash_kernel(*refs, groups, tq, tk, dv, n_kv, chain, has_sink, key_chunk):
    q_ref, qn_ref, k_ref, kn_ref, vt_ref = refs[0:5]
    pos = 5
    if has_sink:
        sink_ref = refs[pos]
        pos += 1
    o_ref, m_ref, l_ref, acc_ref, s_ref, mt_ref = refs[pos:pos + 6]
    i, j = pl.program_id(1), pl.program_id(2)
    n = groups * tq
    bounds = [(r0, min(r0 + key_chunk, tk)) for r0 in range(0, tk, key_chunk)]

    def stacked(qr):
        return qr[...].reshape(n, qr.shape[2])

    def scores(kr, q, r0, r1):
        return lax.dot_general(kr[r0:r1, :], q, (((1,), (1,)), ((), ())), preferred_element_type=F32)

    @pl.when(j == 0)
    def _():
        m_ref[...] = jnp.full(m_ref.shape, NEG_INF, F32)
        l_ref[...] = jnp.zeros(l_ref.shape, F32)
        acc_ref[...] = jnp.zeros(acc_ref.shape, F32)

    @pl.when((j == 0) & (i == 0) if chain else j == 0)
    def _():
        q = stacked(q_ref)
        mt = jnp.full(mt_ref.shape, NEG_INF, F32)
        for r0, r1 in bounds:
            s = scores(k_ref, q, r0, r1)
            s_ref[r0:r1, :] = s
            mt = jnp.maximum(mt, jnp.max(s, axis=0, keepdims=True))
        mt_ref[...] = mt

    def step(kr, qr):
        prefetch = kr is not None
        if prefetch:
            q = stacked(qr)
        m_prev = m_ref[...]
        m_new = jnp.maximum(m_prev, mt_ref[...])
        alpha = jnp.exp2(m_prev - m_new)
        l_new = alpha * l_ref[...]
        acc_new = alpha * acc_ref[...]
        mt = jnp.full(mt_ref.shape, NEG_INF, F32)
        for r0, r1 in bounds:
            if prefetch:
                s_next = scores(kr, q, r0, r1)
            p = jnp.exp2(s_ref[r0:r1, :] - m_new)
            l_new = l_new + jnp.sum(p, axis=0, keepdims=True)
            acc_new = acc_new + _dot(vt_ref[:, r0:r1], p.astype(BF16))
            if prefetch:
                s_ref[r0:r1, :] = s_next
                mt = jnp.maximum(mt, jnp.max(s_next, axis=0, keepdims=True))
        l_ref[...] = l_new
        acc_ref[...] = acc_new
        m_ref[...] = m_new
        if prefetch:
            mt_ref[...] = mt

    if n_kv > 1:
        pl.when(j < n_kv - 1)(lambda: step(kn_ref, q_ref))

    @pl.when(j == n_kv - 1)
    def _():
        if chain:
            step(k_ref, qn_ref)
        else:
            step(None, None)
        m, l, acc = m_ref[...], l_ref[...], acc_ref[...]
        if has_sink:
            sk = sink_ref[0]
            m_fin = jnp.maximum(m, sk)
            a = jnp.exp2(m - m_fin)
            l = l * a + jnp.exp2(sk - m_fin)
            acc = acc * a
        o_t = acc / l
        for g in range(groups):
            o_ref[:, g * dv:(g + 1) * dv] = o_t[:, g * tq:(g + 1) * tq].T.astype(o_ref.dtype)


KEY_CHUNK = 256


def flash(q, k, vt, *, kv_heads, groups, n_q, q_row0, kv_row0, n_kv_rows, tq, tk, dv=HEAD_DIM, sink=None):
    dk = q.shape[2]
    qb0 = q_row0 // tq
    kb0 = kv_row0 // tk
    n_kv = n_kv_rows // tk

    def kv_row(i, j):
        return kb0 + j
    n = groups * tq
    n_qb = n_q // tq
    nxt = lambda j: jnp.minimum(j + 1, n_kv - 1)
    q_nxt = lambda i: qb0 + jnp.minimum(i + 1, n_qb - 1)
    in_specs = [pl.BlockSpec((groups, tq, dk), lambda h, i, j: (h, qb0 + i, 0)),
                pl.BlockSpec((groups, tq, dk), lambda h, i, j: (h, q_nxt(i), 0)),
                pl.BlockSpec((tk, dk), lambda h, i, j: (kv_row(i, 0), h)),
                pl.BlockSpec((tk, dk), lambda h, i, j: (kv_row(i, nxt(j)), h)),
                pl.BlockSpec((dv, tk), lambda h, i, j: (h, kv_row(i, j)))]
    args = [q, q, k, k, vt]
    if sink is not None:
        in_specs.append(pl.BlockSpec((1, 1, n), lambda h, i, j: (h, 0, 0)))
        sink2 = sink.astype(F32).reshape(kv_heads, groups, 1) * LOG2E
        args.append(jnp.broadcast_to(sink2, (kv_heads, groups, tq)).reshape(kv_heads, 1, n))
    kern = functools.partial(
        _flash_kernel, groups=groups, tq=tq, tk=tk, dv=dv, n_kv=n_kv, chain=n_qb > 1, has_sink=sink is not None,
        key_chunk=KEY_CHUNK)
    return pl.pallas_call(
        kern, grid=(kv_heads, n_q // tq, n_kv), in_specs=in_specs,
        out_specs=pl.BlockSpec((tq, groups * dv), lambda h, i, j: (i, h)),
        out_shape=jax.ShapeDtypeStruct((n_q, kv_heads * groups * dv), BF16),
        scratch_shapes=[pltpu.VMEM((1, n), F32), pltpu.VMEM((1, n), F32), pltpu.VMEM((dv, n), F32),
                        pltpu.VMEM((tk, n), F32), pltpu.VMEM((1, n), F32)],
        compiler_params=_params("parallel", "arbitrary", "arbitrary"),
        name="flash_dense",
    )(*args)


def _window_kernel(*refs, groups, tq, n_lat, has_sink):
    q_ref = refs[0]
    n_half = tq // WINDOW + 2
    k_refs = refs[1:2 + n_half]
    vt_refs = refs[2 + n_half:3 + 2 * n_half]
    pos = 3 + 2 * n_half
    if has_sink:
        sink_ref = refs[pos]
        pos += 1
    o_ref = refs[pos]
    i = pl.program_id(1)
    n = groups * tq
    dv = vt_refs[0].shape[0]
    q = q_ref[...].reshape(n, q_ref.shape[2])
    qpos = i * tq + lax.broadcasted_iota(jnp.int32, (1, n), 1) % tq
    scores = []
    m = sink_ref[0] if has_sink else jnp.full((1, n), NEG_INF, F32)
    for b, k_ref in enumerate(k_refs):
        s = lax.dot_general(k_ref[...], q, (((1,), (1,)), ((), ())), preferred_element_type=F32)
        if b > 0:
            kpos = i * tq + (b - 2) * WINDOW + lax.broadcasted_iota(jnp.int32, (WINDOW, 1), 0)
            valid = (jnp.abs(qpos - kpos) <= WINDOW) & (kpos >= 0) & (kpos < n_lat)
            s = jnp.where(valid, s, NEG_INF)
        scores.append(s)
        m = jnp.maximum(m, jnp.max(s, axis=0, keepdims=True))
    l = jnp.exp2(sink_ref[0] - m) if has_sink else jnp.zeros((1, n), F32)
    acc = jnp.zeros((dv, n), F32)
    for s, vt_ref in zip(scores, vt_refs):
        p = jnp.exp2(s - m)
        l = l + jnp.sum(p, axis=0, keepdims=True)
        acc = acc + _dot(vt_ref[...], p.astype(BF16))
    o_t = acc / l
    for g in range(groups):
        o_ref[:, g * dv:(g + 1) * dv] = o_t[:, g * tq:(g + 1) * tq].T.astype(o_ref.dtype)


def window_attention(q, k, vt, *, kv_heads, groups, n_lat, n_ctx, tq, sink=None, dv=HEAD_DIM):
    dk = q.shape[2]
    per = tq // WINDOW
    n_half_blocks = n_lat // WINDOW
    n = groups * tq
    ctx_blk = n_lat // n_ctx
    half = lambda b: (lambda i: jnp.clip(i * per + b - 1, 0, n_half_blocks - 1))
    halves = [half(b) for b in range(per + 2)]
    in_specs = [pl.BlockSpec((groups, tq, dk), lambda h, i: (h, i, 0)),
                pl.BlockSpec((n_ctx, dk), lambda h, i: (ctx_blk, h))]
    in_specs += [pl.BlockSpec((WINDOW, dk), lambda h, i, f=f: (f(i), h)) for f in halves]
    in_specs += [pl.BlockSpec((dv, n_ctx), lambda h, i: (h, ctx_blk))]
    in_specs += [pl.BlockSpec((dv, WINDOW), lambda h, i, f=f: (h, f(i))) for f in halves]
    args = [q] + [k] * (per + 3) + [vt] * (per + 3)
    if sink is not None:
        in_specs.append(pl.BlockSpec((1, 1, n), lambda h, i: (h, 0, 0)))
        sink2 = sink.astype(F32).reshape(kv_heads, groups, 1) * LOG2E
        args.append(jnp.broadcast_to(sink2, (kv_heads, groups, tq)).reshape(kv_heads, 1, n))
    return pl.pallas_call(
        functools.partial(_window_kernel, groups=groups, tq=tq, n_lat=n_lat, has_sink=sink is not None),
        grid=(kv_heads, n_lat // tq), in_specs=in_specs,
        out_specs=pl.BlockSpec((tq, groups * dv), lambda h, i: (i, h)),
        out_shape=jax.ShapeDtypeStruct((n_lat, kv_heads * groups * dv), BF16),
        compiler_params=_params("parallel", "parallel"), name="window_attention",
    )(*args)


def _outproj_kernel(*refs, lat_tiles, with_next):
    o1c_ref, o1l_ref, o2c_ref, o2l_ref, gate_ref, h_ref, mod_ref, w_ref = refs[:8]
    is_ctx = pl.program_id(0) >= lat_tiles
    w1 = o1l_ref.shape[1]
    d = h_ref.shape[1]
    gt = gate_ref[...].astype(F32)
    sg = gt * jax.nn.sigmoid(gt)
    o1 = jnp.where(is_ctx, o1c_ref[...].astype(F32), o1l_ref[...].astype(F32))
    o2 = jnp.where(is_ctx, o2c_ref[...].astype(F32), o2l_ref[...].astype(F32))
    t1 = (o1 * sg[:, :w1]).astype(BF16)
    t2 = (o2 * sg[:, w1:]).astype(BF16)
    y = _dot(t1, w_ref[0:w1, :]) + _dot(t2, w_ref[w1:, :])
    gm = jnp.where(is_ctx, mod_ref[1:2, 2 * d:3 * d], mod_ref[0:1, 2 * d:3 * d])
    h_new = h_ref[...] + gm * y
    if with_next:
        g_next_ref, mod_next_ref, out_ref, u_ref = refs[8:]
        u_ref[...] = _modulated_norm(h_new, g_next_ref[...], mod_next_ref, is_ctx).astype(u_ref.dtype)
    else:
        out_ref = refs[8]
    out_ref[...] = h_new


def out_proj(o1c, o1l, o2c, o2l, proj, h, mod, w, nxt):
    r, d = h.shape
    tm = ROW_TILE
    assert o1c.shape[0] == tm, "the context rows must be exactly one row tile"
    lt = o1l.shape[0] // tm
    w1, w2 = o1l.shape[1], o2l.shape[1]
    lat = lambda i: (jnp.minimum(i, lt - 1), 0)
    row = pl.BlockSpec((tm, d), lambda i: (i, 0))
    in_specs = [pl.BlockSpec((tm, w1), lambda i: (0, 0)), pl.BlockSpec((tm, w1), lat),
                pl.BlockSpec((tm, w2), lambda i: (0, 0)), pl.BlockSpec((tm, w2), lat),
                pl.BlockSpec((tm, w1 + w2), lambda i: (i, OFF_GATE // (w1 + w2))),
                row, pl.BlockSpec(mod.shape, lambda i: (0, 0)), pl.BlockSpec(w.shape, lambda i: (0, 0))]
    args = [o1c, o1l, o2c, o2l, proj, h, mod, w]
    if nxt is None:
        r = lt * tm
        out_specs, out_shape = row, jax.ShapeDtypeStruct((r, d), F32)
    else:
        in_specs += [pl.BlockSpec((1, d), lambda i: (0, 0)), pl.BlockSpec(nxt[1].shape, lambda i: (0, 0))]
        args += [nxt[0].reshape(1, d), nxt[1]]
        out_specs = [row, row]
        out_shape = [jax.ShapeDtypeStruct((r, d), F32), jax.ShapeDtypeStruct((r, d), BF16)]
    return pl.pallas_call(
        functools.partial(_outproj_kernel, lat_tiles=lt, with_next=nxt is not None), grid=(r // tm,),
        in_specs=in_specs, out_specs=out_specs, out_shape=out_shape,
        compiler_params=_params("parallel"), name="out_proj",
    )(*args)


def _conv3_kernel(x_ref, w_ref, b_ref, oc_ref, ol_ref, *, n_ctx):
    x = x_ref[...].astype(F32)
    r = x.shape[0]
    n_lat = r - n_ctx
    row = lax.broadcasted_iota(jnp.int32, (r, 1), 0)
    prev = jnp.where((row == 0) | (row == n_lat), 0.0, pltpu.roll(x, 1, 0))
    nxt = jnp.where((row == n_lat - 1) | (row == r - 1), 0.0, pltpu.roll(x, r - 1, 0))
    y = prev * w_ref[0:1, :] + x * w_ref[1:2, :] + nxt * w_ref[2:3, :] + b_ref[...]
    ol_ref[0] = y[:n_lat]
    oc_ref[0] = y[n_lat:]


def conv3(proj, col_off, w, b, n_ctx):
    r = proj.shape[0]
    cw = w.shape[1] // 3
    nb = cw // LANE
    cb = col_off // LANE
    return pl.pallas_call(
        functools.partial(_conv3_kernel, n_ctx=n_ctx), grid=(3, nb),
        in_specs=[pl.BlockSpec((r, LANE), lambda p, c: (0, cb + p * nb + c)),
                  pl.BlockSpec((3, LANE), lambda p, c: (0, p * nb + c)),
                  pl.BlockSpec((1, LANE), lambda p, c: (0, p * nb + c))],
        out_specs=[pl.BlockSpec((1, n_ctx, LANE), lambda p, c: (p, 0, c)),
                   pl.BlockSpec((1, r - n_ctx, LANE), lambda p, c: (p, 0, c))],
        out_shape=[jax.ShapeDtypeStruct((3, n_ctx, cw), F32), jax.ShapeDtypeStruct((3, r - n_ctx, cw), F32)],
        compiler_params=_params("parallel", "parallel"), name="conv3",
    )(proj, w, b.reshape(1, -1))


def _filt_kernel(z_ref, w1_ref, b1_ref, f1_ref, w2_ref, b2_ref, f2_ref, w3_ref, dl_ref, h_ref, tap_ref, nrm_ref, *,
                 tm, cw):
    i = pl.program_id(0)
    z = z_ref[...]

    def mm(a, w_r):
        return _dot3(*_split_bf16(a), *_split_bf16(w_r[...]))

    a = jnp.sin(f1_ref[...] * (mm(z, w1_ref) + b1_ref[...]))
    a = jnp.sin(f2_ref[...] * (mm(a, w2_ref) + b2_ref[...]))
    dec = jnp.exp(-z[:, 0:1] * dl_ref[...])
    h = _dot(a.astype(BF16), w3_ref[...]) * jnp.concatenate([dec] * (h_ref.shape[1] // cw), axis=1)
    h_ref[...] = h.astype(h_ref.dtype)
    row = i * tm + lax.broadcasted_iota(jnp.int32, (tm, 1), 0)
    col = lax.broadcasted_iota(jnp.int32, (1, h.shape[1]), 1)
    skip = (row == 0) & ((col // cw) % 2 == 1)
    part = jnp.sum(jnp.where(skip, 0.0, jnp.abs(h)), axis=0, keepdims=True)

    @pl.when(i == 0)
    def _():
        nrm_ref[...] = jnp.zeros(nrm_ref.shape, F32)
        tap_ref[...] = h[0:SUBLANES]

    nrm_ref[...] += jnp.broadcast_to(part, nrm_ref.shape)


def hyena_filter(z, w1p, b1, f1, w2, b2, f2, w3, deltas):
    l = z.shape[0]
    n = w3.shape[1]
    cw = deltas.shape[0]
    hid = w2.shape[0]
    tm = min(l, FILTER_ROWS)
    full = lambda a: pl.BlockSpec(a.shape, lambda i: (0, 0))
    ins = [w1p, b1.reshape(1, hid), f1.reshape(1, hid), w2, b2.reshape(1, hid), f2.reshape(1, hid), w3,
           deltas.reshape(1, cw)]
    small = pl.BlockSpec((SUBLANES, n), lambda i: (0, 0))
    return pl.pallas_call(
        functools.partial(_filt_kernel, tm=tm, cw=cw), grid=(l // tm,),
        in_specs=[pl.BlockSpec((tm, LANE), lambda i: (i, 0))] + [full(a) for a in ins],
        out_specs=[pl.BlockSpec((tm, n), lambda i: (i, 0)), small, small],
        out_shape=[jax.ShapeDtypeStruct((l, n), BF16), jax.ShapeDtypeStruct((SUBLANES, n), F32),
                   jax.ShapeDtypeStruct((SUBLANES, n), F32)],
        compiler_params=_params("arbitrary"), name="hyena_filter",
    )(z, *ins)


@functools.lru_cache(maxsize=None)
def _dft_consts(a):
    n = a * a
    hh = a // 2
    n1 = np.arange(hh)
    k1 = np.arange(hh)
    n2 = np.arange(a)
    k2 = np.arange(a)
    ang1 = 2 * np.pi * np.outer(k1 + 0.5, n1) / a
    m1 = np.zeros((hh, 2, hh))
    m1[:, 0], m1[:, 1] = np.cos(ang1), -np.sin(ang1)
    m1 = m1.reshape(2 * hh, hh)
    phi = 2 * np.pi * (n2[None, None, :] * (k1[:, None, None] + 0.5) / n + n2[None, None, :] * k2[None, :, None] / a)
    c, s = np.cos(phi), np.sin(phi)
    g = np.zeros((hh, 2, a, 2, a))
    g[:, 0, :, 0, :], g[:, 0, :, 1, :], g[:, 1, :, 0, :], g[:, 1, :, 1, :] = c, s, -s, c
    g = g.reshape(hh, 2 * a, 2 * a)
    gt = np.ascontiguousarray(np.transpose(g, (0, 2, 1)))
    al = 2 * np.pi * np.outer(n1, k1 + 0.5) / a
    mb = np.zeros((hh, hh, 2))
    mb[:, :, 0], mb[:, :, 1] = 2 / n * np.cos(al), -2 / n * np.sin(al)
    mb = mb.reshape(hh, 2 * hh)
    eye = np.eye(ROWS_BF16)
    return tuple(x.astype(np.float32) for x in (np.kron(m1, eye), g, gt, np.kron(mb, eye)))


@functools.lru_cache(maxsize=None)
def _dense_dft_consts(l):
    n = 2 * l
    ang = 2 * np.pi * np.outer(np.arange(l) + 0.5, np.arange(l)) / n
    mf = np.concatenate([np.cos(ang), -np.sin(ang)], axis=0)
    mi = np.concatenate([2 / n * np.cos(ang.T), -2 / n * np.sin(ang.T)], axis=1)
    return tuple(x.astype(np.float32) for x in (mf, mi))


def _stage1_kernel(x_ref, a_ref, o_ref, a16_ref):
    @pl.when((pl.program_id(0) == 0) & (pl.program_id(1) == 0))
    def _():
        a16_ref[...] = a_ref[...].astype(BF16)

    _, hh, t, ct = x_ref.shape
    x = x_ref[0].reshape(hh * t, ct).astype(BF16)
    o_ref[...] = _dot(a16_ref[...], x).astype(o_ref.dtype).reshape(o_ref.shape)


def dft_stage1(x4, p, a1):
    _, hh, a, c = x4.shape
    ct = _pick_tile(c, DFT_COL_CAP)
    t = ROWS_BF16
    return pl.pallas_call(
        _stage1_kernel, grid=(a // t, c // ct),
        in_specs=[pl.BlockSpec((1, hh, t, ct), lambda j, q: (p, 0, j, q)),
                  pl.BlockSpec(a1.shape, lambda j, q: (0, 0), pipeline_mode=pl.Buffered(1))],
        out_specs=pl.BlockSpec((2 * hh, t, ct), lambda j, q: (0, j, q)),
        out_shape=jax.ShapeDtypeStruct((2 * hh, a, c), BF16),
        scratch_shapes=[pltpu.VMEM(a1.shape, BF16)],
        compiler_params=_params("arbitrary", "arbitrary"), name="dft_stage1",
    )(x4, a1)


def _spectrum_product(x, h0, h1, tap0, n0, n1):
    half = x.shape[0] // 2
    inv = 1.0 / (n0 + n1)
    kr = (h0[:half] + h1[:half] - tap0) * inv
    ki = (h0[half:] - h1[half:]) * inv
    xr, xi = x[:half], x[half:]
    return jnp.concatenate([xr * kr - xi * ki, xr * ki + xi * kr], axis=0)


K1_PER_STEP = 4


def _mid_filter_kernel(y0_ref, y1_ref, g_ref, tap_ref, n0_ref, n1_ref, o_ref):
    inv = 1.0 / (n0_ref[0:1, :] + n1_ref[0:1, :])
    tap = tap_ref[0:1, :]
    for b in range(y0_ref.shape[0]):
        g = g_ref[b].astype(BF16)
        x0 = _dot(g, y0_ref[b])
        x1 = _dot(g, y1_ref[b])
        half = x0.shape[0] // 2
        kr = (x0[:half] + x1[:half] - tap) * inv
        ki = (x0[half:] - x1[half:]) * inv
        o_ref[b] = jnp.concatenate([kr, ki], axis=0).astype(o_ref.dtype)


def filter_spectrum(h, taps, nrm, consts, a):
    a1, g, _, _ = consts
    l, n = h.shape
    c = n // 4
    hh = a // 2
    y = dft_stage1(h.reshape(1, hh, a, n), 0, a1).reshape(hh, 2 * a, n)
    kb = K1_PER_STEP
    slab = lambda side: pl.BlockSpec((kb, 2 * a, c), lambda k, o: (k, 0, 2 * o + side))
    row = lambda side: pl.BlockSpec((SUBLANES, c), lambda k, o: (0, 2 * o + side))
    return pl.pallas_call(
        _mid_filter_kernel, grid=(hh // kb, 2),
        in_specs=[slab(0), slab(1), pl.BlockSpec((kb, 2 * a, 2 * a), lambda k, o: (k, 0, 0)), row(1), row(0), row(1)],
        out_specs=pl.BlockSpec((kb, 2 * a, c), lambda k, o: (k, 0, o)),
        out_shape=jax.ShapeDtypeStruct((hh, 2 * a, 2 * c), BF16),
        compiler_params=_params("parallel", "parallel"), name="dft_mid_filter",
    )(y, y, g, taps, nrm, nrm)


def _mid_conv_kernel(y_ref, g_ref, gt_ref, kf_ref, o_ref):
    for b in range(y_ref.shape[0]):
        x = _dot(g_ref[b].astype(BF16), y_ref[b])
        kf = kf_ref[b].astype(F32)
        half = x.shape[0] // 2
        xr, xi, kr, ki = x[:half], x[half:], kf[:half], kf[half:]
        z = jnp.concatenate([xr * kr - xi * ki, xr * ki + xi * kr], axis=0).astype(BF16)
        o_ref[b] = _dot(gt_ref[b].astype(BF16), z).astype(o_ref.dtype)


def dft_mid_conv(y, g, gt, kf, order):
    hh, a2, c = y.shape
    kb = K1_PER_STEP
    slab = pl.BlockSpec((kb, a2, c), lambda k: (k, 0, 0))
    mat = pl.BlockSpec((kb, a2, a2), lambda k: (k, 0, 0))
    return pl.pallas_call(
        _mid_conv_kernel, grid=(hh // kb,),
        in_specs=[slab, mat, mat, pl.BlockSpec((kb, a2, c), lambda k: (k, 0, order))],
        out_specs=slab, out_shape=jax.ShapeDtypeStruct((hh, a2, c), BF16),
        compiler_params=_params("parallel"), name="dft_mid_conv",
    )(y, g, gt, kf)


def _last_kernel(b_ref, a_ref, xa_ref, zb_ref, bias_ref, o_ref, a16_ref):
    @pl.when((pl.program_id(0) == 0) & (pl.program_id(1) == 0))
    def _():
        a16_ref[...] = a_ref[...].astype(BF16)

    rows, t, ct = b_ref.shape
    conv = _dot(a16_ref[...], b_ref[...].reshape(rows * t, ct)).reshape(rows // 2, t, ct)
    o_ref[0] = (xa_ref[0] * (conv + bias_ref[...] * zb_ref[0])).astype(o_ref.dtype)


def dft_last(b, a2m, xa4, pa, zb4, pb, bias_row):
    rows, a, c = b.shape
    hh = rows // 2
    ct = _pick_tile(c, DFT_COL_CAP)
    t = ROWS_BF16
    sig = lambda pp: pl.BlockSpec((1, hh, t, ct), lambda j, q: (pp, 0, j, q))
    return pl.pallas_call(
        _last_kernel, grid=(a // t, c // ct),
        in_specs=[pl.BlockSpec((rows, t, ct), lambda j, q: (0, j, q)),
                  pl.BlockSpec(a2m.shape, lambda j, q: (0, 0), pipeline_mode=pl.Buffered(1)),
                  sig(pa), sig(pb), pl.BlockSpec((1, ct), lambda j, q: (0, q))],
        out_specs=sig(0), out_shape=jax.ShapeDtypeStruct((1, hh, a, c), F32),
        scratch_shapes=[pltpu.VMEM(a2m.shape, BF16)],
        compiler_params=_params("arbitrary", "arbitrary"), name="dft_last",
    )(b, a2m, xa4, zb4, bias_row)


def long_conv_gated(u3, kf, bias, consts, a):
    a1, g, gt, a2m = consts
    _, l, c = u3.shape
    hh = a // 2
    u4 = u3.reshape(3, hh, a, c)

    def conv_gate(src4, p_src, xa_p, order):
        y = dft_stage1(src4, p_src, a1).reshape(hh, 2 * a, c)
        bb = dft_mid_conv(y, g, gt, kf, order).reshape(2 * hh, a, c)
        return dft_last(bb, a2m, u4, xa_p, src4, p_src, bias[order].reshape(1, c))

    z2 = conv_gate(u4, 2, 0, 0)
    return conv_gate(z2, 0, 1, 1).reshape(l, c)


def _ctx_spec_kernel(h_ref, mf_ref, o_ref):
    o_ref[...] = _dot(mf_ref[...].astype(BF16), h_ref[...])


def ctx_filter_spectrum(h, mf):
    l, n = h.shape
    tn = _pick_tile(n, COL_CAP)
    return pl.pallas_call(
        _ctx_spec_kernel, grid=(n // tn,),
        in_specs=[pl.BlockSpec((l, tn), lambda j: (0, j)), pl.BlockSpec(mf.shape, lambda j: (0, 0))],
        out_specs=pl.BlockSpec((2 * l, tn), lambda j: (0, j)),
        out_shape=jax.ShapeDtypeStruct((2 * l, n), F32),
        compiler_params=_params("parallel"), name="ctx_filter_spectrum",
    )(h, mf)


def _ctx_hyena_kernel(u_ref, mf_ref, mi_ref, hf0a, hf1a, hf0b, hf1b, tap0, tap1, n0a, n1a, n0b, n1b, bias_ref, o_ref):
    x1, x2, z = u_ref[0], u_ref[1], u_ref[2]

    mf, mi = mf_ref[...].astype(BF16), mi_ref[...].astype(BF16)

    def conv(sig, h0_ref, h1_ref, tap_ref, n0_ref, n1_ref):
        x = _dot(mf, sig.astype(BF16))
        zz = _spectrum_product(x, h0_ref[...], h1_ref[...], tap_ref[0:1, :], n0_ref[0:1, :], n1_ref[0:1, :])
        return _dot(mi, zz.astype(BF16))

    z2 = x1 * (conv(z, hf0a, hf1a, tap0, n0a, n1a) + bias_ref[0:1, :] * z)
    o_ref[...] = (x2 * (conv(z2, hf0b, hf1b, tap1, n0b, n1b) + bias_ref[1:2, :] * z2)).astype(o_ref.dtype)


def ctx_hyena(u3, hf, taps, nrm, bias, mf, mi):
    _, l, c = u3.shape
    tc = _pick_tile(c, DFT_COL_CAP)
    nb = c // tc
    const = lambda a: pl.BlockSpec(a.shape, lambda j: (0, 0))
    col = lambda rows, q: pl.BlockSpec((rows, tc), lambda j: (0, q * nb + j))
    mats = [mf, mi]
    return pl.pallas_call(
        _ctx_hyena_kernel, grid=(nb,),
        in_specs=[pl.BlockSpec((3, l, tc), lambda j: (0, 0, j))] + [const(m) for m in mats]
        + [col(2 * l, 0), col(2 * l, 1), col(2 * l, 2), col(2 * l, 3), col(8, 1), col(8, 3),
           col(8, 0), col(8, 1), col(8, 2), col(8, 3), pl.BlockSpec((2, tc), lambda j: (0, j))],
        out_specs=pl.BlockSpec((l, tc), lambda j: (0, j)),
        out_shape=jax.ShapeDtypeStruct((l, c), BF16),
        compiler_params=_params("parallel"), name="ctx_hyena",
    )(u3, *mats, hf, hf, hf, hf, taps, taps, nrm, nrm, nrm, nrm, bias)


def _axial_tables(n_tokens, n_rot):
    rows = n_tokens // GRID_W
    row = jnp.broadcast_to(jnp.arange(rows)[:, None], (rows, GRID_W)).reshape(-1).astype(F32)
    col = jnp.broadcast_to(jnp.arange(GRID_W)[None, :], (rows, GRID_W)).reshape(-1).astype(F32)
    n_freq = n_rot // 4
    inv = ROPE_THETA ** (-jnp.arange(n_freq, dtype=F32) / n_freq)
    ang = jnp.concatenate([row[:, None] * inv, col[:, None] * inv], axis=-1)
    return jnp.cos(ang), jnp.sin(ang)


def _rope_tables_full(s, n_ctx):
    cos, sin = _axial_tables(s, HEAD_DIM)
    t0 = jnp.concatenate([jnp.concatenate([cos, cos], axis=1), jnp.ones((n_ctx, LANE), F32)], axis=0)
    t1 = jnp.concatenate([jnp.concatenate([-sin, sin], axis=1), jnp.zeros((n_ctx, LANE), F32)], axis=0)
    return t0, t1


def _rope_tables_half(s, n_ctx):
    cos, sin = _axial_tables(s, ROPE_DIM)
    q = ROPE_DIM // 2
    zq = jnp.zeros((s, q), F32)
    z2 = jnp.zeros((s, LANE - ROPE_DIM), F32)
    t0 = jnp.concatenate([cos, cos, z2], axis=1)
    t1 = jnp.concatenate([-sin, zq, z2], axis=1)
    t2 = jnp.concatenate([zq, sin, z2], axis=1)
    c0 = jnp.concatenate([jnp.ones((n_ctx, ROPE_DIM), F32), jnp.zeros((n_ctx, LANE - ROPE_DIM), F32)], axis=1)
    cz = jnp.zeros((n_ctx, LANE), F32)
    return jnp.concatenate([t0, c0], axis=0), jnp.concatenate([t1, cz], axis=0), jnp.concatenate([t2, cz], axis=0)


def _filter_features(l):
    pos = jnp.arange(l, dtype=F32)
    t = pos / max(l - 1, 1)
    bands = jnp.linspace(1e-4, HYENA_BANDS - 1, HYENA_BANDS, dtype=F32)
    ang = (2.0 * math.pi / l) * pos[:, None] * bands[None, :]
    z = jnp.concatenate([t[:, None], jnp.cos(ang), -jnp.sin(ang)], axis=-1)
    return jnp.pad(z, ((0, 0), (0, LANE - HYENA_EMB)))


Q_STACK = 1024
KV_TILE_CAP = 1408


def _attention_pair(qp, kp, vt, s, n_ctx, *, kv_heads, groups, window, sink=None):
    common = dict(kv_heads=kv_heads, groups=groups, sink=sink)
    o_ctx = flash(qp, kp, vt, n_q=n_ctx, q_row0=s, kv_row0=s, n_kv_rows=n_ctx, tq=n_ctx, tk=n_ctx, **common)
    if window:
        o_lat = window_attention(qp, kp, vt, kv_heads=kv_heads, groups=groups, n_lat=s, n_ctx=n_ctx, tq=ROW_TILE,
                                 sink=sink)
    else:
        o_lat = flash(qp, kp, vt, n_q=s, q_row0=0, kv_row0=0, n_kv_rows=n_ctx + s, tq=Q_STACK // groups,
                      tk=_pick_tile(n_ctx + s, KV_TILE_CAP), **common)
    return o_ctx, o_lat


def _even_layer(u, s, n_ctx, w_in, qn_g, kn_g, sink, conv_w, conv_b, fparams, hy_bias, rope_full, feats, dft):
    w_all, e = w_in
    src_gate = ATT_WIDTH + 2 * KV_WIDTH + 3 * B_WIDTH
    proj = matmul_permuted(u, w_all, e, [(src_gate, BRANCH), (0, ATT_WIDTH), (ATT_WIDTH, 2 * KV_WIDTH),
                                         (ATT_WIDTH + 2 * KV_WIDTH, 3 * B_WIDTH)], F32)
    qp = head_prep(proj, OFF_Q, A_HEADS, qn_g, rope_full, HEAD_DIM ** -0.5 * LOG2E, True)
    kp = head_prep(proj, OFF_K, A_KV_HEADS, kn_g, rope_full, 1.0)
    vt = transpose_heads(proj, OFF_V, A_KV_HEADS)
    a_ctx, a_lat = _attention_pair(qp, kp, vt, s, n_ctx, kv_heads=A_KV_HEADS, groups=A_HEADS // A_KV_HEADS,
                                   window=True, sink=sink)
    w1, b1, f1, w2, b2, f2, w3 = fparams
    w1p = jnp.pad(w1, ((0, LANE - HYENA_EMB), (0, 0)))
    w3 = w3.astype(BF16)
    deltas = jnp.linspace(DECAY_MAX, DECAY_MIN, B_WIDTH, dtype=F32)
    uc3, ul3 = conv3(proj, OFF_REST, conv_w, conv_b, n_ctx)
    a = int(round(math.sqrt(2 * s)))
    consts, (mf, mi) = dft
    h_lat, tap_lat, nrm_lat = hyena_filter(feats[0], w1p, b1, f1, w2, b2, f2, w3, deltas)
    kf_lat = filter_spectrum(h_lat, tap_lat, nrm_lat, consts, a)
    b_lat = long_conv_gated(ul3, kf_lat, hy_bias, consts, a)
    h_ctx, tap_ctx, nrm_ctx = hyena_filter(feats[1], w1p, b1, f1, w2, b2, f2, w3, deltas)
    hf_ctx = ctx_filter_spectrum(h_ctx, mf)
    b_ctx = ctx_hyena(uc3, hf_ctx, tap_ctx, nrm_ctx, hy_bias, mf, mi)
    return proj, a_ctx, a_lat, b_ctx, b_lat


def _odd_layer(u, s, n_ctx, w_in, qn_g, kn_g, cq_g, ckv_g, wuq, wukv, mq_g, mk_g, rope_full, rope_half):
    d = w_in.shape[0]
    parts = jnp.split(w_in, np.cumsum([ATT_WIDTH, KV_WIDTH, KV_WIDTH, Q_LORA, KV_LORA, ROPE_DIM]).tolist(), axis=1)
    wq, wk, wv, wmq, wmkv, wmkr, wg = parts
    w_perm = jnp.concatenate([wg, wq, wk, wv, wmq, wmkv, wmkr, jnp.zeros((d, LANE - ROPE_DIM), w_in.dtype)], axis=1)
    proj = matmul(u, w_perm.astype(BF16), F32)
    qp = head_prep(proj, OFF_Q, C_HEADS, qn_g, rope_full, HEAD_DIM ** -0.5 * LOG2E, True)
    kp = head_prep(proj, OFF_K, C_KV_HEADS, kn_g, rope_full, 1.0)
    vt = transpose_heads(proj, OFF_V, C_KV_HEADS)
    c_ctx, c_lat = _attention_pair(qp, kp, vt, s, n_ctx, kv_heads=C_KV_HEADS, groups=C_HEADS // C_KV_HEADS,
                                   window=False)
    pad_r = LANE - ROPE_DIM
    wuq3 = wuq.reshape(Q_LORA, M_HEADS, QK_DIM)
    wuq_p = jnp.concatenate([wuq3[:, :, :NOPE_DIM].reshape(Q_LORA, -1),
                             jnp.pad(wuq3[:, :, NOPE_DIM:], ((0, 0), (0, 0), (0, pad_r))).reshape(Q_LORA, -1)], axis=1)
    wukv3 = wukv.reshape(KV_LORA, M_HEADS, NOPE_DIM + V_DIM)
    wukv_p = jnp.concatenate([wukv3[:, :, :NOPE_DIM].reshape(KV_LORA, -1),
                              wukv3[:, :, NOPE_DIM:].reshape(KV_LORA, -1)], axis=1)
    q_raw = norm_matmul(proj, OFF_MQ, cq_g, wuq_p.astype(BF16))
    kv_raw = norm_matmul(proj, OFF_MKV, ckv_g, wukv_p.astype(BF16))
    scale = QK_DIM ** -0.5 * LOG2E
    gq_r = jnp.pad(mq_g[NOPE_DIM:], (0, pad_r))
    gk_r = jnp.pad(mk_g[NOPE_DIM:], (0, pad_r))
    q_full = mla_prep(q_raw, 0, q_raw, M_HEADS * NOPE_DIM, False, M_HEADS, mq_g[:NOPE_DIM], gq_r, rope_half, scale,
                      True)
    k_full = mla_prep(kv_raw, 0, proj, OFF_MKR, True, M_HEADS, mk_g[:NOPE_DIM], gk_r, rope_half, 1.0, False)
    vmt = transpose_heads(kv_raw, M_HEADS * NOPE_DIM, M_HEADS)
    d_ctx, d_lat = _attention_pair(q_full, k_full, vmt, s, n_ctx, kv_heads=M_HEADS, groups=1, window=False)
    return proj, c_ctx, c_lat, d_ctx, d_lat


def kernel(x, c, ctx, c_ctx, ada_w, ada_b, norm_g, w_out, ev_w_in, ev_qn_g, ev_kn_g, ev_sink, ev_conv_w, ev_conv_b, hy_w1, hy_b1, hy_f1, hy_w2, hy_b2, hy_f2, hy_w3, hy_bias, od_w_in, od_qn_g, od_kn_g, od_cq_g, od_ckv_g, od_wuq, od_wukv, od_mq_g, od_mk_g):
    _, s, d = x.shape
    n_ctx = ctx.shape[1]
    depth = ada_w.shape[0]
    h = jnp.concatenate([x[0], ctx[0]], axis=0)
    cc = jnp.zeros((8, d), F32).at[0].set(c[0]).at[1].set(c_ctx)
    mod = ada_mod(cc, ada_w, ada_b)
    rope_full = _rope_tables_full(s, n_ctx)
    rope_half = _rope_tables_half(s, n_ctx)
    feats = (_filter_features(s), _filter_features(n_ctx))
    as_arrays = lambda mats: tuple(jnp.asarray(m) for m in mats)
    dft = (as_arrays(_dft_consts(int(round(math.sqrt(2 * s))))), as_arrays(_dense_dft_consts(n_ctx)))
    u = norm_mod(h, norm_g[0], mod[0], n_ctx)
    for i in range(depth):
        if i % 2 == 0:
            e = i // 2
            proj, o1c, o1l, o2c, o2l = _even_layer(
                u, s, n_ctx, (ev_w_in, e), ev_qn_g[e], ev_kn_g[e], ev_sink[e], ev_conv_w[e], ev_conv_b[e],
                (hy_w1[e], hy_b1[e], hy_f1[e], hy_w2[e], hy_b2[e], hy_f2[e], hy_w3[e]), hy_bias[e], rope_full, feats,
                dft)
        else:
            o = i // 2
            proj, o1c, o1l, o2c, o2l = _odd_layer(
                u, s, n_ctx, od_w_in[o], od_qn_g[o], od_kn_g[o], od_cq_g[o], od_ckv_g[o], od_wuq[o], od_wukv[o],
                od_mq_g[o], od_mk_g[o], rope_full, rope_half)
        w = w_out[i].astype(BF16)
        if i < depth - 1:
            h, u = out_proj(o1c, o1l, o2c, o2l, proj, h, mod[i], w, (norm_g[i + 1], mod[i + 1]))
        else:
            h = out_proj(o1c, o1l, o2c, o2l, proj, h, mod[i], w, None)
    return h[None]
```
